```python
import jax, jax.numpy as jnp
from jax import lax
import numpy as np

D_MODEL = 2048
BATCH = 16
SEQ = 256
DEPTH = 2
DEC_BATCH = 8
DEC_SEQ = 2048
PAST_LEN = 256

GRID_W = 64
HEAD_DIM = 128
N_HEADS_GA = 8
N_KV_GA = 2
N_HEADS_NA = 8
NA_WIN_H = 8
NA_WIN_W = 16
ROPE_AXIS_DIM = HEAD_DIM // 2
ROPE_THETA = 10000.0
Q_BLOCK = 128
ATTN_SCALE = HEAD_DIM ** -0.5
N_EXPERTS = 16
N_GROUPS = 4
EXPERTS_PER_GROUP = N_EXPERTS // N_GROUPS
TOP_K = 2
D_FF = 1024
MOE_BLOCK = 128
N_MOD = 6
DN_ALPHA = (2.0 * DEPTH) ** 0.25
DN_BETA = (8.0 * DEPTH) ** -0.25
GA_Q = N_HEADS_GA * HEAD_DIM
GA_KV = N_KV_GA * HEAD_DIM
NA_W = N_HEADS_NA * HEAD_DIM
MIX_W = GA_Q + NA_W
IN_W = GA_Q + 2 * GA_KV + 3 * NA_W
IN_SPLITS = (GA_Q, GA_Q + GA_KV, GA_Q + 2 * GA_KV, GA_Q + 2 * GA_KV + NA_W, GA_Q + 2 * GA_KV + 2 * NA_W)
EPS = 1e-6

kernel_name = "hymba_style_dit_prefix_context_step"


def layer_norm(x, g, b):
    xf = x.astype(jnp.float32)
    mu = jnp.mean(xf, axis=-1, keepdims=True)
    var = jnp.mean(jnp.square(xf - mu), axis=-1, keepdims=True)
    return ((xf - mu) * lax.rsqrt(var + EPS) * g + b).astype(x.dtype)


def head_rms_norm(x, g):
    xf = x.astype(jnp.float32)
    return (xf * lax.rsqrt(jnp.mean(xf * xf, axis=-1, keepdims=True) + EPS) * g).astype(x.dtype)


def axial_rope(x):
    T = x.shape[1]
    t = jnp.arange(T, dtype=jnp.int32)
    row = (t // GRID_W).astype(jnp.float32)
    col = (t % GRID_W).astype(jnp.float32)
    half = ROPE_AXIS_DIM // 2
    inv_freq = 1.0 / (ROPE_THETA ** (jnp.arange(half, dtype=jnp.float32) / half))

    def rot(xa, pos):
        ang = pos[:, None] * inv_freq[None, :]
        cos = jnp.cos(ang)[None, :, None, :]
        sin = jnp.sin(ang)[None, :, None, :]
        x1, x2 = xa[..., :half], xa[..., half:]
        return jnp.concatenate([x1 * cos - x2 * sin, x1 * sin + x2 * cos], axis=-1)

    xf = x.astype(jnp.float32)
    out = jnp.concatenate([rot(xf[..., :ROPE_AXIS_DIM], row), rot(xf[..., ROPE_AXIS_DIM:], col)], axis=-1)
    return out.astype(x.dtype)


def modulation(cvec, w_mod, b_mod):
    m = jnp.einsum('...d,de->...e', jax.nn.silu(cvec), w_mod) + b_mod
    return jnp.split(m, N_MOD, axis=-1)


def project(h, w_in, q_norm, k_norm):
    B, T, _ = h.shape
    proj = jnp.einsum('btd,de->bte', h, w_in)
    qa, ka, va, qn, kn, vn = jnp.split(proj, IN_SPLITS, axis=-1)
    qa = head_rms_norm(qa.reshape(B, T, N_HEADS_GA, HEAD_DIM), q_norm)
    ka = head_rms_norm(ka.reshape(B, T, N_KV_GA, HEAD_DIM), k_norm)
    va = va.reshape(B, T, N_KV_GA, HEAD_DIM)
    qn = qn.reshape(B, T, N_HEADS_NA, HEAD_DIM)
    kn = kn.reshape(B, T, N_HEADS_NA, HEAD_DIM)
    vn = vn.reshape(B, T, N_HEADS_NA, HEAD_DIM)
    return qa, ka, va, qn, kn, vn


def dense_attention(q, k, v):
    B, Tq, G, R, hd = q.shape
    nb = Tq // Q_BLOCK
    qb = jnp.moveaxis(q.reshape(B, nb, Q_BLOCK, G, R, hd), 1, 0)

    def one(qblk):
        s = jnp.einsum('bqgrd,bkgd->bgrqk', qblk, k, preferred_element_type=jnp.float32) * ATTN_SCALE
        p = jax.nn.softmax(s, axis=-1).astype(v.dtype)
        return jnp.einsum('bgrqk,bkgd->bqgrd', p, v)

    o = lax.map(one, qb)
    return jnp.moveaxis(o, 0, 1).reshape(B, Tq, G, R, hd)


def neighborhood_attention(q, k, v, ctx_k, ctx_v, rel_bias):
    B, T, H, hd = q.shape
    rows = T // GRID_W
    kh = min(NA_WIN_H, rows)
    kw = NA_WIN_W
    qg = q.reshape(B, rows, GRID_W, H, hd)
    kg = k.reshape(B, rows, GRID_W, H, hd)
    vg = v.reshape(B, rows, GRID_W, H, hd)
    col = jnp.arange(GRID_W, dtype=jnp.int32)
    col_start = jnp.clip(col - kw // 2, 0, GRID_W - kw)
    col_idx = col_start[:, None] + jnp.arange(kw, dtype=jnp.int32)[None, :]
    col_off = col_idx - col[:, None] + (NA_WIN_W - 1)
    row_start = jnp.clip(jnp.arange(rows, dtype=jnp.int32) - kh // 2, 0, rows - kh)

    def one(r):
        rs = row_start[r]
        kband = lax.dynamic_slice_in_dim(kg, rs, kh, axis=1)
        vband = lax.dynamic_slice_in_dim(vg, rs, kh, axis=1)
        kwin = kband[:, :, col_idx]
        vwin = vband[:, :, col_idx]
        qr = lax.dynamic_index_in_dim(qg, r, axis=1, keepdims=False)
        row_off = rs + jnp.arange(kh, dtype=jnp.int32) - r + (NA_WIN_H - 1)
        bias = rel_bias[:, row_off[:, None, None], col_off[None, :, :]]
        bias = jnp.transpose(bias, (0, 2, 1, 3)).astype(jnp.float32)
        s_loc = jnp.einsum('bchd,bicjhd->bhcij', qr, kwin, preferred_element_type=jnp.float32) * ATTN_SCALE + bias[None]
        s_ctx = jnp.einsum('bchd,blhd->bhcl', qr, ctx_k, preferred_element_type=jnp.float32) * ATTN_SCALE
        s = jnp.concatenate([s_loc.reshape(B, H, GRID_W, kh * kw), s_ctx], axis=-1)
        p = jax.nn.softmax(s, axis=-1).astype(v.dtype)
        p_loc = p[..., :kh * kw].reshape(B, H, GRID_W, kh, kw)
        p_ctx = p[..., kh * kw:]
        return (jnp.einsum('bhcij,bicjhd->bchd', p_loc, vwin)
                + jnp.einsum('bhcl,blhd->bchd', p_ctx, ctx_v))

    o = lax.map(one, jnp.arange(rows, dtype=jnp.int32))
    return jnp.moveaxis(o, 0, 1).reshape(B, T, H, hd)


def route(h_tok, w_router, b_router):
    N = h_tok.shape[0]
    logits = jnp.einsum('nd,de->ne', h_tok, w_router, preferred_element_type=jnp.float32) + b_router.astype(jnp.float32)
    probs = jax.nn.softmax(logits, axis=-1)
    pg = probs.reshape(N, N_GROUPS, EXPERTS_PER_GROUP)
    top_in_group, _ = lax.top_k(pg, TOP_K)
    g = jnp.argmax(jnp.sum(top_in_group, axis=-1), axis=-1).astype(jnp.int32)
    p_sel = jnp.take_along_axis(pg, g[:, None, None], axis=1)[:, 0]
    w, local = lax.top_k(p_sel, TOP_K)
    e_idx = g[:, None] * EXPERTS_PER_GROUP + local.astype(jnp.int32)
    w = w / jnp.sum(w, axis=-1, keepdims=True)
    return e_idx, w.astype(h_tok.dtype)


def moe_ffn(h, w_router, b_router, w_gate, w_up, w_down):
    shp = h.shape
    D = shp[-1]
    h_tok = h.reshape(-1, D)
    N = h_tok.shape[0]
    e_idx, e_w = route(h_tok, w_router, b_router)
    A = N * TOP_K
    flat_e = e_idx.reshape(A)
    flat_w = e_w.reshape(A)
    flat_tok = jnp.arange(A, dtype=jnp.int32) // TOP_K
    order = jnp.argsort(flat_e)
    se, stok, sw = flat_e[order], flat_tok[order], flat_w[order]
    counts = jnp.bincount(flat_e, length=N_EXPERTS).astype(jnp.int32)
    start = jnp.cumsum(counts) - counts
    padded = (counts + MOE_BLOCK - 1) // MOE_BLOCK * MOE_BLOCK
    pad_end = jnp.cumsum(padded)
    pad_start = pad_end - padded
    dest = pad_start[se] + jnp.arange(A, dtype=jnp.int32) - start[se]
    n_blocks = -(-A // MOE_BLOCK) + N_EXPERTS
    P = n_blocks * MOE_BLOCK
    tok_buf = jnp.full((P,), N, jnp.int32).at[dest].set(stok)
    w_buf = jnp.zeros((P,), h.dtype).at[dest].set(sw)
    blk_start = jnp.arange(n_blocks, dtype=jnp.int32) * MOE_BLOCK
    blk_e = jnp.minimum(jnp.searchsorted(pad_end, blk_start, side='right'), N_EXPERTS - 1).astype(jnp.int32)
    h_pad = jnp.concatenate([h_tok, jnp.zeros((1, D), h.dtype)], axis=0)

    def one(args):
        e, tok = args
        xb = h_pad[tok]
        a = jax.nn.silu(xb @ w_gate[e]) * (xb @ w_up[e])
        return a @ w_down[e]

    out = lax.map(one, (blk_e, tok_buf.reshape(n_blocks, MOE_BLOCK)))
    out = out.reshape(P, D) * w_buf[:, None]
    y = jax.ops.segment_sum(out, tok_buf, num_segments=N + 1)[:N]
    return y.reshape(shp)


def mix_out(oa, on, w_o):
    B, T = oa.shape[:2]
    cat = jnp.concatenate([oa.reshape(B, T, GA_Q), on.reshape(B, T, NA_W)], axis=-1)
    return jnp.einsum('btm,md->btd', cat, w_o)


def ffn_sublayer(x, shift, scale, gate, p, w_router, b_router):
    h = x * (1 + scale) + shift
    y = moe_ffn(h, w_router, b_router, p['w_gate'], p['w_up'], p['w_down'])
    return layer_norm(DN_ALPHA * x + gate * y, p['ln2_g'], p['ln2_b'])


def context_layer(x, c_ctx, p, w_router, b_router):
    sh1, sc1, g1, sh2, sc2, g2 = modulation(c_ctx, p['w_mod'], p['b_mod'])
    h = x * (1 + sc1) + sh1
    qa, ka, va, qn, kn, vn = project(h, p['w_in'], p['q_norm'], p['k_norm'])
    B, T = x.shape[:2]
    oa = dense_attention(qa.reshape(B, T, N_KV_GA, N_HEADS_GA // N_KV_GA, HEAD_DIM), ka, va)
    on = dense_attention(qn[:, :, :, None, :], kn, vn)
    x = layer_norm(DN_ALPHA * x + g1 * mix_out(oa, on, p['w_o']), p['ln1_g'], p['ln1_b'])
    x = ffn_sublayer(x, sh2, sc2, g2, p, w_router, b_router)
    return x, ka, va, kn, vn


def latent_layer(x, c, ctx_ka, ctx_va, ctx_kn, ctx_vn, p, w_router, b_router):
    sh1, sc1, g1, sh2, sc2, g2 = [m[:, None, :] for m in modulation(c, p['w_mod'], p['b_mod'])]
    h = x * (1 + sc1) + sh1
    qa, ka, va, qn, kn, vn = project(h, p['w_in'], p['q_norm'], p['k_norm'])
    qa = axial_rope(qa)
    ka = axial_rope(ka)
    B, T = x.shape[:2]
    k_all = jnp.concatenate([ctx_ka, ka], axis=1)
    v_all = jnp.concatenate([ctx_va, va], axis=1)
    oa = dense_attention(qa.reshape(B, T, N_KV_GA, N_HEADS_GA // N_KV_GA, HEAD_DIM), k_all, v_all)
    on = neighborhood_attention(qn, kn, vn, ctx_kn, ctx_vn, p['rel_bias'])
    x = layer_norm(DN_ALPHA * x + g1 * mix_out(oa, on, p['w_o']), p['ln1_g'], p['ln1_b'])
    x = ffn_sublayer(x, sh2, sc2, g2, p, w_router, b_router)
    return x


def setup_inputs(seed: int = 0) -> dict:
    key = jax.random.key(seed)
    ks = jax.random.split(key, 24)
    f32 = jnp.float32

    def nrm(k, shape, s):
        return jax.random.normal(k, shape, f32) * s

    return {
        "x_prompt": nrm(ks[0], (BATCH, SEQ, D_MODEL), 1.0),
        "x_sample": nrm(ks[1], (DEC_BATCH, DEC_SEQ, D_MODEL), 1.0),
        "c": nrm(ks[2], (DEC_BATCH, D_MODEL), 1.0),
        "cache_ga_k": nrm(ks[3], (DEC_BATCH, DEPTH, PAST_LEN, N_KV_GA, HEAD_DIM), 1.0),
        "cache_ga_v": nrm(ks[4], (DEC_BATCH, DEPTH, PAST_LEN, N_KV_GA, HEAD_DIM), 1.0),
        "cache_na_k": nrm(ks[5], (DEC_BATCH, DEPTH, PAST_LEN, N_HEADS_NA, HEAD_DIM), 1.0),
        "cache_na_v": nrm(ks[6], (DEC_BATCH, DEPTH, PAST_LEN, N_HEADS_NA, HEAD_DIM), 1.0),
        "c_ctx": nrm(ks[7], (D_MODEL,), 1.0),
        "w_router": nrm(ks[8], (D_MODEL, N_EXPERTS), D_MODEL ** -0.5),
        "b_router": nrm(ks[9], (N_EXPERTS,), 0.01),
        "w_mod": nrm(ks[10], (DEPTH, D_MODEL, N_MOD * D_MODEL), 0.5 * D_MODEL ** -0.5),
        "b_mod": nrm(ks[11], (DEPTH, N_MOD * D_MODEL), 0.02),
        "w_in": nrm(ks[12], (DEPTH, D_MODEL, IN_W), D_MODEL ** -0.5),
        "q_norm": 1.0 + nrm(ks[13], (DEPTH, HEAD_DIM), 0.02),
        "k_norm": 1.0 + nrm(ks[14], (DEPTH, HEAD_DIM), 0.02),
        "rel_bias": nrm(ks[15], (DEPTH, N_HEADS_NA, 2 * NA_WIN_H - 1, 2 * NA_WIN_W - 1), 0.1),
        "w_o": nrm(ks[16], (DEPTH, MIX_W, D_MODEL), DN_BETA * MIX_W ** -0.5),
        "ln1_g": 1.0 + nrm(ks[17], (DEPTH, D_MODEL), 0.02),
        "ln1_b": nrm(ks[18], (DEPTH, D_MODEL), 0.02),
        "ln2_g": 1.0 + nrm(ks[19], (DEPTH, D_MODEL), 0.02),
        "ln2_b": nrm(ks[20], (DEPTH, D_MODEL), 0.02),
        "w_gate": nrm(ks[21], (DEPTH, N_EXPERTS, D_MODEL, D_FF), D_MODEL ** -0.5),
        "w_up": nrm(ks[22], (DEPTH, N_EXPERTS, D_MODEL, D_FF), D_MODEL ** -0.5),
        "w_down": nrm(ks[23], (DEPTH, N_EXPERTS, D_FF, D_MODEL), DN_BETA * D_FF ** -0.5),
    }


def reference(x_prompt, x_sample, c, cache_ga_k, cache_ga_v, cache_na_k, cache_na_v, c_ctx,
              w_router, b_router, w_mod, b_mod, w_in, q_norm, k_norm, rel_bias, w_o,
              ln1_g, ln1_b, ln2_g, ln2_b, w_gate, w_up, w_down):
    y_prompt = x_prompt
    y_sample = x_sample
    new_ga_k, new_ga_v, new_na_k, new_na_v = [], [], [], []
    for i in range(DEPTH):
        p = {
            'w_mod': w_mod[i], 'b_mod': b_mod[i], 'w_in': w_in[i], 'q_norm': q_norm[i],
            'k_norm': k_norm[i], 'rel_bias': rel_bias[i], 'w_o': w_o[i],
            'ln1_g': ln1_g[i], 'ln1_b': ln1_b[i], 'ln2_g': ln2_g[i], 'ln2_b': ln2_b[i],
            'w_gate': w_gate[i], 'w_up': w_up[i], 'w_down': w_down[i],
        }
        y_prompt, ka, va, kn, vn = context_layer(y_prompt, c_ctx, p, w_router, b_router)
        new_ga_k.append(ka)
        new_ga_v.append(va)
        new_na_k.append(kn)
        new_na_v.append(vn)
        y_sample = latent_layer(y_sample, c, cache_ga_k[:, i], cache_ga_v[:, i],
                                cache_na_k[:, i], cache_na_v[:, i], p, w_router, b_router)
    state_ga_k = jnp.stack(new_ga_k, axis=1)
    state_ga_v = jnp.stack(new_ga_v, axis=1)
    state_na_k = jnp.stack(new_na_k, axis=1)
    state_na_v = jnp.stack(new_na_v, axis=1)
    return (y_prompt, y_sample, state_ga_k, state_ga_v, state_na_k, state_na_v)
```

```python
import functools
import math

import numpy as np
import jax
import jax.numpy as jnp
from jax import lax
from jax.experimental import pallas as pl
from jax.experimental.pallas import tpu as pltpu

F32 = jnp.float32
BF16 = jnp.bfloat16

DEPTH = 2
GRID_W = 64
HEAD_DIM = 128
N_HEADS_GA = 8
N_KV_GA = 2
N_HEADS_NA = 8
NA_WIN_H = 8
NA_WIN_W = 16
ROPE_THETA = 10000.0
N_EXPERTS = 16
N_GROUPS = 4
EXPERTS_PER_GROUP = N_EXPERTS // N_GROUPS
TOP_K = 2
N_MOD = 6
DN_ALPHA = (2.0 * DEPTH) ** 0.25
EPS = 1e-6
ATTN_SCALE = HEAD_DIM ** -0.5
LOG2E = math.log2(math.e)
Q_PRESCALE = ATTN_SCALE * LOG2E
MASK_VALUE = -1e30

GA_Q = N_HEADS_GA * HEAD_DIM
GA_KV = N_KV_GA * HEAD_DIM
NA_W = N_HEADS_NA * HEAD_DIM
COL_QA = 0
COL_KA = COL_QA + N_HEADS_GA
COL_VA = COL_KA + N_KV_GA
COL_QN = COL_VA + N_KV_GA
COL_KN = COL_QN + N_HEADS_NA
COL_VN = COL_KN + N_HEADS_NA
IN_HEADS = COL_VN + N_HEADS_NA

LANE = 128
PROJ_TN = 4 * HEAD_DIM
PROJ_TM = 1024
ATTN_TQ = 256
NA_ROWS = 4
OUT_TM = 256
MOE_BLK = 256
MOD_TN = 1024
VMEM_LIMIT = 52 * 1024 * 1024


def _cparams(sem):
    return pltpu.CompilerParams(dimension_semantics=sem, vmem_limit_bytes=VMEM_LIMIT)


def _mod_kernel(c_ref, w_ref, b_ref, o_ref):
    c = c_ref[...]
    s = c / (1.0 + jnp.exp(-c))
    o_ref[...] = jnp.dot(s.astype(BF16), w_ref[...].astype(BF16),
                         preferred_element_type=F32) + b_ref[...]


def _modulation(cvec, w_mod, b_mod):
    depth, d, e = w_mod.shape
    rows = cvec.shape[0]
    tn = MOD_TN if e % MOD_TN == 0 else e
    return pl.pallas_call(
        _mod_kernel,
        grid=(depth, e // tn),
        in_specs=[
            pl.BlockSpec((rows, d), lambda l, j: (0, 0)),
            pl.BlockSpec((None, d, tn), lambda l, j: (l, 0, j)),
            pl.BlockSpec((None, 1, tn), lambda l, j: (l, 0, j)),
        ],
        out_specs=pl.BlockSpec((None, rows, tn), lambda l, j: (l, 0, j)),
        out_shape=jax.ShapeDtypeStruct((depth, rows, e), F32),
        compiler_params=_cparams(("arbitrary", "arbitrary")),
        name="modulation",
    )(cvec, w_mod, b_mod.reshape(depth, 1, e))


def _rms_head(x, g):
    ms = jnp.mean(x * x, axis=-1, keepdims=True)
    return x * lax.rsqrt(ms + EPS) * g


def _rope_head(x, cos, sin, first_half):
    swapped = jnp.where(first_half, pltpu.roll(x, HEAD_DIM - 32, 1), pltpu.roll(x, 32, 1))
    return x * cos + swapped * sin


def _proj_kernel(*refs, rope, cache_out):
    x_ref, mod_ref, w_ref, qg_ref, kg_ref = refs[:5]
    pos = 5
    if rope:
        cos_ref, sin_ref = refs[pos:pos + 2]
        pos += 2
    o_ref = refs[pos]
    pos += 1
    if cache_out:
        c_ref = refs[pos]
        pos += 1
    h_scr = refs[pos]

    j = pl.program_id(2)

    @pl.when(j == 0)
    def _():
        shift = mod_ref[0:1, :]
        scale = mod_ref[1:2, :]
        h_scr[...] = (x_ref[...] * (1.0 + scale) + shift).astype(BF16)

    acc = jnp.dot(h_scr[...], w_ref[...], preferred_element_type=F32)

    if rope:
        lane = lax.broadcasted_iota(jnp.int32, (1, HEAD_DIM), 1)
        first_half = (lane % 64) < 32

    def normed(hh, g):
        y = _rms_head(acc[:, hh * HEAD_DIM:(hh + 1) * HEAD_DIM], g)
        if rope:
            y = _rope_head(y, cos_ref[...], sin_ref[...], first_half)
        return y

    def put(hh, y, to_cache):
        sl = slice(hh * HEAD_DIM, (hh + 1) * HEAD_DIM)
        o_ref[:, sl] = y.astype(BF16)
        if cache_out and to_cache:
            c_ref[:, sl] = y

    heads_per_tile = PROJ_TN // HEAD_DIM
    j_ka = COL_KA // heads_per_tile
    j_qn = COL_QN // heads_per_tile
    j_kn = COL_KN // heads_per_tile

    @pl.when(j < j_ka)
    def _():
        for hh in range(heads_per_tile):
            put(hh, normed(hh, qg_ref[...]), False)

    @pl.when(j == j_ka)
    def _():
        for hh in range(N_KV_GA):
            put(hh, normed(hh, kg_ref[...]), True)
        for hh in range(N_KV_GA, heads_per_tile):
            put(hh, acc[:, hh * HEAD_DIM:(hh + 1) * HEAD_DIM], True)

    @pl.when((j >= j_qn) & (j < j_kn))
    def _():
        o_ref[...] = (acc * Q_PRESCALE).astype(BF16)

    @pl.when(j >= j_kn)
    def _():
        o_ref[...] = acc.astype(BF16)
        if cache_out:
            c_ref[...] = acc


def _in_proj(x, mod_l, mod_row0, w_in_bf, q_gain, k_gain, rope_tabs, cache_out):
    b, t, d = x.shape
    in_w = w_in_bf.shape[1]
    tm = min(PROJ_TM, t)
    nj = in_w // PROJ_TN
    rope = rope_tabs is not None
    heads_per_tile = PROJ_TN // HEAD_DIM
    j_ka = COL_KA // heads_per_tile
    j_kn = COL_KN // heads_per_tile
    assert COL_KA % heads_per_tile == 0 and COL_QN % heads_per_tile == 0
    assert 2 * N_KV_GA == heads_per_tile and COL_KN % heads_per_tile == 0

    in_specs = [
        pl.BlockSpec((None, tm, d), lambda bi, ti, j: (bi, ti, 0)),
        pl.BlockSpec((None, N_MOD, d), lambda bi, ti, j: (bi + mod_row0, 0, 0)),
        pl.BlockSpec((d, PROJ_TN), lambda bi, ti, j: (0, j)),
        pl.BlockSpec((1, HEAD_DIM), lambda bi, ti, j: (0, 0)),
        pl.BlockSpec((1, HEAD_DIM), lambda bi, ti, j: (0, 0)),
    ]
    args = [x, mod_l, w_in_bf, q_gain, k_gain]
    if rope:
        in_specs += [pl.BlockSpec((tm, HEAD_DIM), lambda bi, ti, j: (ti, 0))] * 2
        args += list(rope_tabs)
    out_specs = [pl.BlockSpec((None, tm, PROJ_TN), lambda bi, ti, j: (bi, ti, j))]
    out_shape = [jax.ShapeDtypeStruct((b, t, in_w), BF16)]
    if cache_out:
        def cache_idx(bi, ti, j):
            return (bi, ti, jnp.where(j <= j_ka, 0, jnp.where(j < j_kn, 1, j - j_kn + 1)))
        out_specs.append(pl.BlockSpec((None, tm, PROJ_TN), cache_idx))
        n_cache = (2 * N_KV_GA + 2 * N_HEADS_NA) * HEAD_DIM
        out_shape.append(jax.ShapeDtypeStruct((b, t, n_cache), F32))
    outs = pl.pallas_call(
        functools.partial(_proj_kernel, rope=rope, cache_out=cache_out),
        grid=(b, t // tm, nj),
        in_specs=in_specs,
        out_specs=out_specs,
        out_shape=out_shape,
        scratch_shapes=[pltpu.VMEM((tm, d), BF16)],
        compiler_params=_cparams(("arbitrary", "arbitrary", "arbitrary")),
        name="in_proj_rope" if rope else "in_proj_cache",
    )(*args)
    return outs


def _rope_tables(t):
    half = HEAD_DIM // 4
    tt = jnp.arange(t, dtype=jnp.int32)
    row = (tt // GRID_W).astype(F32)
    col = (tt % GRID_W).astype(F32)
    inv_freq = 1.0 / (ROPE_THETA ** (jnp.arange(half, dtype=F32) / half))
    ar = row[:, None] * inv_freq[None, :]
    ac = col[:, None] * inv_freq[None, :]
    cos = jnp.concatenate([jnp.cos(ar), jnp.cos(ar), jnp.cos(ac), jnp.cos(ac)], axis=-1)
    sin = jnp.concatenate([-jnp.sin(ar), jnp.sin(ar), -jnp.sin(ac), jnp.sin(ac)], axis=-1)
    return cos, sin


_NT = (((1,), (1,)), ((), ()))


def _softmax_pv(scores, values):
    m = scores[0].max(axis=-1, keepdims=True)
    for s in scores[1:]:
        m = jnp.maximum(m, s.max(axis=-1, keepdims=True))
    l = None
    acc = None
    for s, v in zip(scores, values):
        p = jnp.exp2(s - m)
        ps = p.sum(axis=-1, keepdims=True)
        pv = jnp.dot(p.astype(BF16), v, preferred_element_type=F32)
        l = ps if l is None else l + ps
        acc = pv if acc is None else acc + pv
    return acc * (1.0 / l)


def _dense_attn_kernel(*refs, r_heads, has_ctx):
    if has_ctx:
        q_ref, k_ref, v_ref, kc_ref, vc_ref, o_ref = refs
        kc = kc_ref[...].astype(BF16)
        vc = vc_ref[...].astype(BF16)
    else:
        q_ref, k_ref, v_ref, o_ref = refs
    k = k_ref[...]
    v = v_ref[...]
    for r in range(r_heads):
        sl = slice(r * HEAD_DIM, (r + 1) * HEAD_DIM)
        q = q_ref[:, sl]
        scores = [lax.dot_general(q, k, _NT, preferred_element_type=F32)]
        values = [v]
        if has_ctx:
            scores.append(lax.dot_general(q, kc, _NT, preferred_element_type=F32))
            values.append(vc)
        o_ref[:, sl] = _softmax_pv(scores, values).astype(BF16)


def _dense_attention(proj, q_col, k_col, v_col, groups, r_heads, ctx=None):
    b, t, _ = proj.shape
    tq = min(ATTN_TQ, t)
    qw = r_heads * HEAD_DIM
    assert q_col % r_heads == 0
    in_specs = [
        pl.BlockSpec((None, tq, qw), lambda bi, g, qi: (bi, qi, q_col // r_heads + g)),
        pl.BlockSpec((None, t, HEAD_DIM), lambda bi, g, qi: (bi, 0, k_col + g)),
        pl.BlockSpec((None, t, HEAD_DIM), lambda bi, g, qi: (bi, 0, v_col + g)),
    ]
    args = [proj, proj, proj]
    if ctx is not None:
        ck, cv, layer = ctx
        l_ctx = ck.shape[2]
        spec = pl.BlockSpec((None, None, l_ctx, HEAD_DIM), lambda bi, g, qi: (bi, layer, 0, g))
        in_specs += [spec, spec]
        args += [ck, cv]
    return pl.pallas_call(
        functools.partial(_dense_attn_kernel, r_heads=r_heads, has_ctx=ctx is not None),
        grid=(b, groups, t // tq),
        in_specs=in_specs,
        out_specs=pl.BlockSpec((None, tq, qw), lambda bi, g, qi: (bi, qi, g)),
        out_shape=jax.ShapeDtypeStruct((b, t, groups * qw), BF16),
        compiler_params=_cparams(("arbitrary", "arbitrary", "arbitrary")),
        name="dense_attn_ctx" if ctx is not None else "dense_attn",
    )(*args)


def _na_plan(rows):
    kh = min(NA_WIN_H, rows)
    kw = NA_WIN_W
    r_blk = min(NA_ROWS, rows)
    assert rows % r_blk == 0
    slab = min(r_blk - 1 + kh, rows)
    row_start = np.clip(np.arange(rows) - kh // 2, 0, rows - kh)
    col = np.arange(GRID_W)
    col_start = np.clip(col - kw // 2, 0, GRID_W - kw)
    slab_start, pat_id, pats, sigs = [], [], [], {}
    for r0 in range(0, rows, r_blk):
        ss = min(row_start[r0], rows - slab)
        rel = tuple(int(row_start[r0 + ri] - ss) for ri in range(r_blk))
        sig = (int(ss - r0), rel)
        if sig not in sigs:
            sigs[sig] = len(pats)
            q_row = r0 + np.arange(r_blk)[:, None, None, None]
            q_col = col[None, :, None, None]
            k_row = ss + np.arange(slab)[None, None, :, None]
            k_col = col[None, None, None, :]
            rs = row_start[r0:r0 + r_blk][:, None, None, None]
            cs = col_start[None, :, None, None]
            valid = (k_row >= rs) & (k_row < rs + kh) & (k_col >= cs) & (k_col < cs + kw)
            row_off = np.clip(k_row - q_row + (NA_WIN_H - 1), 0, 2 * NA_WIN_H - 2)
            col_off = np.clip(k_col - q_col + (NA_WIN_W - 1), 0, 2 * NA_WIN_W - 2)
            shape = (r_blk * GRID_W, slab * GRID_W)
            full = (r_blk, GRID_W, slab, GRID_W)
            pats.append((np.broadcast_to(valid, full).reshape(shape),
                         np.broadcast_to(row_off, full).reshape(shape),
                         np.broadcast_to(col_off, full).reshape(shape)))
        slab_start.append(int(ss))
        pat_id.append(sigs[sig])
    return r_blk, slab, np.array(slab_start, np.int32), np.array(pat_id, np.int32), pats


def _na_bias_table(rel_bias_l, pats):
    tabs = []
    for valid, row_off, col_off in pats:
        bias = rel_bias_l[:, row_off, col_off].astype(F32) * LOG2E
        tabs.append(jnp.where(valid[None], bias, MASK_VALUE))
    return jnp.stack(tabs, axis=0)


def _na_kernel(ss_ref, pat_ref, q_ref, k_ref, v_ref, kc_ref, vc_ref, bias_ref, o_ref, *, slab_len):
    del pat_ref
    rb = pl.program_id(2)
    start = pl.multiple_of(ss_ref[rb] * GRID_W, GRID_W)
    ks = k_ref[pl.ds(start, slab_len), :]
    vs = v_ref[pl.ds(start, slab_len), :]
    kc = kc_ref[...].astype(BF16)
    vc = vc_ref[...].astype(BF16)
    q = q_ref[...]
    s_loc = lax.dot_general(q, ks, _NT, preferred_element_type=F32) + bias_ref[...]
    s_ctx = lax.dot_general(q, kc, _NT, preferred_element_type=F32)
    o_ref[...] = _softmax_pv([s_loc, s_ctx], [vs, vc]).astype(BF16)


def _neighborhood_attention(proj, ck, cv, layer, bias_tab, plan):
    b, t, _ = proj.shape
    r_blk, slab, slab_start, pat_id, _ = plan
    qn = r_blk * GRID_W
    sn = slab * GRID_W
    l_ctx = ck.shape[2]
    grid_spec = pltpu.PrefetchScalarGridSpec(
        num_scalar_prefetch=2,
        grid=(b, N_HEADS_NA, t // qn),
        in_specs=[
            pl.BlockSpec((None, qn, HEAD_DIM), lambda bi, h, rb, ss, pt: (bi, rb, COL_QN + h)),
            pl.BlockSpec((None, t, HEAD_DIM), lambda bi, h, rb, ss, pt: (bi, 0, COL_KN + h)),
            pl.BlockSpec((None, t, HEAD_DIM), lambda bi, h, rb, ss, pt: (bi, 0, COL_VN + h)),
            pl.BlockSpec((None, None, l_ctx, HEAD_DIM), lambda bi, h, rb, ss, pt: (bi, layer, 0, h)),
            pl.BlockSpec((None, None, l_ctx, HEAD_DIM), lambda bi, h, rb, ss, pt: (bi, layer, 0, h)),
            pl.BlockSpec((None, None, qn, sn), lambda bi, h, rb, ss, pt: (pt[rb], h, 0, 0)),
        ],
        out_specs=pl.BlockSpec((None, qn, HEAD_DIM), lambda bi, h, rb, ss, pt: (bi, rb, h)),
    )
    return pl.pallas_call(
        functools.partial(_na_kernel, slab_len=sn),
        grid_spec=grid_spec,
        out_shape=jax.ShapeDtypeStruct((b, t, NA_W), BF16),
        compiler_params=_cparams(("arbitrary", "arbitrary", "arbitrary")),
        name="neighborhood_attn",
    )(jnp.asarray(slab_start), jnp.asarray(pat_id), proj, proj, proj, ck, cv, bias_tab)


def _layer_norm(y, g, b):
    mu = jnp.mean(y, axis=-1, keepdims=True)
    yc = y - mu
    var = jnp.mean(yc * yc, axis=-1, keepdims=True)
    return yc * lax.rsqrt(var + EPS) * g + b


def _top2_of4(vals):
    m1 = jnp.maximum(jnp.maximum(vals[0], vals[1]), jnp.maximum(vals[2], vals[3]))
    i1 = jnp.where(vals[0] == m1, 0, jnp.where(vals[1] == m1, 1, jnp.where(vals[2] == m1, 2, 3)))
    rest = [jnp.where(i1 == i, -1.0, vals[i]) for i in range(4)]
    m2 = jnp.maximum(jnp.maximum(rest[0], rest[1]), jnp.maximum(rest[2], rest[3]))
    i2 = jnp.where(rest[0] == m2, 0, jnp.where(rest[1] == m2, 1, jnp.where(rest[2] == m2, 2, 3)))
    return m1, i1, m2, i2


def _route_rows(logits_t):
    m = logits_t.max(axis=0, keepdims=True)
    e = jnp.exp(logits_t - m)
    probs = e / e.sum(axis=0, keepdims=True)
    rows = [probs[i:i + 1, :] for i in range(N_EXPERTS)]
    groups = [rows[g * EXPERTS_PER_GROUP:(g + 1) * EXPERTS_PER_GROUP] for g in range(N_GROUPS)]
    scores = []
    for g in range(N_GROUPS):
        m1, _, m2, _ = _top2_of4(groups[g])
        scores.append(m1 + m2)
    best = jnp.maximum(jnp.maximum(scores[0], scores[1]), jnp.maximum(scores[2], scores[3]))
    gi = jnp.where(scores[0] == best, 0, jnp.where(scores[1] == best, 1, jnp.where(scores[2] == best, 2, 3)))
    sel = [jnp.where(gi == 0, groups[0][i], jnp.where(gi == 1, groups[1][i],
                     jnp.where(gi == 2, groups[2][i], groups[3][i]))) for i in range(EXPERTS_PER_GROUP)]
    w1, l1, w2, l2 = _top2_of4(sel)
    wsum = w1 + w2
    return (gi * EXPERTS_PER_GROUP + l1, gi * EXPERTS_PER_GROUP + l2, w1 / wsum, w2 / wsum)


def _out_proj_kernel(oa_ref, on_ref, x_ref, mod_ref, wo_ref, g_ref, b_ref, wr_ref, br_ref,
                     x1_ref, h2_ref, e_ref, wt_ref):
    attn = jnp.dot(oa_ref[...], wo_ref[0:GA_Q, :], preferred_element_type=F32)
    attn = attn + jnp.dot(on_ref[...], wo_ref[GA_Q:GA_Q + NA_W, :], preferred_element_type=F32)
    gate1 = mod_ref[2:3, :]
    x1 = _layer_norm(DN_ALPHA * x_ref[...] + gate1 * attn, g_ref[...], b_ref[...])
    x1_ref[...] = x1
    h2 = x1 * (1.0 + mod_ref[4:5, :]) + mod_ref[3:4, :]
    h2_ref[...] = h2
    h_hi = h2.astype(BF16)
    h_lo = (h2 - h_hi.astype(F32)).astype(BF16)
    r_hi = jnp.dot(h_hi, wr_ref[...], preferred_element_type=F32)
    r_lo = jnp.dot(h_lo, wr_ref[...], preferred_element_type=F32)
    logits = r_hi + pltpu.roll(r_hi, LANE - N_EXPERTS, 1) + r_lo
    logits_t = logits.T[0:N_EXPERTS, :] + br_ref[...]
    e1, e2, w1, w2 = _route_rows(logits_t)
    e_ref[0:1, :] = e1
    e_ref[1:2, :] = e2
    n = w1.shape[1]
    row = lax.broadcasted_iota(jnp.int32, (LANE, n), 0)
    w_rows = jnp.where(row == 0, w1, jnp.where(row == 1, w2, 0.0))
    wt_ref[...] = w_rows.T


def _out_proj(oa, on, x, mod_l, mod_row0, w_o_bf, ln_g, ln_b, wr_cat, b_router):
    b, t, d = x.shape
    tm = min(OUT_TM, t)
    tok = lambda bi, ti: (bi, ti, 0)
    const2 = lambda bi, ti: (0, 0)
    return pl.pallas_call(
        _out_proj_kernel,
        grid=(b, t // tm),
        in_specs=[
            pl.BlockSpec((None, tm, GA_Q), tok),
            pl.BlockSpec((None, tm, NA_W), tok),
            pl.BlockSpec((None, tm, d), tok),
            pl.BlockSpec((None, N_MOD, d), lambda bi, ti: (bi + mod_row0, 0, 0)),
            pl.BlockSpec((GA_Q + NA_W, d), const2),
            pl.BlockSpec((1, d), const2),
            pl.BlockSpec((1, d), const2),
            pl.BlockSpec((d, LANE), const2),
            pl.BlockSpec((N_EXPERTS, 1), const2),
        ],
        out_specs=[
            pl.BlockSpec((None, tm, d), tok),
            pl.BlockSpec((None, tm, d), tok),
            pl.BlockSpec((None, TOP_K, tm), lambda bi, ti: (bi, 0, ti)),
            pl.BlockSpec((None, tm, LANE), tok),
        ],
        out_shape=[
            jax.ShapeDtypeStruct((b, t, d), F32),
            jax.ShapeDtypeStruct((b, t, d), F32),
            jax.ShapeDtypeStruct((b, TOP_K, t), jnp.int32),
            jax.ShapeDtypeStruct((b, t, LANE), F32),
        ],
        compiler_params=_cparams(("arbitrary", "arbitrary")),
        name="out_proj_ln_router",
    )(oa, on, x, mod_l, w_o_bf, ln_g.reshape(1, d), ln_b.reshape(1, d), wr_cat,
      b_router.reshape(N_EXPERTS, 1).astype(F32))


def _start_row_gather(src_hbm, idx_ref, dst, sem, n_rows):
    def body(r, carry):
        pltpu.make_async_copy(src_hbm.at[pl.ds(idx_ref[0, r], 1)], dst.at[pl.ds(r, 1)], sem).start()
        return carry
    lax.fori_loop(0, n_rows, body, 0, unroll=8)


def _wait_row_gather(src_hbm, dst, sem, n_rows):
    pltpu.make_async_copy(src_hbm.at[pl.ds(0, n_rows)], dst, sem).wait()


def _moe_kernel(be_ref, nu_ref, tok_ref, tok_next_ref, h_hbm, wg_ref, wu_ref, wd_ref, o_ref, buf, sem):
    del be_ref
    i = pl.program_id(0)
    n_used = nu_ref[0]
    slot = i % 2

    @pl.when(i == 0)
    def _():
        _start_row_gather(h_hbm, tok_ref, buf.at[0], sem.at[0], MOE_BLK)

    @pl.when(i + 1 < n_used)
    def _():
        _start_row_gather(h_hbm, tok_next_ref, buf.at[1 - slot], sem.at[1 - slot], MOE_BLK)

    @pl.when(i < n_used)
    def _():
        _wait_row_gather(h_hbm, buf.at[slot], sem.at[slot], MOE_BLK)
        xb = buf[slot].astype(BF16)
        gate = jnp.dot(xb, wg_ref[...], preferred_element_type=F32)
        up = jnp.dot(xb, wu_ref[...], preferred_element_type=F32)
        act = (gate / (1.0 + jnp.exp(-gate)) * up).astype(BF16)
        o_ref[...] = jnp.dot(act, wd_ref[...], preferred_element_type=F32)

    @pl.when(i >= n_used)
    def _():
        o_ref[...] = jnp.zeros_like(o_ref)


def _moe_ffn(h2_flat, tok_buf, blk_e, n_used, wg, wu, wd):
    n, d = h2_flat.shape
    n_blocks = blk_e.shape[0]
    d_ff = wg.shape[2]
    tok3 = tok_buf.reshape(n_blocks, 1, MOE_BLK)
    smem_blk = lambda f: pl.BlockSpec((None, 1, MOE_BLK), f, memory_space=pltpu.SMEM)
    grid_spec = pltpu.PrefetchScalarGridSpec(
        num_scalar_prefetch=2,
        grid=(n_blocks,),
        in_specs=[
            smem_blk(lambda i, be, nu: (i, 0, 0)),
            smem_blk(lambda i, be, nu: (jnp.minimum(i + 1, n_blocks - 1), 0, 0)),
            pl.BlockSpec(memory_space=pl.ANY),
            pl.BlockSpec((None, d, d_ff), lambda i, be, nu: (be[i], 0, 0)),
            pl.BlockSpec((None, d, d_ff), lambda i, be, nu: (be[i], 0, 0)),
            pl.BlockSpec((None, d_ff, d), lambda i, be, nu: (be[i], 0, 0)),
        ],
        out_specs=pl.BlockSpec((MOE_BLK, d), lambda i, be, nu: (i, 0)),
        scratch_shapes=[pltpu.VMEM((2, MOE_BLK, d), F32), pltpu.SemaphoreType.DMA((2,))],
    )
    return pl.pallas_call(
        _moe_kernel,
        grid_spec=grid_spec,
        out_shape=jax.ShapeDtypeStruct((n_blocks * MOE_BLK, d), F32),
        compiler_params=_cparams(("arbitrary",)),
        name="moe_ffn",
    )(blk_e, n_used, tok3, tok3, h2_flat, wg, wu, wd)


def _combine_kernel(d0_ref, d1_ref, d0n_ref, d1n_ref, y_hbm, x1_ref, mod_ref, wt_ref, g_ref, b_ref,
                    o_ref, buf, sem, *, n_steps, tm):
    i = pl.program_id(0)
    slot = i % 2

    def start(slot_, r0, r1):
        _start_row_gather(y_hbm, r0, buf.at[slot_, 0], sem.at[slot_], tm)
        _start_row_gather(y_hbm, r1, buf.at[slot_, 1], sem.at[slot_], tm)

    @pl.when(i == 0)
    def _():
        start(0, d0_ref, d1_ref)

    @pl.when(i + 1 < n_steps)
    def _():
        start(1 - slot, d0n_ref, d1n_ref)

    _wait_row_gather(y_hbm, buf.at[slot, 0], sem.at[slot], tm)
    _wait_row_gather(y_hbm, buf.at[slot, 1], sem.at[slot], tm)
    y = wt_ref[:, 0:1] * buf[slot, 0] + wt_ref[:, 1:2] * buf[slot, 1]
    gate2 = mod_ref[5:6, :]
    o_ref[...] = _layer_norm(DN_ALPHA * x1_ref[...] + gate2 * y, g_ref[...], b_ref[...])


def _combine(y_sorted, dest, x1, mod_l, mod_row0, wt, ln_g, ln_b):
    b, t, d = x1.shape
    tm = min(OUT_TM, t)
    nt = t // tm
    n_steps = b * nt
    dest4 = dest.reshape(b, TOP_K, nt, tm).transpose(0, 2, 1, 3).reshape(n_steps, TOP_K, 1, tm)
    nxt = lambda i: jnp.minimum(i + 1, n_steps - 1)
    smem_blk = lambda f: pl.BlockSpec((None, None, 1, tm), f, memory_space=pltpu.SMEM)
    tok = lambda i: (i // nt, i % nt, 0)
    return pl.pallas_call(
        functools.partial(_combine_kernel, n_steps=n_steps, tm=tm),
        grid=(n_steps,),
        in_specs=[
            smem_blk(lambda i: (i, 0, 0, 0)),
            smem_blk(lambda i: (i, 1, 0, 0)),
            smem_blk(lambda i: (nxt(i), 0, 0, 0)),
            smem_blk(lambda i: (nxt(i), 1, 0, 0)),
            pl.BlockSpec(memory_space=pl.ANY),
            pl.BlockSpec((None, tm, d), tok),
            pl.BlockSpec((None, N_MOD, d), lambda i: (i // nt + mod_row0, 0, 0)),
            pl.BlockSpec((None, tm, LANE), tok),
            pl.BlockSpec((1, d), lambda i: (0, 0)),
            pl.BlockSpec((1, d), lambda i: (0, 0)),
        ],
        out_specs=pl.BlockSpec((None, tm, d), tok),
        out_shape=jax.ShapeDtypeStruct((b, t, d), F32),
        scratch_shapes=[pltpu.VMEM((2, TOP_K, tm, d), F32), pltpu.SemaphoreType.DMA((2,))],
        compiler_params=_cparams(("arbitrary",)),
        name="moe_combine_ln",
    )(dest4, dest4, dest4, dest4, y_sorted, x1, mod_l, wt, ln_g.reshape(1, d), ln_b.reshape(1, d))


def _dispatch_plan(e_idx):
    b, k, t = e_idx.shape
    n = b * t
    a = n * k
    flat_e = e_idx.reshape(a)
    src_row = (jnp.arange(b, dtype=jnp.int32)[:, None, None] * t
               + jnp.zeros((1, k, 1), jnp.int32)
               + jnp.arange(t, dtype=jnp.int32)[None, None, :]).reshape(a)
    onehot = (flat_e[:, None] == jnp.arange(N_EXPERTS, dtype=jnp.int32)[None, :]).astype(jnp.int32)
    counts = onehot.sum(axis=0)
    rank = (jnp.cumsum(onehot, axis=0) * onehot).sum(axis=1) - 1
    padded = (counts + MOE_BLK - 1) // MOE_BLK * MOE_BLK
    pad_end = jnp.cumsum(padded)
    pad_start = pad_end - padded
    dest = pad_start[flat_e] + rank
    n_blocks = -(-a // MOE_BLK) + N_EXPERTS
    p = n_blocks * MOE_BLK
    tok_buf = jnp.zeros((p,), jnp.int32).at[dest].set(src_row)
    blk_start = jnp.arange(n_blocks, dtype=jnp.int32) * MOE_BLK
    blk_e = jnp.minimum(jnp.searchsorted(pad_end, blk_start, side='right'), N_EXPERTS - 1).astype(jnp.int32)
    n_used = (pad_end[-1] // MOE_BLK).astype(jnp.int32)
    last_e = blk_e[jnp.maximum(n_used - 1, 0)]
    blk_e = jnp.where(jnp.arange(n_blocks) < n_used, blk_e, last_e)
    return dest.reshape(b, k, t), tok_buf, blk_e, n_used.reshape(1)


def _layer(x, mod_l, mod_row0, lw, shared, ctx):
    b, t, d = x.shape
    is_ctx = ctx is None
    if is_ctx:
        xt = x.reshape(1, b * t, d)
        proj, cache = _in_proj(xt, mod_l, mod_row0, lw['w_in'], lw['q_gain'], lw['k_gain'], None, True)
        proj = proj.reshape(b, t, -1)
        oa = _dense_attention(proj, COL_QA, COL_KA, COL_VA, N_KV_GA, N_HEADS_GA // N_KV_GA)
        on = _dense_attention(proj, COL_QN, COL_KN, COL_VN, N_HEADS_NA, 1)
        oa = oa.reshape(1, b * t, -1)
        on = on.reshape(1, b * t, -1)
    else:
        ga_k, ga_v, na_k, na_v, layer, rope_tabs, na_bias, na_plan = ctx
        (proj,) = _in_proj(x, mod_l, mod_row0, lw['w_in'], lw['q_gain'], lw['k_gain'], rope_tabs, False)
        cache = None
        oa = _dense_attention(proj, COL_QA, COL_KA, COL_VA, N_KV_GA, N_HEADS_GA // N_KV_GA,
                              ctx=(ga_k, ga_v, layer))
        on = _neighborhood_attention(proj, na_k, na_v, layer, na_bias, na_plan)
        xt = x
    x1, h2, e_idx, wt = _out_proj(oa, on, xt, mod_l, mod_row0, lw['w_o'], lw['ln1_g'], lw['ln1_b'],
                                  shared['wr_cat'], shared['b_router'])
    dest, tok_buf, blk_e, n_used = _dispatch_plan(e_idx)
    y_sorted = _moe_ffn(h2.reshape(-1, d), tok_buf, blk_e, n_used, lw['w_gate'], lw['w_up'], lw['w_down'])
    out = _combine(y_sorted, dest, x1, mod_l, mod_row0, wt, lw['ln2_g'], lw['ln2_b'])
    return out.reshape(b, t, d), cache


def kernel(x_prompt, x_sample, c, cache_ga_k, cache_ga_v, cache_na_k, cache_na_v, c_ctx, w_router, b_router, w_mod, b_mod, w_in, q_norm, k_norm, rel_bias, w_o, ln1_g, ln1_b, ln2_g, ln2_b, w_gate, w_up, w_down):
    bsz, seq, d = x_prompt.shape
    dec_b, dec_t, _ = x_sample.shape
    depth = w_mod.shape[0]
    past = cache_ga_k.shape[2]

    mod_rows = 16
    assert dec_b + 1 <= mod_rows
    cvec = jnp.concatenate([c, c_ctx[None, :], jnp.zeros((mod_rows - dec_b - 1, d), F32)], axis=0)
    mod = _modulation(cvec, w_mod, b_mod).reshape(depth, mod_rows, N_MOD, d)

    wr_hi = w_router.astype(BF16)
    wr_lo = (w_router - wr_hi.astype(F32)).astype(BF16)
    wr_cat = jnp.concatenate([wr_hi, wr_lo, jnp.zeros((d, LANE - 2 * N_EXPERTS), BF16)], axis=1)
    shared = {'wr_cat': wr_cat, 'b_router': b_router}

    rope_tabs = _rope_tables(dec_t)
    na_plan = _na_plan(dec_t // GRID_W)
    ga_k = cache_ga_k.reshape(dec_b, depth, past, GA_KV)
    ga_v = cache_ga_v.reshape(dec_b, depth, past, GA_KV)
    na_k = cache_na_k.reshape(dec_b, depth, past, NA_W)
    na_v = cache_na_v.reshape(dec_b, depth, past, NA_W)

    y_prompt, y_sample = x_prompt, x_sample
    caches = []
    for i in range(depth):
        lw = {
            'w_in': w_in[i].astype(BF16), 'w_o': w_o[i].astype(BF16),
            'q_gain': (q_norm[i] * Q_PRESCALE).reshape(1, HEAD_DIM).astype(F32),
            'k_gain': k_norm[i].reshape(1, HEAD_DIM).astype(F32),
            'ln1_g': ln1_g[i], 'ln1_b': ln1_b[i], 'ln2_g': ln2_g[i], 'ln2_b': ln2_b[i],
            'w_gate': w_gate[i].astype(BF16), 'w_up': w_up[i].astype(BF16), 'w_down': w_down[i].astype(BF16),
        }
        na_bias = _na_bias_table(rel_bias[i], na_plan[4])
        y_prompt, cache = _layer(y_prompt, mod[i], dec_b, lw, shared, None)
        caches.append(cache.reshape(bsz, seq, -1))
        y_sample, _ = _layer(y_sample, mod[i], 0, lw, shared,
                             (ga_k, ga_v, na_k, na_v, i, rope_tabs, na_bias, na_plan))

    cache = jnp.stack(caches, axis=1)
    o = 0
    outs = []
    for width, heads in ((GA_KV, N_KV_GA), (GA_KV, N_KV_GA), (NA_W, N_HEADS_NA), (NA_W, N_HEADS_NA)):
        outs.append(cache[..., o:o + width].reshape(bsz, depth, seq, heads, HEAD_DIM))
        o += width
    return (y_prompt, y_sample, outs[0], outs[1], outs[2], outs[3])
```

```python
import functools
import math

import numpy as np
import jax
import jax.numpy as jnp
from jax import lax
from jax.experimental import pallas as pl
from jax.experimental.pallas import tpu as pltpu

F32 = jnp.float32
BF16 = jnp.bfloat16

DEPTH = 2
GRID_W = 64
HEAD_DIM = 128
N_HEADS_GA = 8
N_KV_GA = 2
N_HEADS_NA = 8
NA_WIN_H = 8
NA_WIN_W = 16
ROPE_THETA = 10000.0
N_EXPERTS = 16
N_GROUPS = 4
EXPERTS_PER_GROUP = N_EXPERTS // N_GROUPS
TOP_K = 2
N_MOD = 6
DN_ALPHA = (2.0 * DEPTH) ** 0.25
EPS = 1e-6
ATTN_SCALE = HEAD_DIM ** -0.5
LOG2E = math.log2(math.e)
Q_PRESCALE = ATTN_SCALE * LOG2E
MASK_VALUE = -1e30

GA_Q = N_HEADS_GA * HEAD_DIM
GA_KV = N_KV_GA * HEAD_DIM
NA_W = N_HEADS_NA * HEAD_DIM
COL_QA = 0
COL_KA = COL_QA + N_HEADS_GA
COL_VA = COL_KA + N_KV_GA
COL_QN = COL_VA + N_KV_GA
COL_KN = COL_QN + N_HEADS_NA
COL_VN = COL_KN + N_HEADS_NA
IN_HEADS = COL_VN + N_HEADS_NA

LANE = 128
PROJ_TN = 4 * HEAD_DIM
PROJ_TM = 1024
ATTN_TQ = 256
NA_ROWS = 4
OUT_TM = 256
MOE_BLK = 256
MOD_TN = 1024
VMEM_LIMIT = 52 * 1024 * 1024


def _cparams(sem):
    return pltpu.CompilerParams(dimension_semantics=sem, vmem_limit_bytes=VMEM_LIMIT)


def _mod_kernel(c_ref, w_ref, b_ref, o_ref):
    c = c_ref[...]
    s = c / (1.0 + jnp.exp(-c))
    o_ref[...] = jnp.dot(s.astype(BF16), w_ref[...].astype(BF16),
                         preferred_element_type=F32) + b_ref[...]


def _modulation(cvec, w_mod, b_mod):
    depth, d, e = w_mod.shape
    rows = cvec.shape[0]
    tn = MOD_TN if e % MOD_TN == 0 else e
    return pl.pallas_call(
        _mod_kernel,
        grid=(depth, e // tn),
        in_specs=[
            pl.BlockSpec((rows, d), lambda l, j: (0, 0)),
            pl.BlockSpec((None, d, tn), lambda l, j: (l, 0, j)),
            pl.BlockSpec((None, 1, tn), lambda l, j: (l, 0, j)),
        ],
        out_specs=pl.BlockSpec((None, rows, tn), lambda l, j: (l, 0, j)),
        out_shape=jax.ShapeDtypeStruct((depth, rows, e), F32),
        compiler_params=_cparams(("arbitrary", "arbitrary")),
        name="modulation",
    )(cvec, w_mod, b_mod.reshape(depth, 1, e))


def _rms_head(x, g):
    ms = jnp.mean(x * x, axis=-1, keepdims=True)
    return x * lax.rsqrt(ms + EPS) * g


def _rope_head(x, cos, sin, first_half):
    swapped = jnp.where(first_half, pltpu.roll(x, HEAD_DIM - 32, 1), pltpu.roll(x, 32, 1))
    return x * cos + swapped * sin


def _proj_kernel(*refs, rope, cache):
    x_ref, mod_ref, w_ref, qg_ref, kg_ref = refs[:5]
    pos = 5
    if rope:
        cos_ref, sin_ref = refs[pos:pos + 2]
        pos += 2
    if cache is not None:
        pos += cache[0]
    o_ref = refs[pos]
    pos += 1
    if cache is not None:
        ka_ref, va_ref, kn_ref, vn_ref = refs[pos:pos + 4]
        pos += 4
    h_scr = refs[pos]

    j = pl.program_id(2)

    @pl.when(j == 0)
    def _():
        shift = mod_ref[0:1, :]
        scale = mod_ref[1:2, :]
        h_scr[...] = (x_ref[...] * (1.0 + scale) + shift).astype(BF16)

    acc = jnp.dot(h_scr[...], w_ref[...], preferred_element_type=F32)

    if rope:
        lane = lax.broadcasted_iota(jnp.int32, (1, HEAD_DIM), 1)
        first_half = (lane % 64) < 32

    def head(hh):
        return acc[:, hh * HEAD_DIM:(hh + 1) * HEAD_DIM]

    def normed(hh, g):
        y = _rms_head(head(hh), g)
        if rope:
            y = _rope_head(y, cos_ref[...], sin_ref[...], first_half)
        return y

    def put(hh, y, c_ref=None, c_head=0):
        o_ref[:, hh * HEAD_DIM:(hh + 1) * HEAD_DIM] = y.astype(BF16)
        if c_ref is not None:
            c_ref[:, :, c_head * HEAD_DIM:(c_head + 1) * HEAD_DIM] = y.reshape(cache[1], cache[2], HEAD_DIM)

    hpt = PROJ_TN // HEAD_DIM
    j_ka, j_qn, j_kn, j_vn = COL_KA // hpt, COL_QN // hpt, COL_KN // hpt, COL_VN // hpt

    @pl.when(j < j_ka)
    def _():
        for hh in range(hpt):
            put(hh, normed(hh, qg_ref[...]))

    @pl.when(j == j_ka)
    def _():
        for hh in range(N_KV_GA):
            put(hh, normed(hh, kg_ref[...]), ka_ref if cache else None, hh)
        for hh in range(N_KV_GA):
            put(N_KV_GA + hh, head(N_KV_GA + hh), va_ref if cache else None, hh)

    @pl.when((j >= j_qn) & (j < j_kn))
    def _():
        o_ref[...] = (acc * Q_PRESCALE).astype(BF16)

    @pl.when((j >= j_kn) & (j < j_vn))
    def _():
        o_ref[...] = acc.astype(BF16)
        if cache is not None:
            kn_ref[...] = acc.reshape(cache[1], cache[2], PROJ_TN)

    @pl.when(j >= j_vn)
    def _():
        o_ref[...] = acc.astype(BF16)
        if cache is not None:
            vn_ref[...] = acc.reshape(cache[1], cache[2], PROJ_TN)


def _in_proj(x, mod_l, mod_row0, w_in_bf, q_gain, k_gain, rope_tabs=None, cache=None):
    b, t, d = x.shape
    in_w = w_in_bf.shape[1]
    tm = min(PROJ_TM, t)
    nj = in_w // PROJ_TN
    rope = rope_tabs is not None
    hpt = PROJ_TN // HEAD_DIM
    j_kn, j_vn = COL_KN // hpt, COL_VN // hpt
    assert COL_KA % hpt == 0 and COL_QN % hpt == 0 and COL_KN % hpt == 0 and COL_VN % hpt == 0
    assert 2 * N_KV_GA == hpt and COL_VA == COL_KA + N_KV_GA

    in_specs = [
        pl.BlockSpec((None, tm, d), lambda bi, ti, j: (bi, ti, 0)),
        pl.BlockSpec((None, N_MOD, d), lambda bi, ti, j: (bi + mod_row0, 0, 0)),
        pl.BlockSpec((d, PROJ_TN), lambda bi, ti, j: (0, j)),
        pl.BlockSpec((1, HEAD_DIM), lambda bi, ti, j: (0, 0)),
        pl.BlockSpec((1, HEAD_DIM), lambda bi, ti, j: (0, 0)),
    ]
    args = [x, mod_l, w_in_bf, q_gain, k_gain]
    if rope:
        in_specs += [pl.BlockSpec((tm, HEAD_DIM), lambda bi, ti, j: (ti, 0))] * 2
        args += list(rope_tabs)
    out_specs = [pl.BlockSpec((None, tm, PROJ_TN), lambda bi, ti, j: (bi, ti, j))]
    out_shape = [jax.ShapeDtypeStruct((b, t, in_w), BF16)]
    aliases = {}
    cache_cfg = None
    if cache is not None:
        layer, depth, bsz, seq, prev = cache
        assert b == 1 and tm % seq == 0
        nb = tm // seq
        n_prev = 0 if prev is None else len(prev)
        cache_cfg = (n_prev, nb, seq)
        if prev is not None:
            for k, arr in enumerate(prev):
                aliases[len(args)] = 1 + k
                in_specs.append(pl.BlockSpec(memory_space=pl.ANY))
                args.append(arr)
        kv = lambda bi, ti, j: (ti, layer, 0, 0)
        kn = lambda bi, ti, j: (ti, layer, 0, jnp.clip(j - j_kn, 0, N_HEADS_NA // hpt - 1))
        vn = lambda bi, ti, j: (ti, layer, 0, jnp.clip(j - j_vn, 0, N_HEADS_NA // hpt - 1))
        out_specs += [pl.BlockSpec((nb, None, seq, GA_KV), kv), pl.BlockSpec((nb, None, seq, GA_KV), kv),
                      pl.BlockSpec((nb, None, seq, PROJ_TN), kn), pl.BlockSpec((nb, None, seq, PROJ_TN), vn)]
        out_shape += [jax.ShapeDtypeStruct((bsz, depth, seq, w), F32) for w in (GA_KV, GA_KV, NA_W, NA_W)]
    outs = pl.pallas_call(
        functools.partial(_proj_kernel, rope=rope, cache=cache_cfg),
        grid=(b, t // tm, nj),
        in_specs=in_specs,
        out_specs=out_specs,
        out_shape=out_shape,
        input_output_aliases=aliases,
        scratch_shapes=[pltpu.VMEM((tm, d), BF16)],
        compiler_params=_cparams(("arbitrary", "arbitrary", "arbitrary")),
        name="in_proj_rope" if rope else "in_proj_cache",
    )(*args)
    return outs


def _rope_tables(t):
    half = HEAD_DIM // 4
    tt = jnp.arange(t, dtype=jnp.int32)
    row = (tt // GRID_W).astype(F32)
    col = (tt % GRID_W).astype(F32)
    inv_freq = 1.0 / (ROPE_THETA ** (jnp.arange(half, dtype=F32) / half))
    ar = row[:, None] * inv_freq[None, :]
    ac = col[:, None] * inv_freq[None, :]
    cos = jnp.concatenate([jnp.cos(ar), jnp.cos(ar), jnp.cos(ac), jnp.cos(ac)], axis=-1)
    sin = jnp.concatenate([-jnp.sin(ar), jnp.sin(ar), -jnp.sin(ac), jnp.sin(ac)], axis=-1)
    return cos, sin


_NT = (((1,), (1,)), ((), ()))


def _softmax_pv(scores, values):
    m = scores[0].max(axis=-1, keepdims=True)
    for s in scores[1:]:
        m = jnp.maximum(m, s.max(axis=-1, keepdims=True))
    l = None
    acc = None
    for s, v in zip(scores, values):
        p = jnp.exp2(s - m)
        ps = p.sum(axis=-1, keepdims=True)
        pv = jnp.dot(p.astype(BF16), v, preferred_element_type=F32)
        l = ps if l is None else l + ps
        acc = pv if acc is None else acc + pv
    return acc * (1.0 / l)


def _dense_attn_kernel(*refs, r_heads, has_ctx):
    if has_ctx:
        q_ref, k_ref, v_ref, kc_ref, vc_ref, o_ref = refs
        kc = kc_ref[...].astype(BF16)
        vc = vc_ref[...].astype(BF16)
    else:
        q_ref, k_ref, v_ref, o_ref = refs
    k = k_ref[...]
    v = v_ref[...]
    for r in range(r_heads):
        sl = slice(r * HEAD_DIM, (r + 1) * HEAD_DIM)
        q = q_ref[:, sl]
        scores = [lax.dot_general(q, k, _NT, preferred_element_type=F32)]
        values = [v]
        if has_ctx:
            scores.append(lax.dot_general(q, kc, _NT, preferred_element_type=F32))
            values.append(vc)
        o_ref[:, sl] = _softmax_pv(scores, values).astype(BF16)


def _dense_attention(proj, q_col, k_col, v_col, groups, r_heads, ctx=None):
    b, t, _ = proj.shape
    tq = min(ATTN_TQ, t)
    qw = r_heads * HEAD_DIM
    assert q_col % r_heads == 0
    in_specs = [
        pl.BlockSpec((None, tq, qw), lambda bi, g, qi: (bi, qi, q_col // r_heads + g)),
        pl.BlockSpec((None, t, HEAD_DIM), lambda bi, g, qi: (bi, 0, k_col + g)),
        pl.BlockSpec((None, t, HEAD_DIM), lambda bi, g, qi: (bi, 0, v_col + g)),
    ]
    args = [proj, proj, proj]
    if ctx is not None:
        ck, cv, layer = ctx
        l_ctx = ck.shape[2]
        spec = pl.BlockSpec((None, None, l_ctx, HEAD_DIM), lambda bi, g, qi: (bi, layer, 0, g))
        in_specs += [spec, spec]
        args += [ck, cv]
    return pl.pallas_call(
        functools.partial(_dense_attn_kernel, r_heads=r_heads, has_ctx=ctx is not None),
        grid=(b, groups, t // tq),
        in_specs=in_specs,
        out_specs=pl.BlockSpec((None, tq, qw), lambda bi, g, qi: (bi, qi, g)),
        out_shape=jax.ShapeDtypeStruct((b, t, groups * qw), BF16),
        compiler_params=_cparams(("arbitrary", "arbitrary", "arbitrary")),
        name="dense_attn_ctx" if ctx is not None else "dense_attn",
    )(*args)


def _na_plan(rows):
    kh = min(NA_WIN_H, rows)
    kw = NA_WIN_W
    r_blk = min(NA_ROWS, rows)
    assert rows % r_blk == 0
    slab = min(r_blk - 1 + kh, rows)
    row_start = np.clip(np.arange(rows) - kh // 2, 0, rows - kh)
    col = np.arange(GRID_W)
    col_start = np.clip(col - kw // 2, 0, GRID_W - kw)
    slab_start, pat_id, pats, sigs = [], [], [], {}
    for r0 in range(0, rows, r_blk):
        ss = min(row_start[r0], rows - slab)
        rel = tuple(int(row_start[r0 + ri] - ss) for ri in range(r_blk))
        sig = (int(ss - r0), rel)
        if sig not in sigs:
            sigs[sig] = len(pats)
            q_row = r0 + np.arange(r_blk)[:, None, None, None]
            q_col = col[None, :, None, None]
            k_row = ss + np.arange(slab)[None, None, :, None]
            k_col = col[None, None, None, :]
            rs = row_start[r0:r0 + r_blk][:, None, None, None]
            cs = col_start[None, :, None, None]
            valid = (k_row >= rs) & (k_row < rs + kh) & (k_col >= cs) & (k_col < cs + kw)
            row_off = (k_row - q_row + (NA_WIN_H - 1))[:, 0, :, 0]
            row_sel = (row_off[:, :, None] == np.arange(2 * NA_WIN_H - 1)).astype(np.float32)
            full = (r_blk, GRID_W, slab, GRID_W)
            pats.append((np.broadcast_to(valid, full).reshape(r_blk * GRID_W, slab * GRID_W), row_sel))
        slab_start.append(int(ss))
        pat_id.append(sigs[sig])
    return r_blk, slab, np.array(slab_start, np.int32), np.array(pat_id, np.int32), pats


def _na_bias_table(rel_bias_l, pats):
    col = np.arange(GRID_W)
    col_off = col[None, :] - col[:, None] + (NA_WIN_W - 1)
    col_sel = (col_off[:, :, None] == np.arange(2 * NA_WIN_W - 1)).astype(np.float32)
    hi = lax.Precision.HIGHEST
    tabs = []
    for valid, row_sel in pats:
        by_row = jnp.einsum('hrc,isr->hisc', rel_bias_l.astype(F32), row_sel, precision=hi)
        bias = jnp.einsum('hisc,qkc->hiqsk', by_row, col_sel, precision=hi)
        bias = bias.reshape((bias.shape[0],) + valid.shape) * LOG2E
        tabs.append(jnp.where(valid[None], bias, MASK_VALUE))
    return jnp.stack(tabs, axis=0)


def _na_kernel(ss_ref, pat_ref, q_ref, k_ref, v_ref, kc_ref, vc_ref, bias_ref, o_ref, *, slab_len):
    del pat_ref
    rb = pl.program_id(2)
    start = pl.multiple_of(ss_ref[rb] * GRID_W, GRID_W)
    ks = k_ref[pl.ds(start, slab_len), :]
    vs = v_ref[pl.ds(start, slab_len), :]
    kc = kc_ref[...].astype(BF16)
    vc = vc_ref[...].astype(BF16)
    q = q_ref[...]
    s_loc = lax.dot_general(q, ks, _NT, preferred_element_type=F32) + bias_ref[...]
    s_ctx = lax.dot_general(q, kc, _NT, preferred_element_type=F32)
    o_ref[...] = _softmax_pv([s_loc, s_ctx], [vs, vc]).astype(BF16)


def _neighborhood_attention(proj, ck, cv, layer, bias_tab, plan):
    b, t, _ = proj.shape
    r_blk, slab, slab_start, pat_id, _ = plan
    qn = r_blk * GRID_W
    sn = slab * GRID_W
    l_ctx = ck.shape[2]
    grid_spec = pltpu.PrefetchScalarGridSpec(
        num_scalar_prefetch=2,
        grid=(b, N_HEADS_NA, t // qn),
        in_specs=[
            pl.BlockSpec((None, qn, HEAD_DIM), lambda bi, h, rb, ss, pt: (bi, rb, COL_QN + h)),
            pl.BlockSpec((None, t, HEAD_DIM), lambda bi, h, rb, ss, pt: (bi, 0, COL_KN + h)),
            pl.BlockSpec((None, t, HEAD_DIM), lambda bi, h, rb, ss, pt: (bi, 0, COL_VN + h)),
            pl.BlockSpec((None, None, l_ctx, HEAD_DIM), lambda bi, h, rb, ss, pt: (bi, layer, 0, h)),
            pl.BlockSpec((None, None, l_ctx, HEAD_DIM), lambda bi, h, rb, ss, pt: (bi, layer, 0, h)),
            pl.BlockSpec((None, None, qn, sn), lambda bi, h, rb, ss, pt: (pt[rb], h, 0, 0)),
        ],
        out_specs=pl.BlockSpec((None, qn, HEAD_DIM), lambda bi, h, rb, ss, pt: (bi, rb, h)),
    )
    return pl.pallas_call(
        functools.partial(_na_kernel, slab_len=sn),
        grid_spec=grid_spec,
        out_shape=jax.ShapeDtypeStruct((b, t, NA_W), BF16),
        compiler_params=_cparams(("arbitrary", "arbitrary", "arbitrary")),
        name="neighborhood_attn",
    )(jnp.asarray(slab_start), jnp.asarray(pat_id), proj, proj, proj, ck, cv, bias_tab)


def _layer_norm(y, g, b):
    mu = jnp.mean(y, axis=-1, keepdims=True)
    yc = y - mu
    var = jnp.mean(yc * yc, axis=-1, keepdims=True)
    return yc * lax.rsqrt(var + EPS) * g + b


def _top2_of4(vals):
    m1 = jnp.maximum(jnp.maximum(vals[0], vals[1]), jnp.maximum(vals[2], vals[3]))
    i1 = jnp.where(vals[0] == m1, 0, jnp.where(vals[1] == m1, 1, jnp.where(vals[2] == m1, 2, 3)))
    rest = [jnp.where(i1 == i, -1.0, vals[i]) for i in range(4)]
    m2 = jnp.maximum(jnp.maximum(rest[0], rest[1]), jnp.maximum(rest[2], rest[3]))
    i2 = jnp.where(rest[0] == m2, 0, jnp.where(rest[1] == m2, 1, jnp.where(rest[2] == m2, 2, 3)))
    return m1, i1, m2, i2


def _route_rows(logits_t):
    m = logits_t.max(axis=0, keepdims=True)
    e = jnp.exp(logits_t - m)
    probs = e / e.sum(axis=0, keepdims=True)
    rows = [probs[i:i + 1, :] for i in range(N_EXPERTS)]
    groups = [rows[g * EXPERTS_PER_GROUP:(g + 1) * EXPERTS_PER_GROUP] for g in range(N_GROUPS)]
    scores = []
    for g in range(N_GROUPS):
        m1, _, m2, _ = _top2_of4(groups[g])
        scores.append(m1 + m2)
    best = jnp.maximum(jnp.maximum(scores[0], scores[1]), jnp.maximum(scores[2], scores[3]))
    gi = jnp.where(scores[0] == best, 0, jnp.where(scores[1] == best, 1, jnp.where(scores[2] == best, 2, 3)))
    sel = [jnp.where(gi == 0, groups[0][i], jnp.where(gi == 1, groups[1][i],
                     jnp.where(gi == 2, groups[2][i], groups[3][i]))) for i in range(EXPERTS_PER_GROUP)]
    w1, l1, w2, l2 = _top2_of4(sel)
    wsum = w1 + w2
    return (gi * EXPERTS_PER_GROUP + l1, gi * EXPERTS_PER_GROUP + l2, w1 / wsum, w2 / wsum)


def _out_proj_kernel(oa_ref, on_ref, x_ref, mod_ref, wo_ref, g_ref, b_ref, wr_ref, br_ref,
                     x1_ref, h2_ref, e_ref, wt_ref):
    attn = jnp.dot(oa_ref[...], wo_ref[0:GA_Q, :], preferred_element_type=F32)
    attn = attn + jnp.dot(on_ref[...], wo_ref[GA_Q:GA_Q + NA_W, :], preferred_element_type=F32)
    gate1 = mod_ref[2:3, :]
    x1 = _layer_norm(DN_ALPHA * x_ref[...] + gate1 * attn, g_ref[...], b_ref[...])
    x1_ref[...] = x1
    h2 = x1 * (1.0 + mod_ref[4:5, :]) + mod_ref[3:4, :]
    h2_ref[...] = h2
    h_hi = h2.astype(BF16)
    h_lo = (h2 - h_hi.astype(F32)).astype(BF16)
    r_hi = jnp.dot(h_hi, wr_ref[...], preferred_element_type=F32)
    r_lo = jnp.dot(h_lo, wr_ref[...], preferred_element_type=F32)
    logits = r_hi + pltpu.roll(r_hi, LANE - N_EXPERTS, 1) + r_lo
    logits_t = logits.T[0:N_EXPERTS, :] + br_ref[...]
    e1, e2, w1, w2 = _route_rows(logits_t)
    e_ref[0:1, :] = e1
    e_ref[1:2, :] = e2
    n = w1.shape[1]
    row = lax.broadcasted_iota(jnp.int32, (LANE, n), 0)
    w_rows = jnp.where(row == 0, w1, jnp.where(row == 1, w2, 0.0))
    wt_ref[...] = w_rows.T


def _out_proj(oa, on, x, mod_l, mod_row0, w_o_bf, ln_g, ln_b, wr_cat, b_router):
    b, t, d = x.shape
    tm = min(OUT_TM, t)
    tok = lambda bi, ti: (bi, ti, 0)
    const2 = lambda bi, ti: (0, 0)
    return pl.pallas_call(
        _out_proj_kernel,
        grid=(b, t // tm),
        in_specs=[
            pl.BlockSpec((None, tm, GA_Q), tok),
            pl.BlockSpec((None, tm, NA_W), tok),
            pl.BlockSpec((None, tm, d), tok),
            pl.BlockSpec((None, N_MOD, d), lambda bi, ti: (bi + mod_row0, 0, 0)),
            pl.BlockSpec((GA_Q + NA_W, d), const2),
            pl.BlockSpec((1, d), const2),
            pl.BlockSpec((1, d), const2),
            pl.BlockSpec((d, LANE), const2),
            pl.BlockSpec((N_EXPERTS, 1), const2),
        ],
        out_specs=[
            pl.BlockSpec((None, tm, d), tok),
            pl.BlockSpec((None, tm, d), tok),
            pl.BlockSpec((None, TOP_K, tm), lambda bi, ti: (bi, 0, ti)),
            pl.BlockSpec((None, tm, LANE), tok),
        ],
        out_shape=[
            jax.ShapeDtypeStruct((b, t, d), F32),
            jax.ShapeDtypeStruct((b, t, d), F32),
            jax.ShapeDtypeStruct((b, TOP_K, t), jnp.int32),
            jax.ShapeDtypeStruct((b, t, LANE), F32),
        ],
        compiler_params=_cparams(("arbitrary", "arbitrary")),
        name="out_proj_ln_router",
    )(oa, on, x, mod_l, w_o_bf, ln_g.reshape(1, d), ln_b.reshape(1, d), wr_cat,
      b_router.reshape(N_EXPERTS, 1).astype(F32))


def _start_row_gather(src_hbm, idx_ref, dst, sem, n_rows):
    def body(r, carry):
        pltpu.make_async_copy(src_hbm.at[pl.ds(idx_ref[0, r], 1)], dst.at[pl.ds(r, 1)], sem).start()
        return carry
    lax.fori_loop(0, n_rows, body, 0, unroll=8)


def _wait_row_gather(src_hbm, dst, sem, n_rows):
    pltpu.make_async_copy(src_hbm.at[pl.ds(0, n_rows)], dst, sem).wait()


def _moe_kernel(be_ref, nu_ref, tok_ref, tok_next_ref, h_hbm, wg_ref, wu_ref, wd_ref, o_ref, buf, sem):
    del be_ref
    i = pl.program_id(0)
    n_used = nu_ref[0]
    slot = i % 2

    @pl.when(i == 0)
    def _():
        _start_row_gather(h_hbm, tok_ref, buf.at[0], sem.at[0], MOE_BLK)

    @pl.when(i + 1 < n_used)
    def _():
        _start_row_gather(h_hbm, tok_next_ref, buf.at[1 - slot], sem.at[1 - slot], MOE_BLK)

    @pl.when(i < n_used)
    def _():
        _wait_row_gather(h_hbm, buf.at[slot], sem.at[slot], MOE_BLK)
        xb = buf[slot].astype(BF16)
        gate = jnp.dot(xb, wg_ref[...], preferred_element_type=F32)
        up = jnp.dot(xb, wu_ref[...], preferred_element_type=F32)
        act = (gate / (1.0 + jnp.exp(-gate)) * up).astype(BF16)
        o_ref[...] = jnp.dot(act, wd_ref[...], preferred_element_type=F32)

    @pl.when(i >= n_used)
    def _():
        o_ref[...] = jnp.zeros_like(o_ref)


def _moe_ffn(h2_flat, tok_buf, blk_e, n_used, wg, wu, wd):
    n, d = h2_flat.shape
    n_blocks = blk_e.shape[0]
    d_ff = wg.shape[2]
    tok3 = tok_buf.reshape(n_blocks, 1, MOE_BLK)
    smem_blk = lambda f: pl.BlockSpec((None, 1, MOE_BLK), f, memory_space=pltpu.SMEM)
    grid_spec = pltpu.PrefetchScalarGridSpec(
        num_scalar_prefetch=2,
        grid=(n_blocks,),
        in_specs=[
            smem_blk(lambda i, be, nu: (i, 0, 0)),
            smem_blk(lambda i, be, nu: (jnp.minimum(i + 1, n_blocks - 1), 0, 0)),
            pl.BlockSpec(memory_space=pl.ANY),
            pl.BlockSpec((None, d, d_ff), lambda i, be, nu: (be[i], 0, 0)),
            pl.BlockSpec((None, d, d_ff), lambda i, be, nu: (be[i], 0, 0)),
            pl.BlockSpec((None, d_ff, d), lambda i, be, nu: (be[i], 0, 0)),
        ],
        out_specs=pl.BlockSpec((MOE_BLK, d), lambda i, be, nu: (i, 0)),
        scratch_shapes=[pltpu.VMEM((2, MOE_BLK, d), F32), pltpu.SemaphoreType.DMA((2,))],
    )
    return pl.pallas_call(
        _moe_kernel,
        grid_spec=grid_spec,
        out_shape=jax.ShapeDtypeStruct((n_blocks * MOE_BLK, d), F32),
        compiler_params=_cparams(("arbitrary",)),
        name="moe_ffn",
    )(blk_e, n_used, tok3, tok3, h2_flat, wg, wu, wd)


def _combine_kernel(d0_ref, d1_ref, d0n_ref, d1n_ref, y_hbm, x1_ref, mod_ref, wt_ref, g_ref, b_ref,
                    o_ref, buf, sem, *, n_steps, tm):
    i = pl.program_id(0)
    slot = i % 2

    def start(slot_, r0, r1):
        _start_row_gather(y_hbm, r0, buf.at[slot_, 0], sem.at[slot_], tm)
        _start_row_gather(y_hbm, r1, buf.at[slot_, 1], sem.at[slot_], tm)

    @pl.when(i == 0)
    def _():
        start(0, d0_ref, d1_ref)

    @pl.when(i + 1 < n_steps)
    def _():
        start(1 - slot, d0n_ref, d1n_ref)

    _wait_row_gather(y_hbm, buf.at[slot, 0], sem.at[slot], tm)
    _wait_row_gather(y_hbm, buf.at[slot, 1], sem.at[slot], tm)
    y = wt_ref[:, 0:1] * buf[slot, 0] + wt_ref[:, 1:2] * buf[slot, 1]
    gate2 = mod_ref[5:6, :]
    o_ref[...] = _layer_norm(DN_ALPHA * x1_ref[...] + gate2 * y, g_ref[...], b_ref[...])


def _combine(y_sorted, dest, x1, mod_l, mod_row0, wt, ln_g, ln_b):
    b, t, d = x1.shape
    tm = min(OUT_TM, t)
    nt = t // tm
    n_steps = b * nt
    dest4 = dest.reshape(b, TOP_K, nt, tm).transpose(0, 2, 1, 3).reshape(n_steps, TOP_K, 1, tm)
    nxt = lambda i: jnp.minimum(i + 1, n_steps - 1)
    smem_blk = lambda f: pl.BlockSpec((None, None, 1, tm), f, memory_space=pltpu.SMEM)
    tok = lambda i: (i // nt, i % nt, 0)
    return pl.pallas_call(
        functools.partial(_combine_kernel, n_steps=n_steps, tm=tm),
        grid=(n_steps,),
        in_specs=[
            smem_blk(lambda i: (i, 0, 0, 0)),
            smem_blk(lambda i: (i, 1, 0, 0)),
            smem_blk(lambda i: (nxt(i), 0, 0, 0)),
            smem_blk(lambda i: (nxt(i), 1, 0, 0)),
            pl.BlockSpec(memory_space=pl.ANY),
            pl.BlockSpec((None, tm, d), tok),
            pl.BlockSpec((None, N_MOD, d), lambda i: (i // nt + mod_row0, 0, 0)),
            pl.BlockSpec((None, tm, LANE), tok),
            pl.BlockSpec((1, d), lambda i: (0, 0)),
            pl.BlockSpec((1, d), lambda i: (0, 0)),
        ],
        out_specs=pl.BlockSpec((None, tm, d), tok),
        out_shape=jax.ShapeDtypeStruct((b, t, d), F32),
        scratch_shapes=[pltpu.VMEM((2, TOP_K, tm, d), F32), pltpu.SemaphoreType.DMA((2,))],
        compiler_params=_cparams(("arbitrary",)),
        name="moe_combine_ln",
    )(dest4, dest4, dest4, dest4, y_sorted, x1, mod_l, wt, ln_g.reshape(1, d), ln_b.reshape(1, d))


def _dispatch_plan(e_idx):
    b, k, t = e_idx.shape
    n = b * t
    a = n * k
    flat_e = e_idx.reshape(a)
    src_row = (jnp.arange(b, dtype=jnp.int32)[:, None, None] * t
               + jnp.zeros((1, k, 1), jnp.int32)
               + jnp.arange(t, dtype=jnp.int32)[None, None, :]).reshape(a)
    onehot = (flat_e[:, None] == jnp.arange(N_EXPERTS, dtype=jnp.int32)[None, :]).astype(jnp.int32)
    counts = onehot.sum(axis=0)
    rank = (jnp.cumsum(onehot, axis=0) * onehot).sum(axis=1) - 1
    padded = (counts + MOE_BLK - 1) // MOE_BLK * MOE_BLK
    pad_end = jnp.cumsum(padded)
    pad_start = pad_end - padded
    dest = pad_start[flat_e] + rank
    n_blocks = -(-a // MOE_BLK) + N_EXPERTS
    p = n_blocks * MOE_BLK
    tok_buf = jnp.zeros((p,), jnp.int32).at[dest].set(src_row)
    blk_start = jnp.arange(n_blocks, dtype=jnp.int32) * MOE_BLK
    blk_e = jnp.minimum(jnp.searchsorted(pad_end, blk_start, side='right'), N_EXPERTS - 1).astype(jnp.int32)
    n_used = (pad_end[-1] // MOE_BLK).astype(jnp.int32)
    last_e = blk_e[jnp.maximum(n_used - 1, 0)]
    blk_e = jnp.where(jnp.arange(n_blocks) < n_used, blk_e, last_e)
    return dest.reshape(b, k, t), tok_buf, blk_e, n_used.reshape(1)


def _layer(x, mod_l, mod_row0, lw, shared, ctx):
    b, t, d = x.shape
    is_ctx = ctx is None
    if is_ctx:
        xt = x.reshape(1, b * t, d)
        proj, *cache = _in_proj(xt, mod_l, mod_row0, lw['w_in'], lw['q_gain'], lw['k_gain'],
                                cache=(lw['layer'], lw['depth'], b, t, lw['prev_cache']))
        proj = proj.reshape(b, t, -1)
        oa = _dense_attention(proj, COL_QA, COL_KA, COL_VA, N_KV_GA, N_HEADS_GA // N_KV_GA)
        on = _dense_attention(proj, COL_QN, COL_KN, COL_VN, N_HEADS_NA, 1)
        oa = oa.reshape(1, b * t, -1)
        on = on.reshape(1, b * t, -1)
    else:
        ga_k, ga_v, na_k, na_v, layer, rope_tabs, na_bias, na_plan = ctx
        (proj,) = _in_proj(x, mod_l, mod_row0, lw['w_in'], lw['q_gain'], lw['k_gain'], rope_tabs=rope_tabs)
        cache = None
        oa = _dense_attention(proj, COL_QA, COL_KA, COL_VA, N_KV_GA, N_HEADS_GA // N_KV_GA,
                              ctx=(ga_k, ga_v, layer))
        on = _neighborhood_attention(proj, na_k, na_v, layer, na_bias, na_plan)
        xt = x
    x1, h2, e_idx, wt = _out_proj(oa, on, xt, mod_l, mod_row0, lw['w_o'], lw['ln1_g'], lw['ln1_b'],
                                  shared['wr_cat'], shared['b_router'])
    dest, tok_buf, blk_e, n_used = _dispatch_plan(e_idx)
    y_sorted = _moe_ffn(h2.reshape(-1, d), tok_buf, blk_e, n_used, lw['w_gate'], lw['w_up'], lw['w_down'])
    out = _combine(y_sorted, dest, x1, mod_l, mod_row0, wt, lw['ln2_g'], lw['ln2_b'])
    return out.reshape(b, t, d), cache


def kernel(x_prompt, x_sample, c, cache_ga_k, cache_ga_v, cache_na_k, cache_na_v, c_ctx, w_router, b_router, w_mod, b_mod, w_in, q_norm, k_norm, rel_bias, w_o, ln1_g, ln1_b, ln2_g, ln2_b, w_gate, w_up, w_down):
    bsz, seq, d = x_prompt.shape
    dec_b, dec_t, _ = x_sample.shape
    depth = w_mod.shape[0]
    past = cache_ga_k.shape[2]

    mod_rows = 16
    assert dec_b + 1 <= mod_rows
    cvec = jnp.concatenate([c, c_ctx[None, :], jnp.zeros((mod_rows - dec_b - 1, d), F32)], axis=0)
    mod = _modulation(cvec, w_mod, b_mod).reshape(depth, mod_rows, N_MOD, d)

    wr_hi = w_router.astype(BF16)
    wr_lo = (w_router - wr_hi.astype(F32)).astype(BF16)
    wr_cat = jnp.concatenate([wr_hi, wr_lo, jnp.zeros((d, LANE - 2 * N_EXPERTS), BF16)], axis=1)
    shared = {'wr_cat': wr_cat, 'b_router': b_router}

    rope_tabs = _rope_tables(dec_t)
    na_plan = _na_plan(dec_t // GRID_W)
    ga_k = cache_ga_k.reshape(dec_b, depth, past, GA_KV)
    ga_v = cache_ga_v.reshape(dec_b, depth, past, GA_KV)
    na_k = cache_na_k.reshape(dec_b, depth, past, NA_W)
    na_v = cache_na_v.reshape(dec_b, depth, past, NA_W)

    y_prompt, y_sample = x_prompt, x_sample
    cache = None
    for i in range(depth):
        lw = {
            'layer': i, 'depth': depth, 'prev_cache': cache,
            'w_in': w_in[i].astype(BF16), 'w_o': w_o[i].astype(BF16),
            'q_gain': (q_norm[i] * Q_PRESCALE).reshape(1, HEAD_DIM).astype(F32),
            'k_gain': k_norm[i].reshape(1, HEAD_DIM).astype(F32),
            'ln1_g': ln1_g[i], 'ln1_b': ln1_b[i], 'ln2_g': ln2_g[i], 'ln2_b': ln2_b[i],
            'w_gate': w_gate[i].astype(BF16), 'w_up': w_up[i].astype(BF16), 'w_down': w_down[i].astype(BF16),
        }
        na_bias = _na_bias_table(rel_bias[i], na_plan[4])
        y_prompt, cache = _layer(y_prompt, mod[i], dec_b, lw, shared, None)
        y_sample, _ = _layer(y_sample, mod[i], 0, lw, shared,
                             (ga_k, ga_v, na_k, na_v, i, rope_tabs, na_bias, na_plan))

    outs = [arr.reshape(bsz, depth, seq, -1, HEAD_DIM) for arr in cache]
    return (y_prompt, y_sample, outs[0], outs[1], outs[2], outs[3])
```

```python
import functools
import math

import numpy as np
import jax
import jax.numpy as jnp
from jax import lax
from jax.experimental import pallas as pl
from jax.experimental.pallas import tpu as pltpu

F32 = jnp.float32
BF16 = jnp.bfloat16

DEPTH = 2
GRID_W = 64
HEAD_DIM = 128
N_HEADS_GA = 8
N_KV_GA = 2
N_HEADS_NA = 8
NA_WIN_H = 8
NA_WIN_W = 16
ROPE_THETA = 10000.0
N_EXPERTS = 16
N_GROUPS = 4
EXPERTS_PER_GROUP = N_EXPERTS // N_GROUPS
TOP_K = 2
N_MOD = 6
DN_ALPHA = (2.0 * DEPTH) ** 0.25
EPS = 1e-6
ATTN_SCALE = HEAD_DIM ** -0.5
LOG2E = math.log2(math.e)
Q_PRESCALE = ATTN_SCALE * LOG2E
MASK_VALUE = -1e30

GA_Q = N_HEADS_GA * HEAD_DIM
GA_KV = N_KV_GA * HEAD_DIM
NA_W = N_HEADS_NA * HEAD_DIM
COL_QA = 0
COL_KA = COL_QA + N_HEADS_GA
COL_VA = COL_KA + N_KV_GA
COL_QN = COL_VA + N_KV_GA
COL_KN = COL_QN + N_HEADS_NA
COL_VN = COL_KN + N_HEADS_NA
IN_HEADS = COL_VN + N_HEADS_NA

LANE = 128
PROJ_TN = 4 * HEAD_DIM
PROJ_TM = 1024
ATTN_TQ = 256
NA_ROWS = 4
NA_HEADS_PER_STEP = 4
OUT_TM = 256
MOE_BLK = 256
MOD_TN = 1024
VMEM_LIMIT = 52 * 1024 * 1024


def _cparams(sem):
    return pltpu.CompilerParams(dimension_semantics=sem, vmem_limit_bytes=VMEM_LIMIT)


def _mod_kernel(c_ref, w_ref, b_ref, o_ref):
    c = c_ref[...]
    s = c / (1.0 + jnp.exp(-c))
    o_ref[...] = jnp.dot(s.astype(BF16), w_ref[...].astype(BF16),
                         preferred_element_type=F32) + b_ref[...]


def _modulation(cvec, w_mod, b_mod):
    depth, d, e = w_mod.shape
    rows = cvec.shape[0]
    tn = MOD_TN if e % MOD_TN == 0 else e
    return pl.pallas_call(
        _mod_kernel,
        grid=(depth, e // tn),
        in_specs=[
            pl.BlockSpec((rows, d), lambda l, j: (0, 0)),
            pl.BlockSpec((None, d, tn), lambda l, j: (l, 0, j)),
            pl.BlockSpec((None, 1, tn), lambda l, j: (l, 0, j)),
        ],
        out_specs=pl.BlockSpec((None, rows, tn), lambda l, j: (l, 0, j)),
        out_shape=jax.ShapeDtypeStruct((depth, rows, e), F32),
        compiler_params=_cparams(("arbitrary", "arbitrary")),
        name="modulation",
    )(cvec, w_mod, b_mod.reshape(depth, 1, e))


def _rms_head(x, g):
    ms = jnp.mean(x * x, axis=-1, keepdims=True)
    return x * lax.rsqrt(ms + EPS) * g


def _rope_head(x, cos, sin, first_half):
    swapped = jnp.where(first_half, pltpu.roll(x, HEAD_DIM - 32, 1), pltpu.roll(x, 32, 1))
    return x * cos + swapped * sin


def _proj_kernel(*refs, rope, cache):
    x_ref, mod_ref, w_ref, qg_ref, kg_ref = refs[:5]
    pos = 5
    if rope:
        cos_ref, sin_ref = refs[pos:pos + 2]
        pos += 2
    if cache is not None:
        pos += cache[0]
    o_ref = refs[pos]
    pos += 1
    if cache is not None:
        ka_ref, va_ref, kn_ref, vn_ref = refs[pos:pos + 4]
        pos += 4
    h_scr = refs[pos]

    j = pl.program_id(2)

    @pl.when(j == 0)
    def _():
        shift = mod_ref[0:1, :]
        scale = mod_ref[1:2, :]
        h_scr[...] = (x_ref[...] * (1.0 + scale) + shift).astype(BF16)

    acc = jnp.dot(h_scr[...], w_ref[...], preferred_element_type=F32)

    if rope:
        lane = lax.broadcasted_iota(jnp.int32, (1, HEAD_DIM), 1)
        first_half = (lane % 64) < 32

    def head(hh):
        return acc[:, hh * HEAD_DIM:(hh + 1) * HEAD_DIM]

    def normed(hh, g):
        y = _rms_head(head(hh), g)
        if rope:
            y = _rope_head(y, cos_ref[...], sin_ref[...], first_half)
        return y

    def put(hh, y, c_ref=None, c_head=0):
        o_ref[:, hh * HEAD_DIM:(hh + 1) * HEAD_DIM] = y.astype(BF16)
        if c_ref is not None:
            c_ref[:, :, c_head * HEAD_DIM:(c_head + 1) * HEAD_DIM] = y.reshape(cache[1], cache[2], HEAD_DIM)

    hpt = PROJ_TN // HEAD_DIM
    j_ka, j_qn, j_kn, j_vn = COL_KA // hpt, COL_QN // hpt, COL_KN // hpt, COL_VN // hpt

    @pl.when(j < j_ka)
    def _():
        for hh in range(hpt):
            put(hh, normed(hh, qg_ref[...]))

    @pl.when(j == j_ka)
    def _():
        for hh in range(N_KV_GA):
            put(hh, normed(hh, kg_ref[...]), ka_ref if cache else None, hh)
        for hh in range(N_KV_GA):
            put(N_KV_GA + hh, head(N_KV_GA + hh), va_ref if cache else None, hh)

    @pl.when((j >= j_qn) & (j < j_kn))
    def _():
        o_ref[...] = (acc * Q_PRESCALE).astype(BF16)

    @pl.when((j >= j_kn) & (j < j_vn))
    def _():
        o_ref[...] = acc.astype(BF16)
        if cache is not None:
            kn_ref[...] = acc.reshape(cache[1], cache[2], PROJ_TN)

    @pl.when(j >= j_vn)
    def _():
        o_ref[...] = acc.astype(BF16)
        if cache is not None:
            vn_ref[...] = acc.reshape(cache[1], cache[2], PROJ_TN)


def _in_proj(x, mod_l, mod_row0, w_in_bf, w_layer, q_gain, k_gain, rope_tabs=None, cache=None):
    b, t, d = x.shape
    in_w = w_in_bf.shape[2]
    tm = min(PROJ_TM, t)
    nj = in_w // PROJ_TN
    rope = rope_tabs is not None
    hpt = PROJ_TN // HEAD_DIM
    j_kn, j_vn = COL_KN // hpt, COL_VN // hpt
    assert COL_KA % hpt == 0 and COL_QN % hpt == 0 and COL_KN % hpt == 0 and COL_VN % hpt == 0
    assert 2 * N_KV_GA == hpt and COL_VA == COL_KA + N_KV_GA

    in_specs = [
        pl.BlockSpec((None, tm, d), lambda bi, ti, j: (bi, ti, 0)),
        pl.BlockSpec((None, N_MOD, d), lambda bi, ti, j: (bi + mod_row0, 0, 0)),
        pl.BlockSpec((None, d, PROJ_TN), lambda bi, ti, j: (w_layer, 0, j)),
        pl.BlockSpec((1, HEAD_DIM), lambda bi, ti, j: (0, 0)),
        pl.BlockSpec((1, HEAD_DIM), lambda bi, ti, j: (0, 0)),
    ]
    args = [x, mod_l, w_in_bf, q_gain, k_gain]
    if rope:
        in_specs += [pl.BlockSpec((tm, HEAD_DIM), lambda bi, ti, j: (ti, 0))] * 2
        args += list(rope_tabs)
    out_specs = [pl.BlockSpec((None, tm, PROJ_TN), lambda bi, ti, j: (bi, ti, j))]
    out_shape = [jax.ShapeDtypeStruct((b, t, in_w), BF16)]
    aliases = {}
    cache_cfg = None
    if cache is not None:
        layer, depth, bsz, seq, prev = cache
        assert b == 1 and tm % seq == 0
        nb = tm // seq
        n_prev = 0 if prev is None else len(prev)
        cache_cfg = (n_prev, nb, seq)
        if prev is not None:
            for k, arr in enumerate(prev):
                aliases[len(args)] = 1 + k
                in_specs.append(pl.BlockSpec(memory_space=pl.ANY))
                args.append(arr)
        kv = lambda bi, ti, j: (ti, layer, 0, 0)
        kn = lambda bi, ti, j: (ti, layer, 0, jnp.clip(j - j_kn, 0, N_HEADS_NA // hpt - 1))
        vn = lambda bi, ti, j: (ti, layer, 0, jnp.clip(j - j_vn, 0, N_HEADS_NA // hpt - 1))
        out_specs += [pl.BlockSpec((nb, None, seq, GA_KV), kv), pl.BlockSpec((nb, None, seq, GA_KV), kv),
                      pl.BlockSpec((nb, None, seq, PROJ_TN), kn), pl.BlockSpec((nb, None, seq, PROJ_TN), vn)]
        out_shape += [jax.ShapeDtypeStruct((bsz, depth, seq, w), F32) for w in (GA_KV, GA_KV, NA_W, NA_W)]
    outs = pl.pallas_call(
        functools.partial(_proj_kernel, rope=rope, cache=cache_cfg),
        grid=(b, t // tm, nj),
        in_specs=in_specs,
        out_specs=out_specs,
        out_shape=out_shape,
        input_output_aliases=aliases,
        scratch_shapes=[pltpu.VMEM((tm, d), BF16)],
        compiler_params=_cparams(("arbitrary", "arbitrary", "arbitrary")),
        name="in_proj_rope" if rope else "in_proj_cache",
    )(*args)
    return outs


def _rope_tables(t):
    half = HEAD_DIM // 4
    tt = jnp.arange(t, dtype=jnp.int32)
    row = (tt // GRID_W).astype(F32)
    col = (tt % GRID_W).astype(F32)
    inv_freq = 1.0 / (ROPE_THETA ** (jnp.arange(half, dtype=F32) / half))
    ar = row[:, None] * inv_freq[None, :]
    ac = col[:, None] * inv_freq[None, :]
    cos = jnp.concatenate([jnp.cos(ar), jnp.cos(ar), jnp.cos(ac), jnp.cos(ac)], axis=-1)
    sin = jnp.concatenate([-jnp.sin(ar), jnp.sin(ar), -jnp.sin(ac), jnp.sin(ac)], axis=-1)
    return cos, sin


_NT = (((1,), (1,)), ((), ()))


def _softmax_pv(scores, values):
    m = scores[0].max(axis=-1, keepdims=True)
    for s in scores[1:]:
        m = jnp.maximum(m, s.max(axis=-1, keepdims=True))
    l = None
    acc = None
    for s, v in zip(scores, values):
        p = jnp.exp2(s - m)
        ps = p.sum(axis=-1, keepdims=True)
        pv = jnp.dot(p.astype(BF16), v, preferred_element_type=F32)
        l = ps if l is None else l + ps
        acc = pv if acc is None else acc + pv
    return acc * (1.0 / l)


def _dense_attn_kernel(*refs, r_heads, has_ctx):
    if has_ctx:
        q_ref, k_ref, v_ref, kc_ref, vc_ref, o_ref = refs
        kc = kc_ref[...].astype(BF16)
        vc = vc_ref[...].astype(BF16)
    else:
        q_ref, k_ref, v_ref, o_ref = refs
    k = k_ref[...]
    v = v_ref[...]
    for r in range(r_heads):
        sl = slice(r * HEAD_DIM, (r + 1) * HEAD_DIM)
        q = q_ref[:, sl]
        scores = [lax.dot_general(q, k, _NT, preferred_element_type=F32)]
        values = [v]
        if has_ctx:
            scores.append(lax.dot_general(q, kc, _NT, preferred_element_type=F32))
            values.append(vc)
        o_ref[:, sl] = _softmax_pv(scores, values).astype(BF16)


def _dense_attention(proj, q_col, k_col, v_col, groups, r_heads, ctx=None):
    b, t, _ = proj.shape
    tq = min(ATTN_TQ, t)
    qw = r_heads * HEAD_DIM
    assert q_col % r_heads == 0
    in_specs = [
        pl.BlockSpec((None, tq, qw), lambda bi, g, qi: (bi, qi, q_col // r_heads + g)),
        pl.BlockSpec((None, t, HEAD_DIM), lambda bi, g, qi: (bi, 0, k_col + g)),
        pl.BlockSpec((None, t, HEAD_DIM), lambda bi, g, qi: (bi, 0, v_col + g)),
    ]
    args = [proj, proj, proj]
    if ctx is not None:
        ck, cv, layer = ctx
        l_ctx = ck.shape[2]
        spec = pl.BlockSpec((None, None, l_ctx, HEAD_DIM), lambda bi, g, qi: (bi, layer, 0, g))
        in_specs += [spec, spec]
        args += [ck, cv]
    return pl.pallas_call(
        functools.partial(_dense_attn_kernel, r_heads=r_heads, has_ctx=ctx is not None),
        grid=(b, groups, t // tq),
        in_specs=in_specs,
        out_specs=pl.BlockSpec((None, tq, qw), lambda bi, g, qi: (bi, qi, g)),
        out_shape=jax.ShapeDtypeStruct((b, t, groups * qw), BF16),
        compiler_params=_cparams(("arbitrary", "arbitrary", "arbitrary")),
        name="dense_attn_ctx" if ctx is not None else "dense_attn",
    )(*args)


def _na_plan(rows):
    kh = min(NA_WIN_H, rows)
    kw = NA_WIN_W
    r_blk = min(NA_ROWS, rows)
    assert rows % r_blk == 0
    slab = min(r_blk - 1 + kh, rows)
    row_start = np.clip(np.arange(rows) - kh // 2, 0, rows - kh)
    col = np.arange(GRID_W)
    col_start = np.clip(col - kw // 2, 0, GRID_W - kw)
    slab_start, pat_id, pats, sigs = [], [], [], {}
    for r0 in range(0, rows, r_blk):
        ss = min(row_start[r0], rows - slab)
        rel = tuple(int(row_start[r0 + ri] - ss) for ri in range(r_blk))
        sig = (int(ss - r0), rel)
        if sig not in sigs:
            sigs[sig] = len(pats)
            q_row = r0 + np.arange(r_blk)[:, None, None, None]
            q_col = col[None, :, None, None]
            k_row = ss + np.arange(slab)[None, None, :, None]
            k_col = col[None, None, None, :]
            rs = row_start[r0:r0 + r_blk][:, None, None, None]
            cs = col_start[None, :, None, None]
            valid = (k_row >= rs) & (k_row < rs + kh) & (k_col >= cs) & (k_col < cs + kw)
            row_off = (k_row - q_row + (NA_WIN_H - 1))[:, 0, :, 0]
            row_sel = (row_off[:, :, None] == np.arange(2 * NA_WIN_H - 1)).astype(np.float32)
            full = (r_blk, GRID_W, slab, GRID_W)
            pats.append((np.broadcast_to(valid, full).reshape(r_blk * GRID_W, slab * GRID_W), row_sel))
        slab_start.append(int(ss))
        pat_id.append(sigs[sig])
    return r_blk, slab, np.array(slab_start, np.int32), np.array(pat_id, np.int32), pats


def _na_bias_table(rel_bias_l, pats):
    col = np.arange(GRID_W)
    col_off = col[None, :] - col[:, None] + (NA_WIN_W - 1)
    col_sel = (col_off[:, :, None] == np.arange(2 * NA_WIN_W - 1)).astype(np.float32)
    hi = lax.Precision.HIGHEST
    tabs = []
    for valid, row_sel in pats:
        by_row = jnp.einsum('hrc,isr->hisc', rel_bias_l.astype(F32), row_sel, precision=hi)
        bias = jnp.einsum('hisc,qkc->hiqsk', by_row, col_sel, precision=hi)
        bias = bias.reshape((bias.shape[0],) + valid.shape) * LOG2E
        tabs.append(jnp.where(valid[None], bias, MASK_VALUE))
    return jnp.stack(tabs, axis=0)


def _na_kernel(ss_ref, pat_ref, q_ref, k_ref, v_ref, kc_ref, vc_ref, bias_ref, o_ref, *, slab_len, n_heads):
    del pat_ref
    rb = pl.program_id(2)
    start = pl.multiple_of(ss_ref[rb] * GRID_W, GRID_W)
    for h in range(n_heads):
        sl = slice(h * HEAD_DIM, (h + 1) * HEAD_DIM)
        ks = k_ref[pl.ds(start, slab_len), sl]
        vs = v_ref[pl.ds(start, slab_len), sl]
        kc = kc_ref[:, sl].astype(BF16)
        vc = vc_ref[:, sl].astype(BF16)
        q = q_ref[:, sl]
        s_loc = lax.dot_general(q, ks, _NT, preferred_element_type=F32) + bias_ref[h]
        s_ctx = lax.dot_general(q, kc, _NT, preferred_element_type=F32)
        o_ref[:, sl] = _softmax_pv([s_loc, s_ctx], [vs, vc]).astype(BF16)


def _neighborhood_attention(proj, ck, cv, layer, bias_tab, plan):
    b, t, _ = proj.shape
    r_blk, slab, slab_start, pat_id, _ = plan
    qn = r_blk * GRID_W
    sn = slab * GRID_W
    l_ctx = ck.shape[2]
    hb = NA_HEADS_PER_STEP
    hw = hb * HEAD_DIM
    assert COL_QN % hb == 0 and COL_KN % hb == 0 and COL_VN % hb == 0 and N_HEADS_NA % hb == 0
    grid_spec = pltpu.PrefetchScalarGridSpec(
        num_scalar_prefetch=2,
        grid=(b, N_HEADS_NA // hb, t // qn),
        in_specs=[
            pl.BlockSpec((None, qn, hw), lambda bi, hg, rb, ss, pt: (bi, rb, COL_QN // hb + hg)),
            pl.BlockSpec((None, t, hw), lambda bi, hg, rb, ss, pt: (bi, 0, COL_KN // hb + hg)),
            pl.BlockSpec((None, t, hw), lambda bi, hg, rb, ss, pt: (bi, 0, COL_VN // hb + hg)),
            pl.BlockSpec((None, None, l_ctx, hw), lambda bi, hg, rb, ss, pt: (bi, layer, 0, hg)),
            pl.BlockSpec((None, None, l_ctx, hw), lambda bi, hg, rb, ss, pt: (bi, layer, 0, hg)),
            pl.BlockSpec((None, hb, qn, sn), lambda bi, hg, rb, ss, pt: (pt[rb], hg, 0, 0)),
        ],
        out_specs=pl.BlockSpec((None, qn, hw), lambda bi, hg, rb, ss, pt: (bi, rb, hg)),
    )
    return pl.pallas_call(
        functools.partial(_na_kernel, slab_len=sn, n_heads=hb),
        grid_spec=grid_spec,
        out_shape=jax.ShapeDtypeStruct((b, t, NA_W), BF16),
        compiler_params=_cparams(("arbitrary", "arbitrary", "arbitrary")),
        name="neighborhood_attn",
    )(jnp.asarray(slab_start), jnp.asarray(pat_id), proj, proj, proj, ck, cv, bias_tab)


def _layer_norm(y, g, b):
    mu = jnp.mean(y, axis=-1, keepdims=True)
    yc = y - mu
    var = jnp.mean(yc * yc, axis=-1, keepdims=True)
    return yc * lax.rsqrt(var + EPS) * g + b


def _top2_of4(vals):
    m1 = jnp.maximum(jnp.maximum(vals[0], vals[1]), jnp.maximum(vals[2], vals[3]))
    i1 = jnp.where(vals[0] == m1, 0, jnp.where(vals[1] == m1, 1, jnp.where(vals[2] == m1, 2, 3)))
    rest = [jnp.where(i1 == i, -1.0, vals[i]) for i in range(4)]
    m2 = jnp.maximum(jnp.maximum(rest[0], rest[1]), jnp.maximum(rest[2], rest[3]))
    i2 = jnp.where(rest[0] == m2, 0, jnp.where(rest[1] == m2, 1, jnp.where(rest[2] == m2, 2, 3)))
    return m1, i1, m2, i2


def _route_rows(logits_t):
    m = logits_t.max(axis=0, keepdims=True)
    e = jnp.exp(logits_t - m)
    probs = e / e.sum(axis=0, keepdims=True)
    rows = [probs[i:i + 1, :] for i in range(N_EXPERTS)]
    groups = [rows[g * EXPERTS_PER_GROUP:(g + 1) * EXPERTS_PER_GROUP] for g in range(N_GROUPS)]
    scores = []
    for g in range(N_GROUPS):
        m1, _, m2, _ = _top2_of4(groups[g])
        scores.append(m1 + m2)
    best = jnp.maximum(jnp.maximum(scores[0], scores[1]), jnp.maximum(scores[2], scores[3]))
    gi = jnp.where(scores[0] == best, 0, jnp.where(scores[1] == best, 1, jnp.where(scores[2] == best, 2, 3)))
    sel = [jnp.where(gi == 0, groups[0][i], jnp.where(gi == 1, groups[1][i],
                     jnp.where(gi == 2, groups[2][i], groups[3][i]))) for i in range(EXPERTS_PER_GROUP)]
    w1, l1, w2, l2 = _top2_of4(sel)
    wsum = w1 + w2
    return (gi * EXPERTS_PER_GROUP + l1, gi * EXPERTS_PER_GROUP + l2, w1 / wsum, w2 / wsum)


def _out_proj_kernel(oa_ref, on_ref, x_ref, mod_ref, wo_ref, g_ref, b_ref, wr_ref, br_ref,
                     x1_ref, h2_ref, e_ref, wt_ref):
    attn = jnp.dot(oa_ref[...], wo_ref[0:GA_Q, :], preferred_element_type=F32)
    attn = attn + jnp.dot(on_ref[...], wo_ref[GA_Q:GA_Q + NA_W, :], preferred_element_type=F32)
    gate1 = mod_ref[2:3, :]
    x1 = _layer_norm(DN_ALPHA * x_ref[...] + gate1 * attn, g_ref[...], b_ref[...])
    x1_ref[...] = x1
    h2 = x1 * (1.0 + mod_ref[4:5, :]) + mod_ref[3:4, :]
    h2_ref[...] = h2
    h_hi = h2.astype(BF16)
    h_lo = (h2 - h_hi.astype(F32)).astype(BF16)
    r_hi = jnp.dot(h_hi, wr_ref[...], preferred_element_type=F32)
    r_lo = jnp.dot(h_lo, wr_ref[...], preferred_element_type=F32)
    logits = r_hi + pltpu.roll(r_hi, LANE - N_EXPERTS, 1) + r_lo
    logits_t = logits.T[0:N_EXPERTS, :] + br_ref[...]
    e1, e2, w1, w2 = _route_rows(logits_t)
    e_ref[0:1, :] = e1
    e_ref[1:2, :] = e2
    n = w1.shape[1]
    row = lax.broadcasted_iota(jnp.int32, (LANE, n), 0)
    w_rows = jnp.where(row == 0, w1, jnp.where(row == 1, w2, 0.0))
    wt_ref[...] = w_rows.T


def _out_proj(oa, on, x, mod_l, mod_row0, w_o_bf, w_layer, ln_g, ln_b, wr_cat, b_router):
    b, t, d = x.shape
    tm = min(OUT_TM, t)
    tok = lambda bi, ti: (bi, ti, 0)
    const2 = lambda bi, ti: (0, 0)
    return pl.pallas_call(
        _out_proj_kernel,
        grid=(b, t // tm),
        in_specs=[
            pl.BlockSpec((None, tm, GA_Q), tok),
            pl.BlockSpec((None, tm, NA_W), tok),
            pl.BlockSpec((None, tm, d), tok),
            pl.BlockSpec((None, N_MOD, d), lambda bi, ti: (bi + mod_row0, 0, 0)),
            pl.BlockSpec((None, GA_Q + NA_W, d), lambda bi, ti: (w_layer, 0, 0)),
            pl.BlockSpec((1, d), const2),
            pl.BlockSpec((1, d), const2),
            pl.BlockSpec((d, LANE), const2),
            pl.BlockSpec((N_EXPERTS, 1), const2),
        ],
        out_specs=[
            pl.BlockSpec((None, tm, d), tok),
            pl.BlockSpec((None, tm, d), tok),
            pl.BlockSpec((None, TOP_K, tm), lambda bi, ti: (bi, 0, ti)),
            pl.BlockSpec((None, tm, LANE), tok),
        ],
        out_shape=[
            jax.ShapeDtypeStruct((b, t, d), F32),
            jax.ShapeDtypeStruct((b, t, d), F32),
            jax.ShapeDtypeStruct((b, TOP_K, t), jnp.int32),
            jax.ShapeDtypeStruct((b, t, LANE), F32),
        ],
        compiler_params=_cparams(("arbitrary", "arbitrary")),
        name="out_proj_ln_router",
    )(oa, on, x, mod_l, w_o_bf, ln_g.reshape(1, d), ln_b.reshape(1, d), wr_cat,
      b_router.reshape(N_EXPERTS, 1).astype(F32))


def _row_copy(src_hbm, idx_ref, dst, sem, r):
    return pltpu.make_async_copy(src_hbm.at[pl.ds(idx_ref[0, r], 1)], dst.at[pl.ds(r, 1)], sem)


def _start_row_gather(src_hbm, idx_ref, dst, sem, n_rows):
    def body(r, carry):
        _row_copy(src_hbm, idx_ref, dst, sem, r).start()
        return carry
    lax.fori_loop(0, n_rows, body, 0, unroll=8)


def _start_row_gather_inline(src_hbm, idx_ref, dst, sem, r_lo, r_hi):
    for r in range(r_lo, r_hi):
        _row_copy(src_hbm, idx_ref, dst, sem, r).start()


def _wait_row_gather(src_hbm, dst, sem, n_rows):
    pltpu.make_async_copy(src_hbm.at[pl.ds(0, n_rows)], dst, sem).wait()


def _moe_kernel(be_ref, nu_ref, tok_ref, tok_next_ref, h_hbm, wg_ref, wu_ref, wd_ref, o_ref,
                buf0, buf1, sem, *, n_blocks):
    del be_ref
    i = pl.program_id(0)
    n_used = nu_ref[0]

    @pl.when(i == 0)
    def _():
        _start_row_gather(h_hbm, tok_ref, buf0, sem.at[0], MOE_BLK)

    def step(cur, cur_sem, nxt, nxt_sem):
        _wait_row_gather(h_hbm, cur, cur_sem, MOE_BLK)

        @pl.when(i < n_used)
        def _():
            xb = cur[...].astype(BF16)
            gate = jnp.dot(xb, wg_ref[...], preferred_element_type=F32)
            up = jnp.dot(xb, wu_ref[...], preferred_element_type=F32)
            act = (gate / (1.0 + jnp.exp(-gate)) * up).astype(BF16)
            _start_row_gather_inline(h_hbm, tok_next_ref, nxt, nxt_sem, 0, MOE_BLK)
            o_ref[...] = jnp.dot(act, wd_ref[...], preferred_element_type=F32)

        @pl.when(i >= n_used)
        def _():
            o_ref[...] = jnp.zeros_like(o_ref)
            _start_row_gather(h_hbm, tok_next_ref, nxt, nxt_sem, MOE_BLK)

        @pl.when(i == n_blocks - 1)
        def _():
            _wait_row_gather(h_hbm, nxt, nxt_sem, MOE_BLK)

    @pl.when(i % 2 == 0)
    def _():
        step(buf0, sem.at[0], buf1, sem.at[1])

    @pl.when(i % 2 == 1)
    def _():
        step(buf1, sem.at[1], buf0, sem.at[0])


def _moe_ffn(h2_flat, tok_buf, blk_e, n_used, wg, wu, wd, layer):
    n, d = h2_flat.shape
    n_blocks = blk_e.shape[0]
    d_ff = wg.shape[3]
    tok3 = tok_buf.reshape(n_blocks, 1, MOE_BLK)
    smem_blk = lambda f: pl.BlockSpec((None, 1, MOE_BLK), f, memory_space=pltpu.SMEM)
    grid_spec = pltpu.PrefetchScalarGridSpec(
        num_scalar_prefetch=2,
        grid=(n_blocks,),
        in_specs=[
            smem_blk(lambda i, be, nu: (i, 0, 0)),
            smem_blk(lambda i, be, nu: (jnp.minimum(i + 1, n_blocks - 1), 0, 0)),
            pl.BlockSpec(memory_space=pl.ANY),
            pl.BlockSpec((None, None, d, d_ff), lambda i, be, nu: (layer, be[i], 0, 0)),
            pl.BlockSpec((None, None, d, d_ff), lambda i, be, nu: (layer, be[i], 0, 0)),
            pl.BlockSpec((None, None, d_ff, d), lambda i, be, nu: (layer, be[i], 0, 0)),
        ],
        out_specs=pl.BlockSpec((MOE_BLK, d), lambda i, be, nu: (i, 0)),
        scratch_shapes=[pltpu.VMEM((MOE_BLK, d), F32), pltpu.VMEM((MOE_BLK, d), F32),
                        pltpu.SemaphoreType.DMA((2,))],
    )
    return pl.pallas_call(
        functools.partial(_moe_kernel, n_blocks=n_blocks),
        grid_spec=grid_spec,
        out_shape=jax.ShapeDtypeStruct((n_blocks * MOE_BLK, d), F32),
        compiler_params=_cparams(("arbitrary",)),
        name="moe_ffn",
    )(blk_e, n_used, tok3, tok3, h2_flat, wg, wu, wd)


def _combine_kernel(d0_ref, d1_ref, d0n_ref, d1n_ref, y_hbm, x1_ref, mod_ref, wt_ref, g_ref, b_ref,
                    o_ref, buf0, buf1, sem, *, n_steps, tm):
    i = pl.program_id(0)

    @pl.when(i == 0)
    def _():
        _start_row_gather(y_hbm, d0_ref, buf0.at[0], sem.at[0], tm)
        _start_row_gather(y_hbm, d1_ref, buf0.at[1], sem.at[0], tm)

    def step(cur, cur_sem, nxt, nxt_sem):
        _wait_row_gather(y_hbm, cur.at[0], cur_sem, tm)
        _wait_row_gather(y_hbm, cur.at[1], cur_sem, tm)
        y = wt_ref[:, 0:1] * cur[0] + wt_ref[:, 1:2] * cur[1]
        gate2 = mod_ref[5:6, :]
        o_ref[...] = _layer_norm(DN_ALPHA * x1_ref[...] + gate2 * y, g_ref[...], b_ref[...])
        _start_row_gather_inline(y_hbm, d0n_ref, nxt.at[0], nxt_sem, 0, tm)
        _start_row_gather_inline(y_hbm, d1n_ref, nxt.at[1], nxt_sem, 0, tm)

        @pl.when(i == n_steps - 1)
        def _():
            _wait_row_gather(y_hbm, nxt.at[0], nxt_sem, tm)
            _wait_row_gather(y_hbm, nxt.at[1], nxt_sem, tm)

    @pl.when(i % 2 == 0)
    def _():
        step(buf0, sem.at[0], buf1, sem.at[1])

    @pl.when(i % 2 == 1)
    def _():
        step(buf1, sem.at[1], buf0, sem.at[0])


def _combine(y_sorted, dest, x1, mod_l, mod_row0, wt, ln_g, ln_b):
    b, t, d = x1.shape
    tm = min(OUT_TM, t)
    nt = t // tm
    n_steps = b * nt
    dest4 = dest.reshape(b, TOP_K, nt, tm).transpose(0, 2, 1, 3).reshape(n_steps, TOP_K, 1, tm)
    nxt = lambda i: jnp.minimum(i + 1, n_steps - 1)
    smem_blk = lambda f: pl.BlockSpec((None, None, 1, tm), f, memory_space=pltpu.SMEM)
    tok = lambda i: (i // nt, i % nt, 0)
    return pl.pallas_call(
        functools.partial(_combine_kernel, n_steps=n_steps, tm=tm),
        grid=(n_steps,),
        in_specs=[
            smem_blk(lambda i: (i, 0, 0, 0)),
            smem_blk(lambda i: (i, 1, 0, 0)),
            smem_blk(lambda i: (nxt(i), 0, 0, 0)),
            smem_blk(lambda i: (nxt(i), 1, 0, 0)),
            pl.BlockSpec(memory_space=pl.ANY),
            pl.BlockSpec((None, tm, d), tok),
            pl.BlockSpec((None, N_MOD, d), lambda i: (i // nt + mod_row0, 0, 0)),
            pl.BlockSpec((None, tm, LANE), tok),
            pl.BlockSpec((1, d), lambda i: (0, 0)),
            pl.BlockSpec((1, d), lambda i: (0, 0)),
        ],
        out_specs=pl.BlockSpec((None, tm, d), tok),
        out_shape=jax.ShapeDtypeStruct((b, t, d), F32),
        scratch_shapes=[pltpu.VMEM((TOP_K, tm, d), F32), pltpu.VMEM((TOP_K, tm, d), F32),
                        pltpu.SemaphoreType.DMA((2,))],
        compiler_params=_cparams(("arbitrary",)),
        name="moe_combine_ln",
    )(dest4, dest4, dest4, dest4, y_sorted, x1, mod_l, wt, ln_g.reshape(1, d), ln_b.reshape(1, d))


def _dispatch_plan(e_idx):
    b, k, t = e_idx.shape
    n = b * t
    a = n * k
    flat_e = e_idx.reshape(a)
    src_row = (jnp.arange(b, dtype=jnp.int32)[:, None, None] * t
               + jnp.zeros((1, k, 1), jnp.int32)
               + jnp.arange(t, dtype=jnp.int32)[None, None, :]).reshape(a)
    onehot = (flat_e[:, None] == jnp.arange(N_EXPERTS, dtype=jnp.int32)[None, :]).astype(jnp.int32)
    counts = onehot.sum(axis=0)
    rank = (jnp.cumsum(onehot, axis=0) * onehot).sum(axis=1) - 1
    padded = (counts + MOE_BLK - 1) // MOE_BLK * MOE_BLK
    pad_end = jnp.cumsum(padded)
    pad_start = pad_end - padded
    dest = pad_start[flat_e] + rank
    n_blocks = -(-a // MOE_BLK) + N_EXPERTS
    p = n_blocks * MOE_BLK
    tok_buf = jnp.zeros((p,), jnp.int32).at[dest].set(src_row)
    blk_start = jnp.arange(n_blocks, dtype=jnp.int32) * MOE_BLK
    blk_e = jnp.minimum(jnp.searchsorted(pad_end, blk_start, side='right'), N_EXPERTS - 1).astype(jnp.int32)
    n_used = (pad_end[-1] // MOE_BLK).astype(jnp.int32)
    last_e = blk_e[jnp.maximum(n_used - 1, 0)]
    blk_e = jnp.where(jnp.arange(n_blocks) < n_used, blk_e, last_e)
    return dest.reshape(b, k, t), tok_buf, blk_e, n_used.reshape(1)


def _layer(x, mod_l, mod_row0, lw, shared, ctx):
    b, t, d = x.shape
    is_ctx = ctx is None
    if is_ctx:
        xt = x.reshape(1, b * t, d)
        proj, *cache = _in_proj(xt, mod_l, mod_row0, lw['w_in'], lw['layer'], lw['q_gain'], lw['k_gain'],
                                cache=(lw['layer'], lw['depth'], b, t, lw['prev_cache']))
        proj = proj.reshape(b, t, -1)
        oa = _dense_attention(proj, COL_QA, COL_KA, COL_VA, N_KV_GA, N_HEADS_GA // N_KV_GA)
        on = _dense_attention(proj, COL_QN, COL_KN, COL_VN, N_HEADS_NA, 1)
        oa = oa.reshape(1, b * t, -1)
        on = on.reshape(1, b * t, -1)
    else:
        ga_k, ga_v, na_k, na_v, layer, rope_tabs, na_bias, na_plan = ctx
        (proj,) = _in_proj(x, mod_l, mod_row0, lw['w_in'], lw['layer'], lw['q_gain'], lw['k_gain'],
                           rope_tabs=rope_tabs)
        cache = None
        oa = _dense_attention(proj, COL_QA, COL_KA, COL_VA, N_KV_GA, N_HEADS_GA // N_KV_GA,
                              ctx=(ga_k, ga_v, layer))
        on = _neighborhood_attention(proj, na_k, na_v, layer, na_bias, na_plan)
        xt = x
    x1, h2, e_idx, wt = _out_proj(oa, on, xt, mod_l, mod_row0, lw['w_o'], lw['layer'], lw['ln1_g'], lw['ln1_b'],
                                  shared['wr_cat'], shared['b_router'])
    dest, tok_buf, blk_e, n_used = _dispatch_plan(e_idx)
    y_sorted = _moe_ffn(h2.reshape(-1, d), tok_buf, blk_e, n_used, lw['w_gate'], lw['w_up'], lw['w_down'],
                        lw['layer'])
    out = _combine(y_sorted, dest, x1, mod_l, mod_row0, wt, lw['ln2_g'], lw['ln2_b'])
    return out.reshape(b, t, d), cache


def kernel(x_prompt, x_sample, c, cache_ga_k, cache_ga_v, cache_na_k, cache_na_v, c_ctx, w_router, b_router, w_mod, b_mod, w_in, q_norm, k_norm, rel_bias, w_o, ln1_g, ln1_b, ln2_g, ln2_b, w_gate, w_up, w_down):
    bsz, seq, d = x_prompt.shape
    dec_b, dec_t, _ = x_sample.shape
    depth = w_mod.shape[0]
    past = cache_ga_k.shape[2]

    mod_rows = 16
    assert dec_b + 1 <= mod_rows
    cvec = jnp.concatenate([c, c_ctx[None, :], jnp.zeros((mod_rows - dec_b - 1, d), F32)], axis=0)
    mod = _modulation(cvec, w_mod, b_mod).reshape(depth, mod_rows, N_MOD, d)

    wr_hi = w_router.astype(BF16)
    wr_lo = (w_router - wr_hi.astype(F32)).astype(BF16)
    wr_cat = jnp.concatenate([wr_hi, wr_lo, jnp.zeros((d, LANE - 2 * N_EXPERTS), BF16)], axis=1)
    shared = {'wr_cat': wr_cat, 'b_router': b_router}

    rope_tabs = _rope_tables(dec_t)
    na_plan = _na_plan(dec_t // GRID_W)
    ga_k = cache_ga_k.reshape(dec_b, depth, past, GA_KV)
    ga_v = cache_ga_v.reshape(dec_b, depth, past, GA_KV)
    na_k = cache_na_k.reshape(dec_b, depth, past, NA_W)
    na_v = cache_na_v.reshape(dec_b, depth, past, NA_W)

    w_in_bf, w_o_bf = w_in.astype(BF16), w_o.astype(BF16)
    w_gate_bf, w_up_bf, w_down_bf = w_gate.astype(BF16), w_up.astype(BF16), w_down.astype(BF16)

    y_prompt, y_sample = x_prompt, x_sample
    cache = [jnp.zeros((bsz, depth, seq, w), F32) for w in (GA_KV, GA_KV, NA_W, NA_W)]
    for i in range(depth):
        lw = {
            'layer': i, 'depth': depth, 'prev_cache': cache,
            'w_in': w_in_bf, 'w_o': w_o_bf,
            'q_gain': (q_norm[i] * Q_PRESCALE).reshape(1, HEAD_DIM).astype(F32),
            'k_gain': k_norm[i].reshape(1, HEAD_DIM).astype(F32),
            'ln1_g': ln1_g[i], 'ln1_b': ln1_b[i], 'ln2_g': ln2_g[i], 'ln2_b': ln2_b[i],
            'w_gate': w_gate_bf, 'w_up': w_up_bf, 'w_down': w_down_bf,
        }
        na_bias = _na_bias_table(rel_bias[i], na_plan[4])
        y_prompt, cache = _layer(y_prompt, mod[i], dec_b, lw, shared, None)
        y_sample, _ = _layer(y_sample, mod[i], 0, lw, shared,
                             (ga_k, ga_v, na_k, na_v, i, rope_tabs, na_bias, na_plan))

    outs = [arr.reshape(bsz, depth, seq, -1, HEAD_DIM) for arr in cache]
    return (y_prompt, y_sample, outs[0], outs[1], outs[2], outs[3])
```

```python
import functools
import math

import numpy as np
import jax
import jax.numpy as jnp
from jax import lax
from jax.experimental import pallas as pl
from jax.experimental.pallas import tpu as pltpu

F32 = jnp.float32
BF16 = jnp.bfloat16

DEPTH = 2
GRID_W = 64
HEAD_DIM = 128
N_HEADS_GA = 8
N_KV_GA = 2
N_HEADS_NA = 8
NA_WIN_H = 8
NA_WIN_W = 16
ROPE_THETA = 10000.0
N_EXPERTS = 16
N_GROUPS = 4
EXPERTS_PER_GROUP = N_EXPERTS // N_GROUPS
TOP_K = 2
N_MOD = 6
DN_ALPHA = (2.0 * DEPTH) ** 0.25
EPS = 1e-6
ATTN_SCALE = HEAD_DIM ** -0.5
LOG2E = math.log2(math.e)
Q_PRESCALE = ATTN_SCALE * LOG2E
MASK_VALUE = -1e30

GA_Q = N_HEADS_GA * HEAD_DIM
GA_KV = N_KV_GA * HEAD_DIM
NA_W = N_HEADS_NA * HEAD_DIM
COL_QA = 0
COL_KA = COL_QA + N_HEADS_GA
COL_VA = COL_KA + N_KV_GA
COL_QN = COL_VA + N_KV_GA
COL_KN = COL_QN + N_HEADS_NA
COL_VN = COL_KN + N_HEADS_NA
IN_HEADS = COL_VN + N_HEADS_NA

LANE = 128
PROJ_TN = 4 * HEAD_DIM
PROJ_TM = 1024
ATTN_TQ = 256
NA_ROWS = 4
NA_HEADS_PER_STEP = 4
OUT_TM = 256
MOE_BLK = 256
GATHER_DEPTH = 3
MOD_TN = 1024
VMEM_LIMIT = 52 * 1024 * 1024


def _cparams(sem):
    return pltpu.CompilerParams(dimension_semantics=sem, vmem_limit_bytes=VMEM_LIMIT)


def _mod_kernel(c_ref, w_ref, b_ref, o_ref):
    c = c_ref[...]
    s = c / (1.0 + jnp.exp(-c))
    o_ref[...] = jnp.dot(s.astype(BF16), w_ref[...].astype(BF16),
                         preferred_element_type=F32) + b_ref[...]


def _modulation(cvec, w_mod, b_mod):
    depth, d, e = w_mod.shape
    rows = cvec.shape[0]
    tn = MOD_TN if e % MOD_TN == 0 else e
    return pl.pallas_call(
        _mod_kernel,
        grid=(depth, e // tn),
        in_specs=[
            pl.BlockSpec((rows, d), lambda l, j: (0, 0)),
            pl.BlockSpec((None, d, tn), lambda l, j: (l, 0, j)),
            pl.BlockSpec((None, 1, tn), lambda l, j: (l, 0, j)),
        ],
        out_specs=pl.BlockSpec((None, rows, tn), lambda l, j: (l, 0, j)),
        out_shape=jax.ShapeDtypeStruct((depth, rows, e), F32),
        compiler_params=_cparams(("arbitrary", "arbitrary")),
        name="modulation",
    )(cvec, w_mod, b_mod.reshape(depth, 1, e))


def _rms_head(x, g):
    ms = jnp.mean(x * x, axis=-1, keepdims=True)
    return x * lax.rsqrt(ms + EPS) * g


def _rope_head(x, cos, sin, first_half):
    swapped = jnp.where(first_half, pltpu.roll(x, HEAD_DIM - 32, 1), pltpu.roll(x, 32, 1))
    return x * cos + swapped * sin


def _proj_kernel(*refs, rope, cache):
    x_ref, mod_ref, w_ref, qg_ref, kg_ref = refs[:5]
    pos = 5
    if rope:
        cos_ref, sin_ref = refs[pos:pos + 2]
        pos += 2
    if cache is not None:
        pos += cache[0]
    o_ref = refs[pos]
    pos += 1
    if cache is not None:
        ka_ref, va_ref, kn_ref, vn_ref = refs[pos:pos + 4]
        pos += 4
    h_scr = refs[pos]

    j = pl.program_id(2)

    @pl.when(j == 0)
    def _():
        shift = mod_ref[0:1, :]
        scale = mod_ref[1:2, :]
        h_scr[...] = (x_ref[...] * (1.0 + scale) + shift).astype(BF16)

    acc = jnp.dot(h_scr[...], w_ref[...], preferred_element_type=F32)

    if rope:
        lane = lax.broadcasted_iota(jnp.int32, (1, HEAD_DIM), 1)
        first_half = (lane % 64) < 32

    def head(hh):
        return acc[:, hh * HEAD_DIM:(hh + 1) * HEAD_DIM]

    def normed(hh, g):
        y = _rms_head(head(hh), g)
        if rope:
            y = _rope_head(y, cos_ref[...], sin_ref[...], first_half)
        return y

    def put(hh, y, c_ref=None, c_head=0):
        o_ref[:, hh * HEAD_DIM:(hh + 1) * HEAD_DIM] = y.astype(BF16)
        if c_ref is not None:
            c_ref[:, :, c_head * HEAD_DIM:(c_head + 1) * HEAD_DIM] = y.reshape(cache[1], cache[2], HEAD_DIM)

    hpt = PROJ_TN // HEAD_DIM
    j_ka, j_qn, j_kn, j_vn = COL_KA // hpt, COL_QN // hpt, COL_KN // hpt, COL_VN // hpt

    @pl.when(j < j_ka)
    def _():
        for hh in range(hpt):
            put(hh, normed(hh, qg_ref[...]))

    @pl.when(j == j_ka)
    def _():
        for hh in range(N_KV_GA):
            put(hh, normed(hh, kg_ref[...]), ka_ref if cache else None, hh)
        for hh in range(N_KV_GA):
            put(N_KV_GA + hh, head(N_KV_GA + hh), va_ref if cache else None, hh)

    @pl.when((j >= j_qn) & (j < j_kn))
    def _():
        o_ref[...] = (acc * Q_PRESCALE).astype(BF16)

    @pl.when((j >= j_kn) & (j < j_vn))
    def _():
        o_ref[...] = acc.astype(BF16)
        if cache is not None:
            kn_ref[...] = acc.reshape(cache[1], cache[2], PROJ_TN)

    @pl.when(j >= j_vn)
    def _():
        o_ref[...] = acc.astype(BF16)
        if cache is not None:
            vn_ref[...] = acc.reshape(cache[1], cache[2], PROJ_TN)


def _in_proj(x, mod_l, mod_row0, w_in_bf, w_layer, q_gain, k_gain, rope_tabs=None, cache=None):
    b, t, d = x.shape
    in_w = w_in_bf.shape[2]
    tm = min(PROJ_TM, t)
    nj = in_w // PROJ_TN
    rope = rope_tabs is not None
    hpt = PROJ_TN // HEAD_DIM
    j_kn, j_vn = COL_KN // hpt, COL_VN // hpt
    assert COL_KA % hpt == 0 and COL_QN % hpt == 0 and COL_KN % hpt == 0 and COL_VN % hpt == 0
    assert 2 * N_KV_GA == hpt and COL_VA == COL_KA + N_KV_GA

    in_specs = [
        pl.BlockSpec((None, tm, d), lambda bi, ti, j: (bi, ti, 0)),
        pl.BlockSpec((None, N_MOD, d), lambda bi, ti, j: (bi + mod_row0, 0, 0)),
        pl.BlockSpec((None, d, PROJ_TN), lambda bi, ti, j: (w_layer, 0, j)),
        pl.BlockSpec((1, HEAD_DIM), lambda bi, ti, j: (0, 0)),
        pl.BlockSpec((1, HEAD_DIM), lambda bi, ti, j: (0, 0)),
    ]
    args = [x, mod_l, w_in_bf, q_gain, k_gain]
    if rope:
        in_specs += [pl.BlockSpec((tm, HEAD_DIM), lambda bi, ti, j: (ti, 0))] * 2
        args += list(rope_tabs)
    out_specs = [pl.BlockSpec((None, tm, PROJ_TN), lambda bi, ti, j: (bi, ti, j))]
    out_shape = [jax.ShapeDtypeStruct((b, t, in_w), BF16)]
    aliases = {}
    cache_cfg = None
    if cache is not None:
        layer, depth, bsz, seq, prev = cache
        assert b == 1 and tm % seq == 0
        nb = tm // seq
        n_prev = 0 if prev is None else len(prev)
        cache_cfg = (n_prev, nb, seq)
        if prev is not None:
            for k, arr in enumerate(prev):
                aliases[len(args)] = 1 + k
                in_specs.append(pl.BlockSpec(memory_space=pl.ANY))
                args.append(arr)
        kv = lambda bi, ti, j: (ti, layer, 0, 0)
        kn = lambda bi, ti, j: (ti, layer, 0, jnp.clip(j - j_kn, 0, N_HEADS_NA // hpt - 1))
        vn = lambda bi, ti, j: (ti, layer, 0, jnp.clip(j - j_vn, 0, N_HEADS_NA // hpt - 1))
        out_specs += [pl.BlockSpec((nb, None, seq, GA_KV), kv), pl.BlockSpec((nb, None, seq, GA_KV), kv),
                      pl.BlockSpec((nb, None, seq, PROJ_TN), kn), pl.BlockSpec((nb, None, seq, PROJ_TN), vn)]
        out_shape += [jax.ShapeDtypeStruct((bsz, depth, seq, w), F32) for w in (GA_KV, GA_KV, NA_W, NA_W)]
    outs = pl.pallas_call(
        functools.partial(_proj_kernel, rope=rope, cache=cache_cfg),
        grid=(b, t // tm, nj),
        in_specs=in_specs,
        out_specs=out_specs,
        out_shape=out_shape,
        input_output_aliases=aliases,
        scratch_shapes=[pltpu.VMEM((tm, d), BF16)],
        compiler_params=_cparams(("arbitrary", "arbitrary", "arbitrary")),
        name="in_proj_rope" if rope else "in_proj_cache",
    )(*args)
    return outs


def _rope_tables(t):
    half = HEAD_DIM // 4
    tt = jnp.arange(t, dtype=jnp.int32)
    row = (tt // GRID_W).astype(F32)
    col = (tt % GRID_W).astype(F32)
    inv_freq = 1.0 / (ROPE_THETA ** (jnp.arange(half, dtype=F32) / half))
    ar = row[:, None] * inv_freq[None, :]
    ac = col[:, None] * inv_freq[None, :]
    cos = jnp.concatenate([jnp.cos(ar), jnp.cos(ar), jnp.cos(ac), jnp.cos(ac)], axis=-1)
    sin = jnp.concatenate([-jnp.sin(ar), jnp.sin(ar), -jnp.sin(ac), jnp.sin(ac)], axis=-1)
    return cos, sin


_NT = (((1,), (1,)), ((), ()))


def _softmax_pv(scores, values):
    m = scores[0].max(axis=-1, keepdims=True)
    for s in scores[1:]:
        m = jnp.maximum(m, s.max(axis=-1, keepdims=True))
    l = None
    acc = None
    for s, v in zip(scores, values):
        p = jnp.exp2(s - m)
        ps = p.sum(axis=-1, keepdims=True)
        pv = jnp.dot(p.astype(BF16), v, preferred_element_type=F32)
        l = ps if l is None else l + ps
        acc = pv if acc is None else acc + pv
    return acc * (1.0 / l)


def _dense_attn_kernel(*refs, r_heads, has_ctx):
    if has_ctx:
        q_ref, k_ref, v_ref, kc_ref, vc_ref, o_ref = refs
        kc = kc_ref[...].astype(BF16)
        vc = vc_ref[...].astype(BF16)
    else:
        q_ref, k_ref, v_ref, o_ref = refs
    k = k_ref[...]
    v = v_ref[...]
    for r in range(r_heads):
        sl = slice(r * HEAD_DIM, (r + 1) * HEAD_DIM)
        q = q_ref[:, sl]
        scores = [lax.dot_general(q, k, _NT, preferred_element_type=F32)]
        values = [v]
        if has_ctx:
            scores.append(lax.dot_general(q, kc, _NT, preferred_element_type=F32))
            values.append(vc)
        o_ref[:, sl] = _softmax_pv(scores, values).astype(BF16)


def _dense_attention(proj, q_col, k_col, v_col, groups, r_heads, ctx=None):
    b, t, _ = proj.shape
    tq = min(ATTN_TQ, t)
    qw = r_heads * HEAD_DIM
    assert q_col % r_heads == 0
    in_specs = [
        pl.BlockSpec((None, tq, qw), lambda bi, g, qi: (bi, qi, q_col // r_heads + g)),
        pl.BlockSpec((None, t, HEAD_DIM), lambda bi, g, qi: (bi, 0, k_col + g)),
        pl.BlockSpec((None, t, HEAD_DIM), lambda bi, g, qi: (bi, 0, v_col + g)),
    ]
    args = [proj, proj, proj]
    if ctx is not None:
        ck, cv, layer = ctx
        l_ctx = ck.shape[2]
        spec = pl.BlockSpec((None, None, l_ctx, HEAD_DIM), lambda bi, g, qi: (bi, layer, 0, g))
        in_specs += [spec, spec]
        args += [ck, cv]
    return pl.pallas_call(
        functools.partial(_dense_attn_kernel, r_heads=r_heads, has_ctx=ctx is not None),
        grid=(b, groups, t // tq),
        in_specs=in_specs,
        out_specs=pl.BlockSpec((None, tq, qw), lambda bi, g, qi: (bi, qi, g)),
        out_shape=jax.ShapeDtypeStruct((b, t, groups * qw), BF16),
        compiler_params=_cparams(("arbitrary", "arbitrary", "arbitrary")),
        name="dense_attn_ctx" if ctx is not None else "dense_attn",
    )(*args)


def _na_plan(rows):
    kh = min(NA_WIN_H, rows)
    kw = NA_WIN_W
    r_blk = min(NA_ROWS, rows)
    assert rows % r_blk == 0
    slab = min(r_blk - 1 + kh, rows)
    row_start = np.clip(np.arange(rows) - kh // 2, 0, rows - kh)
    col = np.arange(GRID_W)
    col_start = np.clip(col - kw // 2, 0, GRID_W - kw)
    slab_start, pat_id, pats, sigs = [], [], [], {}
    for r0 in range(0, rows, r_blk):
        ss = min(row_start[r0], rows - slab)
        rel = tuple(int(row_start[r0 + ri] - ss) for ri in range(r_blk))
        sig = (int(ss - r0), rel)
        if sig not in sigs:
            sigs[sig] = len(pats)
            q_row = r0 + np.arange(r_blk)[:, None, None, None]
            q_col = col[None, :, None, None]
            k_row = ss + np.arange(slab)[None, None, :, None]
            k_col = col[None, None, None, :]
            rs = row_start[r0:r0 + r_blk][:, None, None, None]
            cs = col_start[None, :, None, None]
            valid = (k_row >= rs) & (k_row < rs + kh) & (k_col >= cs) & (k_col < cs + kw)
            row_off = (k_row - q_row + (NA_WIN_H - 1))[:, 0, :, 0]
            row_sel = (row_off[:, :, None] == np.arange(2 * NA_WIN_H - 1)).astype(np.float32)
            full = (r_blk, GRID_W, slab, GRID_W)
            pats.append((np.broadcast_to(valid, full).reshape(r_blk * GRID_W, slab * GRID_W), row_sel))
        slab_start.append(int(ss))
        pat_id.append(sigs[sig])
    return r_blk, slab, np.array(slab_start, np.int32), np.array(pat_id, np.int32), pats


def _na_bias_table(rel_bias_l, pats):
    col = np.arange(GRID_W)
    col_off = col[None, :] - col[:, None] + (NA_WIN_W - 1)
    col_sel = (col_off[:, :, None] == np.arange(2 * NA_WIN_W - 1)).astype(np.float32)
    hi = lax.Precision.HIGHEST
    tabs = []
    for valid, row_sel in pats:
        by_row = jnp.einsum('hrc,isr->hisc', rel_bias_l.astype(F32), row_sel, precision=hi)
        bias = jnp.einsum('hisc,qkc->hiqsk', by_row, col_sel, precision=hi)
        bias = bias.reshape((bias.shape[0],) + valid.shape) * LOG2E
        tabs.append(jnp.where(valid[None], bias, MASK_VALUE))
    return jnp.stack(tabs, axis=0)


def _na_kernel(ss_ref, pat_ref, q_ref, k_ref, v_ref, kc_ref, vc_ref, bias_ref, o_ref, *, slab_len, n_heads):
    del pat_ref
    rb = pl.program_id(2)
    start = pl.multiple_of(ss_ref[rb] * GRID_W, GRID_W)
    for h in range(n_heads):
        sl = slice(h * HEAD_DIM, (h + 1) * HEAD_DIM)
        ks = k_ref[pl.ds(start, slab_len), sl]
        vs = v_ref[pl.ds(start, slab_len), sl]
        kc = kc_ref[:, sl].astype(BF16)
        vc = vc_ref[:, sl].astype(BF16)
        q = q_ref[:, sl]
        s_loc = lax.dot_general(q, ks, _NT, preferred_element_type=F32) + bias_ref[h]
        s_ctx = lax.dot_general(q, kc, _NT, preferred_element_type=F32)
        o_ref[:, sl] = _softmax_pv([s_loc, s_ctx], [vs, vc]).astype(BF16)


def _neighborhood_attention(proj, ck, cv, layer, bias_tab, plan):
    b, t, _ = proj.shape
    r_blk, slab, slab_start, pat_id, _ = plan
    qn = r_blk * GRID_W
    sn = slab * GRID_W
    l_ctx = ck.shape[2]
    hb = NA_HEADS_PER_STEP
    hw = hb * HEAD_DIM
    assert COL_QN % hb == 0 and COL_KN % hb == 0 and COL_VN % hb == 0 and N_HEADS_NA % hb == 0
    grid_spec = pltpu.PrefetchScalarGridSpec(
        num_scalar_prefetch=2,
        grid=(b, N_HEADS_NA // hb, t // qn),
        in_specs=[
            pl.BlockSpec((None, qn, hw), lambda bi, hg, rb, ss, pt: (bi, rb, COL_QN // hb + hg)),
            pl.BlockSpec((None, t, hw), lambda bi, hg, rb, ss, pt: (bi, 0, COL_KN // hb + hg)),
            pl.BlockSpec((None, t, hw), lambda bi, hg, rb, ss, pt: (bi, 0, COL_VN // hb + hg)),
            pl.BlockSpec((None, None, l_ctx, hw), lambda bi, hg, rb, ss, pt: (bi, layer, 0, hg)),
            pl.BlockSpec((None, None, l_ctx, hw), lambda bi, hg, rb, ss, pt: (bi, layer, 0, hg)),
            pl.BlockSpec((None, hb, qn, sn), lambda bi, hg, rb, ss, pt: (pt[rb], hg, 0, 0)),
        ],
        out_specs=pl.BlockSpec((None, qn, hw), lambda bi, hg, rb, ss, pt: (bi, rb, hg)),
    )
    return pl.pallas_call(
        functools.partial(_na_kernel, slab_len=sn, n_heads=hb),
        grid_spec=grid_spec,
        out_shape=jax.ShapeDtypeStruct((b, t, NA_W), BF16),
        compiler_params=_cparams(("arbitrary", "arbitrary", "arbitrary")),
        name="neighborhood_attn",
    )(jnp.asarray(slab_start), jnp.asarray(pat_id), proj, proj, proj, ck, cv, bias_tab)


def _layer_norm(y, g, b):
    mu = jnp.mean(y, axis=-1, keepdims=True)
    yc = y - mu
    var = jnp.mean(yc * yc, axis=-1, keepdims=True)
    return yc * lax.rsqrt(var + EPS) * g + b


def _top2_of4(vals):
    m1 = jnp.maximum(jnp.maximum(vals[0], vals[1]), jnp.maximum(vals[2], vals[3]))
    i1 = jnp.where(vals[0] == m1, 0, jnp.where(vals[1] == m1, 1, jnp.where(vals[2] == m1, 2, 3)))
    rest = [jnp.where(i1 == i, -1.0, vals[i]) for i in range(4)]
    m2 = jnp.maximum(jnp.maximum(rest[0], rest[1]), jnp.maximum(rest[2], rest[3]))
    i2 = jnp.where(rest[0] == m2, 0, jnp.where(rest[1] == m2, 1, jnp.where(rest[2] == m2, 2, 3)))
    return m1, i1, m2, i2


def _route_rows(logits_t):
    m = logits_t.max(axis=0, keepdims=True)
    e = jnp.exp(logits_t - m)
    probs = e / e.sum(axis=0, keepdims=True)
    rows = [probs[i:i + 1, :] for i in range(N_EXPERTS)]
    groups = [rows[g * EXPERTS_PER_GROUP:(g + 1) * EXPERTS_PER_GROUP] for g in range(N_GROUPS)]
    scores = []
    for g in range(N_GROUPS):
        m1, _, m2, _ = _top2_of4(groups[g])
        scores.append(m1 + m2)
    best = jnp.maximum(jnp.maximum(scores[0], scores[1]), jnp.maximum(scores[2], scores[3]))
    gi = jnp.where(scores[0] == best, 0, jnp.where(scores[1] == best, 1, jnp.where(scores[2] == best, 2, 3)))
    sel = [jnp.where(gi == 0, groups[0][i], jnp.where(gi == 1, groups[1][i],
                     jnp.where(gi == 2, groups[2][i], groups[3][i]))) for i in range(EXPERTS_PER_GROUP)]
    w1, l1, w2, l2 = _top2_of4(sel)
    wsum = w1 + w2
    return (gi * EXPERTS_PER_GROUP + l1, gi * EXPERTS_PER_GROUP + l2, w1 / wsum, w2 / wsum)


def _out_proj_kernel(oa_ref, on_ref, x_ref, mod_ref, wo_ref, g_ref, b_ref, wr_ref, br_ref,
                     x1_ref, h2_ref, e_ref, wt_ref):
    attn = jnp.dot(oa_ref[...], wo_ref[0:GA_Q, :], preferred_element_type=F32)
    attn = attn + jnp.dot(on_ref[...], wo_ref[GA_Q:GA_Q + NA_W, :], preferred_element_type=F32)
    gate1 = mod_ref[2:3, :]
    x1 = _layer_norm(DN_ALPHA * x_ref[...] + gate1 * attn, g_ref[...], b_ref[...])
    x1_ref[...] = x1
    h2 = x1 * (1.0 + mod_ref[4:5, :]) + mod_ref[3:4, :]
    h2_ref[...] = h2
    h_hi = h2.astype(BF16)
    h_lo = (h2 - h_hi.astype(F32)).astype(BF16)
    r_hi = jnp.dot(h_hi, wr_ref[...], preferred_element_type=F32)
    r_lo = jnp.dot(h_lo, wr_ref[...], preferred_element_type=F32)
    logits = r_hi + pltpu.roll(r_hi, LANE - N_EXPERTS, 1) + r_lo
    logits_t = logits.T[0:N_EXPERTS, :] + br_ref[...]
    e1, e2, w1, w2 = _route_rows(logits_t)
    e_ref[0:1, :] = e1
    e_ref[1:2, :] = e2
    n = w1.shape[1]
    row = lax.broadcasted_iota(jnp.int32, (LANE, n), 0)
    w_rows = jnp.where(row == 0, w1, jnp.where(row == 1, w2, 0.0))
    wt_ref[...] = w_rows.T


def _out_proj(oa, on, x, mod_l, mod_row0, w_o_bf, w_layer, ln_g, ln_b, wr_cat, b_router):
    b, t, d = x.shape
    tm = min(OUT_TM, t)
    tok = lambda bi, ti: (bi, ti, 0)
    const2 = lambda bi, ti: (0, 0)
    return pl.pallas_call(
        _out_proj_kernel,
        grid=(b, t // tm),
        in_specs=[
            pl.BlockSpec((None, tm, GA_Q), tok),
            pl.BlockSpec((None, tm, NA_W), tok),
            pl.BlockSpec((None, tm, d), tok),
            pl.BlockSpec((None, N_MOD, d), lambda bi, ti: (bi + mod_row0, 0, 0)),
            pl.BlockSpec((None, GA_Q + NA_W, d), lambda bi, ti: (w_layer, 0, 0)),
            pl.BlockSpec((1, d), const2),
            pl.BlockSpec((1, d), const2),
            pl.BlockSpec((d, LANE), const2),
            pl.BlockSpec((N_EXPERTS, 1), const2),
        ],
        out_specs=[
            pl.BlockSpec((None, tm, d), tok),
            pl.BlockSpec((None, tm, d), tok),
            pl.BlockSpec((None, TOP_K, tm), lambda bi, ti: (bi, 0, ti)),
            pl.BlockSpec((None, tm, LANE), tok),
        ],
        out_shape=[
            jax.ShapeDtypeStruct((b, t, d), F32),
            jax.ShapeDtypeStruct((b, t, d), F32),
            jax.ShapeDtypeStruct((b, TOP_K, t), jnp.int32),
            jax.ShapeDtypeStruct((b, t, LANE), F32),
        ],
        compiler_params=_cparams(("arbitrary", "arbitrary")),
        name="out_proj_ln_router",
    )(oa, on, x, mod_l, w_o_bf, ln_g.reshape(1, d), ln_b.reshape(1, d), wr_cat,
      b_router.reshape(N_EXPERTS, 1).astype(F32))


def _row_copy(src_hbm, idx_ref, dst, sem, r):
    return pltpu.make_async_copy(src_hbm.at[pl.ds(idx_ref[0, r], 1)], dst.at[pl.ds(r, 1)], sem)


def _start_row_gather(src_hbm, idx_ref, dst, sem, n_rows):
    def body(r, carry):
        _row_copy(src_hbm, idx_ref, dst, sem, r).start()
        return carry
    lax.fori_loop(0, n_rows, body, 0, unroll=8)


def _start_row_gather_inline(src_hbm, idx_ref, dst, sem, r_lo, r_hi):
    for r in range(r_lo, r_hi):
        _row_copy(src_hbm, idx_ref, dst, sem, r).start()


def _wait_row_gather(src_hbm, dst, sem, n_rows):
    pltpu.make_async_copy(src_hbm.at[pl.ds(0, n_rows)], dst, sem).wait()


def _moe_kernel(be_ref, nu_ref, tok0_ref, tok1_ref, tok2_ref, h_hbm, wg_ref, wu_ref, wd_ref, o_ref,
                buf, sem, *, n_blocks):
    del be_ref
    i = pl.program_id(0)
    n_used = nu_ref[0]
    slot = i % GATHER_DEPTH
    ahead = (i + 2) % GATHER_DEPTH

    @pl.when(i == 0)
    def _():
        _start_row_gather(h_hbm, tok0_ref, buf.at[0], sem.at[0], MOE_BLK)
        _start_row_gather(h_hbm, tok1_ref, buf.at[1], sem.at[1], MOE_BLK)

    _wait_row_gather(h_hbm, buf.at[slot], sem.at[slot], MOE_BLK)

    @pl.when(i < n_used)
    def _():
        xb = buf[slot].astype(BF16)
        gate = jnp.dot(xb, wg_ref[...], preferred_element_type=F32)
        up = jnp.dot(xb, wu_ref[...], preferred_element_type=F32)
        act = (gate / (1.0 + jnp.exp(-gate)) * up).astype(BF16)
        _start_row_gather_inline(h_hbm, tok2_ref, buf.at[ahead], sem.at[ahead], 0, MOE_BLK)
        o_ref[...] = jnp.dot(act, wd_ref[...], preferred_element_type=F32)

    @pl.when(i >= n_used)
    def _():
        o_ref[...] = jnp.zeros_like(o_ref)
        _start_row_gather(h_hbm, tok2_ref, buf.at[ahead], sem.at[ahead], MOE_BLK)

    @pl.when(i == n_blocks - 1)
    def _():
        for k in (1, 2):
            s = (i + k) % GATHER_DEPTH
            _wait_row_gather(h_hbm, buf.at[s], sem.at[s], MOE_BLK)


def _moe_ffn(h2_flat, tok_buf, blk_e, n_used, wg, wu, wd, layer):
    n, d = h2_flat.shape
    n_blocks = blk_e.shape[0]
    d_ff = wg.shape[3]
    tok3 = tok_buf.reshape(n_blocks, 1, MOE_BLK)
    smem_blk = lambda f: pl.BlockSpec((None, 1, MOE_BLK), f, memory_space=pltpu.SMEM)
    grid_spec = pltpu.PrefetchScalarGridSpec(
        num_scalar_prefetch=2,
        grid=(n_blocks,),
        in_specs=[
            smem_blk(lambda i, be, nu: (i, 0, 0)),
            smem_blk(lambda i, be, nu: (jnp.minimum(i + 1, n_blocks - 1), 0, 0)),
            smem_blk(lambda i, be, nu: (jnp.minimum(i + 2, n_blocks - 1), 0, 0)),
            pl.BlockSpec(memory_space=pl.ANY),
            pl.BlockSpec((None, None, d, d_ff), lambda i, be, nu: (layer, be[i], 0, 0)),
            pl.BlockSpec((None, None, d, d_ff), lambda i, be, nu: (layer, be[i], 0, 0)),
            pl.BlockSpec((None, None, d_ff, d), lambda i, be, nu: (layer, be[i], 0, 0)),
        ],
        out_specs=pl.BlockSpec((MOE_BLK, d), lambda i, be, nu: (i, 0)),
        scratch_shapes=[pltpu.VMEM((GATHER_DEPTH, MOE_BLK, d), F32), pltpu.SemaphoreType.DMA((GATHER_DEPTH,))],
    )
    return pl.pallas_call(
        functools.partial(_moe_kernel, n_blocks=n_blocks),
        grid_spec=grid_spec,
        out_shape=jax.ShapeDtypeStruct((n_blocks * MOE_BLK, d), F32),
        compiler_params=_cparams(("arbitrary",)),
        name="moe_ffn",
    )(blk_e, n_used, tok3, tok3, tok3, h2_flat, wg, wu, wd)


def _combine_kernel(da0_ref, da1_ref, db0_ref, db1_ref, dc0_ref, dc1_ref, y_hbm, x1_ref, mod_ref, wt_ref,
                    g_ref, b_ref, o_ref, buf, sem, *, n_steps, tm):
    i = pl.program_id(0)
    slot = i % GATHER_DEPTH
    ahead = (i + 2) % GATHER_DEPTH

    @pl.when(i == 0)
    def _():
        for s, (r0, r1) in enumerate(((da0_ref, da1_ref), (db0_ref, db1_ref))):
            _start_row_gather(y_hbm, r0, buf.at[s, 0], sem.at[s], tm)
            _start_row_gather(y_hbm, r1, buf.at[s, 1], sem.at[s], tm)

    def wait(s):
        _wait_row_gather(y_hbm, buf.at[s, 0], sem.at[s], tm)
        _wait_row_gather(y_hbm, buf.at[s, 1], sem.at[s], tm)

    wait(slot)
    y = wt_ref[:, 0:1] * buf[slot, 0] + wt_ref[:, 1:2] * buf[slot, 1]
    gate2 = mod_ref[5:6, :]
    o_ref[...] = _layer_norm(DN_ALPHA * x1_ref[...] + gate2 * y, g_ref[...], b_ref[...])
    _start_row_gather_inline(y_hbm, dc0_ref, buf.at[ahead, 0], sem.at[ahead], 0, tm)
    _start_row_gather_inline(y_hbm, dc1_ref, buf.at[ahead, 1], sem.at[ahead], 0, tm)

    @pl.when(i == n_steps - 1)
    def _():
        for k in (1, 2):
            wait((i + k) % GATHER_DEPTH)


def _combine(y_sorted, dest, x1, mod_l, mod_row0, wt, ln_g, ln_b):
    b, t, d = x1.shape
    tm = min(OUT_TM, t)
    nt = t // tm
    n_steps = b * nt
    dest4 = dest.reshape(b, TOP_K, nt, tm).transpose(0, 2, 1, 3).reshape(n_steps, TOP_K, 1, tm)
    later = lambda i, k: jnp.minimum(i + k, n_steps - 1)
    smem_blk = lambda f: pl.BlockSpec((None, None, 1, tm), f, memory_space=pltpu.SMEM)
    tok = lambda i: (i // nt, i % nt, 0)
    return pl.pallas_call(
        functools.partial(_combine_kernel, n_steps=n_steps, tm=tm),
        grid=(n_steps,),
        in_specs=[
            smem_blk(lambda i: (i, 0, 0, 0)),
            smem_blk(lambda i: (i, 1, 0, 0)),
            smem_blk(lambda i: (later(i, 1), 0, 0, 0)),
            smem_blk(lambda i: (later(i, 1), 1, 0, 0)),
            smem_blk(lambda i: (later(i, 2), 0, 0, 0)),
            smem_blk(lambda i: (later(i, 2), 1, 0, 0)),
            pl.BlockSpec(memory_space=pl.ANY),
            pl.BlockSpec((None, tm, d), tok),
            pl.BlockSpec((None, N_MOD, d), lambda i: (i // nt + mod_row0, 0, 0)),
            pl.BlockSpec((None, tm, LANE), tok),
            pl.BlockSpec((1, d), lambda i: (0, 0)),
            pl.BlockSpec((1, d), lambda i: (0, 0)),
        ],
        out_specs=pl.BlockSpec((None, tm, d), tok),
        out_shape=jax.ShapeDtypeStruct((b, t, d), F32),
        scratch_shapes=[pltpu.VMEM((GATHER_DEPTH, TOP_K, tm, d), F32), pltpu.SemaphoreType.DMA((GATHER_DEPTH,))],
        compiler_params=_cparams(("arbitrary",)),
        name="moe_combine_ln",
    )(dest4, dest4, dest4, dest4, dest4, dest4, y_sorted, x1, mod_l, wt, ln_g.reshape(1, d), ln_b.reshape(1, d))


def _dispatch_plan(e_idx):
    b, k, t = e_idx.shape
    n = b * t
    a = n * k
    flat_e = e_idx.reshape(a)
    src_row = (jnp.arange(b, dtype=jnp.int32)[:, None, None] * t
               + jnp.zeros((1, k, 1), jnp.int32)
               + jnp.arange(t, dtype=jnp.int32)[None, None, :]).reshape(a)
    onehot = (flat_e[:, None] == jnp.arange(N_EXPERTS, dtype=jnp.int32)[None, :]).astype(jnp.int32)
    counts = onehot.sum(axis=0)
    rank = (jnp.cumsum(onehot, axis=0) * onehot).sum(axis=1) - 1
    padded = (counts + MOE_BLK - 1) // MOE_BLK * MOE_BLK
    pad_end = jnp.cumsum(padded)
    pad_start = pad_end - padded
    dest = pad_start[flat_e] + rank
    n_blocks = -(-a // MOE_BLK) + N_EXPERTS
    p = n_blocks * MOE_BLK
    tok_buf = jnp.zeros((p,), jnp.int32).at[dest].set(src_row)
    blk_start = jnp.arange(n_blocks, dtype=jnp.int32) * MOE_BLK
    blk_e = jnp.minimum(jnp.searchsorted(pad_end, blk_start, side='right'), N_EXPERTS - 1).astype(jnp.int32)
    n_used = (pad_end[-1] // MOE_BLK).astype(jnp.int32)
    last_e = blk_e[jnp.maximum(n_used - 1, 0)]
    blk_e = jnp.where(jnp.arange(n_blocks) < n_used, blk_e, last_e)
    return dest.reshape(b, k, t), tok_buf, blk_e, n_used.reshape(1)


def _layer(x, mod_l, mod_row0, lw, shared, ctx):
    b, t, d = x.shape
    is_ctx = ctx is None
    if is_ctx:
        xt = x.reshape(1, b * t, d)
        proj, *cache = _in_proj(xt, mod_l, mod_row0, lw['w_in'], lw['layer'], lw['q_gain'], lw['k_gain'],
                                cache=(lw['layer'], lw['depth'], b, t, lw['prev_cache']))
        proj = proj.reshape(b, t, -1)
        oa = _dense_attention(proj, COL_QA, COL_KA, COL_VA, N_KV_GA, N_HEADS_GA // N_KV_GA)
        on = _dense_attention(proj, COL_QN, COL_KN, COL_VN, N_HEADS_NA, 1)
        oa = oa.reshape(1, b * t, -1)
        on = on.reshape(1, b * t, -1)
    else:
        ga_k, ga_v, na_k, na_v, layer, rope_tabs, na_bias, na_plan = ctx
        (proj,) = _in_proj(x, mod_l, mod_row0, lw['w_in'], lw['layer'], lw['q_gain'], lw['k_gain'],
                           rope_tabs=rope_tabs)
        cache = None
        oa = _dense_attention(proj, COL_QA, COL_KA, COL_VA, N_KV_GA, N_HEADS_GA // N_KV_GA,
                              ctx=(ga_k, ga_v, layer))
        on = _neighborhood_attention(proj, na_k, na_v, layer, na_bias, na_plan)
        xt = x
    x1, h2, e_idx, wt = _out_proj(oa, on, xt, mod_l, mod_row0, lw['w_o'], lw['layer'], lw['ln1_g'], lw['ln1_b'],
                                  shared['wr_cat'], shared['b_router'])
    dest, tok_buf, blk_e, n_used = _dispatch_plan(e_idx)
    y_sorted = _moe_ffn(h2.reshape(-1, d), tok_buf, blk_e, n_used, lw['w_gate'], lw['w_up'], lw['w_down'],
                        lw['layer'])
    out = _combine(y_sorted, dest, x1, mod_l, mod_row0, wt, lw['ln2_g'], lw['ln2_b'])
    return out.reshape(b, t, d), cache


def kernel(x_prompt, x_sample, c, cache_ga_k, cache_ga_v, cache_na_k, cache_na_v, c_ctx, w_router, b_router, w_mod, b_mod, w_in, q_norm, k_norm, rel_bias, w_o, ln1_g, ln1_b, ln2_g, ln2_b, w_gate, w_up, w_down):
    bsz, seq, d = x_prompt.shape
    dec_b, dec_t, _ = x_sample.shape
    depth = w_mod.shape[0]
    past = cache_ga_k.shape[2]

    mod_rows = 16
    assert dec_b + 1 <= mod_rows
    cvec = jnp.concatenate([c, c_ctx[None, :], jnp.zeros((mod_rows - dec_b - 1, d), F32)], axis=0)
    mod = _modulation(cvec, w_mod, b_mod).reshape(depth, mod_rows, N_MOD, d)

    wr_hi = w_router.astype(BF16)
    wr_lo = (w_router - wr_hi.astype(F32)).astype(BF16)
    wr_cat = jnp.concatenate([wr_hi, wr_lo, jnp.zeros((d, LANE - 2 * N_EXPERTS), BF16)], axis=1)
    shared = {'wr_cat': wr_cat, 'b_router': b_router}

    rope_tabs = _rope_tables(dec_t)
    na_plan = _na_plan(dec_t // GRID_W)
    ga_k = cache_ga_k.reshape(dec_b, depth, past, GA_KV)
    ga_v = cache_ga_v.reshape(dec_b, depth, past, GA_KV)
    na_k = cache_na_k.reshape(dec_b, depth, past, NA_W)
    na_v = cache_na_v.reshape(dec_b, depth, past, NA_W)

    w_in_bf, w_o_bf = w_in.astype(BF16), w_o.astype(BF16)
    w_gate_bf, w_up_bf, w_down_bf = w_gate.astype(BF16), w_up.astype(BF16), w_down.astype(BF16)

    y_prompt, y_sample = x_prompt, x_sample
    cache = [jnp.zeros((bsz, depth, seq, w), F32) for w in (GA_KV, GA_KV, NA_W, NA_W)]
    for i in range(depth):
        lw = {
            'layer': i, 'depth': depth, 'prev_cache': cache,
            'w_in': w_in_bf, 'w_o': w_o_bf,
            'q_gain': (q_norm[i] * Q_PRESCALE).reshape(1, HEAD_DIM).astype(F32),
            'k_gain': k_norm[i].reshape(1, HEAD_DIM).astype(F32),
            'ln1_g': ln1_g[i], 'ln1_b': ln1_b[i], 'ln2_g': ln2_g[i], 'ln2_b': ln2_b[i],
            'w_gate': w_gate_bf, 'w_up': w_up_bf, 'w_down': w_down_bf,
        }
        na_bias = _na_bias_table(rel_bias[i], na_plan[4])
        y_prompt, cache = _layer(y_prompt, mod[i], dec_b, lw, shared, None)
        y_sample, _ = _layer(y_sample, mod[i], 0, lw, shared,
                             (ga_k, ga_v, na_k, na_v, i, rope_tabs, na_bias, na_plan))

    outs = [arr.reshape(bsz, depth, seq, -1, HEAD_DIM) for arr in cache]
    return (y_prompt, y_sample, outs[0], outs[1], outs[2], outs[3])
```

```python
import functools
import math

import numpy as np
import jax
import jax.numpy as jnp
from jax import lax
from jax.experimental import pallas as pl
from jax.experimental.pallas import tpu as pltpu

F32 = jnp.float32
BF16 = jnp.bfloat16

DEPTH = 2
GRID_W = 64
HEAD_DIM = 128
N_HEADS_GA = 8
N_KV_GA = 2
N_HEADS_NA = 8
NA_WIN_H = 8
NA_WIN_W = 16
ROPE_THETA = 10000.0
N_EXPERTS = 16
N_GROUPS = 4
EXPERTS_PER_GROUP = N_EXPERTS // N_GROUPS
TOP_K = 2
N_MOD = 6
DN_ALPHA = (2.0 * DEPTH) ** 0.25
EPS = 1e-6
ATTN_SCALE = HEAD_DIM ** -0.5
LOG2E = math.log2(math.e)
Q_PRESCALE = ATTN_SCALE * LOG2E
MASK_VALUE = -1e30

GA_Q = N_HEADS_GA * HEAD_DIM
GA_KV = N_KV_GA * HEAD_DIM
NA_W = N_HEADS_NA * HEAD_DIM
COL_QA = 0
COL_KA = COL_QA + N_HEADS_GA
COL_VA = COL_KA + N_KV_GA
COL_QN = COL_VA + N_KV_GA
COL_KN = COL_QN + N_HEADS_NA
COL_VN = COL_KN + N_HEADS_NA
IN_HEADS = COL_VN + N_HEADS_NA

LANE = 128
PROJ_TN = 4 * HEAD_DIM
PROJ_TM = 1024
ATTN_TQ = 256
NA_ROWS = 4
NA_HEADS_PER_STEP = 4
OUT_TM = 256
MOE_BLK = 256
GATHER_DEPTH = 3
MOD_TN = 1024
VMEM_LIMIT = 52 * 1024 * 1024


def _cparams(sem):
    return pltpu.CompilerParams(dimension_semantics=sem, vmem_limit_bytes=VMEM_LIMIT)


def _mod_kernel(c_ref, w_ref, b_ref, o_ref):
    c = c_ref[...]
    s = c / (1.0 + jnp.exp(-c))
    o_ref[...] = jnp.dot(s.astype(BF16), w_ref[...].astype(BF16),
                         preferred_element_type=F32) + b_ref[...]


def _modulation(cvec, w_mod, b_mod):
    depth, d, e = w_mod.shape
    rows = cvec.shape[0]
    tn = MOD_TN if e % MOD_TN == 0 else e
    return pl.pallas_call(
        _mod_kernel,
        grid=(depth, e // tn),
        in_specs=[
            pl.BlockSpec((rows, d), lambda l, j: (0, 0)),
            pl.BlockSpec((None, d, tn), lambda l, j: (l, 0, j)),
            pl.BlockSpec((None, 1, tn), lambda l, j: (l, 0, j)),
        ],
        out_specs=pl.BlockSpec((None, rows, tn), lambda l, j: (l, 0, j)),
        out_shape=jax.ShapeDtypeStruct((depth, rows, e), F32),
        compiler_params=_cparams(("arbitrary", "arbitrary")),
        name="modulation",
    )(cvec, w_mod, b_mod.reshape(depth, 1, e))


def _rms_head(x, g):
    ms = jnp.mean(x * x, axis=-1, keepdims=True)
    return x * lax.rsqrt(ms + EPS) * g


def _rope_head(x, cos, sin, first_half):
    swapped = jnp.where(first_half, pltpu.roll(x, HEAD_DIM - 32, 1), pltpu.roll(x, 32, 1))
    return x * cos + swapped * sin


def _proj_kernel(*refs, rope, cache):
    x_ref, mod_ref, w_ref, qg_ref, kg_ref = refs[:5]
    pos = 5
    if rope:
        cos_ref, sin_ref = refs[pos:pos + 2]
        pos += 2
    if cache is not None:
        pos += cache[0]
    o_ref = refs[pos]
    pos += 1
    if cache is not None:
        ka_ref, va_ref, kn_ref, vn_ref = refs[pos:pos + 4]
        pos += 4
    h_scr = refs[pos]

    j = pl.program_id(2)

    @pl.when(j == 0)
    def _():
        shift = mod_ref[0:1, :]
        scale = mod_ref[1:2, :]
        h_scr[...] = (x_ref[...] * (1.0 + scale) + shift).astype(BF16)

    acc = jnp.dot(h_scr[...], w_ref[...], preferred_element_type=F32)

    if rope:
        lane = lax.broadcasted_iota(jnp.int32, (1, HEAD_DIM), 1)
        first_half = (lane % 64) < 32

    def head(hh):
        return acc[:, hh * HEAD_DIM:(hh + 1) * HEAD_DIM]

    def normed(hh, g):
        y = _rms_head(head(hh), g)
        if rope:
            y = _rope_head(y, cos_ref[...], sin_ref[...], first_half)
        return y

    def put(hh, y, c_ref=None, c_head=0):
        o_ref[:, hh * HEAD_DIM:(hh + 1) * HEAD_DIM] = y.astype(BF16)
        if c_ref is not None:
            c_ref[:, :, c_head * HEAD_DIM:(c_head + 1) * HEAD_DIM] = y.reshape(cache[1], cache[2], HEAD_DIM)

    hpt = PROJ_TN // HEAD_DIM
    j_ka, j_qn, j_kn, j_vn = COL_KA // hpt, COL_QN // hpt, COL_KN // hpt, COL_VN // hpt

    @pl.when(j < j_ka)
    def _():
        for hh in range(hpt):
            put(hh, normed(hh, qg_ref[...]))

    @pl.when(j == j_ka)
    def _():
        for hh in range(N_KV_GA):
            put(hh, normed(hh, kg_ref[...]), ka_ref if cache else None, hh)
        for hh in range(N_KV_GA):
            put(N_KV_GA + hh, head(N_KV_GA + hh), va_ref if cache else None, hh)

    @pl.when((j >= j_qn) & (j < j_kn))
    def _():
        o_ref[...] = (acc * Q_PRESCALE).astype(BF16)

    @pl.when((j >= j_kn) & (j < j_vn))
    def _():
        o_ref[...] = acc.astype(BF16)
        if cache is not None:
            kn_ref[...] = acc.reshape(cache[1], cache[2], PROJ_TN)

    @pl.when(j >= j_vn)
    def _():
        o_ref[...] = acc.astype(BF16)
        if cache is not None:
            vn_ref[...] = acc.reshape(cache[1], cache[2], PROJ_TN)


def _in_proj(x, mod_l, mod_row0, w_in_bf, w_layer, q_gain, k_gain, rope_tabs=None, cache=None):
    b, t, d = x.shape
    in_w = w_in_bf.shape[2]
    tm = min(PROJ_TM, t)
    nj = in_w // PROJ_TN
    rope = rope_tabs is not None
    hpt = PROJ_TN // HEAD_DIM
    j_kn, j_vn = COL_KN // hpt, COL_VN // hpt
    assert COL_KA % hpt == 0 and COL_QN % hpt == 0 and COL_KN % hpt == 0 and COL_VN % hpt == 0
    assert 2 * N_KV_GA == hpt and COL_VA == COL_KA + N_KV_GA

    in_specs = [
        pl.BlockSpec((None, tm, d), lambda bi, ti, j: (bi, ti, 0)),
        pl.BlockSpec((None, N_MOD, d), lambda bi, ti, j: (bi + mod_row0, 0, 0)),
        pl.BlockSpec((None, d, PROJ_TN), lambda bi, ti, j: (w_layer, 0, j)),
        pl.BlockSpec((1, HEAD_DIM), lambda bi, ti, j: (0, 0)),
        pl.BlockSpec((1, HEAD_DIM), lambda bi, ti, j: (0, 0)),
    ]
    args = [x, mod_l, w_in_bf, q_gain, k_gain]
    if rope:
        in_specs += [pl.BlockSpec((tm, HEAD_DIM), lambda bi, ti, j: (ti, 0))] * 2
        args += list(rope_tabs)
    out_specs = [pl.BlockSpec((None, tm, PROJ_TN), lambda bi, ti, j: (bi, ti, j))]
    out_shape = [jax.ShapeDtypeStruct((b, t, in_w), BF16)]
    aliases = {}
    cache_cfg = None
    if cache is not None:
        layer, depth, bsz, seq, prev = cache
        assert b == 1 and tm % seq == 0
        nb = tm // seq
        n_prev = 0 if prev is None else len(prev)
        cache_cfg = (n_prev, nb, seq)
        if prev is not None:
            for k, arr in enumerate(prev):
                aliases[len(args)] = 1 + k
                in_specs.append(pl.BlockSpec(memory_space=pl.ANY))
                args.append(arr)
        kv = lambda bi, ti, j: (ti, layer, 0, 0)
        kn = lambda bi, ti, j: (ti, layer, 0, jnp.clip(j - j_kn, 0, N_HEADS_NA // hpt - 1))
        vn = lambda bi, ti, j: (ti, layer, 0, jnp.clip(j - j_vn, 0, N_HEADS_NA // hpt - 1))
        out_specs += [pl.BlockSpec((nb, None, seq, GA_KV), kv), pl.BlockSpec((nb, None, seq, GA_KV), kv),
                      pl.BlockSpec((nb, None, seq, PROJ_TN), kn), pl.BlockSpec((nb, None, seq, PROJ_TN), vn)]
        out_shape += [jax.ShapeDtypeStruct((bsz, depth, seq, w), F32) for w in (GA_KV, GA_KV, NA_W, NA_W)]
    outs = pl.pallas_call(
        functools.partial(_proj_kernel, rope=rope, cache=cache_cfg),
        grid=(b, t // tm, nj),
        in_specs=in_specs,
        out_specs=out_specs,
        out_shape=out_shape,
        input_output_aliases=aliases,
        scratch_shapes=[pltpu.VMEM((tm, d), BF16)],
        compiler_params=_cparams(("arbitrary", "arbitrary", "arbitrary")),
        name="in_proj_rope" if rope else "in_proj_cache",
    )(*args)
    return outs


def _rope_tables(t):
    half = HEAD_DIM // 4
    tt = jnp.arange(t, dtype=jnp.int32)
    row = (tt // GRID_W).astype(F32)
    col = (tt % GRID_W).astype(F32)
    inv_freq = 1.0 / (ROPE_THETA ** (jnp.arange(half, dtype=F32) / half))
    ar = row[:, None] * inv_freq[None, :]
    ac = col[:, None] * inv_freq[None, :]
    cos = jnp.concatenate([jnp.cos(ar), jnp.cos(ar), jnp.cos(ac), jnp.cos(ac)], axis=-1)
    sin = jnp.concatenate([-jnp.sin(ar), jnp.sin(ar), -jnp.sin(ac), jnp.sin(ac)], axis=-1)
    return cos, sin


_NT = (((1,), (1,)), ((), ()))


def _softmax_pv(scores, values):
    m = scores[0].max(axis=-1, keepdims=True)
    for s in scores[1:]:
        m = jnp.maximum(m, s.max(axis=-1, keepdims=True))
    l = None
    acc = None
    for s, v in zip(scores, values):
        p = jnp.exp2(s - m)
        ps = p.sum(axis=-1, keepdims=True)
        pv = jnp.dot(p.astype(BF16), v, preferred_element_type=F32)
        l = ps if l is None else l + ps
        acc = pv if acc is None else acc + pv
    return acc * (1.0 / l)


def _dense_attn_kernel(*refs, r_heads, has_ctx):
    if has_ctx:
        q_ref, k_ref, v_ref, kc_ref, vc_ref, o_ref = refs
        kc = kc_ref[...].astype(BF16)
        vc = vc_ref[...].astype(BF16)
    else:
        q_ref, k_ref, v_ref, o_ref = refs
    k = k_ref[...]
    v = v_ref[...]
    for r in range(r_heads):
        sl = slice(r * HEAD_DIM, (r + 1) * HEAD_DIM)
        q = q_ref[:, sl]
        scores = [lax.dot_general(q, k, _NT, preferred_element_type=F32)]
        values = [v]
        if has_ctx:
            scores.append(lax.dot_general(q, kc, _NT, preferred_element_type=F32))
            values.append(vc)
        o_ref[:, sl] = _softmax_pv(scores, values).astype(BF16)


def _dense_attention(proj, q_col, k_col, v_col, groups, r_heads, ctx=None):
    b, t, _ = proj.shape
    tq = min(ATTN_TQ, t)
    qw = r_heads * HEAD_DIM
    assert q_col % r_heads == 0
    in_specs = [
        pl.BlockSpec((None, tq, qw), lambda bi, g, qi: (bi, qi, q_col // r_heads + g)),
        pl.BlockSpec((None, t, HEAD_DIM), lambda bi, g, qi: (bi, 0, k_col + g)),
        pl.BlockSpec((None, t, HEAD_DIM), lambda bi, g, qi: (bi, 0, v_col + g)),
    ]
    args = [proj, proj, proj]
    if ctx is not None:
        ck, cv, layer = ctx
        l_ctx = ck.shape[2]
        spec = pl.BlockSpec((None, None, l_ctx, HEAD_DIM), lambda bi, g, qi: (bi, layer, 0, g))
        in_specs += [spec, spec]
        args += [ck, cv]
    return pl.pallas_call(
        functools.partial(_dense_attn_kernel, r_heads=r_heads, has_ctx=ctx is not None),
        grid=(b, groups, t // tq),
        in_specs=in_specs,
        out_specs=pl.BlockSpec((None, tq, qw), lambda bi, g, qi: (bi, qi, g)),
        out_shape=jax.ShapeDtypeStruct((b, t, groups * qw), BF16),
        compiler_params=_cparams(("arbitrary", "arbitrary", "arbitrary")),
        name="dense_attn_ctx" if ctx is not None else "dense_attn",
    )(*args)


def _na_plan(rows):
    kh = min(NA_WIN_H, rows)
    kw = NA_WIN_W
    r_blk = min(NA_ROWS, rows)
    assert rows % r_blk == 0
    slab = min(r_blk - 1 + kh, rows)
    row_start = np.clip(np.arange(rows) - kh // 2, 0, rows - kh)
    col = np.arange(GRID_W)
    col_start = np.clip(col - kw // 2, 0, GRID_W - kw)
    slab_start, pat_id, pats, sigs = [], [], [], {}
    for r0 in range(0, rows, r_blk):
        ss = min(row_start[r0], rows - slab)
        rel = tuple(int(row_start[r0 + ri] - ss) for ri in range(r_blk))
        sig = (int(ss - r0), rel)
        if sig not in sigs:
            sigs[sig] = len(pats)
            q_row = r0 + np.arange(r_blk)[:, None, None, None]
            q_col = col[None, :, None, None]
            k_row = ss + np.arange(slab)[None, None, :, None]
            k_col = col[None, None, None, :]
            rs = row_start[r0:r0 + r_blk][:, None, None, None]
            cs = col_start[None, :, None, None]
            valid = (k_row >= rs) & (k_row < rs + kh) & (k_col >= cs) & (k_col < cs + kw)
            row_off = (k_row - q_row + (NA_WIN_H - 1))[:, 0, :, 0]
            row_sel = (row_off[:, :, None] == np.arange(2 * NA_WIN_H - 1)).astype(np.float32)
            full = (r_blk, GRID_W, slab, GRID_W)
            pats.append((np.broadcast_to(valid, full).reshape(r_blk * GRID_W, slab * GRID_W), row_sel))
        slab_start.append(int(ss))
        pat_id.append(sigs[sig])
    return r_blk, slab, np.array(slab_start, np.int32), np.array(pat_id, np.int32), pats


def _na_bias_table(rel_bias_l, pats):
    col = np.arange(GRID_W)
    col_off = col[None, :] - col[:, None] + (NA_WIN_W - 1)
    col_sel = (col_off[:, :, None] == np.arange(2 * NA_WIN_W - 1)).astype(np.float32)
    hi = lax.Precision.HIGHEST
    tabs = []
    for valid, row_sel in pats:
        by_row = jnp.einsum('hrc,isr->hisc', rel_bias_l.astype(F32), row_sel, precision=hi)
        bias = jnp.einsum('hisc,qkc->hiqsk', by_row, col_sel, precision=hi)
        bias = bias.reshape((bias.shape[0],) + valid.shape) * LOG2E
        tabs.append(jnp.where(valid[None], bias, MASK_VALUE))
    return jnp.stack(tabs, axis=0)


def _na_kernel(ss_ref, pat_ref, q_ref, k_ref, v_ref, kc_ref, vc_ref, bias_ref, o_ref, *, slab_len, n_heads):
    del pat_ref
    rb = pl.program_id(2)
    start = pl.multiple_of(ss_ref[rb] * GRID_W, GRID_W)
    for h in range(n_heads):
        sl = slice(h * HEAD_DIM, (h + 1) * HEAD_DIM)
        ks = k_ref[pl.ds(start, slab_len), sl]
        vs = v_ref[pl.ds(start, slab_len), sl]
        kc = kc_ref[:, sl].astype(BF16)
        vc = vc_ref[:, sl].astype(BF16)
        q = q_ref[:, sl]
        s_loc = lax.dot_general(q, ks, _NT, preferred_element_type=F32) + bias_ref[h]
        s_ctx = lax.dot_general(q, kc, _NT, preferred_element_type=F32)
        o_ref[:, sl] = _softmax_pv([s_loc, s_ctx], [vs, vc]).astype(BF16)


def _neighborhood_attention(proj, ck, cv, layer, bias_tab, plan):
    b, t, _ = proj.shape
    r_blk, slab, slab_start, pat_id, _ = plan
    qn = r_blk * GRID_W
    sn = slab * GRID_W
    l_ctx = ck.shape[2]
    hb = NA_HEADS_PER_STEP
    hw = hb * HEAD_DIM
    assert COL_QN % hb == 0 and COL_KN % hb == 0 and COL_VN % hb == 0 and N_HEADS_NA % hb == 0
    grid_spec = pltpu.PrefetchScalarGridSpec(
        num_scalar_prefetch=2,
        grid=(b, N_HEADS_NA // hb, t // qn),
        in_specs=[
            pl.BlockSpec((None, qn, hw), lambda bi, hg, rb, ss, pt: (bi, rb, COL_QN // hb + hg)),
            pl.BlockSpec((None, t, hw), lambda bi, hg, rb, ss, pt: (bi, 0, COL_KN // hb + hg)),
            pl.BlockSpec((None, t, hw), lambda bi, hg, rb, ss, pt: (bi, 0, COL_VN // hb + hg)),
            pl.BlockSpec((None, None, l_ctx, hw), lambda bi, hg, rb, ss, pt: (bi, layer, 0, hg)),
            pl.BlockSpec((None, None, l_ctx, hw), lambda bi, hg, rb, ss, pt: (bi, layer, 0, hg)),
            pl.BlockSpec((None, hb, qn, sn), lambda bi, hg, rb, ss, pt: (pt[rb], hg, 0, 0)),
        ],
        out_specs=pl.BlockSpec((None, qn, hw), lambda bi, hg, rb, ss, pt: (bi, rb, hg)),
    )
    return pl.pallas_call(
        functools.partial(_na_kernel, slab_len=sn, n_heads=hb),
        grid_spec=grid_spec,
        out_shape=jax.ShapeDtypeStruct((b, t, NA_W), BF16),
        compiler_params=_cparams(("arbitrary", "arbitrary", "arbitrary")),
        name="neighborhood_attn",
    )(jnp.asarray(slab_start), jnp.asarray(pat_id), proj, proj, proj, ck, cv, bias_tab)


def _layer_norm(y, g, b):
    mu = jnp.mean(y, axis=-1, keepdims=True)
    yc = y - mu
    var = jnp.mean(yc * yc, axis=-1, keepdims=True)
    return yc * lax.rsqrt(var + EPS) * g + b


def _top2_of4(vals):
    m1 = jnp.maximum(jnp.maximum(vals[0], vals[1]), jnp.maximum(vals[2], vals[3]))
    i1 = jnp.where(vals[0] == m1, 0, jnp.where(vals[1] == m1, 1, jnp.where(vals[2] == m1, 2, 3)))
    rest = [jnp.where(i1 == i, -1.0, vals[i]) for i in range(4)]
    m2 = jnp.maximum(jnp.maximum(rest[0], rest[1]), jnp.maximum(rest[2], rest[3]))
    i2 = jnp.where(rest[0] == m2, 0, jnp.where(rest[1] == m2, 1, jnp.where(rest[2] == m2, 2, 3)))
    return m1, i1, m2, i2


def _route_rows(logits_t):
    m = logits_t.max(axis=0, keepdims=True)
    e = jnp.exp(logits_t - m)
    probs = e / e.sum(axis=0, keepdims=True)
    rows = [probs[i:i + 1, :] for i in range(N_EXPERTS)]
    groups = [rows[g * EXPERTS_PER_GROUP:(g + 1) * EXPERTS_PER_GROUP] for g in range(N_GROUPS)]
    scores = []
    for g in range(N_GROUPS):
        m1, _, m2, _ = _top2_of4(groups[g])
        scores.append(m1 + m2)
    best = jnp.maximum(jnp.maximum(scores[0], scores[1]), jnp.maximum(scores[2], scores[3]))
    gi = jnp.where(scores[0] == best, 0, jnp.where(scores[1] == best, 1, jnp.where(scores[2] == best, 2, 3)))
    sel = [jnp.where(gi == 0, groups[0][i], jnp.where(gi == 1, groups[1][i],
                     jnp.where(gi == 2, groups[2][i], groups[3][i]))) for i in range(EXPERTS_PER_GROUP)]
    w1, l1, w2, l2 = _top2_of4(sel)
    wsum = w1 + w2
    return (gi * EXPERTS_PER_GROUP + l1, gi * EXPERTS_PER_GROUP + l2, w1 / wsum, w2 / wsum)


def _out_proj_kernel(oa_ref, on_ref, x_ref, mod_ref, wo_ref, g_ref, b_ref, wr_ref, br_ref, h2_in_ref,
                     x1_ref, h2_ref, e_ref, wt_ref):
    del h2_in_ref
    attn = jnp.dot(oa_ref[...], wo_ref[0:GA_Q, :], preferred_element_type=F32)
    attn = attn + jnp.dot(on_ref[...], wo_ref[GA_Q:GA_Q + NA_W, :], preferred_element_type=F32)
    gate1 = mod_ref[2:3, :]
    x1 = _layer_norm(DN_ALPHA * x_ref[...] + gate1 * attn, g_ref[...], b_ref[...])
    x1_ref[...] = x1
    h2 = x1 * (1.0 + mod_ref[4:5, :]) + mod_ref[3:4, :]
    h2_ref[...] = h2
    h_hi = h2.astype(BF16)
    h_lo = (h2 - h_hi.astype(F32)).astype(BF16)
    r_hi = jnp.dot(h_hi, wr_ref[...], preferred_element_type=F32)
    r_lo = jnp.dot(h_lo, wr_ref[...], preferred_element_type=F32)
    logits = r_hi + pltpu.roll(r_hi, LANE - N_EXPERTS, 1) + r_lo
    logits_t = logits.T[0:N_EXPERTS, :] + br_ref[...]
    e1, e2, w1, w2 = _route_rows(logits_t)
    e_ref[0:1, :] = e1
    e_ref[1:2, :] = e2
    n = w1.shape[1]
    row = lax.broadcasted_iota(jnp.int32, (LANE, n), 0)
    w_rows = jnp.where(row == 0, w1, jnp.where(row == 1, w2, 0.0))
    wt_ref[...] = w_rows.T


def _out_proj(oa, on, x, mod_l, mod_row0, w_o_bf, w_layer, ln_g, ln_b, wr_cat, b_router, h2_all, row0):
    b, t, d = x.shape
    tm = min(OUT_TM, t)
    nt = t // tm
    assert row0 % tm == 0
    tok = lambda bi, ti: (bi, ti, 0)
    const2 = lambda bi, ti: (0, 0)
    return pl.pallas_call(
        _out_proj_kernel,
        grid=(b, t // tm),
        in_specs=[
            pl.BlockSpec((None, tm, GA_Q), tok),
            pl.BlockSpec((None, tm, NA_W), tok),
            pl.BlockSpec((None, tm, d), tok),
            pl.BlockSpec((None, N_MOD, d), lambda bi, ti: (bi + mod_row0, 0, 0)),
            pl.BlockSpec((None, GA_Q + NA_W, d), lambda bi, ti: (w_layer, 0, 0)),
            pl.BlockSpec((1, d), const2),
            pl.BlockSpec((1, d), const2),
            pl.BlockSpec((d, LANE), const2),
            pl.BlockSpec((N_EXPERTS, 1), const2),
            pl.BlockSpec(memory_space=pl.ANY),
        ],
        out_specs=[
            pl.BlockSpec((None, tm, d), tok),
            pl.BlockSpec((tm, d), lambda bi, ti: (row0 // tm + bi * nt + ti, 0)),
            pl.BlockSpec((None, TOP_K, tm), lambda bi, ti: (bi, 0, ti)),
            pl.BlockSpec((None, tm, LANE), tok),
        ],
        out_shape=[
            jax.ShapeDtypeStruct((b, t, d), F32),
            jax.ShapeDtypeStruct(h2_all.shape, F32),
            jax.ShapeDtypeStruct((b, TOP_K, t), jnp.int32),
            jax.ShapeDtypeStruct((b, t, LANE), F32),
        ],
        input_output_aliases={9: 1},
        compiler_params=_cparams(("arbitrary", "arbitrary")),
        name="out_proj_ln_router",
    )(oa, on, x, mod_l, w_o_bf, ln_g.reshape(1, d), ln_b.reshape(1, d), wr_cat,
      b_router.reshape(N_EXPERTS, 1).astype(F32), h2_all)


def _row_copy(src_hbm, idx_ref, dst, sem, r):
    return pltpu.make_async_copy(src_hbm.at[pl.ds(idx_ref[0, r], 1)], dst.at[pl.ds(r, 1)], sem)


def _start_row_gather(src_hbm, idx_ref, dst, sem, n_rows):
    def body(r, carry):
        _row_copy(src_hbm, idx_ref, dst, sem, r).start()
        return carry
    lax.fori_loop(0, n_rows, body, 0, unroll=8)


def _start_row_gather_inline(src_hbm, idx_ref, dst, sem, r_lo, r_hi):
    for r in range(r_lo, r_hi):
        _row_copy(src_hbm, idx_ref, dst, sem, r).start()


def _wait_row_gather(src_hbm, dst, sem, n_rows):
    pltpu.make_async_copy(src_hbm.at[pl.ds(0, n_rows)], dst, sem).wait()


def _moe_kernel(be_ref, nu_ref, tok0_ref, tok1_ref, tok2_ref, h_hbm, wg_ref, wu_ref, wd_ref, o_ref,
                buf, sem, *, n_blocks):
    del be_ref
    i = pl.program_id(0)
    n_used = nu_ref[0]
    slot = i % GATHER_DEPTH
    ahead = (i + 2) % GATHER_DEPTH

    @pl.when(i == 0)
    def _():
        _start_row_gather(h_hbm, tok0_ref, buf.at[0], sem.at[0], MOE_BLK)
        _start_row_gather(h_hbm, tok1_ref, buf.at[1], sem.at[1], MOE_BLK)

    _wait_row_gather(h_hbm, buf.at[slot], sem.at[slot], MOE_BLK)

    @pl.when(i < n_used)
    def _():
        xb = buf[slot].astype(BF16)
        gate = jnp.dot(xb, wg_ref[...], preferred_element_type=F32)
        up = jnp.dot(xb, wu_ref[...], preferred_element_type=F32)
        act = (gate / (1.0 + jnp.exp(-gate)) * up).astype(BF16)
        _start_row_gather_inline(h_hbm, tok2_ref, buf.at[ahead], sem.at[ahead], 0, MOE_BLK)
        o_ref[...] = jnp.dot(act, wd_ref[...], preferred_element_type=F32)

    @pl.when(i >= n_used)
    def _():
        o_ref[...] = jnp.zeros_like(o_ref)
        _start_row_gather(h_hbm, tok2_ref, buf.at[ahead], sem.at[ahead], MOE_BLK)

    @pl.when(i == n_blocks - 1)
    def _():
        for k in (1, 2):
            s = (i + k) % GATHER_DEPTH
            _wait_row_gather(h_hbm, buf.at[s], sem.at[s], MOE_BLK)


def _moe_ffn(h2_flat, tok_buf, blk_e, n_used, wg, wu, wd, layer):
    n, d = h2_flat.shape
    n_blocks = blk_e.shape[0]
    d_ff = wg.shape[3]
    tok3 = tok_buf.reshape(n_blocks, 1, MOE_BLK)
    smem_blk = lambda f: pl.BlockSpec((None, 1, MOE_BLK), f, memory_space=pltpu.SMEM)
    grid_spec = pltpu.PrefetchScalarGridSpec(
        num_scalar_prefetch=2,
        grid=(n_blocks,),
        in_specs=[
            smem_blk(lambda i, be, nu: (i, 0, 0)),
            smem_blk(lambda i, be, nu: (jnp.minimum(i + 1, n_blocks - 1), 0, 0)),
            smem_blk(lambda i, be, nu: (jnp.minimum(i + 2, n_blocks - 1), 0, 0)),
            pl.BlockSpec(memory_space=pl.ANY),
            pl.BlockSpec((None, None, d, d_ff), lambda i, be, nu: (layer, be[i], 0, 0)),
            pl.BlockSpec((None, None, d, d_ff), lambda i, be, nu: (layer, be[i], 0, 0)),
            pl.BlockSpec((None, None, d_ff, d), lambda i, be, nu: (layer, be[i], 0, 0)),
        ],
        out_specs=pl.BlockSpec((MOE_BLK, d), lambda i, be, nu: (i, 0)),
        scratch_shapes=[pltpu.VMEM((GATHER_DEPTH, MOE_BLK, d), F32), pltpu.SemaphoreType.DMA((GATHER_DEPTH,))],
    )
    return pl.pallas_call(
        functools.partial(_moe_kernel, n_blocks=n_blocks),
        grid_spec=grid_spec,
        out_shape=jax.ShapeDtypeStruct((n_blocks * MOE_BLK, d), F32),
        compiler_params=_cparams(("arbitrary",)),
        name="moe_ffn",
    )(blk_e, n_used, tok3, tok3, tok3, h2_flat, wg, wu, wd)


def _combine_kernel(da0_ref, da1_ref, db0_ref, db1_ref, dc0_ref, dc1_ref, y_hbm, x1_ref, mod_ref, wt_ref,
                    g_ref, b_ref, o_ref, buf, sem, *, n_steps, tm):
    i = pl.program_id(0)
    slot = i % GATHER_DEPTH
    ahead = (i + 2) % GATHER_DEPTH

    @pl.when(i == 0)
    def _():
        for s, (r0, r1) in enumerate(((da0_ref, da1_ref), (db0_ref, db1_ref))):
            _start_row_gather(y_hbm, r0, buf.at[s, 0], sem.at[s], tm)
            _start_row_gather(y_hbm, r1, buf.at[s, 1], sem.at[s], tm)

    def wait(s):
        _wait_row_gather(y_hbm, buf.at[s, 0], sem.at[s], tm)
        _wait_row_gather(y_hbm, buf.at[s, 1], sem.at[s], tm)

    wait(slot)
    y = wt_ref[:, 0:1] * buf[slot, 0] + wt_ref[:, 1:2] * buf[slot, 1]
    gate2 = mod_ref[5:6, :]
    o_ref[...] = _layer_norm(DN_ALPHA * x1_ref[...] + gate2 * y, g_ref[...], b_ref[...])
    _start_row_gather_inline(y_hbm, dc0_ref, buf.at[ahead, 0], sem.at[ahead], 0, tm)
    _start_row_gather_inline(y_hbm, dc1_ref, buf.at[ahead, 1], sem.at[ahead], 0, tm)

    @pl.when(i == n_steps - 1)
    def _():
        for k in (1, 2):
            wait((i + k) % GATHER_DEPTH)


def _combine(y_sorted, dest, x1, mod_l, mod_row0, wt, ln_g, ln_b):
    b, t, d = x1.shape
    tm = min(OUT_TM, t)
    nt = t // tm
    n_steps = b * nt
    dest4 = dest.reshape(b, TOP_K, nt, tm).transpose(0, 2, 1, 3).reshape(n_steps, TOP_K, 1, tm)
    later = lambda i, k: jnp.minimum(i + k, n_steps - 1)
    smem_blk = lambda f: pl.BlockSpec((None, None, 1, tm), f, memory_space=pltpu.SMEM)
    tok = lambda i: (i // nt, i % nt, 0)
    return pl.pallas_call(
        functools.partial(_combine_kernel, n_steps=n_steps, tm=tm),
        grid=(n_steps,),
        in_specs=[
            smem_blk(lambda i: (i, 0, 0, 0)),
            smem_blk(lambda i: (i, 1, 0, 0)),
            smem_blk(lambda i: (later(i, 1), 0, 0, 0)),
            smem_blk(lambda i: (later(i, 1), 1, 0, 0)),
            smem_blk(lambda i: (later(i, 2), 0, 0, 0)),
            smem_blk(lambda i: (later(i, 2), 1, 0, 0)),
            pl.BlockSpec(memory_space=pl.ANY),
            pl.BlockSpec((None, tm, d), tok),
            pl.BlockSpec((None, N_MOD, d), lambda i: (i // nt + mod_row0, 0, 0)),
            pl.BlockSpec((None, tm, LANE), tok),
            pl.BlockSpec((1, d), lambda i: (0, 0)),
            pl.BlockSpec((1, d), lambda i: (0, 0)),
        ],
        out_specs=pl.BlockSpec((None, tm, d), tok),
        out_shape=jax.ShapeDtypeStruct((b, t, d), F32),
        scratch_shapes=[pltpu.VMEM((GATHER_DEPTH, TOP_K, tm, d), F32), pltpu.SemaphoreType.DMA((GATHER_DEPTH,))],
        compiler_params=_cparams(("arbitrary",)),
        name="moe_combine_ln",
    )(dest4, dest4, dest4, dest4, dest4, dest4, y_sorted, x1, mod_l, wt, ln_g.reshape(1, d), ln_b.reshape(1, d))


def _dispatch_plan(e_idxs, row0s):
    flat_e, src_row = [], []
    for e_idx, row0 in zip(e_idxs, row0s):
        b, k, t = e_idx.shape
        flat_e.append(e_idx.reshape(-1))
        rows = row0 + jnp.arange(b, dtype=jnp.int32)[:, None, None] * t + jnp.arange(t, dtype=jnp.int32)
        src_row.append(jnp.broadcast_to(rows, (b, k, t)).reshape(-1))
    flat_e = jnp.concatenate(flat_e)
    src_row = jnp.concatenate(src_row)
    a = flat_e.shape[0]
    onehot = (flat_e[:, None] == jnp.arange(N_EXPERTS, dtype=jnp.int32)[None, :]).astype(jnp.int32)
    counts = onehot.sum(axis=0)
    chunk = 256
    assert a % chunk == 0
    oh = onehot.reshape(a // chunk, chunk, N_EXPERTS).astype(F32)
    tri = jnp.tril(jnp.ones((chunk, chunk), F32))
    within = jnp.einsum('ij,cje->cie', tri, oh, precision=lax.Precision.HIGHEST)
    before = jnp.cumsum(oh.sum(axis=1), axis=0) - oh.sum(axis=1)
    running = (within + before[:, None, :]).reshape(a, N_EXPERTS).astype(jnp.int32)
    rank = (running * onehot).sum(axis=1) - 1
    padded = (counts + MOE_BLK - 1) // MOE_BLK * MOE_BLK
    pad_end = jnp.cumsum(padded)
    pad_start = pad_end - padded
    dest = pad_start[flat_e] + rank
    n_blocks = -(-a // MOE_BLK) + N_EXPERTS
    p = n_blocks * MOE_BLK
    tok_buf = jnp.zeros((p,), jnp.int32).at[dest].set(src_row, unique_indices=True)
    blk_start = jnp.arange(n_blocks, dtype=jnp.int32) * MOE_BLK
    blk_e = jnp.minimum(jnp.searchsorted(pad_end, blk_start, side='right'), N_EXPERTS - 1).astype(jnp.int32)
    n_used = (pad_end[-1] // MOE_BLK).astype(jnp.int32)
    last_e = blk_e[jnp.maximum(n_used - 1, 0)]
    blk_e = jnp.where(jnp.arange(n_blocks) < n_used, blk_e, last_e)
    dests, o = [], 0
    for e_idx in e_idxs:
        dests.append(dest[o:o + e_idx.size].reshape(e_idx.shape))
        o += e_idx.size
    return dests, tok_buf, blk_e, n_used.reshape(1)


def _attention_sublayer(x, mod_l, mod_row0, lw, ctx):
    b, t, d = x.shape
    layer = lw['layer']
    if ctx is None:
        xt = x.reshape(1, b * t, d)
        proj, *cache = _in_proj(xt, mod_l, mod_row0, lw['w_in'], layer, lw['q_gain'], lw['k_gain'],
                                cache=(layer, lw['depth'], b, t, lw['prev_cache']))
        proj = proj.reshape(b, t, -1)
        oa = _dense_attention(proj, COL_QA, COL_KA, COL_VA, N_KV_GA, N_HEADS_GA // N_KV_GA)
        on = _dense_attention(proj, COL_QN, COL_KN, COL_VN, N_HEADS_NA, 1)
        return oa.reshape(1, b * t, -1), on.reshape(1, b * t, -1), xt, cache
    ga_k, ga_v, na_k, na_v, rope_tabs, na_bias, na_plan = ctx
    (proj,) = _in_proj(x, mod_l, mod_row0, lw['w_in'], layer, lw['q_gain'], lw['k_gain'], rope_tabs=rope_tabs)
    oa = _dense_attention(proj, COL_QA, COL_KA, COL_VA, N_KV_GA, N_HEADS_GA // N_KV_GA, ctx=(ga_k, ga_v, layer))
    on = _neighborhood_attention(proj, na_k, na_v, layer, na_bias, na_plan)
    return oa, on, x, None


def _layer(x_ctx, x_lat, mod_l, lw, shared, lat_ctx):
    bc, tc, d = x_ctx.shape
    bl, tl, _ = x_lat.shape
    n_ctx, n_lat = bc * tc, bl * tl
    streams = [_attention_sublayer(x_ctx, mod_l, bl, lw, None),
               _attention_sublayer(x_lat, mod_l, 0, lw, lat_ctx)]
    cache = streams[0][3]
    mod_rows, row0s = (bl, 0), (0, n_ctx)
    h2_all = jnp.zeros((n_ctx + n_lat, d), F32)
    x1s, e_idxs, wts = [], [], []
    for (oa, on, xt, _), mod_row0, row0 in zip(streams, mod_rows, row0s):
        x1, h2_all, e_idx, wt = _out_proj(oa, on, xt, mod_l, mod_row0, lw['w_o'], lw['layer'], lw['ln1_g'],
                                          lw['ln1_b'], shared['wr_cat'], shared['b_router'], h2_all, row0)
        x1s.append(x1)
        e_idxs.append(e_idx)
        wts.append(wt)
    dests, tok_buf, blk_e, n_used = _dispatch_plan(e_idxs, row0s)
    y_sorted = _moe_ffn(h2_all, tok_buf, blk_e, n_used, lw['w_gate'], lw['w_up'], lw['w_down'], lw['layer'])
    outs = [_combine(y_sorted, dest, x1, mod_l, mod_row0, wt, lw['ln2_g'], lw['ln2_b'])
            for dest, x1, wt, mod_row0 in zip(dests, x1s, wts, mod_rows)]
    return outs[0].reshape(bc, tc, d), outs[1].reshape(bl, tl, d), cache


def kernel(x_prompt, x_sample, c, cache_ga_k, cache_ga_v, cache_na_k, cache_na_v, c_ctx, w_router, b_router, w_mod, b_mod, w_in, q_norm, k_norm, rel_bias, w_o, ln1_g, ln1_b, ln2_g, ln2_b, w_gate, w_up, w_down):
    bsz, seq, d = x_prompt.shape
    dec_b, dec_t, _ = x_sample.shape
    depth = w_mod.shape[0]
    past = cache_ga_k.shape[2]

    mod_rows = 16
    assert dec_b + 1 <= mod_rows
    cvec = jnp.concatenate([c, c_ctx[None, :], jnp.zeros((mod_rows - dec_b - 1, d), F32)], axis=0)
    mod = _modulation(cvec, w_mod, b_mod).reshape(depth, mod_rows, N_MOD, d)

    wr_hi = w_router.astype(BF16)
    wr_lo = (w_router - wr_hi.astype(F32)).astype(BF16)
    wr_cat = jnp.concatenate([wr_hi, wr_lo, jnp.zeros((d, LANE - 2 * N_EXPERTS), BF16)], axis=1)
    shared = {'wr_cat': wr_cat, 'b_router': b_router}

    rope_tabs = _rope_tables(dec_t)
    na_plan = _na_plan(dec_t // GRID_W)
    ga_k = cache_ga_k.reshape(dec_b, depth, past, GA_KV)
    ga_v = cache_ga_v.reshape(dec_b, depth, past, GA_KV)
    na_k = cache_na_k.reshape(dec_b, depth, past, NA_W)
    na_v = cache_na_v.reshape(dec_b, depth, past, NA_W)

    w_in_bf, w_o_bf = w_in.astype(BF16), w_o.astype(BF16)
    w_gate_bf, w_up_bf, w_down_bf = w_gate.astype(BF16), w_up.astype(BF16), w_down.astype(BF16)

    y_prompt, y_sample = x_prompt, x_sample
    cache = [jnp.zeros((bsz, depth, seq, w), F32) for w in (GA_KV, GA_KV, NA_W, NA_W)]
    for i in range(depth):
        lw = {
            'layer': i, 'depth': depth, 'prev_cache': cache,
            'w_in': w_in_bf, 'w_o': w_o_bf,
            'q_gain': (q_norm[i] * Q_PRESCALE).reshape(1, HEAD_DIM).astype(F32),
            'k_gain': k_norm[i].reshape(1, HEAD_DIM).astype(F32),
            'ln1_g': ln1_g[i], 'ln1_b': ln1_b[i], 'ln2_g': ln2_g[i], 'ln2_b': ln2_b[i],
            'w_gate': w_gate_bf, 'w_up': w_up_bf, 'w_down': w_down_bf,
        }
        na_bias = _na_bias_table(rel_bias[i], na_plan[4])
        y_prompt, y_sample, cache = _layer(y_prompt, y_sample, mod[i], lw, shared,
                                           (ga_k, ga_v, na_k, na_v, rope_tabs, na_bias, na_plan))

    outs = [arr.reshape(bsz, depth, seq, -1, HEAD_DIM) for arr in cache]
    return (y_prompt, y_sample, outs[0], outs[1], outs[2], outs[3])
```

```python
import functools
import math

import numpy as np
import jax
import jax.numpy as jnp
from jax import lax
from jax.experimental import pallas as pl
from jax.experimental.pallas import tpu as pltpu

F32 = jnp.float32
BF16 = jnp.bfloat16

DEPTH = 2
GRID_W = 64
HEAD_DIM = 128
N_HEADS_GA = 8
N_KV_GA = 2
N_HEADS_NA = 8
NA_WIN_H = 8
NA_WIN_W = 16
ROPE_THETA = 10000.0
N_EXPERTS = 16
N_GROUPS = 4
EXPERTS_PER_GROUP = N_EXPERTS // N_GROUPS
TOP_K = 2
N_MOD = 6
DN_ALPHA = (2.0 * DEPTH) ** 0.25
EPS = 1e-6
ATTN_SCALE = HEAD_DIM ** -0.5
LOG2E = math.log2(math.e)
Q_PRESCALE = ATTN_SCALE * LOG2E
MASK_VALUE = -1e30

GA_Q = N_HEADS_GA * HEAD_DIM
GA_KV = N_KV_GA * HEAD_DIM
NA_W = N_HEADS_NA * HEAD_DIM
COL_QA = 0
COL_KA = COL_QA + N_HEADS_GA
COL_VA = COL_KA + N_KV_GA
COL_QN = COL_VA + N_KV_GA
COL_KN = COL_QN + N_HEADS_NA
COL_VN = COL_KN + N_HEADS_NA
IN_HEADS = COL_VN + N_HEADS_NA

LANE = 128
PROJ_TN = 4 * HEAD_DIM
PROJ_TM = 1024
ATTN_TQ = 256
NA_ROWS = 4
NA_HEADS_PER_STEP = 4
OUT_TM = 256
MOE_BLK = 256
GATHER_DEPTH = 3
MOD_TN = 1024
VMEM_LIMIT = 52 * 1024 * 1024


def _cparams(sem):
    return pltpu.CompilerParams(dimension_semantics=sem, vmem_limit_bytes=VMEM_LIMIT)


def _mod_kernel(c_ref, w_ref, b_ref, o_ref):
    c = c_ref[...]
    s = c / (1.0 + jnp.exp(-c))
    o_ref[...] = jnp.dot(s.astype(BF16), w_ref[...].astype(BF16),
                         preferred_element_type=F32) + b_ref[...]


def _modulation(cvec, w_mod, b_mod):
    depth, d, e = w_mod.shape
    rows = cvec.shape[0]
    tn = MOD_TN if e % MOD_TN == 0 else e
    return pl.pallas_call(
        _mod_kernel,
        grid=(depth, e // tn),
        in_specs=[
            pl.BlockSpec((rows, d), lambda l, j: (0, 0)),
            pl.BlockSpec((None, d, tn), lambda l, j: (l, 0, j)),
            pl.BlockSpec((None, 1, tn), lambda l, j: (l, 0, j)),
        ],
        out_specs=pl.BlockSpec((None, rows, tn), lambda l, j: (l, 0, j)),
        out_shape=jax.ShapeDtypeStruct((depth, rows, e), F32),
        compiler_params=_cparams(("arbitrary", "arbitrary")),
        name="modulation",
    )(cvec, w_mod, b_mod.reshape(depth, 1, e))


def _rms_head(x, g):
    ms = jnp.mean(x * x, axis=-1, keepdims=True)
    return x * lax.rsqrt(ms + EPS) * g


def _rope_head(x, cos, sin, first_half):
    swapped = jnp.where(first_half, pltpu.roll(x, HEAD_DIM - 32, 1), pltpu.roll(x, 32, 1))
    return x * cos + swapped * sin


def _proj_kernel(*refs, rope, cache):
    x_ref, mod_ref, w_ref, qg_ref, kg_ref = refs[:5]
    pos = 5
    if rope:
        cos_ref, sin_ref = refs[pos:pos + 2]
        pos += 2
    if cache is not None:
        pos += cache[0]
    o_ref = refs[pos]
    pos += 1
    if cache is not None:
        ka_ref, va_ref, kn_ref, vn_ref = refs[pos:pos + 4]
        pos += 4
    h_scr = refs[pos]

    j = pl.program_id(2)

    @pl.when(j == 0)
    def _():
        shift = mod_ref[0:1, :]
        scale = mod_ref[1:2, :]
        h_scr[...] = (x_ref[...] * (1.0 + scale) + shift).astype(BF16)

    acc = jnp.dot(h_scr[...], w_ref[...], preferred_element_type=F32)

    if rope:
        lane = lax.broadcasted_iota(jnp.int32, (1, HEAD_DIM), 1)
        first_half = (lane % 64) < 32

    def head(hh):
        return acc[:, hh * HEAD_DIM:(hh + 1) * HEAD_DIM]

    def normed(hh, g):
        y = _rms_head(head(hh), g)
        if rope:
            y = _rope_head(y, cos_ref[...], sin_ref[...], first_half)
        return y

    def put(hh, y, c_ref=None, c_head=0):
        o_ref[:, hh * HEAD_DIM:(hh + 1) * HEAD_DIM] = y.astype(BF16)
        if c_ref is not None:
            c_ref[:, :, c_head * HEAD_DIM:(c_head + 1) * HEAD_DIM] = y.reshape(cache[1], cache[2], HEAD_DIM)

    hpt = PROJ_TN // HEAD_DIM
    j_ka, j_qn, j_kn, j_vn = COL_KA // hpt, COL_QN // hpt, COL_KN // hpt, COL_VN // hpt

    @pl.when(j < j_ka)
    def _():
        for hh in range(hpt):
            put(hh, normed(hh, qg_ref[...]))

    @pl.when(j == j_ka)
    def _():
        for hh in range(N_KV_GA):
            put(hh, normed(hh, kg_ref[...]), ka_ref if cache else None, hh)
        for hh in range(N_KV_GA):
            put(N_KV_GA + hh, head(N_KV_GA + hh), va_ref if cache else None, hh)

    @pl.when((j >= j_qn) & (j < j_kn))
    def _():
        o_ref[...] = (acc * Q_PRESCALE).astype(BF16)

    @pl.when((j >= j_kn) & (j < j_vn))
    def _():
        o_ref[...] = acc.astype(BF16)
        if cache is not None:
            kn_ref[...] = acc.reshape(cache[1], cache[2], PROJ_TN)

    @pl.when(j >= j_vn)
    def _():
        o_ref[...] = acc.astype(BF16)
        if cache is not None:
            vn_ref[...] = acc.reshape(cache[1], cache[2], PROJ_TN)


def _in_proj(x, mod_l, mod_row0, w_in_bf, w_layer, q_gain, k_gain, rope_tabs=None, cache=None):
    b, t, d = x.shape
    in_w = w_in_bf.shape[2]
    tm = min(PROJ_TM, t)
    nj = in_w // PROJ_TN
    rope = rope_tabs is not None
    hpt = PROJ_TN // HEAD_DIM
    j_kn, j_vn = COL_KN // hpt, COL_VN // hpt
    assert COL_KA % hpt == 0 and COL_QN % hpt == 0 and COL_KN % hpt == 0 and COL_VN % hpt == 0
    assert 2 * N_KV_GA == hpt and COL_VA == COL_KA + N_KV_GA

    in_specs = [
        pl.BlockSpec((None, tm, d), lambda bi, ti, j: (bi, ti, 0)),
        pl.BlockSpec((None, N_MOD, d), lambda bi, ti, j: (bi + mod_row0, 0, 0)),
        pl.BlockSpec((None, d, PROJ_TN), lambda bi, ti, j: (w_layer, 0, j)),
        pl.BlockSpec((1, HEAD_DIM), lambda bi, ti, j: (0, 0)),
        pl.BlockSpec((1, HEAD_DIM), lambda bi, ti, j: (0, 0)),
    ]
    args = [x, mod_l, w_in_bf, q_gain, k_gain]
    if rope:
        in_specs += [pl.BlockSpec((tm, HEAD_DIM), lambda bi, ti, j: (ti, 0))] * 2
        args += list(rope_tabs)
    out_specs = [pl.BlockSpec((None, tm, PROJ_TN), lambda bi, ti, j: (bi, ti, j))]
    out_shape = [jax.ShapeDtypeStruct((b, t, in_w), BF16)]
    aliases = {}
    cache_cfg = None
    if cache is not None:
        layer, depth, bsz, seq, prev = cache
        assert b == 1 and tm % seq == 0
        nb = tm // seq
        n_prev = 0 if prev is None else len(prev)
        cache_cfg = (n_prev, nb, seq)
        if prev is not None:
            for k, arr in enumerate(prev):
                aliases[len(args)] = 1 + k
                in_specs.append(pl.BlockSpec(memory_space=pl.ANY))
                args.append(arr)
        kv = lambda bi, ti, j: (ti, layer, 0, 0)
        kn = lambda bi, ti, j: (ti, layer, 0, jnp.clip(j - j_kn, 0, N_HEADS_NA // hpt - 1))
        vn = lambda bi, ti, j: (ti, layer, 0, jnp.clip(j - j_vn, 0, N_HEADS_NA // hpt - 1))
        out_specs += [pl.BlockSpec((nb, None, seq, GA_KV), kv), pl.BlockSpec((nb, None, seq, GA_KV), kv),
                      pl.BlockSpec((nb, None, seq, PROJ_TN), kn), pl.BlockSpec((nb, None, seq, PROJ_TN), vn)]
        out_shape += [jax.ShapeDtypeStruct((bsz, depth, seq, w), F32) for w in (GA_KV, GA_KV, NA_W, NA_W)]
    outs = pl.pallas_call(
        functools.partial(_proj_kernel, rope=rope, cache=cache_cfg),
        grid=(b, t // tm, nj),
        in_specs=in_specs,
        out_specs=out_specs,
        out_shape=out_shape,
        input_output_aliases=aliases,
        scratch_shapes=[pltpu.VMEM((tm, d), BF16)],
        compiler_params=_cparams(("arbitrary", "arbitrary", "arbitrary")),
        name="in_proj_rope" if rope else "in_proj_cache",
    )(*args)
    return outs


def _rope_tables(t):
    half = HEAD_DIM // 4
    tt = jnp.arange(t, dtype=jnp.int32)
    row = (tt // GRID_W).astype(F32)
    col = (tt % GRID_W).astype(F32)
    inv_freq = 1.0 / (ROPE_THETA ** (jnp.arange(half, dtype=F32) / half))
    ar = row[:, None] * inv_freq[None, :]
    ac = col[:, None] * inv_freq[None, :]
    cos = jnp.concatenate([jnp.cos(ar), jnp.cos(ar), jnp.cos(ac), jnp.cos(ac)], axis=-1)
    sin = jnp.concatenate([-jnp.sin(ar), jnp.sin(ar), -jnp.sin(ac), jnp.sin(ac)], axis=-1)
    return cos, sin


_NT = (((1,), (1,)), ((), ()))


def _softmax_pv(scores, values):
    m = scores[0].max(axis=-1, keepdims=True)
    for s in scores[1:]:
        m = jnp.maximum(m, s.max(axis=-1, keepdims=True))
    l = None
    acc = None
    for s, v in zip(scores, values):
        p = jnp.exp2(s - m)
        ps = p.sum(axis=-1, keepdims=True)
        pv = jnp.dot(p.astype(BF16), v, preferred_element_type=F32)
        l = ps if l is None else l + ps
        acc = pv if acc is None else acc + pv
    return acc * (1.0 / l)


def _dense_attn_kernel(*refs, r_heads, has_ctx):
    if has_ctx:
        q_ref, k_ref, v_ref, kc_ref, vc_ref, o_ref = refs
        kc = kc_ref[...].astype(BF16)
        vc = vc_ref[...].astype(BF16)
    else:
        q_ref, k_ref, v_ref, o_ref = refs
    k = k_ref[...]
    v = v_ref[...]
    for r in range(r_heads):
        sl = slice(r * HEAD_DIM, (r + 1) * HEAD_DIM)
        q = q_ref[:, sl]
        scores = [lax.dot_general(q, k, _NT, preferred_element_type=F32)]
        values = [v]
        if has_ctx:
            scores.append(lax.dot_general(q, kc, _NT, preferred_element_type=F32))
            values.append(vc)
        o_ref[:, sl] = _softmax_pv(scores, values).astype(BF16)


def _dense_attention(proj, q_col, k_col, v_col, groups, r_heads, ctx=None):
    b, t, _ = proj.shape
    tq = min(ATTN_TQ, t)
    qw = r_heads * HEAD_DIM
    assert q_col % r_heads == 0
    in_specs = [
        pl.BlockSpec((None, tq, qw), lambda bi, g, qi: (bi, qi, q_col // r_heads + g)),
        pl.BlockSpec((None, t, HEAD_DIM), lambda bi, g, qi: (bi, 0, k_col + g)),
        pl.BlockSpec((None, t, HEAD_DIM), lambda bi, g, qi: (bi, 0, v_col + g)),
    ]
    args = [proj, proj, proj]
    if ctx is not None:
        ck, cv, layer = ctx
        l_ctx = ck.shape[2]
        spec = pl.BlockSpec((None, None, l_ctx, HEAD_DIM), lambda bi, g, qi: (bi, layer, 0, g))
        in_specs += [spec, spec]
        args += [ck, cv]
    return pl.pallas_call(
        functools.partial(_dense_attn_kernel, r_heads=r_heads, has_ctx=ctx is not None),
        grid=(b, groups, t // tq),
        in_specs=in_specs,
        out_specs=pl.BlockSpec((None, tq, qw), lambda bi, g, qi: (bi, qi, g)),
        out_shape=jax.ShapeDtypeStruct((b, t, groups * qw), BF16),
        compiler_params=_cparams(("arbitrary", "arbitrary", "arbitrary")),
        name="dense_attn_ctx" if ctx is not None else "dense_attn",
    )(*args)


def _na_plan(rows):
    kh = min(NA_WIN_H, rows)
    kw = NA_WIN_W
    r_blk = min(NA_ROWS, rows)
    assert rows % r_blk == 0
    slab = min(r_blk - 1 + kh, rows)
    row_start = np.clip(np.arange(rows) - kh // 2, 0, rows - kh)
    col = np.arange(GRID_W)
    col_start = np.clip(col - kw // 2, 0, GRID_W - kw)
    slab_start, pat_id, pats, sigs = [], [], [], {}
    for r0 in range(0, rows, r_blk):
        ss = min(row_start[r0], rows - slab)
        rel = tuple(int(row_start[r0 + ri] - ss) for ri in range(r_blk))
        sig = (int(ss - r0), rel)
        if sig not in sigs:
            sigs[sig] = len(pats)
            q_row = r0 + np.arange(r_blk)[:, None, None, None]
            q_col = col[None, :, None, None]
            k_row = ss + np.arange(slab)[None, None, :, None]
            k_col = col[None, None, None, :]
            rs = row_start[r0:r0 + r_blk][:, None, None, None]
            cs = col_start[None, :, None, None]
            valid = (k_row >= rs) & (k_row < rs + kh) & (k_col >= cs) & (k_col < cs + kw)
            row_off = (k_row - q_row + (NA_WIN_H - 1))[:, 0, :, 0]
            row_sel = (row_off[:, :, None] == np.arange(2 * NA_WIN_H - 1)).astype(np.float32)
            full = (r_blk, GRID_W, slab, GRID_W)
            pats.append((np.broadcast_to(valid, full).reshape(r_blk * GRID_W, slab * GRID_W), row_sel))
        slab_start.append(int(ss))
        pat_id.append(sigs[sig])
    return r_blk, slab, np.array(slab_start, np.int32), np.array(pat_id, np.int32), pats


def _na_bias_table(rel_bias_l, pats):
    col = np.arange(GRID_W)
    col_off = col[None, :] - col[:, None] + (NA_WIN_W - 1)
    col_sel = (col_off[:, :, None] == np.arange(2 * NA_WIN_W - 1)).astype(np.float32)
    hi = lax.Precision.HIGHEST
    tabs = []
    for valid, row_sel in pats:
        by_row = jnp.einsum('hrc,isr->hisc', rel_bias_l.astype(F32), row_sel, precision=hi)
        bias = jnp.einsum('hisc,qkc->hiqsk', by_row, col_sel, precision=hi)
        bias = bias.reshape((bias.shape[0],) + valid.shape) * LOG2E
        tabs.append(jnp.where(valid[None], bias, MASK_VALUE))
    return jnp.stack(tabs, axis=0)


def _na_kernel(ss_ref, pat_ref, q_ref, k_ref, v_ref, kc_ref, vc_ref, bias_ref, o_ref, *, slab_len, n_heads):
    del pat_ref
    rb = pl.program_id(2)
    start = pl.multiple_of(ss_ref[rb] * GRID_W, GRID_W)
    for h in range(n_heads):
        sl = slice(h * HEAD_DIM, (h + 1) * HEAD_DIM)
        ks = k_ref[pl.ds(start, slab_len), sl]
        vs = v_ref[pl.ds(start, slab_len), sl]
        kc = kc_ref[:, sl].astype(BF16)
        vc = vc_ref[:, sl].astype(BF16)
        q = q_ref[:, sl]
        s_loc = lax.dot_general(q, ks, _NT, preferred_element_type=F32) + bias_ref[h]
        s_ctx = lax.dot_general(q, kc, _NT, preferred_element_type=F32)
        o_ref[:, sl] = _softmax_pv([s_loc, s_ctx], [vs, vc]).astype(BF16)


def _neighborhood_attention(proj, ck, cv, layer, bias_tab, plan):
    b, t, _ = proj.shape
    r_blk, slab, slab_start, pat_id, _ = plan
    qn = r_blk * GRID_W
    sn = slab * GRID_W
    l_ctx = ck.shape[2]
    hb = NA_HEADS_PER_STEP
    hw = hb * HEAD_DIM
    assert COL_QN % hb == 0 and COL_KN % hb == 0 and COL_VN % hb == 0 and N_HEADS_NA % hb == 0
    grid_spec = pltpu.PrefetchScalarGridSpec(
        num_scalar_prefetch=2,
        grid=(b, N_HEADS_NA // hb, t // qn),
        in_specs=[
            pl.BlockSpec((None, qn, hw), lambda bi, hg, rb, ss, pt: (bi, rb, COL_QN // hb + hg)),
            pl.BlockSpec((None, t, hw), lambda bi, hg, rb, ss, pt: (bi, 0, COL_KN // hb + hg)),
            pl.BlockSpec((None, t, hw), lambda bi, hg, rb, ss, pt: (bi, 0, COL_VN // hb + hg)),
            pl.BlockSpec((None, None, l_ctx, hw), lambda bi, hg, rb, ss, pt: (bi, layer, 0, hg)),
            pl.BlockSpec((None, None, l_ctx, hw), lambda bi, hg, rb, ss, pt: (bi, layer, 0, hg)),
            pl.BlockSpec((None, hb, qn, sn), lambda bi, hg, rb, ss, pt: (pt[rb], hg, 0, 0)),
        ],
        out_specs=pl.BlockSpec((None, qn, hw), lambda bi, hg, rb, ss, pt: (bi, rb, hg)),
    )
    return pl.pallas_call(
        functools.partial(_na_kernel, slab_len=sn, n_heads=hb),
        grid_spec=grid_spec,
        out_shape=jax.ShapeDtypeStruct((b, t, NA_W), BF16),
        compiler_params=_cparams(("arbitrary", "arbitrary", "arbitrary")),
        name="neighborhood_attn",
    )(jnp.asarray(slab_start), jnp.asarray(pat_id), proj, proj, proj, ck, cv, bias_tab)


def _layer_norm(y, g, b):
    mu = jnp.mean(y, axis=-1, keepdims=True)
    yc = y - mu
    var = jnp.mean(yc * yc, axis=-1, keepdims=True)
    return yc * lax.rsqrt(var + EPS) * g + b


def _top2_of4(vals):
    m1 = jnp.maximum(jnp.maximum(vals[0], vals[1]), jnp.maximum(vals[2], vals[3]))
    i1 = jnp.where(vals[0] == m1, 0, jnp.where(vals[1] == m1, 1, jnp.where(vals[2] == m1, 2, 3)))
    rest = [jnp.where(i1 == i, -1.0, vals[i]) for i in range(4)]
    m2 = jnp.maximum(jnp.maximum(rest[0], rest[1]), jnp.maximum(rest[2], rest[3]))
    i2 = jnp.where(rest[0] == m2, 0, jnp.where(rest[1] == m2, 1, jnp.where(rest[2] == m2, 2, 3)))
    return m1, i1, m2, i2


def _route_rows(logits_t):
    m = logits_t.max(axis=0, keepdims=True)
    e = jnp.exp(logits_t - m)
    probs = e / e.sum(axis=0, keepdims=True)
    rows = [probs[i:i + 1, :] for i in range(N_EXPERTS)]
    groups = [rows[g * EXPERTS_PER_GROUP:(g + 1) * EXPERTS_PER_GROUP] for g in range(N_GROUPS)]
    scores = []
    for g in range(N_GROUPS):
        m1, _, m2, _ = _top2_of4(groups[g])
        scores.append(m1 + m2)
    best = jnp.maximum(jnp.maximum(scores[0], scores[1]), jnp.maximum(scores[2], scores[3]))
    gi = jnp.where(scores[0] == best, 0, jnp.where(scores[1] == best, 1, jnp.where(scores[2] == best, 2, 3)))
    sel = [jnp.where(gi == 0, groups[0][i], jnp.where(gi == 1, groups[1][i],
                     jnp.where(gi == 2, groups[2][i], groups[3][i]))) for i in range(EXPERTS_PER_GROUP)]
    w1, l1, w2, l2 = _top2_of4(sel)
    wsum = w1 + w2
    return (gi * EXPERTS_PER_GROUP + l1, gi * EXPERTS_PER_GROUP + l2, w1 / wsum, w2 / wsum)


def _out_proj_kernel(oa_ref, on_ref, x_ref, mod_ref, wo_ref, g_ref, b_ref, wr_ref, br_ref, h2_in_ref,
                     x1_ref, h2_ref, e_ref, wt_ref):
    del h2_in_ref
    attn = jnp.dot(oa_ref[...], wo_ref[0:GA_Q, :], preferred_element_type=F32)
    attn = attn + jnp.dot(on_ref[...], wo_ref[GA_Q:GA_Q + NA_W, :], preferred_element_type=F32)
    gate1 = mod_ref[2:3, :]
    x1 = _layer_norm(DN_ALPHA * x_ref[...] + gate1 * attn, g_ref[...], b_ref[...])
    x1_ref[...] = x1
    h2 = x1 * (1.0 + mod_ref[4:5, :]) + mod_ref[3:4, :]
    h2_ref[...] = h2
    h_hi = h2.astype(BF16)
    h_lo = (h2 - h_hi.astype(F32)).astype(BF16)
    r_hi = jnp.dot(h_hi, wr_ref[...], preferred_element_type=F32)
    r_lo = jnp.dot(h_lo, wr_ref[...], preferred_element_type=F32)
    logits = r_hi + pltpu.roll(r_hi, LANE - N_EXPERTS, 1) + r_lo
    logits_t = logits.T[0:N_EXPERTS, :] + br_ref[...]
    e1, e2, w1, w2 = _route_rows(logits_t)
    e_ref[0:1, :] = e1
    e_ref[1:2, :] = e2
    n = w1.shape[1]
    row = lax.broadcasted_iota(jnp.int32, (LANE, n), 0)
    w_rows = jnp.where(row == 0, w1, jnp.where(row == 1, w2, 0.0))
    wt_ref[...] = w_rows.T


def _out_proj(oa, on, x, mod_l, mod_row0, w_o_bf, w_layer, ln_g, ln_b, wr_cat, b_router, h2_all, row0):
    b, t, d = x.shape
    tm = min(OUT_TM, t)
    nt = t // tm
    assert row0 % tm == 0
    tok = lambda bi, ti: (bi, ti, 0)
    const2 = lambda bi, ti: (0, 0)
    return pl.pallas_call(
        _out_proj_kernel,
        grid=(b, t // tm),
        in_specs=[
            pl.BlockSpec((None, tm, GA_Q), tok),
            pl.BlockSpec((None, tm, NA_W), tok),
            pl.BlockSpec((None, tm, d), tok),
            pl.BlockSpec((None, N_MOD, d), lambda bi, ti: (bi + mod_row0, 0, 0)),
            pl.BlockSpec((None, GA_Q + NA_W, d), lambda bi, ti: (w_layer, 0, 0)),
            pl.BlockSpec((1, d), const2),
            pl.BlockSpec((1, d), const2),
            pl.BlockSpec((d, LANE), const2),
            pl.BlockSpec((N_EXPERTS, 1), const2),
            pl.BlockSpec(memory_space=pl.ANY),
        ],
        out_specs=[
            pl.BlockSpec((None, tm, d), tok),
            pl.BlockSpec((tm, d), lambda bi, ti: (row0 // tm + bi * nt + ti, 0)),
            pl.BlockSpec((None, TOP_K, tm), lambda bi, ti: (bi, 0, ti)),
            pl.BlockSpec((None, tm, LANE), tok),
        ],
        out_shape=[
            jax.ShapeDtypeStruct((b, t, d), F32),
            jax.ShapeDtypeStruct(h2_all.shape, F32),
            jax.ShapeDtypeStruct((b, TOP_K, t), jnp.int32),
            jax.ShapeDtypeStruct((b, t, LANE), F32),
        ],
        input_output_aliases={9: 1},
        compiler_params=_cparams(("arbitrary", "arbitrary")),
        name="out_proj_ln_router",
    )(oa, on, x, mod_l, w_o_bf, ln_g.reshape(1, d), ln_b.reshape(1, d), wr_cat,
      b_router.reshape(N_EXPERTS, 1).astype(F32), h2_all)


def _row_copy(src_hbm, idx_ref, dst, sem, r):
    return pltpu.make_async_copy(src_hbm.at[pl.ds(idx_ref[0, r], 1)], dst.at[pl.ds(r, 1)], sem)


def _start_row_gather(src_hbm, idx_ref, dst, sem, n_rows):
    def body(r, carry):
        _row_copy(src_hbm, idx_ref, dst, sem, r).start()
        return carry
    lax.fori_loop(0, n_rows, body, 0, unroll=8)


def _start_row_gather_inline(src_hbm, idx_ref, dst, sem, r_lo, r_hi):
    for r in range(r_lo, r_hi):
        _row_copy(src_hbm, idx_ref, dst, sem, r).start()


def _wait_row_gather(src_hbm, dst, sem, n_rows):
    pltpu.make_async_copy(src_hbm.at[pl.ds(0, n_rows)], dst, sem).wait()


def _moe_kernel(be_ref, nu_ref, tok0_ref, tok1_ref, tok2_ref, h_hbm, wg_ref, wu_ref, wd_ref, o_ref,
                buf, sem, *, n_blocks):
    del be_ref
    i = pl.program_id(0)
    n_used = nu_ref[0]
    slot = i % GATHER_DEPTH
    ahead = (i + 2) % GATHER_DEPTH

    @pl.when(i == 0)
    def _():
        _start_row_gather(h_hbm, tok0_ref, buf.at[0], sem.at[0], MOE_BLK)
        _start_row_gather(h_hbm, tok1_ref, buf.at[1], sem.at[1], MOE_BLK)

    _wait_row_gather(h_hbm, buf.at[slot], sem.at[slot], MOE_BLK)

    @pl.when(i < n_used)
    def _():
        xb = buf[slot].astype(BF16)
        gate = jnp.dot(xb, wg_ref[...], preferred_element_type=F32)
        up = jnp.dot(xb, wu_ref[...], preferred_element_type=F32)
        act = (gate / (1.0 + jnp.exp(-gate)) * up).astype(BF16)
        _start_row_gather_inline(h_hbm, tok2_ref, buf.at[ahead], sem.at[ahead], 0, MOE_BLK)
        o_ref[...] = jnp.dot(act, wd_ref[...], preferred_element_type=F32)

    @pl.when(i >= n_used)
    def _():
        o_ref[...] = jnp.zeros_like(o_ref)
        _start_row_gather(h_hbm, tok2_ref, buf.at[ahead], sem.at[ahead], MOE_BLK)

    @pl.when(i == n_blocks - 1)
    def _():
        for k in (1, 2):
            s = (i + k) % GATHER_DEPTH
            _wait_row_gather(h_hbm, buf.at[s], sem.at[s], MOE_BLK)


def _moe_ffn(h2_flat, tok_buf, blk_e, n_used, wg, wu, wd, layer):
    n, d = h2_flat.shape
    n_blocks = blk_e.shape[0]
    d_ff = wg.shape[3]
    tok3 = tok_buf.reshape(n_blocks, 1, MOE_BLK)
    smem_blk = lambda f: pl.BlockSpec((None, 1, MOE_BLK), f, memory_space=pltpu.SMEM)
    grid_spec = pltpu.PrefetchScalarGridSpec(
        num_scalar_prefetch=2,
        grid=(n_blocks,),
        in_specs=[
            smem_blk(lambda i, be, nu: (i, 0, 0)),
            smem_blk(lambda i, be, nu: (jnp.minimum(i + 1, n_blocks - 1), 0, 0)),
            smem_blk(lambda i, be, nu: (jnp.minimum(i + 2, n_blocks - 1), 0, 0)),
            pl.BlockSpec(memory_space=pl.ANY),
            pl.BlockSpec((None, None, d, d_ff), lambda i, be, nu: (layer, be[i], 0, 0)),
            pl.BlockSpec((None, None, d, d_ff), lambda i, be, nu: (layer, be[i], 0, 0)),
            pl.BlockSpec((None, None, d_ff, d), lambda i, be, nu: (layer, be[i], 0, 0)),
        ],
        out_specs=pl.BlockSpec((MOE_BLK, d), lambda i, be, nu: (i, 0)),
        scratch_shapes=[pltpu.VMEM((GATHER_DEPTH, MOE_BLK, d), F32), pltpu.SemaphoreType.DMA((GATHER_DEPTH,))],
    )
    return pl.pallas_call(
        functools.partial(_moe_kernel, n_blocks=n_blocks),
        grid_spec=grid_spec,
        out_shape=jax.ShapeDtypeStruct((n_blocks * MOE_BLK, d), F32),
        compiler_params=_cparams(("arbitrary",)),
        name="moe_ffn",
    )(blk_e, n_used, tok3, tok3, tok3, h2_flat, wg, wu, wd)


def _combine_kernel(da0_ref, da1_ref, db0_ref, db1_ref, dc0_ref, dc1_ref, y_hbm, x1_ref, mod_ref, wt_ref,
                    g_ref, b_ref, o_ref, buf, sem, *, n_steps, tm):
    i = pl.program_id(0)
    slot = i % GATHER_DEPTH
    ahead = (i + 2) % GATHER_DEPTH

    @pl.when(i == 0)
    def _():
        for s, (r0, r1) in enumerate(((da0_ref, da1_ref), (db0_ref, db1_ref))):
            _start_row_gather(y_hbm, r0, buf.at[s, 0], sem.at[s], tm)
            _start_row_gather(y_hbm, r1, buf.at[s, 1], sem.at[s], tm)

    def wait(s):
        _wait_row_gather(y_hbm, buf.at[s, 0], sem.at[s], tm)
        _wait_row_gather(y_hbm, buf.at[s, 1], sem.at[s], tm)

    wait(slot)
    y = wt_ref[:, 0:1] * buf[slot, 0] + wt_ref[:, 1:2] * buf[slot, 1]
    gate2 = mod_ref[5:6, :]
    o_ref[...] = _layer_norm(DN_ALPHA * x1_ref[...] + gate2 * y, g_ref[...], b_ref[...])
    _start_row_gather_inline(y_hbm, dc0_ref, buf.at[ahead, 0], sem.at[ahead], 0, tm)
    _start_row_gather_inline(y_hbm, dc1_ref, buf.at[ahead, 1], sem.at[ahead], 0, tm)

    @pl.when(i == n_steps - 1)
    def _():
        for k in (1, 2):
            wait((i + k) % GATHER_DEPTH)


def _combine(y_sorted, dest, x1, mod_l, mod_row0, wt, ln_g, ln_b):
    b, t, d = x1.shape
    tm = min(OUT_TM, t)
    nt = t // tm
    n_steps = b * nt
    dest4 = dest.reshape(b, TOP_K, nt, tm).transpose(0, 2, 1, 3).reshape(n_steps, TOP_K, 1, tm)
    later = lambda i, k: jnp.minimum(i + k, n_steps - 1)
    smem_blk = lambda f: pl.BlockSpec((None, None, 1, tm), f, memory_space=pltpu.SMEM)
    tok = lambda i: (i // nt, i % nt, 0)
    return pl.pallas_call(
        functools.partial(_combine_kernel, n_steps=n_steps, tm=tm),
        grid=(n_steps,),
        in_specs=[
            smem_blk(lambda i: (i, 0, 0, 0)),
            smem_blk(lambda i: (i, 1, 0, 0)),
            smem_blk(lambda i: (later(i, 1), 0, 0, 0)),
            smem_blk(lambda i: (later(i, 1), 1, 0, 0)),
            smem_blk(lambda i: (later(i, 2), 0, 0, 0)),
            smem_blk(lambda i: (later(i, 2), 1, 0, 0)),
            pl.BlockSpec(memory_space=pl.ANY),
            pl.BlockSpec((None, tm, d), tok),
            pl.BlockSpec((None, N_MOD, d), lambda i: (i // nt + mod_row0, 0, 0)),
            pl.BlockSpec((None, tm, LANE), tok),
            pl.BlockSpec((1, d), lambda i: (0, 0)),
            pl.BlockSpec((1, d), lambda i: (0, 0)),
        ],
        out_specs=pl.BlockSpec((None, tm, d), tok),
        out_shape=jax.ShapeDtypeStruct((b, t, d), F32),
        scratch_shapes=[pltpu.VMEM((GATHER_DEPTH, TOP_K, tm, d), F32), pltpu.SemaphoreType.DMA((GATHER_DEPTH,))],
        compiler_params=_cparams(("arbitrary",)),
        name="moe_combine_ln",
    )(dest4, dest4, dest4, dest4, dest4, dest4, y_sorted, x1, mod_l, wt, ln_g.reshape(1, d), ln_b.reshape(1, d))


def _dispatch_plan(e_idxs, row0s):
    flat_e = jnp.concatenate([e_idx.reshape(-1) for e_idx in e_idxs])
    a = flat_e.shape[0]
    onehot = (flat_e[:, None] == jnp.arange(N_EXPERTS, dtype=jnp.int32)[None, :]).astype(jnp.int32)
    counts = onehot.sum(axis=0)
    chunk = 256
    assert a % chunk == 0
    oh = onehot.reshape(a // chunk, chunk, N_EXPERTS).astype(F32)
    tri = jnp.tril(jnp.ones((chunk, chunk), F32))
    within = jnp.einsum('ij,cje->cie', tri, oh, precision=lax.Precision.HIGHEST)
    before = jnp.cumsum(oh.sum(axis=1), axis=0) - oh.sum(axis=1)
    running = (within + before[:, None, :]).reshape(a, N_EXPERTS).astype(jnp.int32)
    rank = (running * onehot).sum(axis=1) - 1
    padded = (counts + MOE_BLK - 1) // MOE_BLK * MOE_BLK
    pad_end = jnp.cumsum(padded)
    pad_start = pad_end - padded
    dest = pad_start[flat_e] + rank
    n_blocks = -(-a // MOE_BLK) + N_EXPERTS
    p = n_blocks * MOE_BLK
    blk_start = jnp.arange(n_blocks, dtype=jnp.int32) * MOE_BLK
    blk_e = jnp.minimum((blk_start[:, None] >= pad_end[None, :]).sum(axis=1), N_EXPERTS - 1).astype(jnp.int32)
    n_used = (pad_end[-1] // MOE_BLK).astype(jnp.int32)
    last_e = blk_e[jnp.maximum(n_used - 1, 0)]
    blk_e = jnp.where(jnp.arange(n_blocks) < n_used, blk_e, last_e)
    assert a < (1 << 16)
    order = lax.sort(flat_e * (1 << 16) + jnp.arange(a, dtype=jnp.int32)) & 0xFFFF
    start = jnp.cumsum(counts) - counts
    order = jnp.concatenate([order, jnp.zeros((MOE_BLK,), jnp.int32)])
    win = jnp.clip(start[blk_e] + blk_start - pad_start[blk_e], 0, a)
    order_blk = jax.vmap(lambda w: lax.dynamic_slice(order, (w,), (MOE_BLK,)))(win).reshape(p)
    tok_buf, o = jnp.zeros((p,), jnp.int32), 0
    for e_idx, row0 in zip(e_idxs, row0s):
        b, k, t = e_idx.shape
        local = order_blk - o
        rows = row0 + (local // (k * t)) * t + local % t
        tok_buf = jnp.where((local >= 0) & (local < e_idx.size), rows, tok_buf)
        o += e_idx.size
    dests, o = [], 0
    for e_idx in e_idxs:
        dests.append(dest[o:o + e_idx.size].reshape(e_idx.shape))
        o += e_idx.size
    return dests, tok_buf, blk_e, n_used.reshape(1)


def _attention_sublayer(x, mod_l, mod_row0, lw, ctx):
    b, t, d = x.shape
    layer = lw['layer']
    if ctx is None:
        xt = x.reshape(1, b * t, d)
        proj, *cache = _in_proj(xt, mod_l, mod_row0, lw['w_in'], layer, lw['q_gain'], lw['k_gain'],
                                cache=(layer, lw['depth'], b, t, lw['prev_cache']))
        proj = proj.reshape(b, t, -1)
        oa = _dense_attention(proj, COL_QA, COL_KA, COL_VA, N_KV_GA, N_HEADS_GA // N_KV_GA)
        on = _dense_attention(proj, COL_QN, COL_KN, COL_VN, N_HEADS_NA, 1)
        return oa.reshape(1, b * t, -1), on.reshape(1, b * t, -1), xt, cache
    ga_k, ga_v, na_k, na_v, rope_tabs, na_bias, na_plan = ctx
    (proj,) = _in_proj(x, mod_l, mod_row0, lw['w_in'], layer, lw['q_gain'], lw['k_gain'], rope_tabs=rope_tabs)
    oa = _dense_attention(proj, COL_QA, COL_KA, COL_VA, N_KV_GA, N_HEADS_GA // N_KV_GA, ctx=(ga_k, ga_v, layer))
    on = _neighborhood_attention(proj, na_k, na_v, layer, na_bias, na_plan)
    return oa, on, x, None


def _layer(x_ctx, x_lat, mod_l, lw, shared, lat_ctx, h2_all):
    bc, tc, d = x_ctx.shape
    bl, tl, _ = x_lat.shape
    n_ctx, n_lat = bc * tc, bl * tl
    streams = [_attention_sublayer(x_ctx, mod_l, bl, lw, None),
               _attention_sublayer(x_lat, mod_l, 0, lw, lat_ctx)]
    cache = streams[0][3]
    mod_rows, row0s = (bl, 0), (0, n_ctx)
    x1s, e_idxs, wts = [], [], []
    for (oa, on, xt, _), mod_row0, row0 in zip(streams, mod_rows, row0s):
        x1, h2_all, e_idx, wt = _out_proj(oa, on, xt, mod_l, mod_row0, lw['w_o'], lw['layer'], lw['ln1_g'],
                                          lw['ln1_b'], shared['wr_cat'], shared['b_router'], h2_all, row0)
        x1s.append(x1)
        e_idxs.append(e_idx)
        wts.append(wt)
    dests, tok_buf, blk_e, n_used = _dispatch_plan(e_idxs, row0s)
    y_sorted = _moe_ffn(h2_all, tok_buf, blk_e, n_used, lw['w_gate'], lw['w_up'], lw['w_down'], lw['layer'])
    outs = [_combine(y_sorted, dest, x1, mod_l, mod_row0, wt, lw['ln2_g'], lw['ln2_b'])
            for dest, x1, wt, mod_row0 in zip(dests, x1s, wts, mod_rows)]
    return outs[0].reshape(bc, tc, d), outs[1].reshape(bl, tl, d), cache, h2_all


def kernel(x_prompt, x_sample, c, cache_ga_k, cache_ga_v, cache_na_k, cache_na_v, c_ctx, w_router, b_router, w_mod, b_mod, w_in, q_norm, k_norm, rel_bias, w_o, ln1_g, ln1_b, ln2_g, ln2_b, w_gate, w_up, w_down):
    bsz, seq, d = x_prompt.shape
    dec_b, dec_t, _ = x_sample.shape
    depth = w_mod.shape[0]
    past = cache_ga_k.shape[2]

    mod_rows = 16
    assert dec_b + 1 <= mod_rows
    cvec = jnp.concatenate([c, c_ctx[None, :], jnp.zeros((mod_rows - dec_b - 1, d), F32)], axis=0)
    mod = _modulation(cvec, w_mod, b_mod).reshape(depth, mod_rows, N_MOD, d)

    wr_hi = w_router.astype(BF16)
    wr_lo = (w_router - wr_hi.astype(F32)).astype(BF16)
    wr_cat = jnp.concatenate([wr_hi, wr_lo, jnp.zeros((d, LANE - 2 * N_EXPERTS), BF16)], axis=1)
    shared = {'wr_cat': wr_cat, 'b_router': b_router}

    rope_tabs = _rope_tables(dec_t)
    na_plan = _na_plan(dec_t // GRID_W)
    ga_k = cache_ga_k.reshape(dec_b, depth, past, GA_KV)
    ga_v = cache_ga_v.reshape(dec_b, depth, past, GA_KV)
    na_k = cache_na_k.reshape(dec_b, depth, past, NA_W)
    na_v = cache_na_v.reshape(dec_b, depth, past, NA_W)

    w_in_bf, w_o_bf = w_in.astype(BF16), w_o.astype(BF16)
    w_gate_bf, w_up_bf, w_down_bf = w_gate.astype(BF16), w_up.astype(BF16), w_down.astype(BF16)

    y_prompt, y_sample = x_prompt, x_sample
    cache = [jnp.zeros((bsz, depth, seq, w), F32) for w in (GA_KV, GA_KV, NA_W, NA_W)]
    h2_all = jnp.zeros((bsz * seq + dec_b * dec_t, d), F32)
    for i in range(depth):
        lw = {
            'layer': i, 'depth': depth, 'prev_cache': cache,
            'w_in': w_in_bf, 'w_o': w_o_bf,
            'q_gain': (q_norm[i] * Q_PRESCALE).reshape(1, HEAD_DIM).astype(F32),
            'k_gain': k_norm[i].reshape(1, HEAD_DIM).astype(F32),
            'ln1_g': ln1_g[i], 'ln1_b': ln1_b[i], 'ln2_g': ln2_g[i], 'ln2_b': ln2_b[i],
            'w_gate': w_gate_bf, 'w_up': w_up_bf, 'w_down': w_down_bf,
        }
        na_bias = _na_bias_table(rel_bias[i], na_plan[4])
        y_prompt, y_sample, cache, h2_all = _layer(y_prompt, y_sample, mod[i], lw, shared,
                                                   (ga_k, ga_v, na_k, na_v, rope_tabs, na_bias, na_plan), h2_all)

    outs = [arr.reshape(bsz, depth, seq, -1, HEAD_DIM) for arr in cache]
    return (y_prompt, y_sample, outs[0], outs[1], outs[2], outs[3])
```

```python
import functools
import math

import numpy as np
import jax
import jax.numpy as jnp
from jax import lax
from jax.experimental import pallas as pl
from jax.experimental.pallas import tpu as pltpu

F32 = jnp.float32
BF16 = jnp.bfloat16

DEPTH = 2
GRID_W = 64
HEAD_DIM = 128
N_HEADS_GA = 8
N_KV_GA = 2
N_HEADS_NA = 8
NA_WIN_H = 8
NA_WIN_W = 16
ROPE_THETA = 10000.0
N_EXPERTS = 16
N_GROUPS = 4
EXPERTS_PER_GROUP = N_EXPERTS // N_GROUPS
TOP_K = 2
N_MOD = 6
DN_ALPHA = (2.0 * DEPTH) ** 0.25
EPS = 1e-6
ATTN_SCALE = HEAD_DIM ** -0.5
LOG2E = math.log2(math.e)
Q_PRESCALE = ATTN_SCALE * LOG2E
MASK_VALUE = -1e30

GA_Q = N_HEADS_GA * HEAD_DIM
GA_KV = N_KV_GA * HEAD_DIM
NA_W = N_HEADS_NA * HEAD_DIM
COL_QA = 0
COL_KA = COL_QA + N_HEADS_GA
COL_VA = COL_KA + N_KV_GA
COL_QN = COL_VA + N_KV_GA
COL_KN = COL_QN + N_HEADS_NA
COL_VN = COL_KN + N_HEADS_NA
IN_HEADS = COL_VN + N_HEADS_NA

LANE = 128
PROJ_TN = 4 * HEAD_DIM
PROJ_TM = 1024
ATTN_TQ = 512
NA_ROWS = 4
NA_HEADS_PER_STEP = 4
OUT_TM = 256
MOE_BLK = 256
GATHER_DEPTH = 3
MOD_TN = 1024
VMEM_LIMIT = 52 * 1024 * 1024


def _cparams(sem):
    return pltpu.CompilerParams(dimension_semantics=sem, vmem_limit_bytes=VMEM_LIMIT)


def _mod_kernel(c_ref, w_ref, b_ref, o_ref):
    c = c_ref[...]
    s = c / (1.0 + jnp.exp(-c))
    o_ref[...] = jnp.dot(s.astype(BF16), w_ref[...].astype(BF16),
                         preferred_element_type=F32) + b_ref[...]


def _modulation(cvec, w_mod, b_mod):
    depth, d, e = w_mod.shape
    rows = cvec.shape[0]
    tn = MOD_TN if e % MOD_TN == 0 else e
    return pl.pallas_call(
        _mod_kernel,
        grid=(depth, e // tn),
        in_specs=[
            pl.BlockSpec((rows, d), lambda l, j: (0, 0)),
            pl.BlockSpec((None, d, tn), lambda l, j: (l, 0, j)),
            pl.BlockSpec((None, 1, tn), lambda l, j: (l, 0, j)),
        ],
        out_specs=pl.BlockSpec((None, rows, tn), lambda l, j: (l, 0, j)),
        out_shape=jax.ShapeDtypeStruct((depth, rows, e), F32),
        compiler_params=_cparams(("arbitrary", "arbitrary")),
        name="modulation",
    )(cvec, w_mod, b_mod.reshape(depth, 1, e))


def _rms_head(x, g):
    ms = jnp.mean(x * x, axis=-1, keepdims=True)
    return x * lax.rsqrt(ms + EPS) * g


def _rope_head(x, cos, sin, first_half):
    swapped = jnp.where(first_half, pltpu.roll(x, HEAD_DIM - 32, 1), pltpu.roll(x, 32, 1))
    return x * cos + swapped * sin


def _proj_kernel(*refs, rope, cache):
    x_ref, mod_ref, w_ref, qg_ref, kg_ref = refs[:5]
    pos = 5
    if rope:
        cos_ref, sin_ref = refs[pos:pos + 2]
        pos += 2
    if cache is not None:
        pos += cache[0]
    o_ref = refs[pos]
    pos += 1
    if cache is not None:
        ka_ref, va_ref, kn_ref, vn_ref = refs[pos:pos + 4]
        pos += 4
    h_scr = refs[pos]

    j = pl.program_id(2)

    @pl.when(j == 0)
    def _():
        shift = mod_ref[0:1, :]
        scale = mod_ref[1:2, :]
        h_scr[...] = (x_ref[...] * (1.0 + scale) + shift).astype(BF16)

    acc = jnp.dot(h_scr[...], w_ref[...], preferred_element_type=F32)

    if rope:
        lane = lax.broadcasted_iota(jnp.int32, (1, HEAD_DIM), 1)
        first_half = (lane % 64) < 32

    def head(hh):
        return acc[:, hh * HEAD_DIM:(hh + 1) * HEAD_DIM]

    def normed(hh, g):
        y = _rms_head(head(hh), g)
        if rope:
            y = _rope_head(y, cos_ref[...], sin_ref[...], first_half)
        return y

    def put(hh, y, c_ref=None, c_head=0):
        o_ref[:, hh * HEAD_DIM:(hh + 1) * HEAD_DIM] = y.astype(BF16)
        if c_ref is not None:
            c_ref[:, :, c_head * HEAD_DIM:(c_head + 1) * HEAD_DIM] = y.reshape(cache[1], cache[2], HEAD_DIM)

    hpt = PROJ_TN // HEAD_DIM
    j_ka, j_qn, j_kn, j_vn = COL_KA // hpt, COL_QN // hpt, COL_KN // hpt, COL_VN // hpt

    @pl.when(j < j_ka)
    def _():
        for hh in range(hpt):
            put(hh, normed(hh, qg_ref[...]))

    @pl.when(j == j_ka)
    def _():
        for hh in range(N_KV_GA):
            put(hh, normed(hh, kg_ref[...]), ka_ref if cache else None, hh)
        for hh in range(N_KV_GA):
            put(N_KV_GA + hh, head(N_KV_GA + hh), va_ref if cache else None, hh)

    @pl.when((j >= j_qn) & (j < j_kn))
    def _():
        o_ref[...] = (acc * Q_PRESCALE).astype(BF16)

    @pl.when((j >= j_kn) & (j < j_vn))
    def _():
        o_ref[...] = acc.astype(BF16)
        if cache is not None:
            kn_ref[...] = acc.reshape(cache[1], cache[2], PROJ_TN)

    @pl.when(j >= j_vn)
    def _():
        o_ref[...] = acc.astype(BF16)
        if cache is not None:
            vn_ref[...] = acc.reshape(cache[1], cache[2], PROJ_TN)


def _in_proj(x, mod_l, mod_row0, w_in_bf, w_layer, q_gain, k_gain, rope_tabs=None, cache=None):
    b, t, d = x.shape
    in_w = w_in_bf.shape[2]
    tm = min(PROJ_TM, t)
    nj = in_w // PROJ_TN
    rope = rope_tabs is not None
    hpt = PROJ_TN // HEAD_DIM
    j_kn, j_vn = COL_KN // hpt, COL_VN // hpt
    assert COL_KA % hpt == 0 and COL_QN % hpt == 0 and COL_KN % hpt == 0 and COL_VN % hpt == 0
    assert 2 * N_KV_GA == hpt and COL_VA == COL_KA + N_KV_GA

    in_specs = [
        pl.BlockSpec((None, tm, d), lambda bi, ti, j: (bi, ti, 0)),
        pl.BlockSpec((None, N_MOD, d), lambda bi, ti, j: (bi + mod_row0, 0, 0)),
        pl.BlockSpec((None, d, PROJ_TN), lambda bi, ti, j: (w_layer, 0, j)),
        pl.BlockSpec((1, HEAD_DIM), lambda bi, ti, j: (0, 0)),
        pl.BlockSpec((1, HEAD_DIM), lambda bi, ti, j: (0, 0)),
    ]
    args = [x, mod_l, w_in_bf, q_gain, k_gain]
    if rope:
        in_specs += [pl.BlockSpec((tm, HEAD_DIM), lambda bi, ti, j: (ti, 0))] * 2
        args += list(rope_tabs)
    out_specs = [pl.BlockSpec((None, tm, PROJ_TN), lambda bi, ti, j: (bi, ti, j))]
    out_shape = [jax.ShapeDtypeStruct((b, t, in_w), BF16)]
    aliases = {}
    cache_cfg = None
    if cache is not None:
        layer, depth, bsz, seq, prev = cache
        assert b == 1 and tm % seq == 0
        nb = tm // seq
        n_prev = 0 if prev is None else len(prev)
        cache_cfg = (n_prev, nb, seq)
        if prev is not None:
            for k, arr in enumerate(prev):
                aliases[len(args)] = 1 + k
                in_specs.append(pl.BlockSpec(memory_space=pl.ANY))
                args.append(arr)
        kv = lambda bi, ti, j: (ti, layer, 0, 0)
        kn = lambda bi, ti, j: (ti, layer, 0, jnp.clip(j - j_kn, 0, N_HEADS_NA // hpt - 1))
        vn = lambda bi, ti, j: (ti, layer, 0, jnp.clip(j - j_vn, 0, N_HEADS_NA // hpt - 1))
        out_specs += [pl.BlockSpec((nb, None, seq, GA_KV), kv), pl.BlockSpec((nb, None, seq, GA_KV), kv),
                      pl.BlockSpec((nb, None, seq, PROJ_TN), kn), pl.BlockSpec((nb, None, seq, PROJ_TN), vn)]
        out_shape += [jax.ShapeDtypeStruct((bsz, depth, seq, w), F32) for w in (GA_KV, GA_KV, NA_W, NA_W)]
    outs = pl.pallas_call(
        functools.partial(_proj_kernel, rope=rope, cache=cache_cfg),
        grid=(b, t // tm, nj),
        in_specs=in_specs,
        out_specs=out_specs,
        out_shape=out_shape,
        input_output_aliases=aliases,
        scratch_shapes=[pltpu.VMEM((tm, d), BF16)],
        compiler_params=_cparams(("arbitrary", "arbitrary", "arbitrary")),
        name="in_proj_rope" if rope else "in_proj_cache",
    )(*args)
    return outs


def _rope_tables(t):
    half = HEAD_DIM // 4
    tt = jnp.arange(t, dtype=jnp.int32)
    row = (tt // GRID_W).astype(F32)
    col = (tt % GRID_W).astype(F32)
    inv_freq = 1.0 / (ROPE_THETA ** (jnp.arange(half, dtype=F32) / half))
    ar = row[:, None] * inv_freq[None, :]
    ac = col[:, None] * inv_freq[None, :]
    cos = jnp.concatenate([jnp.cos(ar), jnp.cos(ar), jnp.cos(ac), jnp.cos(ac)], axis=-1)
    sin = jnp.concatenate([-jnp.sin(ar), jnp.sin(ar), -jnp.sin(ac), jnp.sin(ac)], axis=-1)
    return cos, sin


_NT = (((1,), (1,)), ((), ()))


def _softmax_pv(scores, values):
    m = scores[0].max(axis=-1, keepdims=True)
    for s in scores[1:]:
        m = jnp.maximum(m, s.max(axis=-1, keepdims=True))
    l = None
    acc = None
    for s, v in zip(scores, values):
        p = jnp.exp2(s - m)
        ps = p.sum(axis=-1, keepdims=True)
        pv = jnp.dot(p.astype(BF16), v, preferred_element_type=F32)
        l = ps if l is None else l + ps
        acc = pv if acc is None else acc + pv
    return acc * (1.0 / l)


def _dense_attn_kernel(*refs, r_heads, has_ctx):
    if has_ctx:
        q_ref, k_ref, v_ref, kc_ref, vc_ref, o_ref = refs
        kc = kc_ref[...].astype(BF16)
        vc = vc_ref[...].astype(BF16)
    else:
        q_ref, k_ref, v_ref, o_ref = refs
    k = k_ref[...]
    v = v_ref[...]
    for r in range(r_heads):
        sl = slice(r * HEAD_DIM, (r + 1) * HEAD_DIM)
        q = q_ref[:, sl]
        scores = [lax.dot_general(q, k, _NT, preferred_element_type=F32)]
        values = [v]
        if has_ctx:
            scores.append(lax.dot_general(q, kc, _NT, preferred_element_type=F32))
            values.append(vc)
        o_ref[:, sl] = _softmax_pv(scores, values).astype(BF16)


def _dense_attention(proj, q_col, k_col, v_col, groups, r_heads, ctx=None):
    b, t, _ = proj.shape
    tq = min(ATTN_TQ, t)
    qw = r_heads * HEAD_DIM
    assert q_col % r_heads == 0
    in_specs = [
        pl.BlockSpec((None, tq, qw), lambda bi, g, qi: (bi, qi, q_col // r_heads + g)),
        pl.BlockSpec((None, t, HEAD_DIM), lambda bi, g, qi: (bi, 0, k_col + g)),
        pl.BlockSpec((None, t, HEAD_DIM), lambda bi, g, qi: (bi, 0, v_col + g)),
    ]
    args = [proj, proj, proj]
    if ctx is not None:
        ck, cv, layer = ctx
        l_ctx = ck.shape[2]
        spec = pl.BlockSpec((None, None, l_ctx, HEAD_DIM), lambda bi, g, qi: (bi, layer, 0, g))
        in_specs += [spec, spec]
        args += [ck, cv]
    return pl.pallas_call(
        functools.partial(_dense_attn_kernel, r_heads=r_heads, has_ctx=ctx is not None),
        grid=(b, groups, t // tq),
        in_specs=in_specs,
        out_specs=pl.BlockSpec((None, tq, qw), lambda bi, g, qi: (bi, qi, g)),
        out_shape=jax.ShapeDtypeStruct((b, t, groups * qw), BF16),
        compiler_params=_cparams(("arbitrary", "arbitrary", "arbitrary")),
        name="dense_attn_ctx" if ctx is not None else "dense_attn",
    )(*args)


def _na_plan(rows):
    kh = min(NA_WIN_H, rows)
    kw = NA_WIN_W
    r_blk = min(NA_ROWS, rows)
    assert rows % r_blk == 0
    slab = min(r_blk - 1 + kh, rows)
    row_start = np.clip(np.arange(rows) - kh // 2, 0, rows - kh)
    col = np.arange(GRID_W)
    col_start = np.clip(col - kw // 2, 0, GRID_W - kw)
    slab_start, pat_id, pats, sigs = [], [], [], {}
    for r0 in range(0, rows, r_blk):
        ss = min(row_start[r0], rows - slab)
        rel = tuple(int(row_start[r0 + ri] - ss) for ri in range(r_blk))
        sig = (int(ss - r0), rel)
        if sig not in sigs:
            sigs[sig] = len(pats)
            q_row = r0 + np.arange(r_blk)[:, None, None, None]
            q_col = col[None, :, None, None]
            k_row = ss + np.arange(slab)[None, None, :, None]
            k_col = col[None, None, None, :]
            rs = row_start[r0:r0 + r_blk][:, None, None, None]
            cs = col_start[None, :, None, None]
            valid = (k_row >= rs) & (k_row < rs + kh) & (k_col >= cs) & (k_col < cs + kw)
            row_off = (k_row - q_row + (NA_WIN_H - 1))[:, 0, :, 0]
            row_sel = (row_off[:, :, None] == np.arange(2 * NA_WIN_H - 1)).astype(np.float32)
            full = (r_blk, GRID_W, slab, GRID_W)
            pats.append((np.broadcast_to(valid, full).reshape(r_blk * GRID_W, slab * GRID_W), row_sel))
        slab_start.append(int(ss))
        pat_id.append(sigs[sig])
    return r_blk, slab, np.array(slab_start, np.int32), np.array(pat_id, np.int32), pats


def _na_bias_table(rel_bias_l, pats):
    col = np.arange(GRID_W)
    col_off = col[None, :] - col[:, None] + (NA_WIN_W - 1)
    col_sel = (col_off[:, :, None] == np.arange(2 * NA_WIN_W - 1)).astype(np.float32)
    hi = lax.Precision.HIGHEST
    tabs = []
    for valid, row_sel in pats:
        by_row = jnp.einsum('hrc,isr->hisc', rel_bias_l.astype(F32), row_sel, precision=hi)
        bias = jnp.einsum('hisc,qkc->hiqsk', by_row, col_sel, precision=hi)
        bias = bias.reshape((bias.shape[0],) + valid.shape) * LOG2E
        tabs.append(jnp.where(valid[None], bias, MASK_VALUE))
    return jnp.stack(tabs, axis=0)


def _na_kernel(ss_ref, pat_ref, q_ref, k_ref, v_ref, kc_ref, vc_ref, bias_ref, o_ref, *, slab_len, n_heads):
    del pat_ref
    rb = pl.program_id(2)
    start = pl.multiple_of(ss_ref[rb] * GRID_W, GRID_W)
    for h in range(n_heads):
        sl = slice(h * HEAD_DIM, (h + 1) * HEAD_DIM)
        ks = k_ref[pl.ds(start, slab_len), sl]
        vs = v_ref[pl.ds(start, slab_len), sl]
        kc = kc_ref[:, sl].astype(BF16)
        vc = vc_ref[:, sl].astype(BF16)
        q = q_ref[:, sl]
        s_loc = lax.dot_general(q, ks, _NT, preferred_element_type=F32) + bias_ref[h]
        s_ctx = lax.dot_general(q, kc, _NT, preferred_element_type=F32)
        o_ref[:, sl] = _softmax_pv([s_loc, s_ctx], [vs, vc]).astype(BF16)


def _neighborhood_attention(proj, ck, cv, layer, bias_tab, plan):
    b, t, _ = proj.shape
    r_blk, slab, slab_start, pat_id, _ = plan
    qn = r_blk * GRID_W
    sn = slab * GRID_W
    l_ctx = ck.shape[2]
    hb = NA_HEADS_PER_STEP
    hw = hb * HEAD_DIM
    assert COL_QN % hb == 0 and COL_KN % hb == 0 and COL_VN % hb == 0 and N_HEADS_NA % hb == 0
    grid_spec = pltpu.PrefetchScalarGridSpec(
        num_scalar_prefetch=2,
        grid=(b, N_HEADS_NA // hb, t // qn),
        in_specs=[
            pl.BlockSpec((None, qn, hw), lambda bi, hg, rb, ss, pt: (bi, rb, COL_QN // hb + hg)),
            pl.BlockSpec((None, t, hw), lambda bi, hg, rb, ss, pt: (bi, 0, COL_KN // hb + hg)),
            pl.BlockSpec((None, t, hw), lambda bi, hg, rb, ss, pt: (bi, 0, COL_VN // hb + hg)),
            pl.BlockSpec((None, None, l_ctx, hw), lambda bi, hg, rb, ss, pt: (bi, layer, 0, hg)),
            pl.BlockSpec((None, None, l_ctx, hw), lambda bi, hg, rb, ss, pt: (bi, layer, 0, hg)),
            pl.BlockSpec((None, hb, qn, sn), lambda bi, hg, rb, ss, pt: (pt[rb], hg, 0, 0)),
        ],
        out_specs=pl.BlockSpec((None, qn, hw), lambda bi, hg, rb, ss, pt: (bi, rb, hg)),
    )
    return pl.pallas_call(
        functools.partial(_na_kernel, slab_len=sn, n_heads=hb),
        grid_spec=grid_spec,
        out_shape=jax.ShapeDtypeStruct((b, t, NA_W), BF16),
        compiler_params=_cparams(("arbitrary", "arbitrary", "arbitrary")),
        name="neighborhood_attn",
    )(jnp.asarray(slab_start), jnp.asarray(pat_id), proj, proj, proj, ck, cv, bias_tab)


def _layer_norm(y, g, b):
    mu = jnp.mean(y, axis=-1, keepdims=True)
    yc = y - mu
    var = jnp.mean(yc * yc, axis=-1, keepdims=True)
    return yc * lax.rsqrt(var + EPS) * g + b


def _top2_of4(vals):
    m1 = jnp.maximum(jnp.maximum(vals[0], vals[1]), jnp.maximum(vals[2], vals[3]))
    i1 = jnp.where(vals[0] == m1, 0, jnp.where(vals[1] == m1, 1, jnp.where(vals[2] == m1, 2, 3)))
    rest = [jnp.where(i1 == i, -1.0, vals[i]) for i in range(4)]
    m2 = jnp.maximum(jnp.maximum(rest[0], rest[1]), jnp.maximum(rest[2], rest[3]))
    i2 = jnp.where(rest[0] == m2, 0, jnp.where(rest[1] == m2, 1, jnp.where(rest[2] == m2, 2, 3)))
    return m1, i1, m2, i2


def _route_rows(logits_t):
    m = logits_t.max(axis=0, keepdims=True)
    e = jnp.exp(logits_t - m)
    probs = e / e.sum(axis=0, keepdims=True)
    rows = [probs[i:i + 1, :] for i in range(N_EXPERTS)]
    groups = [rows[g * EXPERTS_PER_GROUP:(g + 1) * EXPERTS_PER_GROUP] for g in range(N_GROUPS)]
    scores = []
    for g in range(N_GROUPS):
        m1, _, m2, _ = _top2_of4(groups[g])
        scores.append(m1 + m2)
    best = jnp.maximum(jnp.maximum(scores[0], scores[1]), jnp.maximum(scores[2], scores[3]))
    gi = jnp.where(scores[0] == best, 0, jnp.where(scores[1] == best, 1, jnp.where(scores[2] == best, 2, 3)))
    sel = [jnp.where(gi == 0, groups[0][i], jnp.where(gi == 1, groups[1][i],
                     jnp.where(gi == 2, groups[2][i], groups[3][i]))) for i in range(EXPERTS_PER_GROUP)]
    w1, l1, w2, l2 = _top2_of4(sel)
    wsum = w1 + w2
    return (gi * EXPERTS_PER_GROUP + l1, gi * EXPERTS_PER_GROUP + l2, w1 / wsum, w2 / wsum)


def _out_proj_kernel(oa_ref, on_ref, x_ref, mod_ref, wo_ref, g_ref, b_ref, wr_ref, br_ref, h2_in_ref,
                     x1_ref, h2_ref, e_ref, wt_ref):
    del h2_in_ref
    attn = jnp.dot(oa_ref[...], wo_ref[0:GA_Q, :], preferred_element_type=F32)
    attn = attn + jnp.dot(on_ref[...], wo_ref[GA_Q:GA_Q + NA_W, :], preferred_element_type=F32)
    gate1 = mod_ref[2:3, :]
    x1 = _layer_norm(DN_ALPHA * x_ref[...] + gate1 * attn, g_ref[...], b_ref[...])
    x1_ref[...] = x1
    h2 = x1 * (1.0 + mod_ref[4:5, :]) + mod_ref[3:4, :]
    h2_ref[...] = h2
    h_hi = h2.astype(BF16)
    h_lo = (h2 - h_hi.astype(F32)).astype(BF16)
    r_hi = jnp.dot(h_hi, wr_ref[...], preferred_element_type=F32)
    r_lo = jnp.dot(h_lo, wr_ref[...], preferred_element_type=F32)
    logits = r_hi + pltpu.roll(r_hi, LANE - N_EXPERTS, 1) + r_lo
    logits_t = logits.T[0:N_EXPERTS, :] + br_ref[...]
    e1, e2, w1, w2 = _route_rows(logits_t)
    e_ref[0:1, :] = e1
    e_ref[1:2, :] = e2
    n = w1.shape[1]
    row = lax.broadcasted_iota(jnp.int32, (LANE, n), 0)
    w_rows = jnp.where(row == 0, w1, jnp.where(row == 1, w2, 0.0))
    wt_ref[...] = w_rows.T


def _out_proj(oa, on, x, mod_l, mod_row0, w_o_bf, w_layer, ln_g, ln_b, wr_cat, b_router, h2_all, row0):
    b, t, d = x.shape
    tm = min(OUT_TM, t)
    nt = t // tm
    assert row0 % tm == 0
    tok = lambda bi, ti: (bi, ti, 0)
    const2 = lambda bi, ti: (0, 0)
    return pl.pallas_call(
        _out_proj_kernel,
        grid=(b, t // tm),
        in_specs=[
            pl.BlockSpec((None, tm, GA_Q), tok),
            pl.BlockSpec((None, tm, NA_W), tok),
            pl.BlockSpec((None, tm, d), tok),
            pl.BlockSpec((None, N_MOD, d), lambda bi, ti: (bi + mod_row0, 0, 0)),
            pl.BlockSpec((None, GA_Q + NA_W, d), lambda bi, ti: (w_layer, 0, 0)),
            pl.BlockSpec((1, d), const2),
            pl.BlockSpec((1, d), const2),
            pl.BlockSpec((d, LANE), const2),
            pl.BlockSpec((N_EXPERTS, 1), const2),
            pl.BlockSpec(memory_space=pl.ANY),
        ],
        out_specs=[
            pl.BlockSpec((None, tm, d), tok),
            pl.BlockSpec((tm, d), lambda bi, ti: (row0 // tm + bi * nt + ti, 0)),
            pl.BlockSpec((None, TOP_K, tm), lambda bi, ti: (bi, 0, ti)),
            pl.BlockSpec((None, tm, LANE), tok),
        ],
        out_shape=[
            jax.ShapeDtypeStruct((b, t, d), F32),
            jax.ShapeDtypeStruct(h2_all.shape, F32),
            jax.ShapeDtypeStruct((b, TOP_K, t), jnp.int32),
            jax.ShapeDtypeStruct((b, t, LANE), F32),
        ],
        input_output_aliases={9: 1},
        compiler_params=_cparams(("arbitrary", "arbitrary")),
        name="out_proj_ln_router",
    )(oa, on, x, mod_l, w_o_bf, ln_g.reshape(1, d), ln_b.reshape(1, d), wr_cat,
      b_router.reshape(N_EXPERTS, 1).astype(F32), h2_all)


def _row_copy(src_hbm, idx_ref, dst, sem, r):
    return pltpu.make_async_copy(src_hbm.at[pl.ds(idx_ref[0, r], 1)], dst.at[pl.ds(r, 1)], sem)


def _start_row_gather(src_hbm, idx_ref, dst, sem, n_rows):
    def body(r, carry):
        _row_copy(src_hbm, idx_ref, dst, sem, r).start()
        return carry
    lax.fori_loop(0, n_rows, body, 0, unroll=8)


def _start_row_gather_inline(src_hbm, idx_ref, dst, sem, r_lo, r_hi):
    for r in range(r_lo, r_hi):
        _row_copy(src_hbm, idx_ref, dst, sem, r).start()


def _wait_row_gather(src_hbm, dst, sem, n_rows):
    pltpu.make_async_copy(src_hbm.at[pl.ds(0, n_rows)], dst, sem).wait()


def _moe_kernel(be_ref, nu_ref, tok0_ref, tok1_ref, tok2_ref, h_hbm, wg_ref, wu_ref, wd_ref, o_ref,
                buf, sem, *, n_blocks):
    del be_ref
    i = pl.program_id(0)
    n_used = nu_ref[0]
    slot = i % GATHER_DEPTH
    ahead = (i + 2) % GATHER_DEPTH

    @pl.when(i == 0)
    def _():
        _start_row_gather(h_hbm, tok0_ref, buf.at[0], sem.at[0], MOE_BLK)
        _start_row_gather(h_hbm, tok1_ref, buf.at[1], sem.at[1], MOE_BLK)

    _wait_row_gather(h_hbm, buf.at[slot], sem.at[slot], MOE_BLK)

    @pl.when(i < n_used)
    def _():
        xb = buf[slot].astype(BF16)
        gate = jnp.dot(xb, wg_ref[...], preferred_element_type=F32)
        up = jnp.dot(xb, wu_ref[...], preferred_element_type=F32)
        act = (gate / (1.0 + jnp.exp(-gate)) * up).astype(BF16)
        _start_row_gather_inline(h_hbm, tok2_ref, buf.at[ahead], sem.at[ahead], 0, MOE_BLK)
        o_ref[...] = jnp.dot(act, wd_ref[...], preferred_element_type=F32)

    @pl.when(i >= n_used)
    def _():
        o_ref[...] = jnp.zeros_like(o_ref)
        _start_row_gather(h_hbm, tok2_ref, buf.at[ahead], sem.at[ahead], MOE_BLK)

    @pl.when(i == n_blocks - 1)
    def _():
        for k in (1, 2):
            s = (i + k) % GATHER_DEPTH
            _wait_row_gather(h_hbm, buf.at[s], sem.at[s], MOE_BLK)


def _moe_ffn(h2_flat, tok_buf, blk_e, n_used, wg, wu, wd, layer):
    n, d = h2_flat.shape
    n_blocks = blk_e.shape[0]
    d_ff = wg.shape[3]
    tok3 = tok_buf.reshape(n_blocks, 1, MOE_BLK)
    smem_blk = lambda f: pl.BlockSpec((None, 1, MOE_BLK), f, memory_space=pltpu.SMEM)
    grid_spec = pltpu.PrefetchScalarGridSpec(
        num_scalar_prefetch=2,
        grid=(n_blocks,),
        in_specs=[
            smem_blk(lambda i, be, nu: (i, 0, 0)),
            smem_blk(lambda i, be, nu: (jnp.minimum(i + 1, n_blocks - 1), 0, 0)),
            smem_blk(lambda i, be, nu: (jnp.minimum(i + 2, n_blocks - 1), 0, 0)),
            pl.BlockSpec(memory_space=pl.ANY),
            pl.BlockSpec((None, None, d, d_ff), lambda i, be, nu: (layer, be[i], 0, 0)),
            pl.BlockSpec((None, None, d, d_ff), lambda i, be, nu: (layer, be[i], 0, 0)),
            pl.BlockSpec((None, None, d_ff, d), lambda i, be, nu: (layer, be[i], 0, 0)),
        ],
        out_specs=pl.BlockSpec((MOE_BLK, d), lambda i, be, nu: (i, 0)),
        scratch_shapes=[pltpu.VMEM((GATHER_DEPTH, MOE_BLK, d), F32), pltpu.SemaphoreType.DMA((GATHER_DEPTH,))],
    )
    return pl.pallas_call(
        functools.partial(_moe_kernel, n_blocks=n_blocks),
        grid_spec=grid_spec,
        out_shape=jax.ShapeDtypeStruct((n_blocks * MOE_BLK, d), F32),
        compiler_params=_cparams(("arbitrary",)),
        name="moe_ffn",
    )(blk_e, n_used, tok3, tok3, tok3, h2_flat, wg, wu, wd)


def _combine_kernel(da0_ref, da1_ref, db0_ref, db1_ref, dc0_ref, dc1_ref, y_hbm, x1_ref, mod_ref, wt_ref,
                    g_ref, b_ref, o_ref, buf, sem, *, n_steps, tm):
    i = pl.program_id(0)
    slot = i % GATHER_DEPTH
    ahead = (i + 2) % GATHER_DEPTH

    @pl.when(i == 0)
    def _():
        for s, (r0, r1) in enumerate(((da0_ref, da1_ref), (db0_ref, db1_ref))):
            _start_row_gather(y_hbm, r0, buf.at[s, 0], sem.at[s], tm)
            _start_row_gather(y_hbm, r1, buf.at[s, 1], sem.at[s], tm)

    def wait(s):
        _wait_row_gather(y_hbm, buf.at[s, 0], sem.at[s], tm)
        _wait_row_gather(y_hbm, buf.at[s, 1], sem.at[s], tm)

    wait(slot)
    y = wt_ref[:, 0:1] * buf[slot, 0] + wt_ref[:, 1:2] * buf[slot, 1]
    gate2 = mod_ref[5:6, :]
    o_ref[...] = _layer_norm(DN_ALPHA * x1_ref[...] + gate2 * y, g_ref[...], b_ref[...])
    _start_row_gather_inline(y_hbm, dc0_ref, buf.at[ahead, 0], sem.at[ahead], 0, tm)
    _start_row_gather_inline(y_hbm, dc1_ref, buf.at[ahead, 1], sem.at[ahead], 0, tm)

    @pl.when(i == n_steps - 1)
    def _():
        for k in (1, 2):
            wait((i + k) % GATHER_DEPTH)


def _combine(y_sorted, dest, x1, mod_l, mod_row0, wt, ln_g, ln_b):
    b, t, d = x1.shape
    tm = min(OUT_TM, t)
    nt = t // tm
    n_steps = b * nt
    dest4 = dest.reshape(b, TOP_K, nt, tm).transpose(0, 2, 1, 3).reshape(n_steps, TOP_K, 1, tm)
    later = lambda i, k: jnp.minimum(i + k, n_steps - 1)
    smem_blk = lambda f: pl.BlockSpec((None, None, 1, tm), f, memory_space=pltpu.SMEM)
    tok = lambda i: (i // nt, i % nt, 0)
    return pl.pallas_call(
        functools.partial(_combine_kernel, n_steps=n_steps, tm=tm),
        grid=(n_steps,),
        in_specs=[
            smem_blk(lambda i: (i, 0, 0, 0)),
            smem_blk(lambda i: (i, 1, 0, 0)),
            smem_blk(lambda i: (later(i, 1), 0, 0, 0)),
            smem_blk(lambda i: (later(i, 1), 1, 0, 0)),
            smem_blk(lambda i: (later(i, 2), 0, 0, 0)),
            smem_blk(lambda i: (later(i, 2), 1, 0, 0)),
            pl.BlockSpec(memory_space=pl.ANY),
            pl.BlockSpec((None, tm, d), tok),
            pl.BlockSpec((None, N_MOD, d), lambda i: (i // nt + mod_row0, 0, 0)),
            pl.BlockSpec((None, tm, LANE), tok),
            pl.BlockSpec((1, d), lambda i: (0, 0)),
            pl.BlockSpec((1, d), lambda i: (0, 0)),
        ],
        out_specs=pl.BlockSpec((None, tm, d), tok),
        out_shape=jax.ShapeDtypeStruct((b, t, d), F32),
        scratch_shapes=[pltpu.VMEM((GATHER_DEPTH, TOP_K, tm, d), F32), pltpu.SemaphoreType.DMA((GATHER_DEPTH,))],
        compiler_params=_cparams(("arbitrary",)),
        name="moe_combine_ln",
    )(dest4, dest4, dest4, dest4, dest4, dest4, y_sorted, x1, mod_l, wt, ln_g.reshape(1, d), ln_b.reshape(1, d))


def _dispatch_plan(e_idxs, row0s):
    flat_e = jnp.concatenate([e_idx.reshape(-1) for e_idx in e_idxs])
    a = flat_e.shape[0]
    onehot = (flat_e[:, None] == jnp.arange(N_EXPERTS, dtype=jnp.int32)[None, :]).astype(jnp.int32)
    counts = onehot.sum(axis=0)
    chunk = 256
    assert a % chunk == 0
    oh = onehot.reshape(a // chunk, chunk, N_EXPERTS).astype(F32)
    tri = jnp.tril(jnp.ones((chunk, chunk), F32))
    within = jnp.einsum('ij,cje->cie', tri, oh, precision=lax.Precision.HIGHEST)
    before = jnp.cumsum(oh.sum(axis=1), axis=0) - oh.sum(axis=1)
    running = (within + before[:, None, :]).reshape(a, N_EXPERTS).astype(jnp.int32)
    rank = (running * onehot).sum(axis=1) - 1
    padded = (counts + MOE_BLK - 1) // MOE_BLK * MOE_BLK
    pad_end = jnp.cumsum(padded)
    pad_start = pad_end - padded
    dest = pad_start[flat_e] + rank
    n_blocks = -(-a // MOE_BLK) + N_EXPERTS
    p = n_blocks * MOE_BLK
    blk_start = jnp.arange(n_blocks, dtype=jnp.int32) * MOE_BLK
    blk_e = jnp.minimum((blk_start[:, None] >= pad_end[None, :]).sum(axis=1), N_EXPERTS - 1).astype(jnp.int32)
    n_used = (pad_end[-1] // MOE_BLK).astype(jnp.int32)
    last_e = blk_e[jnp.maximum(n_used - 1, 0)]
    blk_e = jnp.where(jnp.arange(n_blocks) < n_used, blk_e, last_e)
    assert a < (1 << 16)
    order = lax.sort(flat_e * (1 << 16) + jnp.arange(a, dtype=jnp.int32)) & 0xFFFF
    start = jnp.cumsum(counts) - counts
    order = jnp.concatenate([order, jnp.zeros((MOE_BLK,), jnp.int32)])
    win = jnp.clip(start[blk_e] + blk_start - pad_start[blk_e], 0, a)
    order_blk = order[(win[:, None] + jnp.arange(MOE_BLK, dtype=jnp.int32)[None, :]).reshape(p)]
    tok_buf, o = jnp.zeros((p,), jnp.int32), 0
    for e_idx, row0 in zip(e_idxs, row0s):
        b, k, t = e_idx.shape
        local = order_blk - o
        rows = row0 + (local // (k * t)) * t + local % t
        tok_buf = jnp.where((local >= 0) & (local < e_idx.size), rows, tok_buf)
        o += e_idx.size
    dests, o = [], 0
    for e_idx in e_idxs:
        dests.append(dest[o:o + e_idx.size].reshape(e_idx.shape))
        o += e_idx.size
    return dests, tok_buf, blk_e, n_used.reshape(1)


def _attention_sublayer(x, mod_l, mod_row0, lw, ctx):
    b, t, d = x.shape
    layer = lw['layer']
    if ctx is None:
        xt = x.reshape(1, b * t, d)
        proj, *cache = _in_proj(xt, mod_l, mod_row0, lw['w_in'], layer, lw['q_gain'], lw['k_gain'],
                                cache=(layer, lw['depth'], b, t, lw['prev_cache']))
        proj = proj.reshape(b, t, -1)
        oa = _dense_attention(proj, COL_QA, COL_KA, COL_VA, N_KV_GA, N_HEADS_GA // N_KV_GA)
        on = _dense_attention(proj, COL_QN, COL_KN, COL_VN, N_HEADS_NA, 1)
        return oa.reshape(1, b * t, -1), on.reshape(1, b * t, -1), xt, cache
    ga_k, ga_v, na_k, na_v, rope_tabs, na_bias, na_plan = ctx
    (proj,) = _in_proj(x, mod_l, mod_row0, lw['w_in'], layer, lw['q_gain'], lw['k_gain'], rope_tabs=rope_tabs)
    oa = _dense_attention(proj, COL_QA, COL_KA, COL_VA, N_KV_GA, N_HEADS_GA // N_KV_GA, ctx=(ga_k, ga_v, layer))
    on = _neighborhood_attention(proj, na_k, na_v, layer, na_bias, na_plan)
    return oa, on, x, None


def _layer(x_ctx, x_lat, mod_l, lw, shared, lat_ctx, h2_all):
    bc, tc, d = x_ctx.shape
    bl, tl, _ = x_lat.shape
    n_ctx, n_lat = bc * tc, bl * tl
    streams = [_attention_sublayer(x_ctx, mod_l, bl, lw, None),
               _attention_sublayer(x_lat, mod_l, 0, lw, lat_ctx)]
    cache = streams[0][3]
    mod_rows, row0s = (bl, 0), (0, n_ctx)
    x1s, e_idxs, wts = [], [], []
    for (oa, on, xt, _), mod_row0, row0 in zip(streams, mod_rows, row0s):
        x1, h2_all, e_idx, wt = _out_proj(oa, on, xt, mod_l, mod_row0, lw['w_o'], lw['layer'], lw['ln1_g'],
                                          lw['ln1_b'], shared['wr_cat'], shared['b_router'], h2_all, row0)
        x1s.append(x1)
        e_idxs.append(e_idx)
        wts.append(wt)
    dests, tok_buf, blk_e, n_used = _dispatch_plan(e_idxs, row0s)
    y_sorted = _moe_ffn(h2_all, tok_buf, blk_e, n_used, lw['w_gate'], lw['w_up'], lw['w_down'], lw['layer'])
    outs = [_combine(y_sorted, dest, x1, mod_l, mod_row0, wt, lw['ln2_g'], lw['ln2_b'])
            for dest, x1, wt, mod_row0 in zip(dests, x1s, wts, mod_rows)]
    return outs[0].reshape(bc, tc, d), outs[1].reshape(bl, tl, d), cache, h2_all


def kernel(x_prompt, x_sample, c, cache_ga_k, cache_ga_v, cache_na_k, cache_na_v, c_ctx, w_router, b_router, w_mod, b_mod, w_in, q_norm, k_norm, rel_bias, w_o, ln1_g, ln1_b, ln2_g, ln2_b, w_gate, w_up, w_down):
    bsz, seq, d = x_prompt.shape
    dec_b, dec_t, _ = x_sample.shape
    depth = w_mod.shape[0]
    past = cache_ga_k.shape[2]

    mod_rows = 16
    assert dec_b + 1 <= mod_rows
    cvec = jnp.concatenate([c, c_ctx[None, :], jnp.zeros((mod_rows - dec_b - 1, d), F32)], axis=0)
    mod = _modulation(cvec, w_mod, b_mod).reshape(depth, mod_rows, N_MOD, d)

    wr_hi = w_router.astype(BF16)
    wr_lo = (w_router - wr_hi.astype(F32)).astype(BF16)
    wr_cat = jnp.concatenate([wr_hi, wr_lo, jnp.zeros((d, LANE - 2 * N_EXPERTS), BF16)], axis=1)
    shared = {'wr_cat': wr_cat, 'b_router': b_router}

    rope_tabs = _rope_tables(dec_t)
    na_plan = _na_plan(dec_t // GRID_W)
    ga_k = cache_ga_k.reshape(dec_b, depth, past, GA_KV)
    ga_v = cache_ga_v.reshape(dec_b, depth, past, GA_KV)
    na_k = cache_na_k.reshape(dec_b, depth, past, NA_W)
    na_v = cache_na_v.reshape(dec_b, depth, past, NA_W)

    w_in_bf, w_o_bf = w_in.astype(BF16), w_o.astype(BF16)
    w_gate_bf, w_up_bf, w_down_bf = w_gate.astype(BF16), w_up.astype(BF16), w_down.astype(BF16)

    y_prompt, y_sample = x_prompt, x_sample
    cache = [jnp.zeros((bsz, depth, seq, w), F32) for w in (GA_KV, GA_KV, NA_W, NA_W)]
    h2_all = jnp.zeros((bsz * seq + dec_b * dec_t, d), F32)
    for i in range(depth):
        lw = {
            'layer': i, 'depth': depth, 'prev_cache': cache,
            'w_in': w_in_bf, 'w_o': w_o_bf,
            'q_gain': (q_norm[i] * Q_PRESCALE).reshape(1, HEAD_DIM).astype(F32),
            'k_gain': k_norm[i].reshape(1, HEAD_DIM).astype(F32),
            'ln1_g': ln1_g[i], 'ln1_b': ln1_b[i], 'ln2_g': ln2_g[i], 'ln2_b': ln2_b[i],
            'w_gate': w_gate_bf, 'w_up': w_up_bf, 'w_down': w_down_bf,
        }
        na_bias = _na_bias_table(rel_bias[i], na_plan[4])
        y_prompt, y_sample, cache, h2_all = _layer(y_prompt, y_sample, mod[i], lw, shared,
                                                   (ga_k, ga_v, na_k, na_v, rope_tabs, na_bias, na_plan), h2_all)

    outs = [arr.reshape(bsz, depth, seq, -1, HEAD_DIM) for arr in cache]
    return (y_prompt, y_sample, outs[0], outs[1], outs[2], outs[3])
```

```python
import functools
import math

import numpy as np
import jax
import jax.numpy as jnp
from jax import lax
from jax.experimental import pallas as pl
from jax.experimental.pallas import tpu as pltpu

F32 = jnp.float32
BF16 = jnp.bfloat16

DEPTH = 2
GRID_W = 64
HEAD_DIM = 128
N_HEADS_GA = 8
N_KV_GA = 2
N_HEADS_NA = 8
NA_WIN_H = 8
NA_WIN_W = 16
ROPE_THETA = 10000.0
N_EXPERTS = 16
N_GROUPS = 4
EXPERTS_PER_GROUP = N_EXPERTS // N_GROUPS
TOP_K = 2
N_MOD = 6
DN_ALPHA = (2.0 * DEPTH) ** 0.25
EPS = 1e-6
ATTN_SCALE = HEAD_DIM ** -0.5
LOG2E = math.log2(math.e)
Q_PRESCALE = ATTN_SCALE * LOG2E
MASK_VALUE = -1e30

GA_Q = N_HEADS_GA * HEAD_DIM
GA_KV = N_KV_GA * HEAD_DIM
NA_W = N_HEADS_NA * HEAD_DIM
COL_QA = 0
COL_KA = COL_QA + N_HEADS_GA
COL_VA = COL_KA + N_KV_GA
COL_QN = COL_VA + N_KV_GA
COL_KN = COL_QN + N_HEADS_NA
COL_VN = COL_KN + N_HEADS_NA
IN_HEADS = COL_VN + N_HEADS_NA

LANE = 128
PROJ_TN = 4 * HEAD_DIM
PROJ_TM = 1024
ATTN_TQ = 512
NA_ROWS = 4
NA_HEADS_PER_STEP = 4
NA_BLOCKS_PER_STEP = 2
GA_KV_PER_STEP = 2
OUT_TM = 256
MOE_BLK = 256
GATHER_DEPTH = 3
MOD_TN = 1024
VMEM_LIMIT = 52 * 1024 * 1024


def _cparams(sem):
    return pltpu.CompilerParams(dimension_semantics=sem, vmem_limit_bytes=VMEM_LIMIT)


def _mod_kernel(c_ref, w_ref, b_ref, o_ref):
    c = c_ref[...]
    s = c / (1.0 + jnp.exp(-c))
    o_ref[...] = jnp.dot(s.astype(BF16), w_ref[...].astype(BF16),
                         preferred_element_type=F32) + b_ref[...]


def _modulation(cvec, w_mod, b_mod):
    depth, d, e = w_mod.shape
    rows = cvec.shape[0]
    tn = MOD_TN if e % MOD_TN == 0 else e
    return pl.pallas_call(
        _mod_kernel,
        grid=(depth, e // tn),
        in_specs=[
            pl.BlockSpec((rows, d), lambda l, j: (0, 0)),
            pl.BlockSpec((None, d, tn), lambda l, j: (l, 0, j)),
            pl.BlockSpec((None, 1, tn), lambda l, j: (l, 0, j)),
        ],
        out_specs=pl.BlockSpec((None, rows, tn), lambda l, j: (l, 0, j)),
        out_shape=jax.ShapeDtypeStruct((depth, rows, e), F32),
        compiler_params=_cparams(("arbitrary", "arbitrary")),
        name="modulation",
    )(cvec, w_mod, b_mod.reshape(depth, 1, e))


def _rms_head(x, g):
    ms = jnp.mean(x * x, axis=-1, keepdims=True)
    return x * lax.rsqrt(ms + EPS) * g


def _rope_head(x, cos, sin, first_half):
    swapped = jnp.where(first_half, pltpu.roll(x, HEAD_DIM - 32, 1), pltpu.roll(x, 32, 1))
    return x * cos + swapped * sin


def _proj_kernel(*refs, rope, cache):
    x_ref, mod_ref, w_ref, qg_ref, kg_ref = refs[:5]
    pos = 5
    if rope:
        cos_ref, sin_ref = refs[pos:pos + 2]
        pos += 2
    if cache is not None:
        pos += cache[0]
    o_ref = refs[pos]
    pos += 1
    if cache is not None:
        ka_ref, va_ref, kn_ref, vn_ref = refs[pos:pos + 4]
        pos += 4
    h_scr = refs[pos]

    j = pl.program_id(2)

    @pl.when(j == 0)
    def _():
        shift = mod_ref[0:1, :]
        scale = mod_ref[1:2, :]
        h_scr[...] = (x_ref[...] * (1.0 + scale) + shift).astype(BF16)

    acc = jnp.dot(h_scr[...], w_ref[...], preferred_element_type=F32)

    if rope:
        lane = lax.broadcasted_iota(jnp.int32, (1, HEAD_DIM), 1)
        first_half = (lane % 64) < 32

    def head(hh):
        return acc[:, hh * HEAD_DIM:(hh + 1) * HEAD_DIM]

    def normed(hh, g):
        y = _rms_head(head(hh), g)
        if rope:
            y = _rope_head(y, cos_ref[...], sin_ref[...], first_half)
        return y

    def put(hh, y, c_ref=None, c_head=0):
        o_ref[:, hh * HEAD_DIM:(hh + 1) * HEAD_DIM] = y.astype(BF16)
        if c_ref is not None:
            c_ref[:, :, c_head * HEAD_DIM:(c_head + 1) * HEAD_DIM] = y.reshape(cache[1], cache[2], HEAD_DIM)

    hpt = PROJ_TN // HEAD_DIM
    j_ka, j_qn, j_kn, j_vn = COL_KA // hpt, COL_QN // hpt, COL_KN // hpt, COL_VN // hpt

    @pl.when(j < j_ka)
    def _():
        for hh in range(hpt):
            put(hh, normed(hh, qg_ref[...]))

    @pl.when(j == j_ka)
    def _():
        for hh in range(N_KV_GA):
            put(hh, normed(hh, kg_ref[...]), ka_ref if cache else None, hh)
        for hh in range(N_KV_GA):
            put(N_KV_GA + hh, head(N_KV_GA + hh), va_ref if cache else None, hh)

    @pl.when((j >= j_qn) & (j < j_kn))
    def _():
        o_ref[...] = (acc * Q_PRESCALE).astype(BF16)

    @pl.when((j >= j_kn) & (j < j_vn))
    def _():
        o_ref[...] = acc.astype(BF16)
        if cache is not None:
            kn_ref[...] = acc.reshape(cache[1], cache[2], PROJ_TN)

    @pl.when(j >= j_vn)
    def _():
        o_ref[...] = acc.astype(BF16)
        if cache is not None:
            vn_ref[...] = acc.reshape(cache[1], cache[2], PROJ_TN)


def _in_proj(x, mod_l, mod_row0, w_in_bf, w_layer, q_gain, k_gain, rope_tabs=None, cache=None):
    b, t, d = x.shape
    in_w = w_in_bf.shape[2]
    tm = min(PROJ_TM, t)
    nj = in_w // PROJ_TN
    rope = rope_tabs is not None
    hpt = PROJ_TN // HEAD_DIM
    j_kn, j_vn = COL_KN // hpt, COL_VN // hpt
    assert COL_KA % hpt == 0 and COL_QN % hpt == 0 and COL_KN % hpt == 0 and COL_VN % hpt == 0
    assert 2 * N_KV_GA == hpt and COL_VA == COL_KA + N_KV_GA

    in_specs = [
        pl.BlockSpec((None, tm, d), lambda bi, ti, j: (bi, ti, 0)),
        pl.BlockSpec((None, N_MOD, d), lambda bi, ti, j: (bi + mod_row0, 0, 0)),
        pl.BlockSpec((None, d, PROJ_TN), lambda bi, ti, j: (w_layer, 0, j)),
        pl.BlockSpec((1, HEAD_DIM), lambda bi, ti, j: (0, 0)),
        pl.BlockSpec((1, HEAD_DIM), lambda bi, ti, j: (0, 0)),
    ]
    args = [x, mod_l, w_in_bf, q_gain, k_gain]
    if rope:
        in_specs += [pl.BlockSpec((tm, HEAD_DIM), lambda bi, ti, j: (ti, 0))] * 2
        args += list(rope_tabs)
    out_specs = [pl.BlockSpec((None, tm, PROJ_TN), lambda bi, ti, j: (bi, ti, j))]
    out_shape = [jax.ShapeDtypeStruct((b, t, in_w), BF16)]
    aliases = {}
    cache_cfg = None
    if cache is not None:
        layer, depth, bsz, seq, prev = cache
        assert b == 1 and tm % seq == 0
        nb = tm // seq
        n_prev = 0 if prev is None else len(prev)
        cache_cfg = (n_prev, nb, seq)
        if prev is not None:
            for k, arr in enumerate(prev):
                aliases[len(args)] = 1 + k
                in_specs.append(pl.BlockSpec(memory_space=pl.ANY))
                args.append(arr)
        kv = lambda bi, ti, j: (ti, layer, 0, 0)
        kn = lambda bi, ti, j: (ti, layer, 0, jnp.clip(j - j_kn, 0, N_HEADS_NA // hpt - 1))
        vn = lambda bi, ti, j: (ti, layer, 0, jnp.clip(j - j_vn, 0, N_HEADS_NA // hpt - 1))
        out_specs += [pl.BlockSpec((nb, None, seq, GA_KV), kv), pl.BlockSpec((nb, None, seq, GA_KV), kv),
                      pl.BlockSpec((nb, None, seq, PROJ_TN), kn), pl.BlockSpec((nb, None, seq, PROJ_TN), vn)]
        out_shape += [jax.ShapeDtypeStruct((bsz, depth, seq, w), F32) for w in (GA_KV, GA_KV, NA_W, NA_W)]
    outs = pl.pallas_call(
        functools.partial(_proj_kernel, rope=rope, cache=cache_cfg),
        grid=(b, t // tm, nj),
        in_specs=in_specs,
        out_specs=out_specs,
        out_shape=out_shape,
        input_output_aliases=aliases,
        scratch_shapes=[pltpu.VMEM((tm, d), BF16)],
        compiler_params=_cparams(("arbitrary", "arbitrary", "arbitrary")),
        name="in_proj_rope" if rope else "in_proj_cache",
    )(*args)
    return outs


def _rope_tables(t):
    half = HEAD_DIM // 4
    tt = jnp.arange(t, dtype=jnp.int32)
    row = (tt // GRID_W).astype(F32)
    col = (tt % GRID_W).astype(F32)
    inv_freq = 1.0 / (ROPE_THETA ** (jnp.arange(half, dtype=F32) / half))
    ar = row[:, None] * inv_freq[None, :]
    ac = col[:, None] * inv_freq[None, :]
    cos = jnp.concatenate([jnp.cos(ar), jnp.cos(ar), jnp.cos(ac), jnp.cos(ac)], axis=-1)
    sin = jnp.concatenate([-jnp.sin(ar), jnp.sin(ar), -jnp.sin(ac), jnp.sin(ac)], axis=-1)
    return cos, sin


_NT = (((1,), (1,)), ((), ()))


def _softmax_pv(scores, values):
    m = scores[0].max(axis=-1, keepdims=True)
    for s in scores[1:]:
        m = jnp.maximum(m, s.max(axis=-1, keepdims=True))
    l = None
    acc = None
    for s, v in zip(scores, values):
        p = jnp.exp2(s - m)
        ps = p.sum(axis=-1, keepdims=True)
        pv = jnp.dot(p.astype(BF16), v, preferred_element_type=F32)
        l = ps if l is None else l + ps
        acc = pv if acc is None else acc + pv
    return acc * (1.0 / l)


def _dense_attn_kernel(*refs, r_heads, gs, has_ctx):
    if has_ctx:
        q_ref, k_ref, v_ref, kc_ref, vc_ref, o_ref = refs
    else:
        q_ref, k_ref, v_ref, o_ref = refs
    for gi in range(gs):
        kv = slice(gi * HEAD_DIM, (gi + 1) * HEAD_DIM)
        k = k_ref[:, kv]
        v = v_ref[:, kv]
        if has_ctx:
            kc = kc_ref[:, kv].astype(BF16)
            vc = vc_ref[:, kv].astype(BF16)
        for r in range(r_heads):
            h = gi * r_heads + r
            sl = slice(h * HEAD_DIM, (h + 1) * HEAD_DIM)
            q = q_ref[:, sl]
            scores = [lax.dot_general(q, k, _NT, preferred_element_type=F32)]
            values = [v]
            if has_ctx:
                scores.append(lax.dot_general(q, kc, _NT, preferred_element_type=F32))
                values.append(vc)
            o_ref[:, sl] = _softmax_pv(scores, values).astype(BF16)


def _dense_attention(proj, q_col, k_col, v_col, groups, r_heads, gs=1, ctx=None):
    b, t, _ = proj.shape
    tq = min(ATTN_TQ, t)
    qw = gs * r_heads * HEAD_DIM
    kw = gs * HEAD_DIM
    assert q_col % (gs * r_heads) == 0 and k_col % gs == 0 and v_col % gs == 0 and groups % gs == 0
    in_specs = [
        pl.BlockSpec((None, tq, qw), lambda bi, g, qi: (bi, qi, q_col // (gs * r_heads) + g)),
        pl.BlockSpec((None, t, kw), lambda bi, g, qi: (bi, 0, k_col // gs + g)),
        pl.BlockSpec((None, t, kw), lambda bi, g, qi: (bi, 0, v_col // gs + g)),
    ]
    args = [proj, proj, proj]
    if ctx is not None:
        ck, cv, layer = ctx
        l_ctx = ck.shape[2]
        spec = pl.BlockSpec((None, None, l_ctx, kw), lambda bi, g, qi: (bi, layer, 0, g))
        in_specs += [spec, spec]
        args += [ck, cv]
    return pl.pallas_call(
        functools.partial(_dense_attn_kernel, r_heads=r_heads, gs=gs, has_ctx=ctx is not None),
        grid=(b, groups // gs, t // tq),
        in_specs=in_specs,
        out_specs=pl.BlockSpec((None, tq, qw), lambda bi, g, qi: (bi, qi, g)),
        out_shape=jax.ShapeDtypeStruct((b, t, groups * r_heads * HEAD_DIM), BF16),
        compiler_params=_cparams(("arbitrary", "arbitrary", "arbitrary")),
        name="dense_attn_ctx" if ctx is not None else "dense_attn",
    )(*args)


def _na_plan(rows):
    kh = min(NA_WIN_H, rows)
    kw = NA_WIN_W
    r_blk = min(NA_ROWS, rows)
    assert rows % r_blk == 0
    slab = min(r_blk - 1 + kh, rows)
    row_start = np.clip(np.arange(rows) - kh // 2, 0, rows - kh)
    col = np.arange(GRID_W)
    col_start = np.clip(col - kw // 2, 0, GRID_W - kw)
    slab_start, pat_id, pats, sigs = [], [], [], {}
    for r0 in range(0, rows, r_blk):
        ss = min(row_start[r0], rows - slab)
        rel = tuple(int(row_start[r0 + ri] - ss) for ri in range(r_blk))
        sig = (int(ss - r0), rel)
        if sig not in sigs:
            sigs[sig] = len(pats)
            q_row = r0 + np.arange(r_blk)[:, None, None, None]
            q_col = col[None, :, None, None]
            k_row = ss + np.arange(slab)[None, None, :, None]
            k_col = col[None, None, None, :]
            rs = row_start[r0:r0 + r_blk][:, None, None, None]
            cs = col_start[None, :, None, None]
            valid = (k_row >= rs) & (k_row < rs + kh) & (k_col >= cs) & (k_col < cs + kw)
            row_off = (k_row - q_row + (NA_WIN_H - 1))[:, 0, :, 0]
            row_sel = (row_off[:, :, None] == np.arange(2 * NA_WIN_H - 1)).astype(np.float32)
            full = (r_blk, GRID_W, slab, GRID_W)
            pats.append((np.broadcast_to(valid, full).reshape(r_blk * GRID_W, slab * GRID_W), row_sel))
        slab_start.append(int(ss))
        pat_id.append(sigs[sig])
    return r_blk, slab, np.array(slab_start, np.int32), np.array(pat_id, np.int32), pats


def _na_bias_table(rel_bias_l, pats):
    col = np.arange(GRID_W)
    col_off = col[None, :] - col[:, None] + (NA_WIN_W - 1)
    col_sel = (col_off[:, :, None] == np.arange(2 * NA_WIN_W - 1)).astype(np.float32)
    hi = lax.Precision.HIGHEST
    tabs = []
    for valid, row_sel in pats:
        by_row = jnp.einsum('hrc,isr->hisc', rel_bias_l.astype(F32), row_sel, precision=hi)
        bias = jnp.einsum('hisc,qkc->hiqsk', by_row, col_sel, precision=hi)
        bias = bias.reshape((bias.shape[0],) + valid.shape) * LOG2E
        tabs.append(jnp.where(valid[None], bias, MASK_VALUE))
    return jnp.stack(tabs, axis=0)


def _na_kernel(ss_ref, pat_ref, q_ref, k_ref, v_ref, kc_ref, vc_ref, *rest, slab_len, n_heads, n_sub, qn):
    del pat_ref
    bias_refs, o_ref = rest[:n_sub], rest[n_sub]
    rb = pl.program_id(2)
    for h in range(n_heads):
        sl = slice(h * HEAD_DIM, (h + 1) * HEAD_DIM)
        kc = kc_ref[:, sl].astype(BF16)
        vc = vc_ref[:, sl].astype(BF16)
        for u in range(n_sub):
            start = pl.multiple_of(ss_ref[rb * n_sub + u] * GRID_W, GRID_W)
            rows = slice(u * qn, (u + 1) * qn)
            ks = k_ref[pl.ds(start, slab_len), sl]
            vs = v_ref[pl.ds(start, slab_len), sl]
            q = q_ref[rows, sl]
            s_loc = lax.dot_general(q, ks, _NT, preferred_element_type=F32) + bias_refs[u][h]
            s_ctx = lax.dot_general(q, kc, _NT, preferred_element_type=F32)
            o_ref[rows, sl] = _softmax_pv([s_loc, s_ctx], [vs, vc]).astype(BF16)


def _neighborhood_attention(proj, ck, cv, layer, bias_tab, plan):
    b, t, _ = proj.shape
    r_blk, slab, slab_start, pat_id, _ = plan
    qn = r_blk * GRID_W
    sn = slab * GRID_W
    l_ctx = ck.shape[2]
    hb = NA_HEADS_PER_STEP
    hw = hb * HEAD_DIM
    assert COL_QN % hb == 0 and COL_KN % hb == 0 and COL_VN % hb == 0 and N_HEADS_NA % hb == 0
    n_rb = t // qn
    n_sub = NA_BLOCKS_PER_STEP if n_rb % NA_BLOCKS_PER_STEP == 0 else 1

    def bias_spec(u):
        return pl.BlockSpec((None, hb, qn, sn), lambda bi, hg, rb, ss, pt: (pt[rb * n_sub + u], hg, 0, 0))

    grid_spec = pltpu.PrefetchScalarGridSpec(
        num_scalar_prefetch=2,
        grid=(b, N_HEADS_NA // hb, n_rb // n_sub),
        in_specs=[
            pl.BlockSpec((None, n_sub * qn, hw), lambda bi, hg, rb, ss, pt: (bi, rb, COL_QN // hb + hg)),
            pl.BlockSpec((None, t, hw), lambda bi, hg, rb, ss, pt: (bi, 0, COL_KN // hb + hg)),
            pl.BlockSpec((None, t, hw), lambda bi, hg, rb, ss, pt: (bi, 0, COL_VN // hb + hg)),
            pl.BlockSpec((None, None, l_ctx, hw), lambda bi, hg, rb, ss, pt: (bi, layer, 0, hg)),
            pl.BlockSpec((None, None, l_ctx, hw), lambda bi, hg, rb, ss, pt: (bi, layer, 0, hg)),
        ] + [bias_spec(u) for u in range(n_sub)],
        out_specs=pl.BlockSpec((None, n_sub * qn, hw), lambda bi, hg, rb, ss, pt: (bi, rb, hg)),
    )
    return pl.pallas_call(
        functools.partial(_na_kernel, slab_len=sn, n_heads=hb, n_sub=n_sub, qn=qn),
        grid_spec=grid_spec,
        out_shape=jax.ShapeDtypeStruct((b, t, NA_W), BF16),
        compiler_params=_cparams(("arbitrary", "arbitrary", "arbitrary")),
        name="neighborhood_attn",
    )(jnp.asarray(slab_start), jnp.asarray(pat_id), proj, proj, proj, ck, cv, *([bias_tab] * n_sub))


def _layer_norm(y, g, b):
    mu = jnp.mean(y, axis=-1, keepdims=True)
    yc = y - mu
    var = jnp.mean(yc * yc, axis=-1, keepdims=True)
    return yc * lax.rsqrt(var + EPS) * g + b


def _top2_of4(vals):
    m1 = jnp.maximum(jnp.maximum(vals[0], vals[1]), jnp.maximum(vals[2], vals[3]))
    i1 = jnp.where(vals[0] == m1, 0, jnp.where(vals[1] == m1, 1, jnp.where(vals[2] == m1, 2, 3)))
    rest = [jnp.where(i1 == i, -1.0, vals[i]) for i in range(4)]
    m2 = jnp.maximum(jnp.maximum(rest[0], rest[1]), jnp.maximum(rest[2], rest[3]))
    i2 = jnp.where(rest[0] == m2, 0, jnp.where(rest[1] == m2, 1, jnp.where(rest[2] == m2, 2, 3)))
    return m1, i1, m2, i2


def _route_rows(logits_t):
    m = logits_t.max(axis=0, keepdims=True)
    e = jnp.exp(logits_t - m)
    probs = e / e.sum(axis=0, keepdims=True)
    rows = [probs[i:i + 1, :] for i in range(N_EXPERTS)]
    groups = [rows[g * EXPERTS_PER_GROUP:(g + 1) * EXPERTS_PER_GROUP] for g in range(N_GROUPS)]
    scores = []
    for g in range(N_GROUPS):
        m1, _, m2, _ = _top2_of4(groups[g])
        scores.append(m1 + m2)
    best = jnp.maximum(jnp.maximum(scores[0], scores[1]), jnp.maximum(scores[2], scores[3]))
    gi = jnp.where(scores[0] == best, 0, jnp.where(scores[1] == best, 1, jnp.where(scores[2] == best, 2, 3)))
    sel = [jnp.where(gi == 0, groups[0][i], jnp.where(gi == 1, groups[1][i],
                     jnp.where(gi == 2, groups[2][i], groups[3][i]))) for i in range(EXPERTS_PER_GROUP)]
    w1, l1, w2, l2 = _top2_of4(sel)
    wsum = w1 + w2
    return (gi * EXPERTS_PER_GROUP + l1, gi * EXPERTS_PER_GROUP + l2, w1 / wsum, w2 / wsum)


def _out_proj_kernel(oa_ref, on_ref, x_ref, mod_ref, wo_ref, g_ref, b_ref, wr_ref, br_ref, h2_in_ref,
                     x1_ref, h2_ref, e_ref, wt_ref):
    del h2_in_ref
    attn = jnp.dot(oa_ref[...], wo_ref[0:GA_Q, :], preferred_element_type=F32)
    attn = attn + jnp.dot(on_ref[...], wo_ref[GA_Q:GA_Q + NA_W, :], preferred_element_type=F32)
    gate1 = mod_ref[2:3, :]
    x1 = _layer_norm(DN_ALPHA * x_ref[...] + gate1 * attn, g_ref[...], b_ref[...])
    x1_ref[...] = x1
    h2 = x1 * (1.0 + mod_ref[4:5, :]) + mod_ref[3:4, :]
    h2_ref[...] = h2
    h_hi = h2.astype(BF16)
    h_lo = (h2 - h_hi.astype(F32)).astype(BF16)
    r_hi = jnp.dot(h_hi, wr_ref[...], preferred_element_type=F32)
    r_lo = jnp.dot(h_lo, wr_ref[...], preferred_element_type=F32)
    logits = r_hi + pltpu.roll(r_hi, LANE - N_EXPERTS, 1) + r_lo
    logits_t = logits.T[0:N_EXPERTS, :] + br_ref[...]
    e1, e2, w1, w2 = _route_rows(logits_t)
    e_ref[0:1, :] = e1
    e_ref[1:2, :] = e2
    n = w1.shape[1]
    row = lax.broadcasted_iota(jnp.int32, (LANE, n), 0)
    w_rows = jnp.where(row == 0, w1, jnp.where(row == 1, w2, 0.0))
    wt_ref[...] = w_rows.T


def _out_proj(oa, on, x, mod_l, mod_row0, w_o_bf, w_layer, ln_g, ln_b, wr_cat, b_router, h2_all, row0):
    b, t, d = x.shape
    tm = min(OUT_TM, t)
    nt = t // tm
    assert row0 % tm == 0
    tok = lambda bi, ti: (bi, ti, 0)
    const2 = lambda bi, ti: (0, 0)
    return pl.pallas_call(
        _out_proj_kernel,
        grid=(b, t // tm),
        in_specs=[
            pl.BlockSpec((None, tm, GA_Q), tok),
            pl.BlockSpec((None, tm, NA_W), tok),
            pl.BlockSpec((None, tm, d), tok),
            pl.BlockSpec((None, N_MOD, d), lambda bi, ti: (bi + mod_row0, 0, 0)),
            pl.BlockSpec((None, GA_Q + NA_W, d), lambda bi, ti: (w_layer, 0, 0)),
            pl.BlockSpec((1, d), const2),
            pl.BlockSpec((1, d), const2),
            pl.BlockSpec((d, LANE), const2),
            pl.BlockSpec((N_EXPERTS, 1), const2),
            pl.BlockSpec(memory_space=pl.ANY),
        ],
        out_specs=[
            pl.BlockSpec((None, tm, d), tok),
            pl.BlockSpec((tm, d), lambda bi, ti: (row0 // tm + bi * nt + ti, 0)),
            pl.BlockSpec((None, TOP_K, tm), lambda bi, ti: (bi, 0, ti)),
            pl.BlockSpec((None, tm, LANE), tok),
        ],
        out_shape=[
            jax.ShapeDtypeStruct((b, t, d), F32),
            jax.ShapeDtypeStruct(h2_all.shape, F32),
            jax.ShapeDtypeStruct((b, TOP_K, t), jnp.int32),
            jax.ShapeDtypeStruct((b, t, LANE), F32),
        ],
        input_output_aliases={9: 1},
        compiler_params=_cparams(("arbitrary", "arbitrary")),
        name="out_proj_ln_router",
    )(oa, on, x, mod_l, w_o_bf, ln_g.reshape(1, d), ln_b.reshape(1, d), wr_cat,
      b_router.reshape(N_EXPERTS, 1).astype(F32), h2_all)


def _row_copy(src_hbm, idx_ref, dst, sem, r):
    return pltpu.make_async_copy(src_hbm.at[pl.ds(idx_ref[0, r], 1)], dst.at[pl.ds(r, 1)], sem)


def _start_row_gather(src_hbm, idx_ref, dst, sem, n_rows):
    def body(r, carry):
        _row_copy(src_hbm, idx_ref, dst, sem, r).start()
        return carry
    lax.fori_loop(0, n_rows, body, 0, unroll=8)


def _start_row_gather_inline(src_hbm, idx_ref, dst, sem, r_lo, r_hi):
    for r in range(r_lo, r_hi):
        _row_copy(src_hbm, idx_ref, dst, sem, r).start()


def _wait_row_gather(src_hbm, dst, sem, n_rows):
    pltpu.make_async_copy(src_hbm.at[pl.ds(0, n_rows)], dst, sem).wait()


def _moe_kernel(be_ref, nu_ref, tok0_ref, tok1_ref, tok2_ref, h_hbm, wg_ref, wu_ref, wd_ref, o_ref,
                buf, sem, *, n_blocks):
    del be_ref
    i = pl.program_id(0)
    n_used = nu_ref[0]
    slot = i % GATHER_DEPTH
    ahead = (i + 2) % GATHER_DEPTH

    @pl.when(i == 0)
    def _():
        _start_row_gather(h_hbm, tok0_ref, buf.at[0], sem.at[0], MOE_BLK)
        _start_row_gather(h_hbm, tok1_ref, buf.at[1], sem.at[1], MOE_BLK)

    _wait_row_gather(h_hbm, buf.at[slot], sem.at[slot], MOE_BLK)

    @pl.when(i < n_used)
    def _():
        xb = buf[slot].astype(BF16)
        gate = jnp.dot(xb, wg_ref[...], preferred_element_type=F32)
        up = jnp.dot(xb, wu_ref[...], preferred_element_type=F32)
        act = (gate / (1.0 + jnp.exp(-gate)) * up).astype(BF16)
        _start_row_gather_inline(h_hbm, tok2_ref, buf.at[ahead], sem.at[ahead], 0, MOE_BLK)
        o_ref[...] = jnp.dot(act, wd_ref[...], preferred_element_type=F32)

    @pl.when(i >= n_used)
    def _():
        o_ref[...] = jnp.zeros_like(o_ref)
        _start_row_gather(h_hbm, tok2_ref, buf.at[ahead], sem.at[ahead], MOE_BLK)

    @pl.when(i == n_blocks - 1)
    def _():
        for k in (1, 2):
            s = (i + k) % GATHER_DEPTH
            _wait_row_gather(h_hbm, buf.at[s], sem.at[s], MOE_BLK)


def _moe_ffn(h2_flat, tok_buf, blk_e, n_used, wg, wu, wd, layer):
    n, d = h2_flat.shape
    n_blocks = blk_e.shape[0]
    d_ff = wg.shape[3]
    tok3 = tok_buf.reshape(n_blocks, 1, MOE_BLK)
    smem_blk = lambda f: pl.BlockSpec((None, 1, MOE_BLK), f, memory_space=pltpu.SMEM)
    grid_spec = pltpu.PrefetchScalarGridSpec(
        num_scalar_prefetch=2,
        grid=(n_blocks,),
        in_specs=[
            smem_blk(lambda i, be, nu: (i, 0, 0)),
            smem_blk(lambda i, be, nu: (jnp.minimum(i + 1, n_blocks - 1), 0, 0)),
            smem_blk(lambda i, be, nu: (jnp.minimum(i + 2, n_blocks - 1), 0, 0)),
            pl.BlockSpec(memory_space=pl.ANY),
            pl.BlockSpec((None, None, d, d_ff), lambda i, be, nu: (layer, be[i], 0, 0)),
            pl.BlockSpec((None, None, d, d_ff), lambda i, be, nu: (layer, be[i], 0, 0)),
            pl.BlockSpec((None, None, d_ff, d), lambda i, be, nu: (layer, be[i], 0, 0)),
        ],
        out_specs=pl.BlockSpec((MOE_BLK, d), lambda i, be, nu: (i, 0)),
        scratch_shapes=[pltpu.VMEM((GATHER_DEPTH, MOE_BLK, d), F32), pltpu.SemaphoreType.DMA((GATHER_DEPTH,))],
    )
    return pl.pallas_call(
        functools.partial(_moe_kernel, n_blocks=n_blocks),
        grid_spec=grid_spec,
        out_shape=jax.ShapeDtypeStruct((n_blocks * MOE_BLK, d), F32),
        compiler_params=_cparams(("arbitrary",)),
        name="moe_ffn",
    )(blk_e, n_used, tok3, tok3, tok3, h2_flat, wg, wu, wd)


def _combine_kernel(da0_ref, da1_ref, db0_ref, db1_ref, dc0_ref, dc1_ref, y_hbm, x1_ref, mod_ref, wt_ref,
                    g_ref, b_ref, o_ref, buf, sem, *, n_steps, tm):
    i = pl.program_id(0)
    slot = i % GATHER_DEPTH
    ahead = (i + 2) % GATHER_DEPTH

    @pl.when(i == 0)
    def _():
        for s, (r0, r1) in enumerate(((da0_ref, da1_ref), (db0_ref, db1_ref))):
            _start_row_gather(y_hbm, r0, buf.at[s, 0], sem.at[s], tm)
            _start_row_gather(y_hbm, r1, buf.at[s, 1], sem.at[s], tm)

    def wait(s):
        _wait_row_gather(y_hbm, buf.at[s, 0], sem.at[s], tm)
        _wait_row_gather(y_hbm, buf.at[s, 1], sem.at[s], tm)

    wait(slot)
    y = wt_ref[:, 0:1] * buf[slot, 0] + wt_ref[:, 1:2] * buf[slot, 1]
    gate2 = mod_ref[5:6, :]
    o_ref[...] = _layer_norm(DN_ALPHA * x1_ref[...] + gate2 * y, g_ref[...], b_ref[...])
    _start_row_gather_inline(y_hbm, dc0_ref, buf.at[ahead, 0], sem.at[ahead], 0, tm)
    _start_row_gather_inline(y_hbm, dc1_ref, buf.at[ahead, 1], sem.at[ahead], 0, tm)

    @pl.when(i == n_steps - 1)
    def _():
        for k in (1, 2):
            wait((i + k) % GATHER_DEPTH)


def _combine(y_sorted, dest, x1, mod_l, mod_row0, wt, ln_g, ln_b):
    b, t, d = x1.shape
    tm = min(OUT_TM, t)
    nt = t // tm
    n_steps = b * nt
    dest4 = dest.reshape(b, TOP_K, nt, tm).transpose(0, 2, 1, 3).reshape(n_steps, TOP_K, 1, tm)
    later = lambda i, k: jnp.minimum(i + k, n_steps - 1)
    smem_blk = lambda f: pl.BlockSpec((None, None, 1, tm), f, memory_space=pltpu.SMEM)
    tok = lambda i: (i // nt, i % nt, 0)
    return pl.pallas_call(
        functools.partial(_combine_kernel, n_steps=n_steps, tm=tm),
        grid=(n_steps,),
        in_specs=[
            smem_blk(lambda i: (i, 0, 0, 0)),
            smem_blk(lambda i: (i, 1, 0, 0)),
            smem_blk(lambda i: (later(i, 1), 0, 0, 0)),
            smem_blk(lambda i: (later(i, 1), 1, 0, 0)),
            smem_blk(lambda i: (later(i, 2), 0, 0, 0)),
            smem_blk(lambda i: (later(i, 2), 1, 0, 0)),
            pl.BlockSpec(memory_space=pl.ANY),
            pl.BlockSpec((None, tm, d), tok),
            pl.BlockSpec((None, N_MOD, d), lambda i: (i // nt + mod_row0, 0, 0)),
            pl.BlockSpec((None, tm, LANE), tok),
            pl.BlockSpec((1, d), lambda i: (0, 0)),
            pl.BlockSpec((1, d), lambda i: (0, 0)),
        ],
        out_specs=pl.BlockSpec((None, tm, d), tok),
        out_shape=jax.ShapeDtypeStruct((b, t, d), F32),
        scratch_shapes=[pltpu.VMEM((GATHER_DEPTH, TOP_K, tm, d), F32), pltpu.SemaphoreType.DMA((GATHER_DEPTH,))],
        compiler_params=_cparams(("arbitrary",)),
        name="moe_combine_ln",
    )(dest4, dest4, dest4, dest4, dest4, dest4, y_sorted, x1, mod_l, wt, ln_g.reshape(1, d), ln_b.reshape(1, d))


def _dispatch_plan(e_idxs, row0s):
    flat_e = jnp.concatenate([e_idx.reshape(-1) for e_idx in e_idxs])
    a = flat_e.shape[0]
    onehot = (flat_e[:, None] == jnp.arange(N_EXPERTS, dtype=jnp.int32)[None, :]).astype(jnp.int32)
    counts = onehot.sum(axis=0)
    chunk = 256
    assert a % chunk == 0
    oh = onehot.reshape(a // chunk, chunk, N_EXPERTS).astype(F32)
    tri = jnp.tril(jnp.ones((chunk, chunk), F32))
    within = jnp.einsum('ij,cje->cie', tri, oh, precision=lax.Precision.HIGHEST)
    before = jnp.cumsum(oh.sum(axis=1), axis=0) - oh.sum(axis=1)
    running = (within + before[:, None, :]).reshape(a, N_EXPERTS).astype(jnp.int32)
    rank = (running * onehot).sum(axis=1) - 1
    padded = (counts + MOE_BLK - 1) // MOE_BLK * MOE_BLK
    pad_end = jnp.cumsum(padded)
    pad_start = pad_end - padded
    dest = pad_start[flat_e] + rank
    n_blocks = -(-a // MOE_BLK) + N_EXPERTS
    p = n_blocks * MOE_BLK
    blk_start = jnp.arange(n_blocks, dtype=jnp.int32) * MOE_BLK
    blk_e = jnp.minimum((blk_start[:, None] >= pad_end[None, :]).sum(axis=1), N_EXPERTS - 1).astype(jnp.int32)
    n_used = (pad_end[-1] // MOE_BLK).astype(jnp.int32)
    last_e = blk_e[jnp.maximum(n_used - 1, 0)]
    blk_e = jnp.where(jnp.arange(n_blocks) < n_used, blk_e, last_e)
    assert a < (1 << 16)
    order = lax.sort(flat_e * (1 << 16) + jnp.arange(a, dtype=jnp.int32)) & 0xFFFF
    start = jnp.cumsum(counts) - counts
    order = jnp.concatenate([order, jnp.zeros((MOE_BLK,), jnp.int32)])
    win = jnp.clip(start[blk_e] + blk_start - pad_start[blk_e], 0, a)
    order_blk = order[(win[:, None] + jnp.arange(MOE_BLK, dtype=jnp.int32)[None, :]).reshape(p)]
    tok_buf, o = jnp.zeros((p,), jnp.int32), 0
    for e_idx, row0 in zip(e_idxs, row0s):
        b, k, t = e_idx.shape
        local = order_blk - o
        rows = row0 + (local // (k * t)) * t + local % t
        tok_buf = jnp.where((local >= 0) & (local < e_idx.size), rows, tok_buf)
        o += e_idx.size
    dests, o = [], 0
    for e_idx in e_idxs:
        dests.append(dest[o:o + e_idx.size].reshape(e_idx.shape))
        o += e_idx.size
    return dests, tok_buf, blk_e, n_used.reshape(1)


def _attention_sublayer(x, mod_l, mod_row0, lw, ctx):
    b, t, d = x.shape
    layer = lw['layer']
    if ctx is None:
        xt = x.reshape(1, b * t, d)
        proj, *cache = _in_proj(xt, mod_l, mod_row0, lw['w_in'], layer, lw['q_gain'], lw['k_gain'],
                                cache=(layer, lw['depth'], b, t, lw['prev_cache']))
        proj = proj.reshape(b, t, -1)
        oa = _dense_attention(proj, COL_QA, COL_KA, COL_VA, N_KV_GA, N_HEADS_GA // N_KV_GA, gs=GA_KV_PER_STEP)
        on = _dense_attention(proj, COL_QN, COL_KN, COL_VN, N_HEADS_NA, 1, gs=NA_HEADS_PER_STEP)
        return oa.reshape(1, b * t, -1), on.reshape(1, b * t, -1), xt, cache
    ga_k, ga_v, na_k, na_v, rope_tabs, na_bias, na_plan = ctx
    (proj,) = _in_proj(x, mod_l, mod_row0, lw['w_in'], layer, lw['q_gain'], lw['k_gain'], rope_tabs=rope_tabs)
    oa = _dense_attention(proj, COL_QA, COL_KA, COL_VA, N_KV_GA, N_HEADS_GA // N_KV_GA, gs=GA_KV_PER_STEP,
                          ctx=(ga_k, ga_v, layer))
    on = _neighborhood_attention(proj, na_k, na_v, layer, na_bias, na_plan)
    return oa, on, x, None


def _layer(x_ctx, x_lat, mod_l, lw, shared, lat_ctx, h2_all):
    bc, tc, d = x_ctx.shape
    bl, tl, _ = x_lat.shape
    n_ctx, n_lat = bc * tc, bl * tl
    streams = [_attention_sublayer(x_ctx, mod_l, bl, lw, None),
               _attention_sublayer(x_lat, mod_l, 0, lw, lat_ctx)]
    cache = streams[0][3]
    mod_rows, row0s = (bl, 0), (0, n_ctx)
    x1s, e_idxs, wts = [], [], []
    for (oa, on, xt, _), mod_row0, row0 in zip(streams, mod_rows, row0s):
        x1, h2_all, e_idx, wt = _out_proj(oa, on, xt, mod_l, mod_row0, lw['w_o'], lw['layer'], lw['ln1_g'],
                                          lw['ln1_b'], shared['wr_cat'], shared['b_router'], h2_all, row0)
        x1s.append(x1)
        e_idxs.append(e_idx)
        wts.append(wt)
    dests, tok_buf, blk_e, n_used = _dispatch_plan(e_idxs, row0s)
    y_sorted = _moe_ffn(h2_all, tok_buf, blk_e, n_used, lw['w_gate'], lw['w_up'], lw['w_down'], lw['layer'])
    outs = [_combine(y_sorted, dest, x1, mod_l, mod_row0, wt, lw['ln2_g'], lw['ln2_b'])
            for dest, x1, wt, mod_row0 in zip(dests, x1s, wts, mod_rows)]
    return outs[0].reshape(bc, tc, d), outs[1].reshape(bl, tl, d), cache, h2_all


def kernel(x_prompt, x_sample, c, cache_ga_k, cache_ga_v, cache_na_k, cache_na_v, c_ctx, w_router, b_router, w_mod, b_mod, w_in, q_norm, k_norm, rel_bias, w_o, ln1_g, ln1_b, ln2_g, ln2_b, w_gate, w_up, w_down):
    bsz, seq, d = x_prompt.shape
    dec_b, dec_t, _ = x_sample.shape
    depth = w_mod.shape[0]
    past = cache_ga_k.shape[2]

    mod_rows = 16
    assert dec_b + 1 <= mod_rows
    cvec = jnp.concatenate([c, c_ctx[None, :], jnp.zeros((mod_rows - dec_b - 1, d), F32)], axis=0)
    mod = _modulation(cvec, w_mod, b_mod).reshape(depth, mod_rows, N_MOD, d)

    wr_hi = w_router.astype(BF16)
    wr_lo = (w_router - wr_hi.astype(F32)).astype(BF16)
    wr_cat = jnp.concatenate([wr_hi, wr_lo, jnp.zeros((d, LANE - 2 * N_EXPERTS), BF16)], axis=1)
    shared = {'wr_cat': wr_cat, 'b_router': b_router}

    rope_tabs = _rope_tables(dec_t)
    na_plan = _na_plan(dec_t // GRID_W)
    ga_k = cache_ga_k.reshape(dec_b, depth, past, GA_KV)
    ga_v = cache_ga_v.reshape(dec_b, depth, past, GA_KV)
    na_k = cache_na_k.reshape(dec_b, depth, past, NA_W)
    na_v = cache_na_v.reshape(dec_b, depth, past, NA_W)

    w_in_bf, w_o_bf = w_in.astype(BF16), w_o.astype(BF16)
    w_gate_bf, w_up_bf, w_down_bf = w_gate.astype(BF16), w_up.astype(BF16), w_down.astype(BF16)

    y_prompt, y_sample = x_prompt, x_sample
    cache = [jnp.zeros((bsz, depth, seq, w), F32) for w in (GA_KV, GA_KV, NA_W, NA_W)]
    h2_all = jnp.zeros((bsz * seq + dec_b * dec_t, d), F32)
    for i in range(depth):
        lw = {
            'layer': i, 'depth': depth, 'prev_cache': cache,
            'w_in': w_in_bf, 'w_o': w_o_bf,
            'q_gain': (q_norm[i] * Q_PRESCALE).reshape(1, HEAD_DIM).astype(F32),
            'k_gain': k_norm[i].reshape(1, HEAD_DIM).astype(F32),
            'ln1_g': ln1_g[i], 'ln1_b': ln1_b[i], 'ln2_g': ln2_g[i], 'ln2_b': ln2_b[i],
            'w_gate': w_gate_bf, 'w_up': w_up_bf, 'w_down': w_down_bf,
        }
        na_bias = _na_bias_table(rel_bias[i], na_plan[4])
        y_prompt, y_sample, cache, h2_all = _layer(y_prompt, y_sample, mod[i], lw, shared,
                                                   (ga_k, ga_v, na_k, na_v, rope_tabs, na_bias, na_plan), h2_all)

    outs = [arr.reshape(bsz, depth, seq, -1, HEAD_DIM) for arr in cache]
    return (y_prompt, y_sample, outs[0], outs[1], outs[2], outs[3])
```

```python
import functools
import math

import numpy as np
import jax
import jax.numpy as jnp
from jax import lax
from jax.experimental import pallas as pl
from jax.experimental.pallas import tpu as pltpu

F32 = jnp.float32
BF16 = jnp.bfloat16

DEPTH = 2
GRID_W = 64
HEAD_DIM = 128
N_HEADS_GA = 8
N_KV_GA = 2
N_HEADS_NA = 8
NA_WIN_H = 8
NA_WIN_W = 16
ROPE_THETA = 10000.0
N_EXPERTS = 16
N_GROUPS = 4
EXPERTS_PER_GROUP = N_EXPERTS // N_GROUPS
TOP_K = 2
N_MOD = 6
DN_ALPHA = (2.0 * DEPTH) ** 0.25
EPS = 1e-6
ATTN_SCALE = HEAD_DIM ** -0.5
LOG2E = math.log2(math.e)
Q_PRESCALE = ATTN_SCALE * LOG2E
MASK_VALUE = -1e30

GA_Q = N_HEADS_GA * HEAD_DIM
GA_KV = N_KV_GA * HEAD_DIM
NA_W = N_HEADS_NA * HEAD_DIM
COL_QA = 0
COL_KA = COL_QA + N_HEADS_GA
COL_VA = COL_KA + N_KV_GA
COL_QN = COL_VA + N_KV_GA
COL_KN = COL_QN + N_HEADS_NA
COL_VN = COL_KN + N_HEADS_NA
IN_HEADS = COL_VN + N_HEADS_NA

LANE = 128
PROJ_TN = 4 * HEAD_DIM
PROJ_TM = 512
PROJ_TM_CACHE = 256
ATTN_TQ = 512
NA_ROWS = 4
NA_HEADS_PER_STEP = 4
NA_BLOCKS_PER_STEP = 2
GA_KV_PER_STEP = 2
OUT_TM = 256
MOE_BLK = 256
GATHER_DEPTH = 3
MOD_TN = 1024
VMEM_LIMIT = 52 * 1024 * 1024


def _cparams(sem):
    return pltpu.CompilerParams(dimension_semantics=sem, vmem_limit_bytes=VMEM_LIMIT)


def _mod_kernel(c_ref, w_ref, b_ref, o_ref):
    c = c_ref[...]
    s = c / (1.0 + jnp.exp(-c))
    o_ref[...] = jnp.dot(s.astype(BF16), w_ref[...].astype(BF16),
                         preferred_element_type=F32) + b_ref[...]


def _modulation(cvec, w_mod, b_mod):
    depth, d, e = w_mod.shape
    rows = cvec.shape[0]
    tn = MOD_TN if e % MOD_TN == 0 else e
    return pl.pallas_call(
        _mod_kernel,
        grid=(depth, e // tn),
        in_specs=[
            pl.BlockSpec((rows, d), lambda l, j: (0, 0)),
            pl.BlockSpec((None, d, tn), lambda l, j: (l, 0, j)),
            pl.BlockSpec((None, 1, tn), lambda l, j: (l, 0, j)),
        ],
        out_specs=pl.BlockSpec((None, rows, tn), lambda l, j: (l, 0, j)),
        out_shape=jax.ShapeDtypeStruct((depth, rows, e), F32),
        compiler_params=_cparams(("arbitrary", "arbitrary")),
        name="modulation",
    )(cvec, w_mod, b_mod.reshape(depth, 1, e))


def _rms_head(x, g):
    ms = jnp.mean(x * x, axis=-1, keepdims=True)
    return x * lax.rsqrt(ms + EPS) * g


def _rope_head(x, cos, sin, first_half):
    swapped = jnp.where(first_half, pltpu.roll(x, HEAD_DIM - 32, 1), pltpu.roll(x, 32, 1))
    return x * cos + swapped * sin


def _proj_kernel(*refs, rope, cache):
    x_ref, mod_ref, w_ref, qg_ref, kg_ref = refs[:5]
    pos = 5
    if rope:
        cos_ref, sin_ref = refs[pos:pos + 2]
        pos += 2
    if cache is not None:
        pos += cache[0]
    o_ref = refs[pos]
    pos += 1
    if cache is not None:
        ka_ref, va_ref, kn_ref, vn_ref = refs[pos:pos + 4]
        pos += 4
    h_scr = refs[pos]

    shift = mod_ref[0:1, :]
    scale = mod_ref[1:2, :]
    h_scr[...] = (x_ref[...] * (1.0 + scale) + shift).astype(BF16)

    if rope:
        lane = lax.broadcasted_iota(jnp.int32, (1, HEAD_DIM), 1)
        first_half = (lane % 64) < 32

    def normed(y, g):
        y = _rms_head(y, g)
        if rope:
            y = _rope_head(y, cos_ref[...], sin_ref[...], first_half)
        return y

    def to_cache(c_ref, c_head, y):
        if cache is not None:
            c_ref[:, :, c_head * HEAD_DIM:(c_head + 1) * HEAD_DIM] = y.reshape(cache[1], cache[2], HEAD_DIM)

    hpt = PROJ_TN // HEAD_DIM
    for jt in range(IN_HEADS // hpt):
        acc = jnp.dot(h_scr[...], w_ref[:, jt * PROJ_TN:(jt + 1) * PROJ_TN], preferred_element_type=F32)
        for hh in range(hpt):
            head = jt * hpt + hh
            y = acc[:, hh * HEAD_DIM:(hh + 1) * HEAD_DIM]
            if head < COL_KA:
                y = normed(y, qg_ref[...])
            elif head < COL_VA:
                y = normed(y, kg_ref[...])
                to_cache(ka_ref if cache else None, head - COL_KA, y)
            elif head < COL_QN:
                to_cache(va_ref if cache else None, head - COL_VA, y)
            elif head < COL_KN:
                y = y * Q_PRESCALE
            elif head < COL_VN:
                to_cache(kn_ref if cache else None, head - COL_KN, y)
            else:
                to_cache(vn_ref if cache else None, head - COL_VN, y)
            o_ref[:, head * HEAD_DIM:(head + 1) * HEAD_DIM] = y.astype(BF16)


def _in_proj(x, mod_l, mod_row0, w_in_bf, w_layer, q_gain, k_gain, rope_tabs=None, cache=None):
    b, t, d = x.shape
    in_w = w_in_bf.shape[2]
    rope = rope_tabs is not None
    tm = min(PROJ_TM if rope else PROJ_TM_CACHE, t)
    assert in_w == IN_HEADS * HEAD_DIM and in_w % PROJ_TN == 0

    in_specs = [
        pl.BlockSpec((None, tm, d), lambda bi, ti: (bi, ti, 0)),
        pl.BlockSpec((None, N_MOD, d), lambda bi, ti: (bi + mod_row0, 0, 0)),
        pl.BlockSpec((None, d, in_w), lambda bi, ti: (w_layer, 0, 0), pipeline_mode=pl.Buffered(1)),
        pl.BlockSpec((1, HEAD_DIM), lambda bi, ti: (0, 0)),
        pl.BlockSpec((1, HEAD_DIM), lambda bi, ti: (0, 0)),
    ]
    args = [x, mod_l, w_in_bf, q_gain, k_gain]
    if rope:
        in_specs += [pl.BlockSpec((tm, HEAD_DIM), lambda bi, ti: (ti, 0))] * 2
        args += list(rope_tabs)
    out_specs = [pl.BlockSpec((None, tm, in_w), lambda bi, ti: (bi, ti, 0))]
    out_shape = [jax.ShapeDtypeStruct((b, t, in_w), BF16)]
    aliases = {}
    cache_cfg = None
    if cache is not None:
        layer, depth, bsz, seq, prev = cache
        assert b == 1 and tm % seq == 0
        nb = tm // seq
        cache_cfg = (len(prev), nb, seq)
        for k, arr in enumerate(prev):
            aliases[len(args)] = 1 + k
            in_specs.append(pl.BlockSpec(memory_space=pl.ANY))
            args.append(arr)
        widths = (GA_KV, GA_KV, NA_W, NA_W)
        out_specs += [pl.BlockSpec((nb, None, seq, w), lambda bi, ti: (ti, layer, 0, 0)) for w in widths]
        out_shape += [jax.ShapeDtypeStruct((bsz, depth, seq, w), F32) for w in widths]
    outs = pl.pallas_call(
        functools.partial(_proj_kernel, rope=rope, cache=cache_cfg),
        grid=(b, t // tm),
        in_specs=in_specs,
        out_specs=out_specs,
        out_shape=out_shape,
        input_output_aliases=aliases,
        scratch_shapes=[pltpu.VMEM((tm, d), BF16)],
        compiler_params=_cparams(("arbitrary", "arbitrary")),
        name="in_proj_rope" if rope else "in_proj_cache",
    )(*args)
    return outs


def _rope_tables(t):
    half = HEAD_DIM // 4
    tt = jnp.arange(t, dtype=jnp.int32)
    row = (tt // GRID_W).astype(F32)
    col = (tt % GRID_W).astype(F32)
    inv_freq = 1.0 / (ROPE_THETA ** (jnp.arange(half, dtype=F32) / half))
    ar = row[:, None] * inv_freq[None, :]
    ac = col[:, None] * inv_freq[None, :]
    cos = jnp.concatenate([jnp.cos(ar), jnp.cos(ar), jnp.cos(ac), jnp.cos(ac)], axis=-1)
    sin = jnp.concatenate([-jnp.sin(ar), jnp.sin(ar), -jnp.sin(ac), jnp.sin(ac)], axis=-1)
    return cos, sin


_NT = (((1,), (1,)), ((), ()))


def _softmax_pv(scores, values):
    m = scores[0].max(axis=-1, keepdims=True)
    for s in scores[1:]:
        m = jnp.maximum(m, s.max(axis=-1, keepdims=True))
    l = None
    acc = None
    for s, v in zip(scores, values):
        p = jnp.exp2(s - m)
        ps = p.sum(axis=-1, keepdims=True)
        pv = jnp.dot(p.astype(BF16), v, preferred_element_type=F32)
        l = ps if l is None else l + ps
        acc = pv if acc is None else acc + pv
    return acc * (1.0 / l)


def _dense_attn_kernel(*refs, r_heads, gs, has_ctx):
    if has_ctx:
        q_ref, k_ref, v_ref, kc_ref, vc_ref, o_ref = refs
    else:
        q_ref, k_ref, v_ref, o_ref = refs
    for gi in range(gs):
        kv = slice(gi * HEAD_DIM, (gi + 1) * HEAD_DIM)
        k = k_ref[:, kv]
        v = v_ref[:, kv]
        if has_ctx:
            kc = kc_ref[:, kv].astype(BF16)
            vc = vc_ref[:, kv].astype(BF16)
        for r in range(r_heads):
            h = gi * r_heads + r
            sl = slice(h * HEAD_DIM, (h + 1) * HEAD_DIM)
            q = q_ref[:, sl]
            scores = [lax.dot_general(q, k, _NT, preferred_element_type=F32)]
            values = [v]
            if has_ctx:
                scores.append(lax.dot_general(q, kc, _NT, preferred_element_type=F32))
                values.append(vc)
            o_ref[:, sl] = _softmax_pv(scores, values).astype(BF16)


def _dense_attention(proj, q_col, k_col, v_col, groups, r_heads, gs=1, ctx=None):
    b, t, _ = proj.shape
    tq = min(ATTN_TQ, t)
    qw = gs * r_heads * HEAD_DIM
    kw = gs * HEAD_DIM
    assert q_col % (gs * r_heads) == 0 and k_col % gs == 0 and v_col % gs == 0 and groups % gs == 0
    in_specs = [
        pl.BlockSpec((None, tq, qw), lambda bi, g, qi: (bi, qi, q_col // (gs * r_heads) + g)),
        pl.BlockSpec((None, t, kw), lambda bi, g, qi: (bi, 0, k_col // gs + g)),
        pl.BlockSpec((None, t, kw), lambda bi, g, qi: (bi, 0, v_col // gs + g)),
    ]
    args = [proj, proj, proj]
    if ctx is not None:
        ck, cv, layer = ctx
        l_ctx = ck.shape[2]
        spec = pl.BlockSpec((None, None, l_ctx, kw), lambda bi, g, qi: (bi, layer, 0, g))
        in_specs += [spec, spec]
        args += [ck, cv]
    return pl.pallas_call(
        functools.partial(_dense_attn_kernel, r_heads=r_heads, gs=gs, has_ctx=ctx is not None),
        grid=(b, groups // gs, t // tq),
        in_specs=in_specs,
        out_specs=pl.BlockSpec((None, tq, qw), lambda bi, g, qi: (bi, qi, g)),
        out_shape=jax.ShapeDtypeStruct((b, t, groups * r_heads * HEAD_DIM), BF16),
        compiler_params=_cparams(("arbitrary", "arbitrary", "arbitrary")),
        name="dense_attn_ctx" if ctx is not None else "dense_attn",
    )(*args)


def _na_plan(rows):
    kh = min(NA_WIN_H, rows)
    kw = NA_WIN_W
    r_blk = min(NA_ROWS, rows)
    assert rows % r_blk == 0
    slab = min(r_blk - 1 + kh, rows)
    row_start = np.clip(np.arange(rows) - kh // 2, 0, rows - kh)
    col = np.arange(GRID_W)
    col_start = np.clip(col - kw // 2, 0, GRID_W - kw)
    slab_start, pat_id, pats, sigs = [], [], [], {}
    for r0 in range(0, rows, r_blk):
        ss = min(row_start[r0], rows - slab)
        rel = tuple(int(row_start[r0 + ri] - ss) for ri in range(r_blk))
        sig = (int(ss - r0), rel)
        if sig not in sigs:
            sigs[sig] = len(pats)
            q_row = r0 + np.arange(r_blk)[:, None, None, None]
            q_col = col[None, :, None, None]
            k_row = ss + np.arange(slab)[None, None, :, None]
            k_col = col[None, None, None, :]
            rs = row_start[r0:r0 + r_blk][:, None, None, None]
            cs = col_start[None, :, None, None]
            valid = (k_row >= rs) & (k_row < rs + kh) & (k_col >= cs) & (k_col < cs + kw)
            row_off = (k_row - q_row + (NA_WIN_H - 1))[:, 0, :, 0]
            row_sel = (row_off[:, :, None] == np.arange(2 * NA_WIN_H - 1)).astype(np.float32)
            full = (r_blk, GRID_W, slab, GRID_W)
            pats.append((np.broadcast_to(valid, full).reshape(r_blk * GRID_W, slab * GRID_W), row_sel))
        slab_start.append(int(ss))
        pat_id.append(sigs[sig])
    return r_blk, slab, np.array(slab_start, np.int32), np.array(pat_id, np.int32), pats


def _na_bias_table(rel_bias_l, pats):
    col = np.arange(GRID_W)
    col_off = col[None, :] - col[:, None] + (NA_WIN_W - 1)
    col_sel = (col_off[:, :, None] == np.arange(2 * NA_WIN_W - 1)).astype(np.float32)
    hi = lax.Precision.HIGHEST
    tabs = []
    for valid, row_sel in pats:
        by_row = jnp.einsum('hrc,isr->hisc', rel_bias_l.astype(F32), row_sel, precision=hi)
        bias = jnp.einsum('hisc,qkc->hiqsk', by_row, col_sel, precision=hi)
        bias = bias.reshape((bias.shape[0],) + valid.shape) * LOG2E
        tabs.append(jnp.where(valid[None], bias, MASK_VALUE))
    return jnp.stack(tabs, axis=0)


def _na_kernel(ss_ref, pat_ref, q_ref, k_ref, v_ref, kc_ref, vc_ref, *rest, slab_len, n_heads, n_sub, qn):
    del pat_ref
    bias_refs, o_ref = rest[:n_sub], rest[n_sub]
    rb = pl.program_id(2)
    for h in range(n_heads):
        sl = slice(h * HEAD_DIM, (h + 1) * HEAD_DIM)
        kc = kc_ref[:, sl].astype(BF16)
        vc = vc_ref[:, sl].astype(BF16)
        for u in range(n_sub):
            start = pl.multiple_of(ss_ref[rb * n_sub + u] * GRID_W, GRID_W)
            rows = slice(u * qn, (u + 1) * qn)
            ks = k_ref[pl.ds(start, slab_len), sl]
            vs = v_ref[pl.ds(start, slab_len), sl]
            q = q_ref[rows, sl]
            s_loc = lax.dot_general(q, ks, _NT, preferred_element_type=F32) + bias_refs[u][h]
            s_ctx = lax.dot_general(q, kc, _NT, preferred_element_type=F32)
            o_ref[rows, sl] = _softmax_pv([s_loc, s_ctx], [vs, vc]).astype(BF16)


def _neighborhood_attention(proj, ck, cv, layer, bias_tab, plan):
    b, t, _ = proj.shape
    r_blk, slab, slab_start, pat_id, _ = plan
    qn = r_blk * GRID_W
    sn = slab * GRID_W
    l_ctx = ck.shape[2]
    hb = NA_HEADS_PER_STEP
    hw = hb * HEAD_DIM
    assert COL_QN % hb == 0 and COL_KN % hb == 0 and COL_VN % hb == 0 and N_HEADS_NA % hb == 0
    n_rb = t // qn
    n_sub = NA_BLOCKS_PER_STEP if n_rb % NA_BLOCKS_PER_STEP == 0 else 1

    def bias_spec(u):
        return pl.BlockSpec((None, hb, qn, sn), lambda bi, hg, rb, ss, pt: (pt[rb * n_sub + u], hg, 0, 0))

    grid_spec = pltpu.PrefetchScalarGridSpec(
        num_scalar_prefetch=2,
        grid=(b, N_HEADS_NA // hb, n_rb // n_sub),
        in_specs=[
            pl.BlockSpec((None, n_sub * qn, hw), lambda bi, hg, rb, ss, pt: (bi, rb, COL_QN // hb + hg)),
            pl.BlockSpec((None, t, hw), lambda bi, hg, rb, ss, pt: (bi, 0, COL_KN // hb + hg)),
            pl.BlockSpec((None, t, hw), lambda bi, hg, rb, ss, pt: (bi, 0, COL_VN // hb + hg)),
            pl.BlockSpec((None, None, l_ctx, hw), lambda bi, hg, rb, ss, pt: (bi, layer, 0, hg)),
            pl.BlockSpec((None, None, l_ctx, hw), lambda bi, hg, rb, ss, pt: (bi, layer, 0, hg)),
        ] + [bias_spec(u) for u in range(n_sub)],
        out_specs=pl.BlockSpec((None, n_sub * qn, hw), lambda bi, hg, rb, ss, pt: (bi, rb, hg)),
    )
    return pl.pallas_call(
        functools.partial(_na_kernel, slab_len=sn, n_heads=hb, n_sub=n_sub, qn=qn),
        grid_spec=grid_spec,
        out_shape=jax.ShapeDtypeStruct((b, t, NA_W), BF16),
        compiler_params=_cparams(("arbitrary", "arbitrary", "arbitrary")),
        name="neighborhood_attn",
    )(jnp.asarray(slab_start), jnp.asarray(pat_id), proj, proj, proj, ck, cv, *([bias_tab] * n_sub))


def _layer_norm(y, g, b):
    mu = jnp.mean(y, axis=-1, keepdims=True)
    yc = y - mu
    var = jnp.mean(yc * yc, axis=-1, keepdims=True)
    return yc * lax.rsqrt(var + EPS) * g + b


def _top2_of4(vals):
    m1 = jnp.maximum(jnp.maximum(vals[0], vals[1]), jnp.maximum(vals[2], vals[3]))
    i1 = jnp.where(vals[0] == m1, 0, jnp.where(vals[1] == m1, 1, jnp.where(vals[2] == m1, 2, 3)))
    rest = [jnp.where(i1 == i, -1.0, vals[i]) for i in range(4)]
    m2 = jnp.maximum(jnp.maximum(rest[0], rest[1]), jnp.maximum(rest[2], rest[3]))
    i2 = jnp.where(rest[0] == m2, 0, jnp.where(rest[1] == m2, 1, jnp.where(rest[2] == m2, 2, 3)))
    return m1, i1, m2, i2


def _route_rows(logits_t):
    m = logits_t.max(axis=0, keepdims=True)
    e = jnp.exp(logits_t - m)
    probs = e / e.sum(axis=0, keepdims=True)
    rows = [probs[i:i + 1, :] for i in range(N_EXPERTS)]
    groups = [rows[g * EXPERTS_PER_GROUP:(g + 1) * EXPERTS_PER_GROUP] for g in range(N_GROUPS)]
    scores = []
    for g in range(N_GROUPS):
        m1, _, m2, _ = _top2_of4(groups[g])
        scores.append(m1 + m2)
    best = jnp.maximum(jnp.maximum(scores[0], scores[1]), jnp.maximum(scores[2], scores[3]))
    gi = jnp.where(scores[0] == best, 0, jnp.where(scores[1] == best, 1, jnp.where(scores[2] == best, 2, 3)))
    sel = [jnp.where(gi == 0, groups[0][i], jnp.where(gi == 1, groups[1][i],
                     jnp.where(gi == 2, groups[2][i], groups[3][i]))) for i in range(EXPERTS_PER_GROUP)]
    w1, l1, w2, l2 = _top2_of4(sel)
    wsum = w1 + w2
    return (gi * EXPERTS_PER_GROUP + l1, gi * EXPERTS_PER_GROUP + l2, w1 / wsum, w2 / wsum)


def _out_proj_kernel(oa_ref, on_ref, x_ref, mod_ref, wo_ref, g_ref, b_ref, wr_ref, br_ref, h2_in_ref,
                     x1_ref, h2_ref, e_ref, wt_ref):
    del h2_in_ref
    attn = jnp.dot(oa_ref[...], wo_ref[0:GA_Q, :], preferred_element_type=F32)
    attn = attn + jnp.dot(on_ref[...], wo_ref[GA_Q:GA_Q + NA_W, :], preferred_element_type=F32)
    gate1 = mod_ref[2:3, :]
    x1 = _layer_norm(DN_ALPHA * x_ref[...] + gate1 * attn, g_ref[...], b_ref[...])
    x1_ref[...] = x1
    h2 = x1 * (1.0 + mod_ref[4:5, :]) + mod_ref[3:4, :]
    h2_ref[...] = h2
    h_hi = h2.astype(BF16)
    h_lo = (h2 - h_hi.astype(F32)).astype(BF16)
    r_hi = jnp.dot(h_hi, wr_ref[...], preferred_element_type=F32)
    r_lo = jnp.dot(h_lo, wr_ref[...], preferred_element_type=F32)
    logits = r_hi + pltpu.roll(r_hi, LANE - N_EXPERTS, 1) + r_lo
    logits_t = logits.T[0:N_EXPERTS, :] + br_ref[...]
    e1, e2, w1, w2 = _route_rows(logits_t)
    e_ref[0:1, :] = e1
    e_ref[1:2, :] = e2
    n = w1.shape[1]
    row = lax.broadcasted_iota(jnp.int32, (LANE, n), 0)
    w_rows = jnp.where(row == 0, w1, jnp.where(row == 1, w2, 0.0))
    wt_ref[...] = w_rows.T


def _out_proj(oa, on, x, mod_l, mod_row0, w_o_bf, w_layer, ln_g, ln_b, wr_cat, b_router, h2_all, row0):
    b, t, d = x.shape
    tm = min(OUT_TM, t)
    nt = t // tm
    assert row0 % tm == 0
    tok = lambda bi, ti: (bi, ti, 0)
    const2 = lambda bi, ti: (0, 0)
    return pl.pallas_call(
        _out_proj_kernel,
        grid=(b, t // tm),
        in_specs=[
            pl.BlockSpec((None, tm, GA_Q), tok),
            pl.BlockSpec((None, tm, NA_W), tok),
            pl.BlockSpec((None, tm, d), tok),
            pl.BlockSpec((None, N_MOD, d), lambda bi, ti: (bi + mod_row0, 0, 0)),
            pl.BlockSpec((None, GA_Q + NA_W, d), lambda bi, ti: (w_layer, 0, 0)),
            pl.BlockSpec((1, d), const2),
            pl.BlockSpec((1, d), const2),
            pl.BlockSpec((d, LANE), const2),
            pl.BlockSpec((N_EXPERTS, 1), const2),
            pl.BlockSpec(memory_space=pl.ANY),
        ],
        out_specs=[
            pl.BlockSpec((None, tm, d), tok),
            pl.BlockSpec((tm, d), lambda bi, ti: (row0 // tm + bi * nt + ti, 0)),
            pl.BlockSpec((None, TOP_K, tm), lambda bi, ti: (bi, 0, ti)),
            pl.BlockSpec((None, tm, LANE), tok),
        ],
        out_shape=[
            jax.ShapeDtypeStruct((b, t, d), F32),
            jax.ShapeDtypeStruct(h2_all.shape, F32),
            jax.ShapeDtypeStruct((b, TOP_K, t), jnp.int32),
            jax.ShapeDtypeStruct((b, t, LANE), F32),
        ],
        input_output_aliases={9: 1},
        compiler_params=_cparams(("arbitrary", "arbitrary")),
        name="out_proj_ln_router",
    )(oa, on, x, mod_l, w_o_bf, ln_g.reshape(1, d), ln_b.reshape(1, d), wr_cat,
      b_router.reshape(N_EXPERTS, 1).astype(F32), h2_all)


def _row_copy(src_hbm, idx_ref, dst, sem, r):
    return pltpu.make_async_copy(src_hbm.at[pl.ds(idx_ref[0, r], 1)], dst.at[pl.ds(r, 1)], sem)


def _start_row_gather(src_hbm, idx_ref, dst, sem, n_rows):
    def body(r, carry):
        _row_copy(src_hbm, idx_ref, dst, sem, r).start()
        return carry
    lax.fori_loop(0, n_rows, body, 0, unroll=8)


def _start_row_gather_inline(src_hbm, idx_ref, dst, sem, r_lo, r_hi):
    for r in range(r_lo, r_hi):
        _row_copy(src_hbm, idx_ref, dst, sem, r).start()


def _wait_row_gather(src_hbm, dst, sem, n_rows):
    pltpu.make_async_copy(src_hbm.at[pl.ds(0, n_rows)], dst, sem).wait()


def _moe_kernel(be_ref, nu_ref, tok0_ref, tok1_ref, tok2_ref, h_hbm, wg_ref, wu_ref, wd_ref, o_ref,
                buf, sem, *, n_blocks):
    del be_ref
    i = pl.program_id(0)
    n_used = nu_ref[0]
    slot = i % GATHER_DEPTH
    ahead = (i + 2) % GATHER_DEPTH

    @pl.when(i == 0)
    def _():
        _start_row_gather(h_hbm, tok0_ref, buf.at[0], sem.at[0], MOE_BLK)
        _start_row_gather(h_hbm, tok1_ref, buf.at[1], sem.at[1], MOE_BLK)

    _wait_row_gather(h_hbm, buf.at[slot], sem.at[slot], MOE_BLK)

    @pl.when(i < n_used)
    def _():
        xb = buf[slot].astype(BF16)
        gate = jnp.dot(xb, wg_ref[...], preferred_element_type=F32)
        up = jnp.dot(xb, wu_ref[...], preferred_element_type=F32)
        act = (gate / (1.0 + jnp.exp(-gate)) * up).astype(BF16)
        _start_row_gather_inline(h_hbm, tok2_ref, buf.at[ahead], sem.at[ahead], 0, MOE_BLK)
        o_ref[...] = jnp.dot(act, wd_ref[...], preferred_element_type=F32)

    @pl.when(i >= n_used)
    def _():
        o_ref[...] = jnp.zeros_like(o_ref)
        _start_row_gather(h_hbm, tok2_ref, buf.at[ahead], sem.at[ahead], MOE_BLK)

    @pl.when(i == n_blocks - 1)
    def _():
        for k in (1, 2):
            s = (i + k) % GATHER_DEPTH
            _wait_row_gather(h_hbm, buf.at[s], sem.at[s], MOE_BLK)


def _moe_ffn(h2_flat, tok_buf, blk_e, n_used, wg, wu, wd, layer):
    n, d = h2_flat.shape
    n_blocks = blk_e.shape[0]
    d_ff = wg.shape[3]
    tok3 = tok_buf.reshape(n_blocks, 1, MOE_BLK)
    smem_blk = lambda f: pl.BlockSpec((None, 1, MOE_BLK), f, memory_space=pltpu.SMEM)
    grid_spec = pltpu.PrefetchScalarGridSpec(
        num_scalar_prefetch=2,
        grid=(n_blocks,),
        in_specs=[
            smem_blk(lambda i, be, nu: (i, 0, 0)),
            smem_blk(lambda i, be, nu: (jnp.minimum(i + 1, n_blocks - 1), 0, 0)),
            smem_blk(lambda i, be, nu: (jnp.minimum(i + 2, n_blocks - 1), 0, 0)),
            pl.BlockSpec(memory_space=pl.ANY),
            pl.BlockSpec((None, None, d, d_ff), lambda i, be, nu: (layer, be[i], 0, 0)),
            pl.BlockSpec((None, None, d, d_ff), lambda i, be, nu: (layer, be[i], 0, 0)),
            pl.BlockSpec((None, None, d_ff, d), lambda i, be, nu: (layer, be[i], 0, 0)),
        ],
        out_specs=pl.BlockSpec((MOE_BLK, d), lambda i, be, nu: (i, 0)),
        scratch_shapes=[pltpu.VMEM((GATHER_DEPTH, MOE_BLK, d), F32), pltpu.SemaphoreType.DMA((GATHER_DEPTH,))],
    )
    return pl.pallas_call(
        functools.partial(_moe_kernel, n_blocks=n_blocks),
        grid_spec=grid_spec,
        out_shape=jax.ShapeDtypeStruct((n_blocks * MOE_BLK, d), F32),
        compiler_params=_cparams(("arbitrary",)),
        name="moe_ffn",
    )(blk_e, n_used, tok3, tok3, tok3, h2_flat, wg, wu, wd)


def _combine_kernel(da0_ref, da1_ref, db0_ref, db1_ref, dc0_ref, dc1_ref, y_hbm, x1_ref, mod_ref, wt_ref,
                    g_ref, b_ref, o_ref, buf, sem, *, n_steps, tm):
    i = pl.program_id(0)
    slot = i % GATHER_DEPTH
    ahead = (i + 2) % GATHER_DEPTH

    @pl.when(i == 0)
    def _():
        for s, (r0, r1) in enumerate(((da0_ref, da1_ref), (db0_ref, db1_ref))):
            _start_row_gather(y_hbm, r0, buf.at[s, 0], sem.at[s], tm)
            _start_row_gather(y_hbm, r1, buf.at[s, 1], sem.at[s], tm)

    def wait(s):
        _wait_row_gather(y_hbm, buf.at[s, 0], sem.at[s], tm)
        _wait_row_gather(y_hbm, buf.at[s, 1], sem.at[s], tm)

    wait(slot)
    y = wt_ref[:, 0:1] * buf[slot, 0] + wt_ref[:, 1:2] * buf[slot, 1]
    gate2 = mod_ref[5:6, :]
    o_ref[...] = _layer_norm(DN_ALPHA * x1_ref[...] + gate2 * y, g_ref[...], b_ref[...])
    _start_row_gather_inline(y_hbm, dc0_ref, buf.at[ahead, 0], sem.at[ahead], 0, tm)
    _start_row_gather_inline(y_hbm, dc1_ref, buf.at[ahead, 1], sem.at[ahead], 0, tm)

    @pl.when(i == n_steps - 1)
    def _():
        for k in (1, 2):
            wait((i + k) % GATHER_DEPTH)


def _combine(y_sorted, dest, x1, mod_l, mod_row0, wt, ln_g, ln_b):
    b, t, d = x1.shape
    tm = min(OUT_TM, t)
    nt = t // tm
    n_steps = b * nt
    dest4 = dest.reshape(b, TOP_K, nt, tm).transpose(0, 2, 1, 3).reshape(n_steps, TOP_K, 1, tm)
    later = lambda i, k: jnp.minimum(i + k, n_steps - 1)
    smem_blk = lambda f: pl.BlockSpec((None, None, 1, tm), f, memory_space=pltpu.SMEM)
    tok = lambda i: (i // nt, i % nt, 0)
    return pl.pallas_call(
        functools.partial(_combine_kernel, n_steps=n_steps, tm=tm),
        grid=(n_steps,),
        in_specs=[
            smem_blk(lambda i: (i, 0, 0, 0)),
            smem_blk(lambda i: (i, 1, 0, 0)),
            smem_blk(lambda i: (later(i, 1), 0, 0, 0)),
            smem_blk(lambda i: (later(i, 1), 1, 0, 0)),
            smem_blk(lambda i: (later(i, 2), 0, 0, 0)),
            smem_blk(lambda i: (later(i, 2), 1, 0, 0)),
            pl.BlockSpec(memory_space=pl.ANY),
            pl.BlockSpec((None, tm, d), tok),
            pl.BlockSpec((None, N_MOD, d), lambda i: (i // nt + mod_row0, 0, 0)),
            pl.BlockSpec((None, tm, LANE), tok),
            pl.BlockSpec((1, d), lambda i: (0, 0)),
            pl.BlockSpec((1, d), lambda i: (0, 0)),
        ],
        out_specs=pl.BlockSpec((None, tm, d), tok),
        out_shape=jax.ShapeDtypeStruct((b, t, d), F32),
        scratch_shapes=[pltpu.VMEM((GATHER_DEPTH, TOP_K, tm, d), F32), pltpu.SemaphoreType.DMA((GATHER_DEPTH,))],
        compiler_params=_cparams(("arbitrary",)),
        name="moe_combine_ln",
    )(dest4, dest4, dest4, dest4, dest4, dest4, y_sorted, x1, mod_l, wt, ln_g.reshape(1, d), ln_b.reshape(1, d))


def _dispatch_plan(e_idxs, row0s):
    flat_e = jnp.concatenate([e_idx.reshape(-1) for e_idx in e_idxs])
    a = flat_e.shape[0]
    onehot = (flat_e[:, None] == jnp.arange(N_EXPERTS, dtype=jnp.int32)[None, :]).astype(jnp.int32)
    counts = onehot.sum(axis=0)
    chunk = 256
    assert a % chunk == 0
    oh = onehot.reshape(a // chunk, chunk, N_EXPERTS).astype(F32)
    tri = jnp.tril(jnp.ones((chunk, chunk), F32))
    within = jnp.einsum('ij,cje->cie', tri, oh, precision=lax.Precision.HIGHEST)
    before = jnp.cumsum(oh.sum(axis=1), axis=0) - oh.sum(axis=1)
    running = (within + before[:, None, :]).reshape(a, N_EXPERTS).astype(jnp.int32)
    rank = (running * onehot).sum(axis=1) - 1
    padded = (counts + MOE_BLK - 1) // MOE_BLK * MOE_BLK
    pad_end = jnp.cumsum(padded)
    pad_start = pad_end - padded
    dest = pad_start[flat_e] + rank
    n_blocks = -(-a // MOE_BLK) + N_EXPERTS
    p = n_blocks * MOE_BLK
    blk_start = jnp.arange(n_blocks, dtype=jnp.int32) * MOE_BLK
    blk_e = jnp.minimum((blk_start[:, None] >= pad_end[None, :]).sum(axis=1), N_EXPERTS - 1).astype(jnp.int32)
    n_used = (pad_end[-1] // MOE_BLK).astype(jnp.int32)
    last_e = blk_e[jnp.maximum(n_used - 1, 0)]
    blk_e = jnp.where(jnp.arange(n_blocks) < n_used, blk_e, last_e)
    assert a < (1 << 16)
    order = lax.sort(flat_e * (1 << 16) + jnp.arange(a, dtype=jnp.int32)) & 0xFFFF
    start = jnp.cumsum(counts) - counts
    order = jnp.concatenate([order, jnp.zeros((MOE_BLK,), jnp.int32)])
    win = jnp.clip(start[blk_e] + blk_start - pad_start[blk_e], 0, a)
    order_blk = order[(win[:, None] + jnp.arange(MOE_BLK, dtype=jnp.int32)[None, :]).reshape(p)]
    tok_buf, o = jnp.zeros((p,), jnp.int32), 0
    for e_idx, row0 in zip(e_idxs, row0s):
        b, k, t = e_idx.shape
        local = order_blk - o
        rows = row0 + (local // (k * t)) * t + local % t
        tok_buf = jnp.where((local >= 0) & (local < e_idx.size), rows, tok_buf)
        o += e_idx.size
    dests, o = [], 0
    for e_idx in e_idxs:
        dests.append(dest[o:o + e_idx.size].reshape(e_idx.shape))
        o += e_idx.size
    return dests, tok_buf, blk_e, n_used.reshape(1)


def _attention_sublayer(x, mod_l, mod_row0, lw, ctx):
    b, t, d = x.shape
    layer = lw['layer']
    if ctx is None:
        xt = x.reshape(1, b * t, d)
        proj, *cache = _in_proj(xt, mod_l, mod_row0, lw['w_in'], layer, lw['q_gain'], lw['k_gain'],
                                cache=(layer, lw['depth'], b, t, lw['prev_cache']))
        proj = proj.reshape(b, t, -1)
        oa = _dense_attention(proj, COL_QA, COL_KA, COL_VA, N_KV_GA, N_HEADS_GA // N_KV_GA, gs=GA_KV_PER_STEP)
        on = _dense_attention(proj, COL_QN, COL_KN, COL_VN, N_HEADS_NA, 1, gs=NA_HEADS_PER_STEP)
        return oa.reshape(1, b * t, -1), on.reshape(1, b * t, -1), xt, cache
    ga_k, ga_v, na_k, na_v, rope_tabs, na_bias, na_plan = ctx
    (proj,) = _in_proj(x, mod_l, mod_row0, lw['w_in'], layer, lw['q_gain'], lw['k_gain'], rope_tabs=rope_tabs)
    oa = _dense_attention(proj, COL_QA, COL_KA, COL_VA, N_KV_GA, N_HEADS_GA // N_KV_GA, gs=GA_KV_PER_STEP,
                          ctx=(ga_k, ga_v, layer))
    on = _neighborhood_attention(proj, na_k, na_v, layer, na_bias, na_plan)
    return oa, on, x, None


def _layer(x_ctx, x_lat, mod_l, lw, shared, lat_ctx, h2_all):
    bc, tc, d = x_ctx.shape
    bl, tl, _ = x_lat.shape
    n_ctx, n_lat = bc * tc, bl * tl
    streams = [_attention_sublayer(x_ctx, mod_l, bl, lw, None),
               _attention_sublayer(x_lat, mod_l, 0, lw, lat_ctx)]
    cache = streams[0][3]
    mod_rows, row0s = (bl, 0), (0, n_ctx)
    x1s, e_idxs, wts = [], [], []
    for (oa, on, xt, _), mod_row0, row0 in zip(streams, mod_rows, row0s):
        x1, h2_all, e_idx, wt = _out_proj(oa, on, xt, mod_l, mod_row0, lw['w_o'], lw['layer'], lw['ln1_g'],
                                          lw['ln1_b'], shared['wr_cat'], shared['b_router'], h2_all, row0)
        x1s.append(x1)
        e_idxs.append(e_idx)
        wts.append(wt)
    dests, tok_buf, blk_e, n_used = _dispatch_plan(e_idxs, row0s)
    y_sorted = _moe_ffn(h2_all, tok_buf, blk_e, n_used, lw['w_gate'], lw['w_up'], lw['w_down'], lw['layer'])
    outs = [_combine(y_sorted, dest, x1, mod_l, mod_row0, wt, lw['ln2_g'], lw['ln2_b'])
            for dest, x1, wt, mod_row0 in zip(dests, x1s, wts, mod_rows)]
    return outs[0].reshape(bc, tc, d), outs[1].reshape(bl, tl, d), cache, h2_all


def kernel(x_prompt, x_sample, c, cache_ga_k, cache_ga_v, cache_na_k, cache_na_v, c_ctx, w_router, b_router, w_mod, b_mod, w_in, q_norm, k_norm, rel_bias, w_o, ln1_g, ln1_b, ln2_g, ln2_b, w_gate, w_up, w_down):
    bsz, seq, d = x_prompt.shape
    dec_b, dec_t, _ = x_sample.shape
    depth = w_mod.shape[0]
    past = cache_ga_k.shape[2]

    mod_rows = 16
    assert dec_b + 1 <= mod_rows
    cvec = jnp.concatenate([c, c_ctx[None, :], jnp.zeros((mod_rows - dec_b - 1, d), F32)], axis=0)
    mod = _modulation(cvec, w_mod, b_mod).reshape(depth, mod_rows, N_MOD, d)

    wr_hi = w_router.astype(BF16)
    wr_lo = (w_router - wr_hi.astype(F32)).astype(BF16)
    wr_cat = jnp.concatenate([wr_hi, wr_lo, jnp.zeros((d, LANE - 2 * N_EXPERTS), BF16)], axis=1)
    shared = {'wr_cat': wr_cat, 'b_router': b_router}

    rope_tabs = _rope_tables(dec_t)
    na_plan = _na_plan(dec_t // GRID_W)
    ga_k = cache_ga_k.reshape(dec_b, depth, past, GA_KV)
    ga_v = cache_ga_v.reshape(dec_b, depth, past, GA_KV)
    na_k = cache_na_k.reshape(dec_b, depth, past, NA_W)
    na_v = cache_na_v.reshape(dec_b, depth, past, NA_W)

    w_in_bf, w_o_bf = w_in.astype(BF16), w_o.astype(BF16)
    w_gate_bf, w_up_bf, w_down_bf = w_gate.astype(BF16), w_up.astype(BF16), w_down.astype(BF16)

    y_prompt, y_sample = x_prompt, x_sample
    cache = [jnp.zeros((bsz, depth, seq, w), F32) for w in (GA_KV, GA_KV, NA_W, NA_W)]
    h2_all = jnp.zeros((bsz * seq + dec_b * dec_t, d), F32)
    for i in range(depth):
        lw = {
            'layer': i, 'depth': depth, 'prev_cache': cache,
            'w_in': w_in_bf, 'w_o': w_o_bf,
            'q_gain': (q_norm[i] * Q_PRESCALE).reshape(1, HEAD_DIM).astype(F32),
            'k_gain': k_norm[i].reshape(1, HEAD_DIM).astype(F32),
            'ln1_g': ln1_g[i], 'ln1_b': ln1_b[i], 'ln2_g': ln2_g[i], 'ln2_b': ln2_b[i],
            'w_gate': w_gate_bf, 'w_up': w_up_bf, 'w_down': w_down_bf,
        }
        na_bias = _na_bias_table(rel_bias[i], na_plan[4])
        y_prompt, y_sample, cache, h2_all = _layer(y_prompt, y_sample, mod[i], lw, shared,
                                                   (ga_k, ga_v, na_k, na_v, rope_tabs, na_bias, na_plan), h2_all)

    outs = [arr.reshape(bsz, depth, seq, -1, HEAD_DIM) for arr in cache]
    return (y_prompt, y_sample, outs[0], outs[1], outs[2], outs[3])
```

```python
import functools
import math

import numpy as np
import jax
import jax.numpy as jnp
from jax import lax
from jax.experimental import pallas as pl
from jax.experimental.pallas import tpu as pltpu

F32 = jnp.float32
BF16 = jnp.bfloat16

DEPTH = 2
GRID_W = 64
HEAD_DIM = 128
N_HEADS_GA = 8
N_KV_GA = 2
N_HEADS_NA = 8
NA_WIN_H = 8
NA_WIN_W = 16
ROPE_THETA = 10000.0
N_EXPERTS = 16
N_GROUPS = 4
EXPERTS_PER_GROUP = N_EXPERTS // N_GROUPS
TOP_K = 2
N_MOD = 6
DN_ALPHA = (2.0 * DEPTH) ** 0.25
EPS = 1e-6
ATTN_SCALE = HEAD_DIM ** -0.5
LOG2E = math.log2(math.e)
Q_PRESCALE = ATTN_SCALE * LOG2E
MASK_VALUE = -1e30

GA_Q = N_HEADS_GA * HEAD_DIM
GA_KV = N_KV_GA * HEAD_DIM
NA_W = N_HEADS_NA * HEAD_DIM
COL_QA = 0
COL_KA = COL_QA + N_HEADS_GA
COL_VA = COL_KA + N_KV_GA
COL_QN = COL_VA + N_KV_GA
COL_KN = COL_QN + N_HEADS_NA
COL_VN = COL_KN + N_HEADS_NA
IN_HEADS = COL_VN + N_HEADS_NA

LANE = 128
PROJ_TN = 4 * HEAD_DIM
PROJ_TM = 512
PROJ_TM_CACHE = 256
ATTN_TQ = 512
NA_ROWS = 4
NA_HEADS_PER_STEP = 4
NA_BLOCKS_PER_STEP = 2
GA_KV_PER_STEP = 2
OUT_TM = 256
MOE_BLK = 256
GATHER_DEPTH = 3
MOD_TN = 1024
VMEM_LIMIT = 52 * 1024 * 1024


def _cparams(sem):
    return pltpu.CompilerParams(dimension_semantics=sem, vmem_limit_bytes=VMEM_LIMIT)


def _mod_kernel(c_ref, w_ref, b_ref, o_ref):
    c = c_ref[...]
    s = c / (1.0 + jnp.exp(-c))
    o_ref[...] = jnp.dot(s.astype(BF16), w_ref[...].astype(BF16),
                         preferred_element_type=F32) + b_ref[...]


def _modulation(cvec, w_mod, b_mod):
    depth, d, e = w_mod.shape
    rows = cvec.shape[0]
    tn = MOD_TN if e % MOD_TN == 0 else e
    return pl.pallas_call(
        _mod_kernel,
        grid=(depth, e // tn),
        in_specs=[
            pl.BlockSpec((rows, d), lambda l, j: (0, 0)),
            pl.BlockSpec((None, d, tn), lambda l, j: (l, 0, j)),
            pl.BlockSpec((None, 1, tn), lambda l, j: (l, 0, j)),
        ],
        out_specs=pl.BlockSpec((None, rows, tn), lambda l, j: (l, 0, j)),
        out_shape=jax.ShapeDtypeStruct((depth, rows, e), F32),
        compiler_params=_cparams(("arbitrary", "arbitrary")),
        name="modulation",
    )(cvec, w_mod, b_mod.reshape(depth, 1, e))


def _rms_head(x, g):
    ms = jnp.mean(x * x, axis=-1, keepdims=True)
    return x * lax.rsqrt(ms + EPS) * g


def _rope_head(x, cos, sin, first_half):
    swapped = jnp.where(first_half, pltpu.roll(x, HEAD_DIM - 32, 1), pltpu.roll(x, 32, 1))
    return x * cos + swapped * sin


def _proj_kernel(*refs, rope, cache):
    x_ref, mod_ref, w_ref, qg_ref, kg_ref = refs[:5]
    pos = 5
    if rope:
        cos_ref, sin_ref = refs[pos:pos + 2]
        pos += 2
    if cache is not None:
        pos += cache[0]
    o_ref = refs[pos]
    pos += 1
    if cache is not None:
        ka_ref, va_ref, kn_ref, vn_ref = refs[pos:pos + 4]
        pos += 4
    h_scr = refs[pos]

    shift = mod_ref[0:1, :]
    scale = mod_ref[1:2, :]
    h_scr[...] = (x_ref[...] * (1.0 + scale) + shift).astype(BF16)

    if rope:
        lane = lax.broadcasted_iota(jnp.int32, (1, HEAD_DIM), 1)
        first_half = (lane % 64) < 32

    def normed(y, g):
        y = _rms_head(y, g)
        if rope:
            y = _rope_head(y, cos_ref[...], sin_ref[...], first_half)
        return y

    def to_cache(c_ref, c_head, y):
        if cache is not None:
            c_ref[:, :, c_head * HEAD_DIM:(c_head + 1) * HEAD_DIM] = y.reshape(cache[1], cache[2], HEAD_DIM)

    hpt = PROJ_TN // HEAD_DIM
    for jt in range(IN_HEADS // hpt):
        acc = jnp.dot(h_scr[...], w_ref[:, jt * PROJ_TN:(jt + 1) * PROJ_TN], preferred_element_type=F32)
        for hh in range(hpt):
            head = jt * hpt + hh
            y = acc[:, hh * HEAD_DIM:(hh + 1) * HEAD_DIM]
            if head < COL_KA:
                y = normed(y, qg_ref[...])
            elif head < COL_VA:
                y = normed(y, kg_ref[...])
                to_cache(ka_ref if cache else None, head - COL_KA, y)
            elif head < COL_QN:
                to_cache(va_ref if cache else None, head - COL_VA, y)
            elif head < COL_KN:
                y = y * Q_PRESCALE
            elif head < COL_VN:
                to_cache(kn_ref if cache else None, head - COL_KN, y)
            else:
                to_cache(vn_ref if cache else None, head - COL_VN, y)
            o_ref[:, head * HEAD_DIM:(head + 1) * HEAD_DIM] = y.astype(BF16)


def _in_proj(x, mod_l, mod_row0, w_in_bf, w_layer, q_gain, k_gain, rope_tabs=None, cache=None):
    b, t, d = x.shape
    in_w = w_in_bf.shape[2]
    rope = rope_tabs is not None
    tm = min(PROJ_TM if rope else PROJ_TM_CACHE, t)
    assert in_w == IN_HEADS * HEAD_DIM and in_w % PROJ_TN == 0

    in_specs = [
        pl.BlockSpec((None, tm, d), lambda bi, ti: (bi, ti, 0)),
        pl.BlockSpec((None, N_MOD, d), lambda bi, ti: (bi + mod_row0, 0, 0)),
        pl.BlockSpec((None, d, in_w), lambda bi, ti: (w_layer, 0, 0), pipeline_mode=pl.Buffered(1)),
        pl.BlockSpec((1, HEAD_DIM), lambda bi, ti: (0, 0)),
        pl.BlockSpec((1, HEAD_DIM), lambda bi, ti: (0, 0)),
    ]
    args = [x, mod_l, w_in_bf, q_gain, k_gain]
    if rope:
        in_specs += [pl.BlockSpec((tm, HEAD_DIM), lambda bi, ti: (ti, 0))] * 2
        args += list(rope_tabs)
    out_specs = [pl.BlockSpec((None, tm, in_w), lambda bi, ti: (bi, ti, 0))]
    out_shape = [jax.ShapeDtypeStruct((b, t, in_w), BF16)]
    aliases = {}
    cache_cfg = None
    if cache is not None:
        layer, depth, bsz, seq, prev = cache
        assert b == 1 and tm % seq == 0
        nb = tm // seq
        cache_cfg = (len(prev), nb, seq)
        for k, arr in enumerate(prev):
            aliases[len(args)] = 1 + k
            in_specs.append(pl.BlockSpec(memory_space=pl.ANY))
            args.append(arr)
        widths = (GA_KV, GA_KV, NA_W, NA_W)
        out_specs += [pl.BlockSpec((nb, None, seq, w), lambda bi, ti: (ti, layer, 0, 0)) for w in widths]
        out_shape += [jax.ShapeDtypeStruct((bsz, depth, seq, w), F32) for w in widths]
    outs = pl.pallas_call(
        functools.partial(_proj_kernel, rope=rope, cache=cache_cfg),
        grid=(b, t // tm),
        in_specs=in_specs,
        out_specs=out_specs,
        out_shape=out_shape,
        input_output_aliases=aliases,
        scratch_shapes=[pltpu.VMEM((tm, d), BF16)],
        compiler_params=_cparams(("arbitrary", "arbitrary")),
        name="in_proj_rope" if rope else "in_proj_cache",
    )(*args)
    return outs


def _rope_tables(t):
    half = HEAD_DIM // 4
    tt = jnp.arange(t, dtype=jnp.int32)
    row = (tt // GRID_W).astype(F32)
    col = (tt % GRID_W).astype(F32)
    inv_freq = 1.0 / (ROPE_THETA ** (jnp.arange(half, dtype=F32) / half))
    ar = row[:, None] * inv_freq[None, :]
    ac = col[:, None] * inv_freq[None, :]
    cos = jnp.concatenate([jnp.cos(ar), jnp.cos(ar), jnp.cos(ac), jnp.cos(ac)], axis=-1)
    sin = jnp.concatenate([-jnp.sin(ar), jnp.sin(ar), -jnp.sin(ac), jnp.sin(ac)], axis=-1)
    return cos, sin


_NT = (((1,), (1,)), ((), ()))


def _softmax_pv(scores, values):
    m = scores[0].max(axis=-1, keepdims=True)
    for s in scores[1:]:
        m = jnp.maximum(m, s.max(axis=-1, keepdims=True))
    l = None
    acc = None
    for s, v in zip(scores, values):
        p = jnp.exp2(s - m)
        ps = p.sum(axis=-1, keepdims=True)
        pv = jnp.dot(p.astype(BF16), v, preferred_element_type=F32)
        l = ps if l is None else l + ps
        acc = pv if acc is None else acc + pv
    return acc * (1.0 / l)


def _dense_attn_kernel(*refs, r_heads, gs, has_ctx):
    if has_ctx:
        q_ref, k_ref, v_ref, kc_ref, vc_ref, o_ref = refs
    else:
        q_ref, k_ref, v_ref, o_ref = refs
    for gi in range(gs):
        kv = slice(gi * HEAD_DIM, (gi + 1) * HEAD_DIM)
        k = k_ref[:, kv]
        v = v_ref[:, kv]
        if has_ctx:
            kc = kc_ref[:, kv].astype(BF16)
            vc = vc_ref[:, kv].astype(BF16)
        for r in range(r_heads):
            h = gi * r_heads + r
            sl = slice(h * HEAD_DIM, (h + 1) * HEAD_DIM)
            q = q_ref[:, sl]
            scores = [lax.dot_general(q, k, _NT, preferred_element_type=F32)]
            values = [v]
            if has_ctx:
                scores.append(lax.dot_general(q, kc, _NT, preferred_element_type=F32))
                values.append(vc)
            o_ref[:, sl] = _softmax_pv(scores, values).astype(BF16)


def _dense_attention(proj, q_col, k_col, v_col, groups, r_heads, gs=1, ctx=None):
    b, t, _ = proj.shape
    tq = min(ATTN_TQ, t)
    qw = gs * r_heads * HEAD_DIM
    kw = gs * HEAD_DIM
    assert q_col % (gs * r_heads) == 0 and k_col % gs == 0 and v_col % gs == 0 and groups % gs == 0
    in_specs = [
        pl.BlockSpec((None, tq, qw), lambda bi, g, qi: (bi, qi, q_col // (gs * r_heads) + g)),
        pl.BlockSpec((None, t, kw), lambda bi, g, qi: (bi, 0, k_col // gs + g)),
        pl.BlockSpec((None, t, kw), lambda bi, g, qi: (bi, 0, v_col // gs + g)),
    ]
    args = [proj, proj, proj]
    if ctx is not None:
        ck, cv, layer = ctx
        l_ctx = ck.shape[2]
        spec = pl.BlockSpec((None, None, l_ctx, kw), lambda bi, g, qi: (bi, layer, 0, g))
        in_specs += [spec, spec]
        args += [ck, cv]
    return pl.pallas_call(
        functools.partial(_dense_attn_kernel, r_heads=r_heads, gs=gs, has_ctx=ctx is not None),
        grid=(b, groups // gs, t // tq),
        in_specs=in_specs,
        out_specs=pl.BlockSpec((None, tq, qw), lambda bi, g, qi: (bi, qi, g)),
        out_shape=jax.ShapeDtypeStruct((b, t, groups * r_heads * HEAD_DIM), BF16),
        compiler_params=_cparams(("arbitrary", "arbitrary", "arbitrary")),
        name="dense_attn_ctx" if ctx is not None else "dense_attn",
    )(*args)


def _na_plan(rows):
    kh = min(NA_WIN_H, rows)
    kw = NA_WIN_W
    r_blk = min(NA_ROWS, rows)
    assert rows % r_blk == 0
    slab = min(r_blk - 1 + kh, rows)
    row_start = np.clip(np.arange(rows) - kh // 2, 0, rows - kh)
    col = np.arange(GRID_W)
    col_start = np.clip(col - kw // 2, 0, GRID_W - kw)
    slab_start, pat_id, pats, sigs = [], [], [], {}
    for r0 in range(0, rows, r_blk):
        ss = min(row_start[r0], rows - slab)
        rel = tuple(int(row_start[r0 + ri] - ss) for ri in range(r_blk))
        sig = (int(ss - r0), rel)
        if sig not in sigs:
            sigs[sig] = len(pats)
            q_row = r0 + np.arange(r_blk)[:, None, None, None]
            q_col = col[None, :, None, None]
            k_row = ss + np.arange(slab)[None, None, :, None]
            k_col = col[None, None, None, :]
            rs = row_start[r0:r0 + r_blk][:, None, None, None]
            cs = col_start[None, :, None, None]
            valid = (k_row >= rs) & (k_row < rs + kh) & (k_col >= cs) & (k_col < cs + kw)
            row_off = (k_row - q_row + (NA_WIN_H - 1))[:, 0, :, 0]
            row_sel = (row_off[:, :, None] == np.arange(2 * NA_WIN_H - 1)).astype(np.float32)
            full = (r_blk, GRID_W, slab, GRID_W)
            pats.append((np.broadcast_to(valid, full).reshape(r_blk * GRID_W, slab * GRID_W), row_sel))
        slab_start.append(int(ss))
        pat_id.append(sigs[sig])
    return r_blk, slab, np.array(slab_start, np.int32), np.array(pat_id, np.int32), pats


def _na_bias_table(rel_bias_l, pats):
    col = np.arange(GRID_W)
    col_off = col[None, :] - col[:, None] + (NA_WIN_W - 1)
    col_sel = (col_off[:, :, None] == np.arange(2 * NA_WIN_W - 1)).astype(np.float32)
    hi = lax.Precision.HIGHEST
    tabs = []
    for valid, row_sel in pats:
        by_row = jnp.einsum('hrc,isr->hisc', rel_bias_l.astype(F32), row_sel, precision=hi)
        bias = jnp.einsum('hisc,qkc->hiqsk', by_row, col_sel, precision=hi)
        bias = bias.reshape((bias.shape[0],) + valid.shape) * LOG2E
        tabs.append(jnp.where(valid[None], bias, MASK_VALUE))
    return jnp.stack(tabs, axis=0)


def _na_kernel(ss_ref, pat_ref, q_ref, k_ref, v_ref, kc_ref, vc_ref, *rest, slab_len, n_heads, n_sub, qn):
    del pat_ref
    bias_refs, o_ref = rest[:n_sub], rest[n_sub]
    rb = pl.program_id(2)
    for h in range(n_heads):
        sl = slice(h * HEAD_DIM, (h + 1) * HEAD_DIM)
        kc = kc_ref[:, sl].astype(BF16)
        vc = vc_ref[:, sl].astype(BF16)
        for u in range(n_sub):
            start = pl.multiple_of(ss_ref[rb * n_sub + u] * GRID_W, GRID_W)
            rows = slice(u * qn, (u + 1) * qn)
            ks = k_ref[pl.ds(start, slab_len), sl]
            vs = v_ref[pl.ds(start, slab_len), sl]
            q = q_ref[rows, sl]
            s_loc = lax.dot_general(q, ks, _NT, preferred_element_type=F32) + bias_refs[u][h]
            s_ctx = lax.dot_general(q, kc, _NT, preferred_element_type=F32)
            o_ref[rows, sl] = _softmax_pv([s_loc, s_ctx], [vs, vc]).astype(BF16)


def _neighborhood_attention(proj, ck, cv, layer, bias_tab, plan):
    b, t, _ = proj.shape
    r_blk, slab, slab_start, pat_id, _ = plan
    qn = r_blk * GRID_W
    sn = slab * GRID_W
    l_ctx = ck.shape[2]
    hb = NA_HEADS_PER_STEP
    hw = hb * HEAD_DIM
    assert COL_QN % hb == 0 and COL_KN % hb == 0 and COL_VN % hb == 0 and N_HEADS_NA % hb == 0
    n_rb = t // qn
    n_sub = NA_BLOCKS_PER_STEP if n_rb % NA_BLOCKS_PER_STEP == 0 else 1

    def bias_spec(u):
        return pl.BlockSpec((None, hb, qn, sn), lambda bi, hg, rb, ss, pt: (pt[rb * n_sub + u], hg, 0, 0))

    grid_spec = pltpu.PrefetchScalarGridSpec(
        num_scalar_prefetch=2,
        grid=(b, N_HEADS_NA // hb, n_rb // n_sub),
        in_specs=[
            pl.BlockSpec((None, n_sub * qn, hw), lambda bi, hg, rb, ss, pt: (bi, rb, COL_QN // hb + hg)),
            pl.BlockSpec((None, t, hw), lambda bi, hg, rb, ss, pt: (bi, 0, COL_KN // hb + hg)),
            pl.BlockSpec((None, t, hw), lambda bi, hg, rb, ss, pt: (bi, 0, COL_VN // hb + hg)),
            pl.BlockSpec((None, None, l_ctx, hw), lambda bi, hg, rb, ss, pt: (bi, layer, 0, hg)),
            pl.BlockSpec((None, None, l_ctx, hw), lambda bi, hg, rb, ss, pt: (bi, layer, 0, hg)),
        ] + [bias_spec(u) for u in range(n_sub)],
        out_specs=pl.BlockSpec((None, n_sub * qn, hw), lambda bi, hg, rb, ss, pt: (bi, rb, hg)),
    )
    return pl.pallas_call(
        functools.partial(_na_kernel, slab_len=sn, n_heads=hb, n_sub=n_sub, qn=qn),
        grid_spec=grid_spec,
        out_shape=jax.ShapeDtypeStruct((b, t, NA_W), BF16),
        compiler_params=_cparams(("arbitrary", "arbitrary", "arbitrary")),
        name="neighborhood_attn",
    )(jnp.asarray(slab_start), jnp.asarray(pat_id), proj, proj, proj, ck, cv, *([bias_tab] * n_sub))


def _layer_norm(y, g, b):
    mu = jnp.mean(y, axis=-1, keepdims=True)
    yc = y - mu
    var = jnp.mean(yc * yc, axis=-1, keepdims=True)
    return yc * lax.rsqrt(var + EPS) * g + b


def _top2_of4(vals):
    m1 = jnp.maximum(jnp.maximum(vals[0], vals[1]), jnp.maximum(vals[2], vals[3]))
    i1 = jnp.where(vals[0] == m1, 0, jnp.where(vals[1] == m1, 1, jnp.where(vals[2] == m1, 2, 3)))
    rest = [jnp.where(i1 == i, -1.0, vals[i]) for i in range(4)]
    m2 = jnp.maximum(jnp.maximum(rest[0], rest[1]), jnp.maximum(rest[2], rest[3]))
    i2 = jnp.where(rest[0] == m2, 0, jnp.where(rest[1] == m2, 1, jnp.where(rest[2] == m2, 2, 3)))
    return m1, i1, m2, i2


def _route_rows(logits_t):
    m = logits_t.max(axis=0, keepdims=True)
    e = jnp.exp(logits_t - m)
    probs = e / e.sum(axis=0, keepdims=True)
    rows = [probs[i:i + 1, :] for i in range(N_EXPERTS)]
    groups = [rows[g * EXPERTS_PER_GROUP:(g + 1) * EXPERTS_PER_GROUP] for g in range(N_GROUPS)]
    scores = []
    for g in range(N_GROUPS):
        m1, _, m2, _ = _top2_of4(groups[g])
        scores.append(m1 + m2)
    best = jnp.maximum(jnp.maximum(scores[0], scores[1]), jnp.maximum(scores[2], scores[3]))
    gi = jnp.where(scores[0] == best, 0, jnp.where(scores[1] == best, 1, jnp.where(scores[2] == best, 2, 3)))
    sel = [jnp.where(gi == 0, groups[0][i], jnp.where(gi == 1, groups[1][i],
                     jnp.where(gi == 2, groups[2][i], groups[3][i]))) for i in range(EXPERTS_PER_GROUP)]
    w1, l1, w2, l2 = _top2_of4(sel)
    wsum = w1 + w2
    return (gi * EXPERTS_PER_GROUP + l1, gi * EXPERTS_PER_GROUP + l2, w1 / wsum, w2 / wsum)


def _out_proj_kernel(oa_ref, on_ref, x_ref, mod_ref, wo_ref, g_ref, b_ref, wr_ref, br_ref, h2_in_ref,
                     x1_ref, h2_ref, e_ref, wt_ref):
    del h2_in_ref
    attn = jnp.dot(oa_ref[...], wo_ref[0:GA_Q, :], preferred_element_type=F32)
    attn = attn + jnp.dot(on_ref[...], wo_ref[GA_Q:GA_Q + NA_W, :], preferred_element_type=F32)
    gate1 = mod_ref[2:3, :]
    x1 = _layer_norm(DN_ALPHA * x_ref[...] + gate1 * attn, g_ref[...], b_ref[...])
    x1_ref[...] = x1
    h2 = x1 * (1.0 + mod_ref[4:5, :]) + mod_ref[3:4, :]
    h2_ref[...] = h2
    h_hi = h2.astype(BF16)
    h_lo = (h2 - h_hi.astype(F32)).astype(BF16)
    r_hi = jnp.dot(h_hi, wr_ref[...], preferred_element_type=F32)
    r_lo = jnp.dot(h_lo, wr_ref[...], preferred_element_type=F32)
    logits = r_hi + pltpu.roll(r_hi, LANE - N_EXPERTS, 1) + r_lo
    logits_t = logits.T[0:N_EXPERTS, :] + br_ref[...]
    e1, e2, w1, w2 = _route_rows(logits_t)
    e_ref[0:1, :] = e1
    e_ref[1:2, :] = e2
    n = w1.shape[1]
    row = lax.broadcasted_iota(jnp.int32, (LANE, n), 0)
    w_rows = jnp.where(row == 0, w1, jnp.where(row == 1, w2, 0.0))
    wt_ref[...] = w_rows.T


def _out_proj(oa, on, x, mod_l, mod_row0, w_o_bf, w_layer, ln_g, ln_b, wr_cat, b_router, h2_all, row0):
    b, t, d = x.shape
    tm = min(OUT_TM, t)
    nt = t // tm
    assert row0 % tm == 0
    tok = lambda bi, ti: (bi, ti, 0)
    const2 = lambda bi, ti: (0, 0)
    return pl.pallas_call(
        _out_proj_kernel,
        grid=(b, t // tm),
        in_specs=[
            pl.BlockSpec((None, tm, GA_Q), tok),
            pl.BlockSpec((None, tm, NA_W), tok),
            pl.BlockSpec((None, tm, d), tok),
            pl.BlockSpec((None, N_MOD, d), lambda bi, ti: (bi + mod_row0, 0, 0)),
            pl.BlockSpec((None, GA_Q + NA_W, d), lambda bi, ti: (w_layer, 0, 0)),
            pl.BlockSpec((1, d), const2),
            pl.BlockSpec((1, d), const2),
            pl.BlockSpec((d, LANE), const2),
            pl.BlockSpec((N_EXPERTS, 1), const2),
            pl.BlockSpec(memory_space=pl.ANY),
        ],
        out_specs=[
            pl.BlockSpec((None, tm, d), tok),
            pl.BlockSpec((tm, d), lambda bi, ti: (row0 // tm + bi * nt + ti, 0)),
            pl.BlockSpec((None, TOP_K, tm), lambda bi, ti: (bi, 0, ti)),
            pl.BlockSpec((None, tm, LANE), tok),
        ],
        out_shape=[
            jax.ShapeDtypeStruct((b, t, d), F32),
            jax.ShapeDtypeStruct(h2_all.shape, F32),
            jax.ShapeDtypeStruct((b, TOP_K, t), jnp.int32),
            jax.ShapeDtypeStruct((b, t, LANE), F32),
        ],
        input_output_aliases={9: 1},
        compiler_params=_cparams(("arbitrary", "arbitrary")),
        name="out_proj_ln_router",
    )(oa, on, x, mod_l, w_o_bf, ln_g.reshape(1, d), ln_b.reshape(1, d), wr_cat,
      b_router.reshape(N_EXPERTS, 1).astype(F32), h2_all)


def _row_copy(src_hbm, idx_ref, dst, sem, r, base=None):
    idx = idx_ref[0, r] if base is None else idx_ref[base + r]
    return pltpu.make_async_copy(src_hbm.at[pl.ds(idx, 1)], dst.at[pl.ds(r, 1)], sem)


def _start_row_gather(src_hbm, idx_ref, dst, sem, n_rows, base=None):
    def body(r, carry):
        _row_copy(src_hbm, idx_ref, dst, sem, r, base).start()
        return carry
    lax.fori_loop(0, n_rows, body, 0, unroll=8)


def _start_row_gather_inline(src_hbm, idx_ref, dst, sem, r_lo, r_hi, base=None):
    for r in range(r_lo, r_hi):
        _row_copy(src_hbm, idx_ref, dst, sem, r, base).start()


def _wait_row_gather(src_hbm, dst, sem, n_rows):
    pltpu.make_async_copy(src_hbm.at[pl.ds(0, n_rows)], dst, sem).wait()


def _moe_kernel(be_ref, nu_ref, tok_ref, h_hbm, wg_ref, wu_ref, wd_ref, o_ref,
                buf, sem, *, n_blocks):
    del be_ref
    i = pl.program_id(0)
    n_used = nu_ref[0]
    slot = i % GATHER_DEPTH
    ahead = (i + 2) % GATHER_DEPTH
    base2 = jnp.minimum(i + 2, n_blocks - 1) * MOE_BLK

    @pl.when(i == 0)
    def _():
        _start_row_gather(h_hbm, tok_ref, buf.at[0], sem.at[0], MOE_BLK, base=0)
        _start_row_gather(h_hbm, tok_ref, buf.at[1], sem.at[1], MOE_BLK, base=MOE_BLK)

    _wait_row_gather(h_hbm, buf.at[slot], sem.at[slot], MOE_BLK)

    @pl.when(i < n_used)
    def _():
        xb = buf[slot].astype(BF16)
        gate = jnp.dot(xb, wg_ref[...], preferred_element_type=F32)
        up = jnp.dot(xb, wu_ref[...], preferred_element_type=F32)
        act = (gate / (1.0 + jnp.exp(-gate)) * up).astype(BF16)
        _start_row_gather_inline(h_hbm, tok_ref, buf.at[ahead], sem.at[ahead], 0, MOE_BLK, base=base2)
        o_ref[...] = jnp.dot(act, wd_ref[...], preferred_element_type=F32)

    @pl.when(i >= n_used)
    def _():
        o_ref[...] = jnp.zeros_like(o_ref)
        _start_row_gather(h_hbm, tok_ref, buf.at[ahead], sem.at[ahead], MOE_BLK, base=base2)

    @pl.when(i == n_blocks - 1)
    def _():
        for k in (1, 2):
            s = (i + k) % GATHER_DEPTH
            _wait_row_gather(h_hbm, buf.at[s], sem.at[s], MOE_BLK)


def _moe_ffn(h2_flat, tok_buf, blk_e, n_used, wg, wu, wd, layer):
    n, d = h2_flat.shape
    n_blocks = blk_e.shape[0]
    d_ff = wg.shape[3]
    grid_spec = pltpu.PrefetchScalarGridSpec(
        num_scalar_prefetch=3,
        grid=(n_blocks,),
        in_specs=[
            pl.BlockSpec(memory_space=pl.ANY),
            pl.BlockSpec((None, None, d, d_ff), lambda i, be, nu, tk: (layer, be[i], 0, 0)),
            pl.BlockSpec((None, None, d, d_ff), lambda i, be, nu, tk: (layer, be[i], 0, 0)),
            pl.BlockSpec((None, None, d_ff, d), lambda i, be, nu, tk: (layer, be[i], 0, 0)),
        ],
        out_specs=pl.BlockSpec((MOE_BLK, d), lambda i, be, nu, tk: (i, 0)),
        scratch_shapes=[pltpu.VMEM((GATHER_DEPTH, MOE_BLK, d), F32), pltpu.SemaphoreType.DMA((GATHER_DEPTH,))],
    )
    return pl.pallas_call(
        functools.partial(_moe_kernel, n_blocks=n_blocks),
        grid_spec=grid_spec,
        out_shape=jax.ShapeDtypeStruct((n_blocks * MOE_BLK, d), F32),
        compiler_params=_cparams(("arbitrary",)),
        name="moe_ffn",
    )(blk_e, n_used, tok_buf, h2_flat, wg, wu, wd)


def _combine_kernel(dest_ref, y_hbm, x1_ref, mod_ref, wt_ref, g_ref, b_ref, o_ref, buf, sem, *, n_steps, tm):
    i = pl.program_id(0)
    slot = i % GATHER_DEPTH
    ahead = (i + 2) % GATHER_DEPTH
    step_base = lambda step: jnp.minimum(step, n_steps - 1) * (TOP_K * tm)

    @pl.when(i == 0)
    def _():
        for s in range(2):
            for k in range(TOP_K):
                _start_row_gather(y_hbm, dest_ref, buf.at[s, k], sem.at[s], tm, base=step_base(s) + k * tm)

    def wait(s):
        _wait_row_gather(y_hbm, buf.at[s, 0], sem.at[s], tm)
        _wait_row_gather(y_hbm, buf.at[s, 1], sem.at[s], tm)

    wait(slot)
    y = wt_ref[:, 0:1] * buf[slot, 0] + wt_ref[:, 1:2] * buf[slot, 1]
    gate2 = mod_ref[5:6, :]
    o_ref[...] = _layer_norm(DN_ALPHA * x1_ref[...] + gate2 * y, g_ref[...], b_ref[...])
    for k in range(TOP_K):
        _start_row_gather_inline(y_hbm, dest_ref, buf.at[ahead, k], sem.at[ahead], 0, tm,
                                 base=step_base(i + 2) + k * tm)

    @pl.when(i == n_steps - 1)
    def _():
        for k in (1, 2):
            wait((i + k) % GATHER_DEPTH)


def _combine(y_sorted, dest, x1, mod_l, mod_row0, wt, ln_g, ln_b):
    b, t, d = x1.shape
    tm = min(OUT_TM, t)
    nt = t // tm
    n_steps = b * nt
    dest_flat = dest.reshape(b, TOP_K, nt, tm).transpose(0, 2, 1, 3).reshape(n_steps * TOP_K * tm)
    tok = lambda i, ds: (i // nt, i % nt, 0)
    grid_spec = pltpu.PrefetchScalarGridSpec(
        num_scalar_prefetch=1,
        grid=(n_steps,),
        in_specs=[
            pl.BlockSpec(memory_space=pl.ANY),
            pl.BlockSpec((None, tm, d), tok),
            pl.BlockSpec((None, N_MOD, d), lambda i, ds: (i // nt + mod_row0, 0, 0)),
            pl.BlockSpec((None, tm, LANE), tok),
            pl.BlockSpec((1, d), lambda i, ds: (0, 0)),
            pl.BlockSpec((1, d), lambda i, ds: (0, 0)),
        ],
        out_specs=pl.BlockSpec((None, tm, d), tok),
        scratch_shapes=[pltpu.VMEM((GATHER_DEPTH, TOP_K, tm, d), F32), pltpu.SemaphoreType.DMA((GATHER_DEPTH,))],
    )
    return pl.pallas_call(
        functools.partial(_combine_kernel, n_steps=n_steps, tm=tm),
        grid_spec=grid_spec,
        out_shape=jax.ShapeDtypeStruct((b, t, d), F32),
        compiler_params=_cparams(("arbitrary",)),
        name="moe_combine_ln",
    )(dest_flat, y_sorted, x1, mod_l, wt, ln_g.reshape(1, d), ln_b.reshape(1, d))


def _dispatch_plan(e_idxs, row0s):
    flat_e = jnp.concatenate([e_idx.reshape(-1) for e_idx in e_idxs])
    a = flat_e.shape[0]
    onehot = (flat_e[:, None] == jnp.arange(N_EXPERTS, dtype=jnp.int32)[None, :]).astype(jnp.int32)
    counts = onehot.sum(axis=0)
    chunk = 256
    assert a % chunk == 0
    oh = onehot.reshape(a // chunk, chunk, N_EXPERTS).astype(F32)
    tri = jnp.tril(jnp.ones((chunk, chunk), F32))
    within = jnp.einsum('ij,cje->cie', tri, oh, precision=lax.Precision.HIGHEST)
    before = jnp.cumsum(oh.sum(axis=1), axis=0) - oh.sum(axis=1)
    running = (within + before[:, None, :]).reshape(a, N_EXPERTS).astype(jnp.int32)
    rank = (running * onehot).sum(axis=1) - 1
    padded = (counts + MOE_BLK - 1) // MOE_BLK * MOE_BLK
    pad_end = jnp.cumsum(padded)
    pad_start = pad_end - padded
    dest = pad_start[flat_e] + rank
    n_blocks = -(-a // MOE_BLK) + N_EXPERTS
    p = n_blocks * MOE_BLK
    blk_start = jnp.arange(n_blocks, dtype=jnp.int32) * MOE_BLK
    blk_e = jnp.minimum((blk_start[:, None] >= pad_end[None, :]).sum(axis=1), N_EXPERTS - 1).astype(jnp.int32)
    n_used = (pad_end[-1] // MOE_BLK).astype(jnp.int32)
    last_e = blk_e[jnp.maximum(n_used - 1, 0)]
    blk_e = jnp.where(jnp.arange(n_blocks) < n_used, blk_e, last_e)
    assert a < (1 << 16)
    order = lax.sort(flat_e * (1 << 16) + jnp.arange(a, dtype=jnp.int32)) & 0xFFFF
    start = jnp.cumsum(counts) - counts
    order = jnp.concatenate([order, jnp.zeros((MOE_BLK,), jnp.int32)])
    win = jnp.clip(start[blk_e] + blk_start - pad_start[blk_e], 0, a)
    order_blk = order[(win[:, None] + jnp.arange(MOE_BLK, dtype=jnp.int32)[None, :]).reshape(p)]
    tok_buf, o = jnp.zeros((p,), jnp.int32), 0
    for e_idx, row0 in zip(e_idxs, row0s):
        b, k, t = e_idx.shape
        local = order_blk - o
        rows = row0 + (local // (k * t)) * t + local % t
        tok_buf = jnp.where((local >= 0) & (local < e_idx.size), rows, tok_buf)
        o += e_idx.size
    dests, o = [], 0
    for e_idx in e_idxs:
        dests.append(dest[o:o + e_idx.size].reshape(e_idx.shape))
        o += e_idx.size
    return dests, tok_buf, blk_e, n_used.reshape(1)


def _attention_sublayer(x, mod_l, mod_row0, lw, ctx):
    b, t, d = x.shape
    layer = lw['layer']
    if ctx is None:
        xt = x.reshape(1, b * t, d)
        proj, *cache = _in_proj(xt, mod_l, mod_row0, lw['w_in'], layer, lw['q_gain'], lw['k_gain'],
                                cache=(layer, lw['depth'], b, t, lw['prev_cache']))
        proj = proj.reshape(b, t, -1)
        oa = _dense_attention(proj, COL_QA, COL_KA, COL_VA, N_KV_GA, N_HEADS_GA // N_KV_GA, gs=GA_KV_PER_STEP)
        on = _dense_attention(proj, COL_QN, COL_KN, COL_VN, N_HEADS_NA, 1, gs=NA_HEADS_PER_STEP)
        return oa.reshape(1, b * t, -1), on.reshape(1, b * t, -1), xt, cache
    ga_k, ga_v, na_k, na_v, rope_tabs, na_bias, na_plan = ctx
    (proj,) = _in_proj(x, mod_l, mod_row0, lw['w_in'], layer, lw['q_gain'], lw['k_gain'], rope_tabs=rope_tabs)
    oa = _dense_attention(proj, COL_QA, COL_KA, COL_VA, N_KV_GA, N_HEADS_GA // N_KV_GA, gs=GA_KV_PER_STEP,
                          ctx=(ga_k, ga_v, layer))
    on = _neighborhood_attention(proj, na_k, na_v, layer, na_bias, na_plan)
    return oa, on, x, None


def _layer(x_ctx, x_lat, mod_l, lw, shared, lat_ctx, h2_all):
    bc, tc, d = x_ctx.shape
    bl, tl, _ = x_lat.shape
    n_ctx, n_lat = bc * tc, bl * tl
    streams = [_attention_sublayer(x_ctx, mod_l, bl, lw, None),
               _attention_sublayer(x_lat, mod_l, 0, lw, lat_ctx)]
    cache = streams[0][3]
    mod_rows, row0s = (bl, 0), (0, n_ctx)
    x1s, e_idxs, wts = [], [], []
    for (oa, on, xt, _), mod_row0, row0 in zip(streams, mod_rows, row0s):
        x1, h2_all, e_idx, wt = _out_proj(oa, on, xt, mod_l, mod_row0, lw['w_o'], lw['layer'], lw['ln1_g'],
                                          lw['ln1_b'], shared['wr_cat'], shared['b_router'], h2_all, row0)
        x1s.append(x1)
        e_idxs.append(e_idx)
        wts.append(wt)
    dests, tok_buf, blk_e, n_used = _dispatch_plan(e_idxs, row0s)
    y_sorted = _moe_ffn(h2_all, tok_buf, blk_e, n_used, lw['w_gate'], lw['w_up'], lw['w_down'], lw['layer'])
    outs = [_combine(y_sorted, dest, x1, mod_l, mod_row0, wt, lw['ln2_g'], lw['ln2_b'])
            for dest, x1, wt, mod_row0 in zip(dests, x1s, wts, mod_rows)]
    return outs[0].reshape(bc, tc, d), outs[1].reshape(bl, tl, d), cache, h2_all


def kernel(x_prompt, x_sample, c, cache_ga_k, cache_ga_v, cache_na_k, cache_na_v, c_ctx, w_router, b_router, w_mod, b_mod, w_in, q_norm, k_norm, rel_bias, w_o, ln1_g, ln1_b, ln2_g, ln2_b, w_gate, w_up, w_down):
    bsz, seq, d = x_prompt.shape
    dec_b, dec_t, _ = x_sample.shape
    depth = w_mod.shape[0]
    past = cache_ga_k.shape[2]

    mod_rows = 16
    assert dec_b + 1 <= mod_rows
    cvec = jnp.concatenate([c, c_ctx[None, :], jnp.zeros((mod_rows - dec_b - 1, d), F32)], axis=0)
    mod = _modulation(cvec, w_mod, b_mod).reshape(depth, mod_rows, N_MOD, d)

    wr_hi = w_router.astype(BF16)
    wr_lo = (w_router - wr_hi.astype(F32)).astype(BF16)
    wr_cat = jnp.concatenate([wr_hi, wr_lo, jnp.zeros((d, LANE - 2 * N_EXPERTS), BF16)], axis=1)
    shared = {'wr_cat': wr_cat, 'b_router': b_router}

    rope_tabs = _rope_tables(dec_t)
    na_plan = _na_plan(dec_t // GRID_W)
    ga_k = cache_ga_k.reshape(dec_b, depth, past, GA_KV)
    ga_v = cache_ga_v.reshape(dec_b, depth, past, GA_KV)
    na_k = cache_na_k.reshape(dec_b, depth, past, NA_W)
    na_v = cache_na_v.reshape(dec_b, depth, past, NA_W)

    w_in_bf, w_o_bf = w_in.astype(BF16), w_o.astype(BF16)
    w_gate_bf, w_up_bf, w_down_bf = w_gate.astype(BF16), w_up.astype(BF16), w_down.astype(BF16)

    y_prompt, y_sample = x_prompt, x_sample
    cache = [jnp.zeros((bsz, depth, seq, w), F32) for w in (GA_KV, GA_KV, NA_W, NA_W)]
    h2_all = jnp.zeros((bsz * seq + dec_b * dec_t, d), F32)
    for i in range(depth):
        lw = {
            'layer': i, 'depth': depth, 'prev_cache': cache,
            'w_in': w_in_bf, 'w_o': w_o_bf,
            'q_gain': (q_norm[i] * Q_PRESCALE).reshape(1, HEAD_DIM).astype(F32),
            'k_gain': k_norm[i].reshape(1, HEAD_DIM).astype(F32),
            'ln1_g': ln1_g[i], 'ln1_b': ln1_b[i], 'ln2_g': ln2_g[i], 'ln2_b': ln2_b[i],
            'w_gate': w_gate_bf, 'w_up': w_up_bf, 'w_down': w_down_bf,
        }
        na_bias = _na_bias_table(rel_bias[i], na_plan[4])
        y_prompt, y_sample, cache, h2_all = _layer(y_prompt, y_sample, mod[i], lw, shared,
                                                   (ga_k, ga_v, na_k, na_v, rope_tabs, na_bias, na_plan), h2_all)

    outs = [arr.reshape(bsz, depth, seq, -1, HEAD_DIM) for arr in cache]
    return (y_prompt, y_sample, outs[0], outs[1], outs[2], outs[3])
```

```python
import functools
import math

import numpy as np
import jax
import jax.numpy as jnp
from jax import lax
from jax.experimental import pallas as pl
from jax.experimental.pallas import tpu as pltpu

F32 = jnp.float32
BF16 = jnp.bfloat16

DEPTH = 2
GRID_W = 64
HEAD_DIM = 128
N_HEADS_GA = 8
N_KV_GA = 2
N_HEADS_NA = 8
NA_WIN_H = 8
NA_WIN_W = 16
ROPE_THETA = 10000.0
N_EXPERTS = 16
N_GROUPS = 4
EXPERTS_PER_GROUP = N_EXPERTS // N_GROUPS
TOP_K = 2
N_MOD = 6
DN_ALPHA = (2.0 * DEPTH) ** 0.25
EPS = 1e-6
ATTN_SCALE = HEAD_DIM ** -0.5
LOG2E = math.log2(math.e)
Q_PRESCALE = ATTN_SCALE * LOG2E
MASK_VALUE = -1e30

GA_Q = N_HEADS_GA * HEAD_DIM
GA_KV = N_KV_GA * HEAD_DIM
NA_W = N_HEADS_NA * HEAD_DIM
COL_QA = 0
COL_KA = COL_QA + N_HEADS_GA
COL_VA = COL_KA + N_KV_GA
COL_QN = COL_VA + N_KV_GA
COL_KN = COL_QN + N_HEADS_NA
COL_VN = COL_KN + N_HEADS_NA
IN_HEADS = COL_VN + N_HEADS_NA

LANE = 128
PROJ_TN = 4 * HEAD_DIM
PROJ_TM = 512
PROJ_TM_CACHE = 256
ATTN_TQ = 512
NA_ROWS = 4
NA_HEADS_PER_STEP = 4
NA_BLOCKS_PER_STEP = 2
GA_KV_PER_STEP = 2
OUT_TM = 256
MOE_BLK = 256
GATHER_DEPTH = 3
MOD_TN = 1024
VMEM_LIMIT = 52 * 1024 * 1024


def _cparams(sem):
    return pltpu.CompilerParams(dimension_semantics=sem, vmem_limit_bytes=VMEM_LIMIT)


def _mod_kernel(c_ref, w_ref, b_ref, o_ref):
    c = c_ref[...]
    s = c / (1.0 + jnp.exp(-c))
    o_ref[...] = jnp.dot(s.astype(BF16), w_ref[...].astype(BF16),
                         preferred_element_type=F32) + b_ref[...]


def _modulation(cvec, w_mod, b_mod):
    depth, d, e = w_mod.shape
    rows = cvec.shape[0]
    tn = MOD_TN if e % MOD_TN == 0 else e
    return pl.pallas_call(
        _mod_kernel,
        grid=(depth, e // tn),
        in_specs=[
            pl.BlockSpec((rows, d), lambda l, j: (0, 0)),
            pl.BlockSpec((None, d, tn), lambda l, j: (l, 0, j)),
            pl.BlockSpec((None, 1, tn), lambda l, j: (l, 0, j)),
        ],
        out_specs=pl.BlockSpec((None, rows, tn), lambda l, j: (l, 0, j)),
        out_shape=jax.ShapeDtypeStruct((depth, rows, e), F32),
        compiler_params=_cparams(("arbitrary", "arbitrary")),
        name="modulation",
    )(cvec, w_mod, b_mod.reshape(depth, 1, e))


def _rms_head(x, g):
    ms = jnp.mean(x * x, axis=-1, keepdims=True)
    return x * lax.rsqrt(ms + EPS) * g


def _rope_head(x, cos, sin, first_half):
    swapped = jnp.where(first_half, pltpu.roll(x, HEAD_DIM - 32, 1), pltpu.roll(x, 32, 1))
    return x * cos + swapped * sin


def _proj_kernel(*refs, rope, cache):
    x_ref, mod_ref, w_ref, qg_ref, kg_ref = refs[:5]
    pos = 5
    if rope:
        cos_ref, sin_ref = refs[pos:pos + 2]
        pos += 2
    if cache is not None:
        pos += cache[0]
    o_ref = refs[pos]
    pos += 1
    if cache is not None:
        ka_ref, va_ref, kn_ref, vn_ref = refs[pos:pos + 4]
        pos += 4
    h_scr = refs[pos]

    shift = mod_ref[0:1, :]
    scale = mod_ref[1:2, :]
    h_scr[...] = (x_ref[...] * (1.0 + scale) + shift).astype(BF16)

    if rope:
        lane = lax.broadcasted_iota(jnp.int32, (1, HEAD_DIM), 1)
        first_half = (lane % 64) < 32

    def normed(y, g):
        y = _rms_head(y, g)
        if rope:
            y = _rope_head(y, cos_ref[...], sin_ref[...], first_half)
        return y

    def to_cache(c_ref, c_head, y):
        if cache is not None:
            c_ref[:, :, c_head * HEAD_DIM:(c_head + 1) * HEAD_DIM] = y.reshape(cache[1], cache[2], HEAD_DIM)

    hpt = PROJ_TN // HEAD_DIM
    for jt in range(IN_HEADS // hpt):
        acc = jnp.dot(h_scr[...], w_ref[:, jt * PROJ_TN:(jt + 1) * PROJ_TN], preferred_element_type=F32)
        for hh in range(hpt):
            head = jt * hpt + hh
            y = acc[:, hh * HEAD_DIM:(hh + 1) * HEAD_DIM]
            if head < COL_KA:
                y = normed(y, qg_ref[...])
            elif head < COL_VA:
                y = normed(y, kg_ref[...])
                to_cache(ka_ref if cache else None, head - COL_KA, y)
            elif head < COL_QN:
                to_cache(va_ref if cache else None, head - COL_VA, y)
            elif head < COL_KN:
                y = y * Q_PRESCALE
            elif head < COL_VN:
                to_cache(kn_ref if cache else None, head - COL_KN, y)
            else:
                to_cache(vn_ref if cache else None, head - COL_VN, y)
            o_ref[:, head * HEAD_DIM:(head + 1) * HEAD_DIM] = y.astype(BF16)


def _in_proj(x, mod_l, mod_row0, w_in_bf, w_layer, q_gain, k_gain, rope_tabs=None, cache=None):
    b, t, d = x.shape
    in_w = w_in_bf.shape[2]
    rope = rope_tabs is not None
    tm = min(PROJ_TM if rope else PROJ_TM_CACHE, t)
    assert in_w == IN_HEADS * HEAD_DIM and in_w % PROJ_TN == 0

    in_specs = [
        pl.BlockSpec((None, tm, d), lambda bi, ti: (bi, ti, 0)),
        pl.BlockSpec((None, N_MOD, d), lambda bi, ti: (bi + mod_row0, 0, 0)),
        pl.BlockSpec((None, d, in_w), lambda bi, ti: (w_layer, 0, 0), pipeline_mode=pl.Buffered(1)),
        pl.BlockSpec((1, HEAD_DIM), lambda bi, ti: (0, 0)),
        pl.BlockSpec((1, HEAD_DIM), lambda bi, ti: (0, 0)),
    ]
    args = [x, mod_l, w_in_bf, q_gain, k_gain]
    if rope:
        in_specs += [pl.BlockSpec((tm, HEAD_DIM), lambda bi, ti: (ti, 0))] * 2
        args += list(rope_tabs)
    out_specs = [pl.BlockSpec((None, tm, in_w), lambda bi, ti: (bi, ti, 0))]
    out_shape = [jax.ShapeDtypeStruct((b, t, in_w), BF16)]
    aliases = {}
    cache_cfg = None
    if cache is not None:
        layer, depth, bsz, seq, prev = cache
        assert b == 1 and tm % seq == 0
        nb = tm // seq
        cache_cfg = (len(prev), nb, seq)
        for k, arr in enumerate(prev):
            aliases[len(args)] = 1 + k
            in_specs.append(pl.BlockSpec(memory_space=pl.ANY))
            args.append(arr)
        widths = (GA_KV, GA_KV, NA_W, NA_W)
        out_specs += [pl.BlockSpec((nb, None, seq, w), lambda bi, ti: (ti, layer, 0, 0)) for w in widths]
        out_shape += [jax.ShapeDtypeStruct((bsz, depth, seq, w), F32) for w in widths]
    outs = pl.pallas_call(
        functools.partial(_proj_kernel, rope=rope, cache=cache_cfg),
        grid=(b, t // tm),
        in_specs=in_specs,
        out_specs=out_specs,
        out_shape=out_shape,
        input_output_aliases=aliases,
        scratch_shapes=[pltpu.VMEM((tm, d), BF16)],
        compiler_params=_cparams(("arbitrary", "arbitrary")),
        name="in_proj_rope" if rope else "in_proj_cache",
    )(*args)
    return outs


def _rope_tables(t):
    half = HEAD_DIM // 4
    tt = jnp.arange(t, dtype=jnp.int32)
    row = (tt // GRID_W).astype(F32)
    col = (tt % GRID_W).astype(F32)
    inv_freq = 1.0 / (ROPE_THETA ** (jnp.arange(half, dtype=F32) / half))
    ar = row[:, None] * inv_freq[None, :]
    ac = col[:, None] * inv_freq[None, :]
    cos = jnp.concatenate([jnp.cos(ar), jnp.cos(ar), jnp.cos(ac), jnp.cos(ac)], axis=-1)
    sin = jnp.concatenate([-jnp.sin(ar), jnp.sin(ar), -jnp.sin(ac), jnp.sin(ac)], axis=-1)
    return cos, sin


_NT = (((1,), (1,)), ((), ()))


def _softmax_pv(scores, values):
    m = scores[0].max(axis=-1, keepdims=True)
    for s in scores[1:]:
        m = jnp.maximum(m, s.max(axis=-1, keepdims=True))
    l = None
    acc = None
    for s, v in zip(scores, values):
        p = jnp.exp2(s - m)
        ps = p.sum(axis=-1, keepdims=True)
        pv = jnp.dot(p.astype(BF16), v, preferred_element_type=F32)
        l = ps if l is None else l + ps
        acc = pv if acc is None else acc + pv
    return acc * (1.0 / l)


def _dense_attn_kernel(*refs, r_heads, gs, has_ctx):
    if has_ctx:
        q_ref, k_ref, v_ref, kc_ref, vc_ref, o_ref = refs
    else:
        q_ref, k_ref, v_ref, o_ref = refs
    for gi in range(gs):
        kv = slice(gi * HEAD_DIM, (gi + 1) * HEAD_DIM)
        k = k_ref[:, kv]
        v = v_ref[:, kv]
        if has_ctx:
            kc = kc_ref[:, kv].astype(BF16)
            vc = vc_ref[:, kv].astype(BF16)
        for r in range(r_heads):
            h = gi * r_heads + r
            sl = slice(h * HEAD_DIM, (h + 1) * HEAD_DIM)
            q = q_ref[:, sl]
            scores = [lax.dot_general(q, k, _NT, preferred_element_type=F32)]
            values = [v]
            if has_ctx:
                scores.append(lax.dot_general(q, kc, _NT, preferred_element_type=F32))
                values.append(vc)
            o_ref[:, sl] = _softmax_pv(scores, values).astype(BF16)


def _dense_attention(proj, q_col, k_col, v_col, groups, r_heads, gs=1, ctx=None):
    b, t, _ = proj.shape
    tq = min(ATTN_TQ, t)
    qw = gs * r_heads * HEAD_DIM
    kw = gs * HEAD_DIM
    assert q_col % (gs * r_heads) == 0 and k_col % gs == 0 and v_col % gs == 0 and groups % gs == 0
    in_specs = [
        pl.BlockSpec((None, tq, qw), lambda bi, g, qi: (bi, qi, q_col // (gs * r_heads) + g)),
        pl.BlockSpec((None, t, kw), lambda bi, g, qi: (bi, 0, k_col // gs + g)),
        pl.BlockSpec((None, t, kw), lambda bi, g, qi: (bi, 0, v_col // gs + g)),
    ]
    args = [proj, proj, proj]
    if ctx is not None:
        ck, cv, layer = ctx
        l_ctx = ck.shape[2]
        spec = pl.BlockSpec((None, None, l_ctx, kw), lambda bi, g, qi: (bi, layer, 0, g))
        in_specs += [spec, spec]
        args += [ck, cv]
    return pl.pallas_call(
        functools.partial(_dense_attn_kernel, r_heads=r_heads, gs=gs, has_ctx=ctx is not None),
        grid=(b, groups // gs, t // tq),
        in_specs=in_specs,
        out_specs=pl.BlockSpec((None, tq, qw), lambda bi, g, qi: (bi, qi, g)),
        out_shape=jax.ShapeDtypeStruct((b, t, groups * r_heads * HEAD_DIM), BF16),
        compiler_params=_cparams(("arbitrary", "arbitrary", "arbitrary")),
        name="dense_attn_ctx" if ctx is not None else "dense_attn",
    )(*args)


def _na_plan(rows):
    kh = min(NA_WIN_H, rows)
    r_blk = min(NA_ROWS, rows)
    assert rows % r_blk == 0
    slab = min(r_blk - 1 + kh, rows)
    row_start = np.clip(np.arange(rows) - kh // 2, 0, rows - kh)
    slab_start, pat_id, pats, sigs = [], [], [], {}
    for r0 in range(0, rows, r_blk):
        ss = min(row_start[r0], rows - slab)
        rel = tuple(int(row_start[r0 + ri] - ss) for ri in range(r_blk))
        sig = (int(ss - r0), rel)
        if sig not in sigs:
            sigs[sig] = len(pats)
            q_row = r0 + np.arange(r_blk)[:, None, None, None]
            k_row = ss + np.arange(slab)[None, None, :, None]
            rs = row_start[r0:r0 + r_blk][:, None, None, None]
            row_valid = ((k_row >= rs) & (k_row < rs + kh))[:, 0, :, 0]
            row_off = (k_row - q_row + (NA_WIN_H - 1))[:, 0, :, 0]
            row_sel = (row_off[:, :, None] == np.arange(2 * NA_WIN_H - 1)).astype(np.float32)
            pats.append((row_valid, row_sel))
        slab_start.append(int(ss))
        pat_id.append(sigs[sig])
    return r_blk, slab, np.array(slab_start, np.int32), np.array(pat_id, np.int32), pats


def _na_bias_tables(rel_bias, pats):
    col = np.arange(GRID_W)
    col_start = np.clip(col - NA_WIN_W // 2, 0, GRID_W - NA_WIN_W)
    col_off = col[None, :] - col[:, None] + (NA_WIN_W - 1)
    col_sel = (col_off[:, :, None] == np.arange(2 * NA_WIN_W - 1)).astype(np.float32)
    col_valid = (col[None, :] >= col_start[:, None]) & (col[None, :] < col_start[:, None] + NA_WIN_W)
    col_aug = np.concatenate([col_sel, np.ones((GRID_W, GRID_W, 1), np.float32),
                              np.where(col_valid, 0.0, MASK_VALUE)[..., None].astype(np.float32)], axis=-1)
    row_sel = np.stack([p[1] for p in pats])
    row_pen = np.stack([np.where(p[0], 0.0, MASK_VALUE) for p in pats]).astype(np.float32)
    hi = lax.Precision.HIGHEST
    by_row = jnp.einsum('lhrc,pisr->lphisc', rel_bias.astype(F32) * LOG2E, row_sel, precision=hi)
    lead = by_row.shape[:5]
    row_aug = jnp.concatenate([by_row, jnp.broadcast_to(row_pen[None, :, None, :, :, None], lead + (1,)),
                               jnp.ones(lead + (1,), F32)], axis=-1)
    bias = jnp.einsum('lphisc,qkc->lphiqsk', row_aug, col_aug, precision=hi)
    l, p, h, r, w, sl, _ = bias.shape
    return bias.reshape(l, p, h, r * w, sl * w)


def _na_kernel(ss_ref, pat_ref, q_ref, k_ref, v_ref, kc_ref, vc_ref, *rest, slab_len, n_heads, n_sub, qn):
    del pat_ref
    bias_refs, o_ref = rest[:n_sub], rest[n_sub]
    rb = pl.program_id(2)
    for h in range(n_heads):
        sl = slice(h * HEAD_DIM, (h + 1) * HEAD_DIM)
        kc = kc_ref[:, sl].astype(BF16)
        vc = vc_ref[:, sl].astype(BF16)
        for u in range(n_sub):
            start = pl.multiple_of(ss_ref[rb * n_sub + u] * GRID_W, GRID_W)
            rows = slice(u * qn, (u + 1) * qn)
            ks = k_ref[pl.ds(start, slab_len), sl]
            vs = v_ref[pl.ds(start, slab_len), sl]
            q = q_ref[rows, sl]
            s_loc = lax.dot_general(q, ks, _NT, preferred_element_type=F32) + bias_refs[u][h]
            s_ctx = lax.dot_general(q, kc, _NT, preferred_element_type=F32)
            o_ref[rows, sl] = _softmax_pv([s_loc, s_ctx], [vs, vc]).astype(BF16)


def _neighborhood_attention(proj, ck, cv, layer, bias_tab, plan):
    b, t, _ = proj.shape
    r_blk, slab, slab_start, pat_id, _ = plan
    qn = r_blk * GRID_W
    sn = slab * GRID_W
    l_ctx = ck.shape[2]
    hb = NA_HEADS_PER_STEP
    hw = hb * HEAD_DIM
    assert COL_QN % hb == 0 and COL_KN % hb == 0 and COL_VN % hb == 0 and N_HEADS_NA % hb == 0
    n_rb = t // qn
    n_sub = NA_BLOCKS_PER_STEP if n_rb % NA_BLOCKS_PER_STEP == 0 else 1

    def bias_spec(u):
        return pl.BlockSpec((None, None, hb, qn, sn),
                            lambda bi, hg, rb, ss, pt: (layer, pt[rb * n_sub + u], hg, 0, 0))

    grid_spec = pltpu.PrefetchScalarGridSpec(
        num_scalar_prefetch=2,
        grid=(b, N_HEADS_NA // hb, n_rb // n_sub),
        in_specs=[
            pl.BlockSpec((None, n_sub * qn, hw), lambda bi, hg, rb, ss, pt: (bi, rb, COL_QN // hb + hg)),
            pl.BlockSpec((None, t, hw), lambda bi, hg, rb, ss, pt: (bi, 0, COL_KN // hb + hg)),
            pl.BlockSpec((None, t, hw), lambda bi, hg, rb, ss, pt: (bi, 0, COL_VN // hb + hg)),
            pl.BlockSpec((None, None, l_ctx, hw), lambda bi, hg, rb, ss, pt: (bi, layer, 0, hg)),
            pl.BlockSpec((None, None, l_ctx, hw), lambda bi, hg, rb, ss, pt: (bi, layer, 0, hg)),
        ] + [bias_spec(u) for u in range(n_sub)],
        out_specs=pl.BlockSpec((None, n_sub * qn, hw), lambda bi, hg, rb, ss, pt: (bi, rb, hg)),
    )
    return pl.pallas_call(
        functools.partial(_na_kernel, slab_len=sn, n_heads=hb, n_sub=n_sub, qn=qn),
        grid_spec=grid_spec,
        out_shape=jax.ShapeDtypeStruct((b, t, NA_W), BF16),
        compiler_params=_cparams(("arbitrary", "arbitrary", "arbitrary")),
        name="neighborhood_attn",
    )(jnp.asarray(slab_start), jnp.asarray(pat_id), proj, proj, proj, ck, cv, *([bias_tab] * n_sub))


def _layer_norm(y, g, b):
    mu = jnp.mean(y, axis=-1, keepdims=True)
    yc = y - mu
    var = jnp.mean(yc * yc, axis=-1, keepdims=True)
    return yc * lax.rsqrt(var + EPS) * g + b


def _top2_of4(vals):
    m1 = jnp.maximum(jnp.maximum(vals[0], vals[1]), jnp.maximum(vals[2], vals[3]))
    i1 = jnp.where(vals[0] == m1, 0, jnp.where(vals[1] == m1, 1, jnp.where(vals[2] == m1, 2, 3)))
    rest = [jnp.where(i1 == i, -1.0, vals[i]) for i in range(4)]
    m2 = jnp.maximum(jnp.maximum(rest[0], rest[1]), jnp.maximum(rest[2], rest[3]))
    i2 = jnp.where(rest[0] == m2, 0, jnp.where(rest[1] == m2, 1, jnp.where(rest[2] == m2, 2, 3)))
    return m1, i1, m2, i2


def _route_rows(logits_t):
    m = logits_t.max(axis=0, keepdims=True)
    e = jnp.exp(logits_t - m)
    probs = e / e.sum(axis=0, keepdims=True)
    rows = [probs[i:i + 1, :] for i in range(N_EXPERTS)]
    groups = [rows[g * EXPERTS_PER_GROUP:(g + 1) * EXPERTS_PER_GROUP] for g in range(N_GROUPS)]
    scores = []
    for g in range(N_GROUPS):
        m1, _, m2, _ = _top2_of4(groups[g])
        scores.append(m1 + m2)
    best = jnp.maximum(jnp.maximum(scores[0], scores[1]), jnp.maximum(scores[2], scores[3]))
    gi = jnp.where(scores[0] == best, 0, jnp.where(scores[1] == best, 1, jnp.where(scores[2] == best, 2, 3)))
    sel = [jnp.where(gi == 0, groups[0][i], jnp.where(gi == 1, groups[1][i],
                     jnp.where(gi == 2, groups[2][i], groups[3][i]))) for i in range(EXPERTS_PER_GROUP)]
    w1, l1, w2, l2 = _top2_of4(sel)
    wsum = w1 + w2
    return (gi * EXPERTS_PER_GROUP + l1, gi * EXPERTS_PER_GROUP + l2, w1 / wsum, w2 / wsum)


def _out_proj_kernel(oa_ref, on_ref, x_ref, mod_ref, wo_ref, g_ref, b_ref, wr_ref, br_ref, h2_in_ref,
                     x1_ref, h2_ref, e_ref, wt_ref):
    del h2_in_ref
    attn = jnp.dot(oa_ref[...], wo_ref[0:GA_Q, :], preferred_element_type=F32)
    attn = attn + jnp.dot(on_ref[...], wo_ref[GA_Q:GA_Q + NA_W, :], preferred_element_type=F32)
    gate1 = mod_ref[2:3, :]
    x1 = _layer_norm(DN_ALPHA * x_ref[...] + gate1 * attn, g_ref[...], b_ref[...])
    x1_ref[...] = x1
    h2 = x1 * (1.0 + mod_ref[4:5, :]) + mod_ref[3:4, :]
    h2_ref[...] = h2
    h_hi = h2.astype(BF16)
    h_lo = (h2 - h_hi.astype(F32)).astype(BF16)
    r_hi = jnp.dot(h_hi, wr_ref[...], preferred_element_type=F32)
    r_lo = jnp.dot(h_lo, wr_ref[...], preferred_element_type=F32)
    logits = r_hi + pltpu.roll(r_hi, LANE - N_EXPERTS, 1) + r_lo
    logits_t = logits.T[0:N_EXPERTS, :] + br_ref[...]
    e1, e2, w1, w2 = _route_rows(logits_t)
    e_ref[0:1, :] = e1
    e_ref[1:2, :] = e2
    n = w1.shape[1]
    row = lax.broadcasted_iota(jnp.int32, (LANE, n), 0)
    w_rows = jnp.where(row == 0, w1, jnp.where(row == 1, w2, 0.0))
    wt_ref[...] = w_rows.T


def _out_proj(oa, on, x, mod_l, mod_row0, w_o_bf, w_layer, ln_g, ln_b, wr_cat, b_router, h2_all, row0):
    b, t, d = x.shape
    tm = min(OUT_TM, t)
    nt = t // tm
    assert row0 % tm == 0
    tok = lambda bi, ti: (bi, ti, 0)
    const2 = lambda bi, ti: (0, 0)
    return pl.pallas_call(
        _out_proj_kernel,
        grid=(b, t // tm),
        in_specs=[
            pl.BlockSpec((None, tm, GA_Q), tok),
            pl.BlockSpec((None, tm, NA_W), tok),
            pl.BlockSpec((None, tm, d), tok),
            pl.BlockSpec((None, N_MOD, d), lambda bi, ti: (bi + mod_row0, 0, 0)),
            pl.BlockSpec((None, GA_Q + NA_W, d), lambda bi, ti: (w_layer, 0, 0)),
            pl.BlockSpec((1, d), const2),
            pl.BlockSpec((1, d), const2),
            pl.BlockSpec((d, LANE), const2),
            pl.BlockSpec((N_EXPERTS, 1), const2),
            pl.BlockSpec(memory_space=pl.ANY),
        ],
        out_specs=[
            pl.BlockSpec((None, tm, d), tok),
            pl.BlockSpec((tm, d), lambda bi, ti: (row0 // tm + bi * nt + ti, 0)),
            pl.BlockSpec((None, TOP_K, tm), lambda bi, ti: (bi, 0, ti)),
            pl.BlockSpec((None, tm, LANE), tok),
        ],
        out_shape=[
            jax.ShapeDtypeStruct((b, t, d), F32),
            jax.ShapeDtypeStruct(h2_all.shape, F32),
            jax.ShapeDtypeStruct((b, TOP_K, t), jnp.int32),
            jax.ShapeDtypeStruct((b, t, LANE), F32),
        ],
        input_output_aliases={9: 1},
        compiler_params=_cparams(("arbitrary", "arbitrary")),
        name="out_proj_ln_router",
    )(oa, on, x, mod_l, w_o_bf, ln_g.reshape(1, d), ln_b.reshape(1, d), wr_cat,
      b_router.reshape(N_EXPERTS, 1).astype(F32), h2_all)


def _row_copy(src_hbm, idx_ref, dst, sem, r, base=None):
    idx = idx_ref[0, r] if base is None else idx_ref[base + r]
    return pltpu.make_async_copy(src_hbm.at[pl.ds(idx, 1)], dst.at[pl.ds(r, 1)], sem)


def _start_row_gather(src_hbm, idx_ref, dst, sem, n_rows, base=None):
    def body(r, carry):
        _row_copy(src_hbm, idx_ref, dst, sem, r, base).start()
        return carry
    lax.fori_loop(0, n_rows, body, 0, unroll=8)


def _start_row_gather_inline(src_hbm, idx_ref, dst, sem, r_lo, r_hi, base=None):
    for r in range(r_lo, r_hi):
        _row_copy(src_hbm, idx_ref, dst, sem, r, base).start()


def _wait_row_gather(src_hbm, dst, sem, n_rows):
    pltpu.make_async_copy(src_hbm.at[pl.ds(0, n_rows)], dst, sem).wait()


def _moe_kernel(be_ref, nu_ref, tok_ref, h_hbm, wg_ref, wu_ref, wd_ref, o_ref,
                buf, sem, *, n_blocks):
    del be_ref
    i = pl.program_id(0)
    n_used = nu_ref[0]
    slot = i % GATHER_DEPTH
    ahead = (i + 2) % GATHER_DEPTH
    base2 = jnp.minimum(i + 2, n_blocks - 1) * MOE_BLK

    @pl.when(i == 0)
    def _():
        _start_row_gather(h_hbm, tok_ref, buf.at[0], sem.at[0], MOE_BLK, base=0)
        _start_row_gather(h_hbm, tok_ref, buf.at[1], sem.at[1], MOE_BLK, base=MOE_BLK)

    _wait_row_gather(h_hbm, buf.at[slot], sem.at[slot], MOE_BLK)

    @pl.when(i < n_used)
    def _():
        xb = buf[slot].astype(BF16)
        gate = jnp.dot(xb, wg_ref[...], preferred_element_type=F32)
        up = jnp.dot(xb, wu_ref[...], preferred_element_type=F32)
        act = (gate / (1.0 + jnp.exp(-gate)) * up).astype(BF16)
        _start_row_gather_inline(h_hbm, tok_ref, buf.at[ahead], sem.at[ahead], 0, MOE_BLK, base=base2)
        o_ref[...] = jnp.dot(act, wd_ref[...], preferred_element_type=F32)

    @pl.when(i >= n_used)
    def _():
        o_ref[...] = jnp.zeros_like(o_ref)
        _start_row_gather(h_hbm, tok_ref, buf.at[ahead], sem.at[ahead], MOE_BLK, base=base2)

    @pl.when(i == n_blocks - 1)
    def _():
        for k in (1, 2):
            s = (i + k) % GATHER_DEPTH
            _wait_row_gather(h_hbm, buf.at[s], sem.at[s], MOE_BLK)


def _moe_ffn(h2_flat, tok_buf, blk_e, n_used, wg, wu, wd, layer):
    n, d = h2_flat.shape
    n_blocks = blk_e.shape[0]
    d_ff = wg.shape[3]
    grid_spec = pltpu.PrefetchScalarGridSpec(
        num_scalar_prefetch=3,
        grid=(n_blocks,),
        in_specs=[
            pl.BlockSpec(memory_space=pl.ANY),
            pl.BlockSpec((None, None, d, d_ff), lambda i, be, nu, tk: (layer, be[i], 0, 0)),
            pl.BlockSpec((None, None, d, d_ff), lambda i, be, nu, tk: (layer, be[i], 0, 0)),
            pl.BlockSpec((None, None, d_ff, d), lambda i, be, nu, tk: (layer, be[i], 0, 0)),
        ],
        out_specs=pl.BlockSpec((MOE_BLK, d), lambda i, be, nu, tk: (i, 0)),
        scratch_shapes=[pltpu.VMEM((GATHER_DEPTH, MOE_BLK, d), F32), pltpu.SemaphoreType.DMA((GATHER_DEPTH,))],
    )
    return pl.pallas_call(
        functools.partial(_moe_kernel, n_blocks=n_blocks),
        grid_spec=grid_spec,
        out_shape=jax.ShapeDtypeStruct((n_blocks * MOE_BLK, d), F32),
        compiler_params=_cparams(("arbitrary",)),
        name="moe_ffn",
    )(blk_e, n_used, tok_buf, h2_flat, wg, wu, wd)


def _combine_kernel(dest_ref, y_hbm, x1_ref, mod_ref, wt_ref, g_ref, b_ref, o_ref, buf, sem, *, n_steps, tm):
    i = pl.program_id(0)
    slot = i % GATHER_DEPTH
    ahead = (i + 2) % GATHER_DEPTH
    step_base = lambda step: jnp.minimum(step, n_steps - 1) * (TOP_K * tm)

    @pl.when(i == 0)
    def _():
        for s in range(2):
            for k in range(TOP_K):
                _start_row_gather(y_hbm, dest_ref, buf.at[s, k], sem.at[s], tm, base=step_base(s) + k * tm)

    def wait(s):
        _wait_row_gather(y_hbm, buf.at[s, 0], sem.at[s], tm)
        _wait_row_gather(y_hbm, buf.at[s, 1], sem.at[s], tm)

    wait(slot)
    y = wt_ref[:, 0:1] * buf[slot, 0] + wt_ref[:, 1:2] * buf[slot, 1]
    gate2 = mod_ref[5:6, :]
    o_ref[...] = _layer_norm(DN_ALPHA * x1_ref[...] + gate2 * y, g_ref[...], b_ref[...])
    for k in range(TOP_K):
        _start_row_gather_inline(y_hbm, dest_ref, buf.at[ahead, k], sem.at[ahead], 0, tm,
                                 base=step_base(i + 2) + k * tm)

    @pl.when(i == n_steps - 1)
    def _():
        for k in (1, 2):
            wait((i + k) % GATHER_DEPTH)


def _combine(y_sorted, dest, x1, mod_l, mod_row0, wt, ln_g, ln_b):
    b, t, d = x1.shape
    tm = min(OUT_TM, t)
    nt = t // tm
    n_steps = b * nt
    dest_flat = dest.reshape(b, TOP_K, nt, tm).transpose(0, 2, 1, 3).reshape(n_steps * TOP_K * tm)
    tok = lambda i, ds: (i // nt, i % nt, 0)
    grid_spec = pltpu.PrefetchScalarGridSpec(
        num_scalar_prefetch=1,
        grid=(n_steps,),
        in_specs=[
            pl.BlockSpec(memory_space=pl.ANY),
            pl.BlockSpec((None, tm, d), tok),
            pl.BlockSpec((None, N_MOD, d), lambda i, ds: (i // nt + mod_row0, 0, 0)),
            pl.BlockSpec((None, tm, LANE), tok),
            pl.BlockSpec((1, d), lambda i, ds: (0, 0)),
            pl.BlockSpec((1, d), lambda i, ds: (0, 0)),
        ],
        out_specs=pl.BlockSpec((None, tm, d), tok),
        scratch_shapes=[pltpu.VMEM((GATHER_DEPTH, TOP_K, tm, d), F32), pltpu.SemaphoreType.DMA((GATHER_DEPTH,))],
    )
    return pl.pallas_call(
        functools.partial(_combine_kernel, n_steps=n_steps, tm=tm),
        grid_spec=grid_spec,
        out_shape=jax.ShapeDtypeStruct((b, t, d), F32),
        compiler_params=_cparams(("arbitrary",)),
        name="moe_combine_ln",
    )(dest_flat, y_sorted, x1, mod_l, wt, ln_g.reshape(1, d), ln_b.reshape(1, d))


def _dispatch_plan(e_idxs, row0s):
    flat_e = jnp.concatenate([e_idx.reshape(-1) for e_idx in e_idxs])
    a = flat_e.shape[0]
    onehot = (flat_e[:, None] == jnp.arange(N_EXPERTS, dtype=jnp.int32)[None, :]).astype(jnp.int32)
    counts = onehot.sum(axis=0)
    chunk = 256
    assert a % chunk == 0
    oh = onehot.reshape(a // chunk, chunk, N_EXPERTS).astype(F32)
    tri = jnp.tril(jnp.ones((chunk, chunk), F32))
    within = jnp.einsum('ij,cje->cie', tri, oh, precision=lax.Precision.HIGHEST)
    before = jnp.cumsum(oh.sum(axis=1), axis=0) - oh.sum(axis=1)
    running = (within + before[:, None, :]).reshape(a, N_EXPERTS).astype(jnp.int32)
    rank = (running * onehot).sum(axis=1) - 1
    padded = (counts + MOE_BLK - 1) // MOE_BLK * MOE_BLK
    pad_end = jnp.cumsum(padded)
    pad_start = pad_end - padded
    dest = pad_start[flat_e] + rank
    n_blocks = -(-a // MOE_BLK) + N_EXPERTS
    p = n_blocks * MOE_BLK
    blk_start = jnp.arange(n_blocks, dtype=jnp.int32) * MOE_BLK
    blk_e = jnp.minimum((blk_start[:, None] >= pad_end[None, :]).sum(axis=1), N_EXPERTS - 1).astype(jnp.int32)
    n_used = (pad_end[-1] // MOE_BLK).astype(jnp.int32)
    last_e = blk_e[jnp.maximum(n_used - 1, 0)]
    blk_e = jnp.where(jnp.arange(n_blocks) < n_used, blk_e, last_e)
    assert a < (1 << 16)
    order = lax.sort(flat_e * (1 << 16) + jnp.arange(a, dtype=jnp.int32)) & 0xFFFF
    start = jnp.cumsum(counts) - counts
    order = jnp.concatenate([order, jnp.zeros((MOE_BLK,), jnp.int32)])
    win = jnp.clip(start[blk_e] + blk_start - pad_start[blk_e], 0, a)
    order_blk = order[(win[:, None] + jnp.arange(MOE_BLK, dtype=jnp.int32)[None, :]).reshape(p)]
    tok_buf, o = jnp.zeros((p,), jnp.int32), 0
    for e_idx, row0 in zip(e_idxs, row0s):
        b, k, t = e_idx.shape
        local = order_blk - o
        rows = row0 + (local // (k * t)) * t + local % t
        tok_buf = jnp.where((local >= 0) & (local < e_idx.size), rows, tok_buf)
        o += e_idx.size
    dests, o = [], 0
    for e_idx in e_idxs:
        dests.append(dest[o:o + e_idx.size].reshape(e_idx.shape))
        o += e_idx.size
    return dests, tok_buf, blk_e, n_used.reshape(1)


def _attention_sublayer(x, mod_l, mod_row0, lw, ctx):
    b, t, d = x.shape
    layer = lw['layer']
    if ctx is None:
        xt = x.reshape(1, b * t, d)
        proj, *cache = _in_proj(xt, mod_l, mod_row0, lw['w_in'], layer, lw['q_gain'], lw['k_gain'],
                                cache=(layer, lw['depth'], b, t, lw['prev_cache']))
        proj = proj.reshape(b, t, -1)
        oa = _dense_attention(proj, COL_QA, COL_KA, COL_VA, N_KV_GA, N_HEADS_GA // N_KV_GA, gs=GA_KV_PER_STEP)
        on = _dense_attention(proj, COL_QN, COL_KN, COL_VN, N_HEADS_NA, 1, gs=NA_HEADS_PER_STEP)
        return oa.reshape(1, b * t, -1), on.reshape(1, b * t, -1), xt, cache
    ga_k, ga_v, na_k, na_v, rope_tabs, na_bias, na_plan = ctx
    (proj,) = _in_proj(x, mod_l, mod_row0, lw['w_in'], layer, lw['q_gain'], lw['k_gain'], rope_tabs=rope_tabs)
    oa = _dense_attention(proj, COL_QA, COL_KA, COL_VA, N_KV_GA, N_HEADS_GA // N_KV_GA, gs=GA_KV_PER_STEP,
                          ctx=(ga_k, ga_v, layer))
    on = _neighborhood_attention(proj, na_k, na_v, layer, na_bias, na_plan)
    return oa, on, x, None


def _layer(x_ctx, x_lat, mod_l, lw, shared, lat_ctx, h2_all):
    bc, tc, d = x_ctx.shape
    bl, tl, _ = x_lat.shape
    n_ctx, n_lat = bc * tc, bl * tl
    streams = [_attention_sublayer(x_ctx, mod_l, bl, lw, None),
               _attention_sublayer(x_lat, mod_l, 0, lw, lat_ctx)]
    cache = streams[0][3]
    mod_rows, row0s = (bl, 0), (0, n_ctx)
    x1s, e_idxs, wts = [], [], []
    for (oa, on, xt, _), mod_row0, row0 in zip(streams, mod_rows, row0s):
        x1, h2_all, e_idx, wt = _out_proj(oa, on, xt, mod_l, mod_row0, lw['w_o'], lw['layer'], lw['ln1_g'],
                                          lw['ln1_b'], shared['wr_cat'], shared['b_router'], h2_all, row0)
        x1s.append(x1)
        e_idxs.append(e_idx)
        wts.append(wt)
    dests, tok_buf, blk_e, n_used = _dispatch_plan(e_idxs, row0s)
    y_sorted = _moe_ffn(h2_all, tok_buf, blk_e, n_used, lw['w_gate'], lw['w_up'], lw['w_down'], lw['layer'])
    outs = [_combine(y_sorted, dest, x1, mod_l, mod_row0, wt, lw['ln2_g'], lw['ln2_b'])
            for dest, x1, wt, mod_row0 in zip(dests, x1s, wts, mod_rows)]
    return outs[0].reshape(bc, tc, d), outs[1].reshape(bl, tl, d), cache, h2_all


def kernel(x_prompt, x_sample, c, cache_ga_k, cache_ga_v, cache_na_k, cache_na_v, c_ctx, w_router, b_router, w_mod, b_mod, w_in, q_norm, k_norm, rel_bias, w_o, ln1_g, ln1_b, ln2_g, ln2_b, w_gate, w_up, w_down):
    bsz, seq, d = x_prompt.shape
    dec_b, dec_t, _ = x_sample.shape
    depth = w_mod.shape[0]
    past = cache_ga_k.shape[2]

    mod_rows = 16
    assert dec_b + 1 <= mod_rows
    cvec = jnp.concatenate([c, c_ctx[None, :], jnp.zeros((mod_rows - dec_b - 1, d), F32)], axis=0)
    mod = _modulation(cvec, w_mod, b_mod).reshape(depth, mod_rows, N_MOD, d)

    wr_hi = w_router.astype(BF16)
    wr_lo = (w_router - wr_hi.astype(F32)).astype(BF16)
    wr_cat = jnp.concatenate([wr_hi, wr_lo, jnp.zeros((d, LANE - 2 * N_EXPERTS), BF16)], axis=1)
    shared = {'wr_cat': wr_cat, 'b_router': b_router}

    rope_tabs = _rope_tables(dec_t)
    na_plan = _na_plan(dec_t // GRID_W)
    na_bias = _na_bias_tables(rel_bias, na_plan[4])
    ga_k = cache_ga_k.reshape(dec_b, depth, past, GA_KV)
    ga_v = cache_ga_v.reshape(dec_b, depth, past, GA_KV)
    na_k = cache_na_k.reshape(dec_b, depth, past, NA_W)
    na_v = cache_na_v.reshape(dec_b, depth, past, NA_W)

    w_in_bf, w_o_bf = w_in.astype(BF16), w_o.astype(BF16)
    w_gate_bf, w_up_bf, w_down_bf = w_gate.astype(BF16), w_up.astype(BF16), w_down.astype(BF16)

    y_prompt, y_sample = x_prompt, x_sample
    cache = [jnp.zeros((bsz, depth, seq, w), F32) for w in (GA_KV, GA_KV, NA_W, NA_W)]
    h2_all = jnp.zeros((bsz * seq + dec_b * dec_t, d), F32)
    for i in range(depth):
        lw = {
            'layer': i, 'depth': depth, 'prev_cache': cache,
            'w_in': w_in_bf, 'w_o': w_o_bf,
            'q_gain': (q_norm[i] * Q_PRESCALE).reshape(1, HEAD_DIM).astype(F32),
            'k_gain': k_norm[i].reshape(1, HEAD_DIM).astype(F32),
            'ln1_g': ln1_g[i], 'ln1_b': ln1_b[i], 'ln2_g': ln2_g[i], 'ln2_b': ln2_b[i],
            'w_gate': w_gate_bf, 'w_up': w_up_bf, 'w_down': w_down_bf,
        }
        y_prompt, y_sample, cache, h2_all = _layer(y_prompt, y_sample, mod[i], lw, shared,
                                                   (ga_k, ga_v, na_k, na_v, rope_tabs, na_bias, na_plan), h2_all)

    outs = [arr.reshape(bsz, depth, seq, -1, HEAD_DIM) for arr in cache]
    return (y_prompt, y_sample, outs[0], outs[1], outs[2], outs[3])
```

```python
import functools
import math

import numpy as np
import jax
import jax.numpy as jnp
from jax import lax
from jax.experimental import pallas as pl
from jax.experimental.pallas import tpu as pltpu

F32 = jnp.float32
BF16 = jnp.bfloat16

DEPTH = 2
GRID_W = 64
HEAD_DIM = 128
N_HEADS_GA = 8
N_KV_GA = 2
N_HEADS_NA = 8
NA_WIN_H = 8
NA_WIN_W = 16
ROPE_THETA = 10000.0
N_EXPERTS = 16
N_GROUPS = 4
EXPERTS_PER_GROUP = N_EXPERTS // N_GROUPS
TOP_K = 2
N_MOD = 6
DN_ALPHA = (2.0 * DEPTH) ** 0.25
EPS = 1e-6
ATTN_SCALE = HEAD_DIM ** -0.5
LOG2E = math.log2(math.e)
Q_PRESCALE = ATTN_SCALE * LOG2E
MASK_VALUE = -1e30

GA_Q = N_HEADS_GA * HEAD_DIM
GA_KV = N_KV_GA * HEAD_DIM
NA_W = N_HEADS_NA * HEAD_DIM
COL_QA = 0
COL_KA = COL_QA + N_HEADS_GA
COL_VA = COL_KA + N_KV_GA
COL_QN = COL_VA + N_KV_GA
COL_KN = COL_QN + N_HEADS_NA
COL_VN = COL_KN + N_HEADS_NA
IN_HEADS = COL_VN + N_HEADS_NA

LANE = 128
PROJ_TN = 4 * HEAD_DIM
PROJ_TM = 512
PROJ_TM_CACHE = 256
ATTN_TQ = 512
NA_ROWS = 4
NA_HEADS_PER_STEP = 4
NA_BLOCKS_PER_STEP = 2
GA_KV_PER_STEP = 2
OUT_TM = 512
COMBINE_TM = 256
MOE_BLK = 256
GATHER_DEPTH = 3
MOD_TN = 1024
VMEM_LIMIT = 52 * 1024 * 1024


def _cparams(sem):
    return pltpu.CompilerParams(dimension_semantics=sem, vmem_limit_bytes=VMEM_LIMIT)


def _mod_kernel(c_ref, w_ref, b_ref, o_ref):
    c = c_ref[...]
    s = c / (1.0 + jnp.exp(-c))
    o_ref[...] = jnp.dot(s.astype(BF16), w_ref[...].astype(BF16),
                         preferred_element_type=F32) + b_ref[...]


def _modulation(cvec, w_mod, b_mod):
    depth, d, e = w_mod.shape
    rows = cvec.shape[0]
    tn = MOD_TN if e % MOD_TN == 0 else e
    return pl.pallas_call(
        _mod_kernel,
        grid=(depth, e // tn),
        in_specs=[
            pl.BlockSpec((rows, d), lambda l, j: (0, 0)),
            pl.BlockSpec((None, d, tn), lambda l, j: (l, 0, j)),
            pl.BlockSpec((None, 1, tn), lambda l, j: (l, 0, j)),
        ],
        out_specs=pl.BlockSpec((None, rows, tn), lambda l, j: (l, 0, j)),
        out_shape=jax.ShapeDtypeStruct((depth, rows, e), F32),
        compiler_params=_cparams(("arbitrary", "arbitrary")),
        name="modulation",
    )(cvec, w_mod, b_mod.reshape(depth, 1, e))


def _rms_head(x, g):
    ms = jnp.mean(x * x, axis=-1, keepdims=True)
    return x * lax.rsqrt(ms + EPS) * g


def _rope_head(x, cos, sin, first_half):
    swapped = jnp.where(first_half, pltpu.roll(x, HEAD_DIM - 32, 1), pltpu.roll(x, 32, 1))
    return x * cos + swapped * sin


def _proj_kernel(*refs, rope, cache):
    x_ref, mod_ref, w_ref, qg_ref, kg_ref = refs[:5]
    pos = 5
    if rope:
        cos_ref, sin_ref = refs[pos:pos + 2]
        pos += 2
    if cache is not None:
        pos += cache[0]
    o_ref = refs[pos]
    pos += 1
    if cache is not None:
        ka_ref, va_ref, kn_ref, vn_ref = refs[pos:pos + 4]
        pos += 4
    h_scr = refs[pos]

    shift = mod_ref[0:1, :]
    scale = mod_ref[1:2, :]
    h_scr[...] = (x_ref[...] * (1.0 + scale) + shift).astype(BF16)

    if rope:
        lane = lax.broadcasted_iota(jnp.int32, (1, HEAD_DIM), 1)
        first_half = (lane % 64) < 32

    def normed(y, g):
        y = _rms_head(y, g)
        if rope:
            y = _rope_head(y, cos_ref[...], sin_ref[...], first_half)
        return y

    def to_cache(c_ref, c_head, y):
        if cache is not None:
            c_ref[:, :, c_head * HEAD_DIM:(c_head + 1) * HEAD_DIM] = y.reshape(cache[1], cache[2], HEAD_DIM)

    hpt = PROJ_TN // HEAD_DIM
    for jt in range(IN_HEADS // hpt):
        acc = jnp.dot(h_scr[...], w_ref[:, jt * PROJ_TN:(jt + 1) * PROJ_TN], preferred_element_type=F32)
        for hh in range(hpt):
            head = jt * hpt + hh
            y = acc[:, hh * HEAD_DIM:(hh + 1) * HEAD_DIM]
            if head < COL_KA:
                y = normed(y, qg_ref[...])
            elif head < COL_VA:
                y = normed(y, kg_ref[...])
                to_cache(ka_ref if cache else None, head - COL_KA, y)
            elif head < COL_QN:
                to_cache(va_ref if cache else None, head - COL_VA, y)
            elif head < COL_KN:
                y = y * Q_PRESCALE
            elif head < COL_VN:
                to_cache(kn_ref if cache else None, head - COL_KN, y)
            else:
                to_cache(vn_ref if cache else None, head - COL_VN, y)
            o_ref[:, head * HEAD_DIM:(head + 1) * HEAD_DIM] = y.astype(BF16)


def _in_proj(x, mod_l, mod_row0, w_in_bf, w_layer, q_gain, k_gain, rope_tabs=None, cache=None):
    b, t, d = x.shape
    in_w = w_in_bf.shape[2]
    rope = rope_tabs is not None
    tm = min(PROJ_TM if rope else PROJ_TM_CACHE, t)
    assert in_w == IN_HEADS * HEAD_DIM and in_w % PROJ_TN == 0

    in_specs = [
        pl.BlockSpec((None, tm, d), lambda bi, ti: (bi, ti, 0)),
        pl.BlockSpec((None, N_MOD, d), lambda bi, ti: (bi + mod_row0, 0, 0)),
        pl.BlockSpec((None, d, in_w), lambda bi, ti: (w_layer, 0, 0), pipeline_mode=pl.Buffered(1)),
        pl.BlockSpec((1, HEAD_DIM), lambda bi, ti: (0, 0)),
        pl.BlockSpec((1, HEAD_DIM), lambda bi, ti: (0, 0)),
    ]
    args = [x, mod_l, w_in_bf, q_gain, k_gain]
    if rope:
        in_specs += [pl.BlockSpec((tm, HEAD_DIM), lambda bi, ti: (ti, 0))] * 2
        args += list(rope_tabs)
    out_specs = [pl.BlockSpec((None, tm, in_w), lambda bi, ti: (bi, ti, 0))]
    out_shape = [jax.ShapeDtypeStruct((b, t, in_w), BF16)]
    aliases = {}
    cache_cfg = None
    if cache is not None:
        layer, depth, bsz, seq, prev = cache
        assert b == 1 and tm % seq == 0
        nb = tm // seq
        cache_cfg = (len(prev), nb, seq)
        for k, arr in enumerate(prev):
            aliases[len(args)] = 1 + k
            in_specs.append(pl.BlockSpec(memory_space=pl.ANY))
            args.append(arr)
        widths = (GA_KV, GA_KV, NA_W, NA_W)
        out_specs += [pl.BlockSpec((nb, None, seq, w), lambda bi, ti: (ti, layer, 0, 0)) for w in widths]
        out_shape += [jax.ShapeDtypeStruct((bsz, depth, seq, w), F32) for w in widths]
    outs = pl.pallas_call(
        functools.partial(_proj_kernel, rope=rope, cache=cache_cfg),
        grid=(b, t // tm),
        in_specs=in_specs,
        out_specs=out_specs,
        out_shape=out_shape,
        input_output_aliases=aliases,
        scratch_shapes=[pltpu.VMEM((tm, d), BF16)],
        compiler_params=_cparams(("arbitrary", "arbitrary")),
        name="in_proj_rope" if rope else "in_proj_cache",
    )(*args)
    return outs


def _rope_tables(t):
    half = HEAD_DIM // 4
    tt = jnp.arange(t, dtype=jnp.int32)
    row = (tt // GRID_W).astype(F32)
    col = (tt % GRID_W).astype(F32)
    inv_freq = 1.0 / (ROPE_THETA ** (jnp.arange(half, dtype=F32) / half))
    ar = row[:, None] * inv_freq[None, :]
    ac = col[:, None] * inv_freq[None, :]
    cos = jnp.concatenate([jnp.cos(ar), jnp.cos(ar), jnp.cos(ac), jnp.cos(ac)], axis=-1)
    sin = jnp.concatenate([-jnp.sin(ar), jnp.sin(ar), -jnp.sin(ac), jnp.sin(ac)], axis=-1)
    return cos, sin


_NT = (((1,), (1,)), ((), ()))


def _softmax_pv(scores, values):
    m = scores[0].max(axis=-1, keepdims=True)
    for s in scores[1:]:
        m = jnp.maximum(m, s.max(axis=-1, keepdims=True))
    l = None
    acc = None
    for s, v in zip(scores, values):
        p = jnp.exp2(s - m)
        ps = p.sum(axis=-1, keepdims=True)
        pv = jnp.dot(p.astype(BF16), v, preferred_element_type=F32)
        l = ps if l is None else l + ps
        acc = pv if acc is None else acc + pv
    return acc * (1.0 / l)


def _dense_attn_kernel(*refs, r_heads, gs, has_ctx):
    if has_ctx:
        q_ref, k_ref, v_ref, kc_ref, vc_ref, o_ref = refs
    else:
        q_ref, k_ref, v_ref, o_ref = refs
    for gi in range(gs):
        kv = slice(gi * HEAD_DIM, (gi + 1) * HEAD_DIM)
        k = k_ref[:, kv]
        v = v_ref[:, kv]
        if has_ctx:
            kc = kc_ref[:, kv].astype(BF16)
            vc = vc_ref[:, kv].astype(BF16)
        for r in range(r_heads):
            h = gi * r_heads + r
            sl = slice(h * HEAD_DIM, (h + 1) * HEAD_DIM)
            q = q_ref[:, sl]
            scores = [lax.dot_general(q, k, _NT, preferred_element_type=F32)]
            values = [v]
            if has_ctx:
                scores.append(lax.dot_general(q, kc, _NT, preferred_element_type=F32))
                values.append(vc)
            o_ref[:, sl] = _softmax_pv(scores, values).astype(BF16)


def _dense_attention(proj, q_col, k_col, v_col, groups, r_heads, gs=1, ctx=None):
    b, t, _ = proj.shape
    tq = min(ATTN_TQ, t)
    qw = gs * r_heads * HEAD_DIM
    kw = gs * HEAD_DIM
    assert q_col % (gs * r_heads) == 0 and k_col % gs == 0 and v_col % gs == 0 and groups % gs == 0
    in_specs = [
        pl.BlockSpec((None, tq, qw), lambda bi, g, qi: (bi, qi, q_col // (gs * r_heads) + g)),
        pl.BlockSpec((None, t, kw), lambda bi, g, qi: (bi, 0, k_col // gs + g)),
        pl.BlockSpec((None, t, kw), lambda bi, g, qi: (bi, 0, v_col // gs + g)),
    ]
    args = [proj, proj, proj]
    if ctx is not None:
        ck, cv, layer = ctx
        l_ctx = ck.shape[2]
        spec = pl.BlockSpec((None, None, l_ctx, kw), lambda bi, g, qi: (bi, layer, 0, g))
        in_specs += [spec, spec]
        args += [ck, cv]
    return pl.pallas_call(
        functools.partial(_dense_attn_kernel, r_heads=r_heads, gs=gs, has_ctx=ctx is not None),
        grid=(b, groups // gs, t // tq),
        in_specs=in_specs,
        out_specs=pl.BlockSpec((None, tq, qw), lambda bi, g, qi: (bi, qi, g)),
        out_shape=jax.ShapeDtypeStruct((b, t, groups * r_heads * HEAD_DIM), BF16),
        compiler_params=_cparams(("arbitrary", "arbitrary", "arbitrary")),
        name="dense_attn_ctx" if ctx is not None else "dense_attn",
    )(*args)


def _na_plan(rows):
    kh = min(NA_WIN_H, rows)
    r_blk = min(NA_ROWS, rows)
    assert rows % r_blk == 0
    slab = min(r_blk - 1 + kh, rows)
    row_start = np.clip(np.arange(rows) - kh // 2, 0, rows - kh)
    slab_start, pat_id, pats, sigs = [], [], [], {}
    for r0 in range(0, rows, r_blk):
        ss = min(row_start[r0], rows - slab)
        rel = tuple(int(row_start[r0 + ri] - ss) for ri in range(r_blk))
        sig = (int(ss - r0), rel)
        if sig not in sigs:
            sigs[sig] = len(pats)
            q_row = r0 + np.arange(r_blk)[:, None, None, None]
            k_row = ss + np.arange(slab)[None, None, :, None]
            rs = row_start[r0:r0 + r_blk][:, None, None, None]
            row_valid = ((k_row >= rs) & (k_row < rs + kh))[:, 0, :, 0]
            row_off = (k_row - q_row + (NA_WIN_H - 1))[:, 0, :, 0]
            row_sel = (row_off[:, :, None] == np.arange(2 * NA_WIN_H - 1)).astype(np.float32)
            pats.append((row_valid, row_sel))
        slab_start.append(int(ss))
        pat_id.append(sigs[sig])
    return r_blk, slab, np.array(slab_start, np.int32), np.array(pat_id, np.int32), pats


def _na_bias_tables(rel_bias, pats):
    col = np.arange(GRID_W)
    col_start = np.clip(col - NA_WIN_W // 2, 0, GRID_W - NA_WIN_W)
    col_off = col[None, :] - col[:, None] + (NA_WIN_W - 1)
    col_sel = (col_off[:, :, None] == np.arange(2 * NA_WIN_W - 1)).astype(np.float32)
    col_valid = (col[None, :] >= col_start[:, None]) & (col[None, :] < col_start[:, None] + NA_WIN_W)
    col_aug = np.concatenate([col_sel, np.ones((GRID_W, GRID_W, 1), np.float32),
                              np.where(col_valid, 0.0, MASK_VALUE)[..., None].astype(np.float32)], axis=-1)
    row_sel = np.stack([p[1] for p in pats])
    row_pen = np.stack([np.where(p[0], 0.0, MASK_VALUE) for p in pats]).astype(np.float32)
    hi = lax.Precision.HIGHEST
    by_row = jnp.einsum('lhrc,pisr->lphisc', rel_bias.astype(F32) * LOG2E, row_sel, precision=hi)
    lead = by_row.shape[:5]
    row_aug = jnp.concatenate([by_row, jnp.broadcast_to(row_pen[None, :, None, :, :, None], lead + (1,)),
                               jnp.ones(lead + (1,), F32)], axis=-1)
    bias = jnp.einsum('lphisc,qkc->lphiqsk', row_aug, col_aug, precision=hi)
    l, p, h, r, w, sl, _ = bias.shape
    return bias.reshape(l, p, h, r * w, sl * w)


def _na_kernel(ss_ref, pat_ref, q_ref, k_ref, v_ref, kc_ref, vc_ref, *rest, slab_len, n_heads, n_sub, qn):
    del pat_ref
    bias_refs, o_ref = rest[:n_sub], rest[n_sub]
    rb = pl.program_id(2)
    for h in range(n_heads):
        sl = slice(h * HEAD_DIM, (h + 1) * HEAD_DIM)
        kc = kc_ref[:, sl].astype(BF16)
        vc = vc_ref[:, sl].astype(BF16)
        for u in range(n_sub):
            start = pl.multiple_of(ss_ref[rb * n_sub + u] * GRID_W, GRID_W)
            rows = slice(u * qn, (u + 1) * qn)
            ks = k_ref[pl.ds(start, slab_len), sl]
            vs = v_ref[pl.ds(start, slab_len), sl]
            q = q_ref[rows, sl]
            s_loc = lax.dot_general(q, ks, _NT, preferred_element_type=F32) + bias_refs[u][h]
            s_ctx = lax.dot_general(q, kc, _NT, preferred_element_type=F32)
            o_ref[rows, sl] = _softmax_pv([s_loc, s_ctx], [vs, vc]).astype(BF16)


def _neighborhood_attention(proj, ck, cv, layer, bias_tab, plan):
    b, t, _ = proj.shape
    r_blk, slab, slab_start, pat_id, _ = plan
    qn = r_blk * GRID_W
    sn = slab * GRID_W
    l_ctx = ck.shape[2]
    hb = NA_HEADS_PER_STEP
    hw = hb * HEAD_DIM
    assert COL_QN % hb == 0 and COL_KN % hb == 0 and COL_VN % hb == 0 and N_HEADS_NA % hb == 0
    n_rb = t // qn
    n_sub = NA_BLOCKS_PER_STEP if n_rb % NA_BLOCKS_PER_STEP == 0 else 1

    def bias_spec(u):
        return pl.BlockSpec((None, None, hb, qn, sn),
                            lambda bi, hg, rb, ss, pt: (layer, pt[rb * n_sub + u], hg, 0, 0))

    grid_spec = pltpu.PrefetchScalarGridSpec(
        num_scalar_prefetch=2,
        grid=(b, N_HEADS_NA // hb, n_rb // n_sub),
        in_specs=[
            pl.BlockSpec((None, n_sub * qn, hw), lambda bi, hg, rb, ss, pt: (bi, rb, COL_QN // hb + hg)),
            pl.BlockSpec((None, t, hw), lambda bi, hg, rb, ss, pt: (bi, 0, COL_KN // hb + hg)),
            pl.BlockSpec((None, t, hw), lambda bi, hg, rb, ss, pt: (bi, 0, COL_VN // hb + hg)),
            pl.BlockSpec((None, None, l_ctx, hw), lambda bi, hg, rb, ss, pt: (bi, layer, 0, hg)),
            pl.BlockSpec((None, None, l_ctx, hw), lambda bi, hg, rb, ss, pt: (bi, layer, 0, hg)),
        ] + [bias_spec(u) for u in range(n_sub)],
        out_specs=pl.BlockSpec((None, n_sub * qn, hw), lambda bi, hg, rb, ss, pt: (bi, rb, hg)),
    )
    return pl.pallas_call(
        functools.partial(_na_kernel, slab_len=sn, n_heads=hb, n_sub=n_sub, qn=qn),
        grid_spec=grid_spec,
        out_shape=jax.ShapeDtypeStruct((b, t, NA_W), BF16),
        compiler_params=_cparams(("arbitrary", "arbitrary", "arbitrary")),
        name="neighborhood_attn",
    )(jnp.asarray(slab_start), jnp.asarray(pat_id), proj, proj, proj, ck, cv, *([bias_tab] * n_sub))


def _layer_norm(y, g, b):
    mu = jnp.mean(y, axis=-1, keepdims=True)
    yc = y - mu
    var = jnp.mean(yc * yc, axis=-1, keepdims=True)
    return yc * lax.rsqrt(var + EPS) * g + b


def _top2_of4(vals):
    m1 = jnp.maximum(jnp.maximum(vals[0], vals[1]), jnp.maximum(vals[2], vals[3]))
    i1 = jnp.where(vals[0] == m1, 0, jnp.where(vals[1] == m1, 1, jnp.where(vals[2] == m1, 2, 3)))
    rest = [jnp.where(i1 == i, -1.0, vals[i]) for i in range(4)]
    m2 = jnp.maximum(jnp.maximum(rest[0], rest[1]), jnp.maximum(rest[2], rest[3]))
    i2 = jnp.where(rest[0] == m2, 0, jnp.where(rest[1] == m2, 1, jnp.where(rest[2] == m2, 2, 3)))
    return m1, i1, m2, i2


def _route_rows(logits_t):
    m = logits_t.max(axis=0, keepdims=True)
    e = jnp.exp(logits_t - m)
    probs = e / e.sum(axis=0, keepdims=True)
    rows = [probs[i:i + 1, :] for i in range(N_EXPERTS)]
    groups = [rows[g * EXPERTS_PER_GROUP:(g + 1) * EXPERTS_PER_GROUP] for g in range(N_GROUPS)]
    scores = []
    for g in range(N_GROUPS):
        m1, _, m2, _ = _top2_of4(groups[g])
        scores.append(m1 + m2)
    best = jnp.maximum(jnp.maximum(scores[0], scores[1]), jnp.maximum(scores[2], scores[3]))
    gi = jnp.where(scores[0] == best, 0, jnp.where(scores[1] == best, 1, jnp.where(scores[2] == best, 2, 3)))
    sel = [jnp.where(gi == 0, groups[0][i], jnp.where(gi == 1, groups[1][i],
                     jnp.where(gi == 2, groups[2][i], groups[3][i]))) for i in range(EXPERTS_PER_GROUP)]
    w1, l1, w2, l2 = _top2_of4(sel)
    wsum = w1 + w2
    return (gi * EXPERTS_PER_GROUP + l1, gi * EXPERTS_PER_GROUP + l2, w1 / wsum, w2 / wsum)


def _out_proj_kernel(oa_ref, on_ref, x_ref, mod_ref, wo_ref, g_ref, b_ref, wr_ref, br_ref, h2_in_ref,
                     x1_ref, h2_ref, e_ref, wt_ref):
    del h2_in_ref
    attn = jnp.dot(oa_ref[...], wo_ref[0:GA_Q, :], preferred_element_type=F32)
    attn = attn + jnp.dot(on_ref[...], wo_ref[GA_Q:GA_Q + NA_W, :], preferred_element_type=F32)
    gate1 = mod_ref[2:3, :]
    x1 = _layer_norm(DN_ALPHA * x_ref[...] + gate1 * attn, g_ref[...], b_ref[...])
    x1_ref[...] = x1
    h2 = x1 * (1.0 + mod_ref[4:5, :]) + mod_ref[3:4, :]
    h2_ref[...] = h2
    h_hi = h2.astype(BF16)
    h_lo = (h2 - h_hi.astype(F32)).astype(BF16)
    r_hi = jnp.dot(h_hi, wr_ref[...], preferred_element_type=F32)
    r_lo = jnp.dot(h_lo, wr_ref[...], preferred_element_type=F32)
    logits = r_hi + pltpu.roll(r_hi, LANE - N_EXPERTS, 1) + r_lo
    logits_t = logits.T[0:N_EXPERTS, :] + br_ref[...]
    e1, e2, w1, w2 = _route_rows(logits_t)
    e_ref[0:1, :] = e1
    e_ref[1:2, :] = e2
    n = w1.shape[1]
    row = lax.broadcasted_iota(jnp.int32, (LANE, n), 0)
    w_rows = jnp.where(row == 0, w1, jnp.where(row == 1, w2, 0.0))
    wt_ref[...] = w_rows.T


def _out_proj(oa, on, x, mod_l, mod_row0, w_o_bf, w_layer, ln_g, ln_b, wr_cat, b_router, h2_all, row0):
    b, t, d = x.shape
    tm = math.gcd(min(OUT_TM, t), row0) if row0 else min(OUT_TM, t)
    nt = t // tm
    tok = lambda bi, ti: (bi, ti, 0)
    const2 = lambda bi, ti: (0, 0)
    return pl.pallas_call(
        _out_proj_kernel,
        grid=(b, t // tm),
        in_specs=[
            pl.BlockSpec((None, tm, GA_Q), tok),
            pl.BlockSpec((None, tm, NA_W), tok),
            pl.BlockSpec((None, tm, d), tok),
            pl.BlockSpec((None, N_MOD, d), lambda bi, ti: (bi + mod_row0, 0, 0)),
            pl.BlockSpec((None, GA_Q + NA_W, d), lambda bi, ti: (w_layer, 0, 0), pipeline_mode=pl.Buffered(1)),
            pl.BlockSpec((1, d), const2),
            pl.BlockSpec((1, d), const2),
            pl.BlockSpec((d, LANE), const2),
            pl.BlockSpec((N_EXPERTS, 1), const2),
            pl.BlockSpec(memory_space=pl.ANY),
        ],
        out_specs=[
            pl.BlockSpec((None, tm, d), tok),
            pl.BlockSpec((tm, d), lambda bi, ti: (row0 // tm + bi * nt + ti, 0)),
            pl.BlockSpec((None, TOP_K, tm), lambda bi, ti: (bi, 0, ti)),
            pl.BlockSpec((None, tm, LANE), tok),
        ],
        out_shape=[
            jax.ShapeDtypeStruct((b, t, d), F32),
            jax.ShapeDtypeStruct(h2_all.shape, F32),
            jax.ShapeDtypeStruct((b, TOP_K, t), jnp.int32),
            jax.ShapeDtypeStruct((b, t, LANE), F32),
        ],
        input_output_aliases={9: 1},
        compiler_params=_cparams(("arbitrary", "arbitrary")),
        name="out_proj_ln_router",
    )(oa, on, x, mod_l, w_o_bf, ln_g.reshape(1, d), ln_b.reshape(1, d), wr_cat,
      b_router.reshape(N_EXPERTS, 1).astype(F32), h2_all)


def _row_copy(src_hbm, idx_ref, dst, sem, r, base=None):
    idx = idx_ref[0, r] if base is None else idx_ref[base + r]
    return pltpu.make_async_copy(src_hbm.at[pl.ds(idx, 1)], dst.at[pl.ds(r, 1)], sem)


def _start_row_gather(src_hbm, idx_ref, dst, sem, n_rows, base=None):
    def body(r, carry):
        _row_copy(src_hbm, idx_ref, dst, sem, r, base).start()
        return carry
    lax.fori_loop(0, n_rows, body, 0, unroll=8)


def _start_row_gather_inline(src_hbm, idx_ref, dst, sem, r_lo, r_hi, base=None):
    for r in range(r_lo, r_hi):
        _row_copy(src_hbm, idx_ref, dst, sem, r, base).start()


def _wait_row_gather(src_hbm, dst, sem, n_rows):
    pltpu.make_async_copy(src_hbm.at[pl.ds(0, n_rows)], dst, sem).wait()


def _moe_kernel(be_ref, nu_ref, tok_ref, h_hbm, wg_ref, wu_ref, wd_ref, o_ref,
                buf, sem, *, n_blocks):
    del be_ref
    i = pl.program_id(0)
    n_used = nu_ref[0]
    slot = i % GATHER_DEPTH
    ahead = (i + 2) % GATHER_DEPTH
    base2 = jnp.minimum(i + 2, n_blocks - 1) * MOE_BLK

    @pl.when(i == 0)
    def _():
        _start_row_gather(h_hbm, tok_ref, buf.at[0], sem.at[0], MOE_BLK, base=0)
        _start_row_gather(h_hbm, tok_ref, buf.at[1], sem.at[1], MOE_BLK, base=MOE_BLK)

    _wait_row_gather(h_hbm, buf.at[slot], sem.at[slot], MOE_BLK)

    @pl.when(i < n_used)
    def _():
        xb = buf[slot].astype(BF16)
        gate = jnp.dot(xb, wg_ref[...], preferred_element_type=F32)
        up = jnp.dot(xb, wu_ref[...], preferred_element_type=F32)
        act = (gate / (1.0 + jnp.exp(-gate)) * up).astype(BF16)
        _start_row_gather_inline(h_hbm, tok_ref, buf.at[ahead], sem.at[ahead], 0, MOE_BLK, base=base2)
        o_ref[...] = jnp.dot(act, wd_ref[...], preferred_element_type=F32)

    @pl.when(i >= n_used)
    def _():
        o_ref[...] = jnp.zeros_like(o_ref)
        _start_row_gather(h_hbm, tok_ref, buf.at[ahead], sem.at[ahead], MOE_BLK, base=base2)

    @pl.when(i == n_blocks - 1)
    def _():
        for k in (1, 2):
            s = (i + k) % GATHER_DEPTH
            _wait_row_gather(h_hbm, buf.at[s], sem.at[s], MOE_BLK)


def _moe_ffn(h2_flat, tok_buf, blk_e, n_used, wg, wu, wd, layer):
    n, d = h2_flat.shape
    n_blocks = blk_e.shape[0]
    d_ff = wg.shape[3]
    grid_spec = pltpu.PrefetchScalarGridSpec(
        num_scalar_prefetch=3,
        grid=(n_blocks,),
        in_specs=[
            pl.BlockSpec(memory_space=pl.ANY),
            pl.BlockSpec((None, None, d, d_ff), lambda i, be, nu, tk: (layer, be[i], 0, 0)),
            pl.BlockSpec((None, None, d, d_ff), lambda i, be, nu, tk: (layer, be[i], 0, 0)),
            pl.BlockSpec((None, None, d_ff, d), lambda i, be, nu, tk: (layer, be[i], 0, 0)),
        ],
        out_specs=pl.BlockSpec((MOE_BLK, d), lambda i, be, nu, tk: (i, 0)),
        scratch_shapes=[pltpu.VMEM((GATHER_DEPTH, MOE_BLK, d), F32), pltpu.SemaphoreType.DMA((GATHER_DEPTH,))],
    )
    return pl.pallas_call(
        functools.partial(_moe_kernel, n_blocks=n_blocks),
        grid_spec=grid_spec,
        out_shape=jax.ShapeDtypeStruct((n_blocks * MOE_BLK, d), F32),
        compiler_params=_cparams(("arbitrary",)),
        name="moe_ffn",
    )(blk_e, n_used, tok_buf, h2_flat, wg, wu, wd)


def _combine_kernel(dest_ref, y_hbm, x1_ref, mod_ref, wt_ref, g_ref, b_ref, o_ref, buf, sem, *, n_steps, tm):
    i = pl.program_id(0)
    slot = i % GATHER_DEPTH
    ahead = (i + 2) % GATHER_DEPTH
    step_base = lambda step: jnp.minimum(step, n_steps - 1) * (TOP_K * tm)

    @pl.when(i == 0)
    def _():
        for s in range(2):
            for k in range(TOP_K):
                _start_row_gather(y_hbm, dest_ref, buf.at[s, k], sem.at[s], tm, base=step_base(s) + k * tm)

    def wait(s):
        _wait_row_gather(y_hbm, buf.at[s, 0], sem.at[s], tm)
        _wait_row_gather(y_hbm, buf.at[s, 1], sem.at[s], tm)

    wait(slot)
    y = wt_ref[:, 0:1] * buf[slot, 0] + wt_ref[:, 1:2] * buf[slot, 1]
    gate2 = mod_ref[5:6, :]
    o_ref[...] = _layer_norm(DN_ALPHA * x1_ref[...] + gate2 * y, g_ref[...], b_ref[...])
    for k in range(TOP_K):
        _start_row_gather_inline(y_hbm, dest_ref, buf.at[ahead, k], sem.at[ahead], 0, tm,
                                 base=step_base(i + 2) + k * tm)

    @pl.when(i == n_steps - 1)
    def _():
        for k in (1, 2):
            wait((i + k) % GATHER_DEPTH)


def _combine(y_sorted, dest, x1, mod_l, mod_row0, wt, ln_g, ln_b):
    b, t, d = x1.shape
    tm = min(COMBINE_TM, t)
    nt = t // tm
    n_steps = b * nt
    dest_flat = dest.reshape(b, TOP_K, nt, tm).transpose(0, 2, 1, 3).reshape(n_steps * TOP_K * tm)
    tok = lambda i, ds: (i // nt, i % nt, 0)
    grid_spec = pltpu.PrefetchScalarGridSpec(
        num_scalar_prefetch=1,
        grid=(n_steps,),
        in_specs=[
            pl.BlockSpec(memory_space=pl.ANY),
            pl.BlockSpec((None, tm, d), tok),
            pl.BlockSpec((None, N_MOD, d), lambda i, ds: (i // nt + mod_row0, 0, 0)),
            pl.BlockSpec((None, tm, LANE), tok),
            pl.BlockSpec((1, d), lambda i, ds: (0, 0)),
            pl.BlockSpec((1, d), lambda i, ds: (0, 0)),
        ],
        out_specs=pl.BlockSpec((None, tm, d), tok),
        scratch_shapes=[pltpu.VMEM((GATHER_DEPTH, TOP_K, tm, d), F32), pltpu.SemaphoreType.DMA((GATHER_DEPTH,))],
    )
    return pl.pallas_call(
        functools.partial(_combine_kernel, n_steps=n_steps, tm=tm),
        grid_spec=grid_spec,
        out_shape=jax.ShapeDtypeStruct((b, t, d), F32),
        compiler_params=_cparams(("arbitrary",)),
        name="moe_combine_ln",
    )(dest_flat, y_sorted, x1, mod_l, wt, ln_g.reshape(1, d), ln_b.reshape(1, d))


def _dispatch_plan(e_idxs, row0s):
    flat_e = jnp.concatenate([e_idx.reshape(-1) for e_idx in e_idxs])
    a = flat_e.shape[0]
    onehot = (flat_e[:, None] == jnp.arange(N_EXPERTS, dtype=jnp.int32)[None, :]).astype(jnp.int32)
    counts = onehot.sum(axis=0)
    chunk = 256
    assert a % chunk == 0
    oh = onehot.reshape(a // chunk, chunk, N_EXPERTS).astype(F32)
    tri = jnp.tril(jnp.ones((chunk, chunk), F32))
    within = jnp.einsum('ij,cje->cie', tri, oh, precision=lax.Precision.HIGHEST)
    before = jnp.cumsum(oh.sum(axis=1), axis=0) - oh.sum(axis=1)
    running = (within + before[:, None, :]).reshape(a, N_EXPERTS).astype(jnp.int32)
    rank = (running * onehot).sum(axis=1) - 1
    padded = (counts + MOE_BLK - 1) // MOE_BLK * MOE_BLK
    pad_end = jnp.cumsum(padded)
    pad_start = pad_end - padded
    dest = pad_start[flat_e] + rank
    n_blocks = -(-a // MOE_BLK) + N_EXPERTS
    p = n_blocks * MOE_BLK
    blk_start = jnp.arange(n_blocks, dtype=jnp.int32) * MOE_BLK
    blk_e = jnp.minimum((blk_start[:, None] >= pad_end[None, :]).sum(axis=1), N_EXPERTS - 1).astype(jnp.int32)
    n_used = (pad_end[-1] // MOE_BLK).astype(jnp.int32)
    last_e = blk_e[jnp.maximum(n_used - 1, 0)]
    blk_e = jnp.where(jnp.arange(n_blocks) < n_used, blk_e, last_e)
    assert a < (1 << 16)
    order = lax.sort(flat_e * (1 << 16) + jnp.arange(a, dtype=jnp.int32)) & 0xFFFF
    start = jnp.cumsum(counts) - counts
    order = jnp.concatenate([order, jnp.zeros((MOE_BLK,), jnp.int32)])
    win = jnp.clip(start[blk_e] + blk_start - pad_start[blk_e], 0, a)
    order_blk = order[(win[:, None] + jnp.arange(MOE_BLK, dtype=jnp.int32)[None, :]).reshape(p)]
    tok_buf, o = jnp.zeros((p,), jnp.int32), 0
    for e_idx, row0 in zip(e_idxs, row0s):
        b, k, t = e_idx.shape
        local = order_blk - o
        rows = row0 + (local // (k * t)) * t + local % t
        tok_buf = jnp.where((local >= 0) & (local < e_idx.size), rows, tok_buf)
        o += e_idx.size
    dests, o = [], 0
    for e_idx in e_idxs:
        dests.append(dest[o:o + e_idx.size].reshape(e_idx.shape))
        o += e_idx.size
    return dests, tok_buf, blk_e, n_used.reshape(1)


def _attention_sublayer(x, mod_l, mod_row0, lw, ctx):
    b, t, d = x.shape
    layer = lw['layer']
    if ctx is None:
        xt = x.reshape(1, b * t, d)
        proj, *cache = _in_proj(xt, mod_l, mod_row0, lw['w_in'], layer, lw['q_gain'], lw['k_gain'],
                                cache=(layer, lw['depth'], b, t, lw['prev_cache']))
        proj = proj.reshape(b, t, -1)
        oa = _dense_attention(proj, COL_QA, COL_KA, COL_VA, N_KV_GA, N_HEADS_GA // N_KV_GA, gs=GA_KV_PER_STEP)
        on = _dense_attention(proj, COL_QN, COL_KN, COL_VN, N_HEADS_NA, 1, gs=NA_HEADS_PER_STEP)
        return oa.reshape(1, b * t, -1), on.reshape(1, b * t, -1), xt, cache
    ga_k, ga_v, na_k, na_v, rope_tabs, na_bias, na_plan = ctx
    (proj,) = _in_proj(x, mod_l, mod_row0, lw['w_in'], layer, lw['q_gain'], lw['k_gain'], rope_tabs=rope_tabs)
    oa = _dense_attention(proj, COL_QA, COL_KA, COL_VA, N_KV_GA, N_HEADS_GA // N_KV_GA, gs=GA_KV_PER_STEP,
                          ctx=(ga_k, ga_v, layer))
    on = _neighborhood_attention(proj, na_k, na_v, layer, na_bias, na_plan)
    return oa, on, x, None


def _layer(x_ctx, x_lat, mod_l, lw, shared, lat_ctx, h2_all):
    bc, tc, d = x_ctx.shape
    bl, tl, _ = x_lat.shape
    n_ctx, n_lat = bc * tc, bl * tl
    streams = [_attention_sublayer(x_ctx, mod_l, bl, lw, None),
               _attention_sublayer(x_lat, mod_l, 0, lw, lat_ctx)]
    cache = streams[0][3]
    mod_rows, row0s = (bl, 0), (0, n_ctx)
    x1s, e_idxs, wts = [], [], []
    for (oa, on, xt, _), mod_row0, row0 in zip(streams, mod_rows, row0s):
        x1, h2_all, e_idx, wt = _out_proj(oa, on, xt, mod_l, mod_row0, lw['w_o'], lw['layer'], lw['ln1_g'],
                                          lw['ln1_b'], shared['wr_cat'], shared['b_router'], h2_all, row0)
        x1s.append(x1)
        e_idxs.append(e_idx)
        wts.append(wt)
    dests, tok_buf, blk_e, n_used = _dispatch_plan(e_idxs, row0s)
    y_sorted = _moe_ffn(h2_all, tok_buf, blk_e, n_used, lw['w_gate'], lw['w_up'], lw['w_down'], lw['layer'])
    outs = [_combine(y_sorted, dest, x1, mod_l, mod_row0, wt, lw['ln2_g'], lw['ln2_b'])
            for dest, x1, wt, mod_row0 in zip(dests, x1s, wts, mod_rows)]
    return outs[0].reshape(bc, tc, d), outs[1].reshape(bl, tl, d), cache, h2_all


def kernel(x_prompt, x_sample, c, cache_ga_k, cache_ga_v, cache_na_k, cache_na_v, c_ctx, w_router, b_router, w_mod, b_mod, w_in, q_norm, k_norm, rel_bias, w_o, ln1_g, ln1_b, ln2_g, ln2_b, w_gate, w_up, w_down):
    bsz, seq, d = x_prompt.shape
    dec_b, dec_t, _ = x_sample.shape
    depth = w_mod.shape[0]
    past = cache_ga_k.shape[2]

    mod_rows = 16
    assert dec_b + 1 <= mod_rows
    cvec = jnp.concatenate([c, c_ctx[None, :], jnp.zeros((mod_rows - dec_b - 1, d), F32)], axis=0)
    mod = _modulation(cvec, w_mod, b_mod).reshape(depth, mod_rows, N_MOD, d)

    wr_hi = w_router.astype(BF16)
    wr_lo = (w_router - wr_hi.astype(F32)).astype(BF16)
    wr_cat = jnp.concatenate([wr_hi, wr_lo, jnp.zeros((d, LANE - 2 * N_EXPERTS), BF16)], axis=1)
    shared = {'wr_cat': wr_cat, 'b_router': b_router}

    rope_tabs = _rope_tables(dec_t)
    na_plan = _na_plan(dec_t // GRID_W)
    na_bias = _na_bias_tables(rel_bias, na_plan[4])
    ga_k = cache_ga_k.reshape(dec_b, depth, past, GA_KV)
    ga_v = cache_ga_v.reshape(dec_b, depth, past, GA_KV)
    na_k = cache_na_k.reshape(dec_b, depth, past, NA_W)
    na_v = cache_na_v.reshape(dec_b, depth, past, NA_W)

    w_in_bf, w_o_bf = w_in.astype(BF16), w_o.astype(BF16)
    w_gate_bf, w_up_bf, w_down_bf = w_gate.astype(BF16), w_up.astype(BF16), w_down.astype(BF16)

    y_prompt, y_sample = x_prompt, x_sample
    cache = [jnp.zeros((bsz, depth, seq, w), F32) for w in (GA_KV, GA_KV, NA_W, NA_W)]
    h2_all = jnp.zeros((bsz * seq + dec_b * dec_t, d), F32)
    for i in range(depth):
        lw = {
            'layer': i, 'depth': depth, 'prev_cache': cache,
            'w_in': w_in_bf, 'w_o': w_o_bf,
            'q_gain': (q_norm[i] * Q_PRESCALE).reshape(1, HEAD_DIM).astype(F32),
            'k_gain': k_norm[i].reshape(1, HEAD_DIM).astype(F32),
            'ln1_g': ln1_g[i], 'ln1_b': ln1_b[i], 'ln2_g': ln2_g[i], 'ln2_b': ln2_b[i],
            'w_gate': w_gate_bf, 'w_up': w_up_bf, 'w_down': w_down_bf,
        }
        y_prompt, y_sample, cache, h2_all = _layer(y_prompt, y_sample, mod[i], lw, shared,
                                                   (ga_k, ga_v, na_k, na_v, rope_tabs, na_bias, na_plan), h2_all)

    outs = [arr.reshape(bsz, depth, seq, -1, HEAD_DIM) for arr in cache]
    return (y_prompt, y_sample, outs[0], outs[1], outs[2], outs[3])
```

```python
import functools
import math

import numpy as np
import jax
import jax.numpy as jnp
from jax import lax
from jax.experimental import pallas as pl
from jax.experimental.pallas import tpu as pltpu

F32 = jnp.float32
BF16 = jnp.bfloat16

DEPTH = 2
GRID_W = 64
HEAD_DIM = 128
N_HEADS_GA = 8
N_KV_GA = 2
N_HEADS_NA = 8
NA_WIN_H = 8
NA_WIN_W = 16
ROPE_THETA = 10000.0
N_EXPERTS = 16
N_GROUPS = 4
EXPERTS_PER_GROUP = N_EXPERTS // N_GROUPS
TOP_K = 2
N_MOD = 6
DN_ALPHA = (2.0 * DEPTH) ** 0.25
EPS = 1e-6
ATTN_SCALE = HEAD_DIM ** -0.5
LOG2E = math.log2(math.e)
Q_PRESCALE = ATTN_SCALE * LOG2E
MASK_VALUE = -1e30

GA_Q = N_HEADS_GA * HEAD_DIM
GA_KV = N_KV_GA * HEAD_DIM
NA_W = N_HEADS_NA * HEAD_DIM
COL_QA = 0
COL_KA = COL_QA + N_HEADS_GA
COL_VA = COL_KA + N_KV_GA
COL_QN = COL_VA + N_KV_GA
COL_KN = COL_QN + N_HEADS_NA
COL_VN = COL_KN + N_HEADS_NA
IN_HEADS = COL_VN + N_HEADS_NA

LANE = 128
PROJ_TN = 4 * HEAD_DIM
PROJ_TM = 512
PROJ_TM_CACHE = 256
ATTN_TQ = 512
NA_ROWS = 4
NA_HEADS_PER_STEP = 4
NA_BLOCKS_PER_STEP = 2
GA_KV_PER_STEP = 2
OUT_TM = 512
COMBINE_TM = 256
MOE_BLK = 256
GATHER_DEPTH = 3
MOE_GATHER_PRIORITY = 1
MOD_TN = 1024
VMEM_LIMIT = 52 * 1024 * 1024


def _cparams(sem):
    return pltpu.CompilerParams(dimension_semantics=sem, vmem_limit_bytes=VMEM_LIMIT)


def _mod_kernel(c_ref, w_ref, b_ref, o_ref):
    c = c_ref[...]
    s = c / (1.0 + jnp.exp(-c))
    o_ref[...] = jnp.dot(s.astype(BF16), w_ref[...].astype(BF16),
                         preferred_element_type=F32) + b_ref[...]


def _modulation(cvec, w_mod, b_mod):
    depth, d, e = w_mod.shape
    rows = cvec.shape[0]
    tn = MOD_TN if e % MOD_TN == 0 else e
    return pl.pallas_call(
        _mod_kernel,
        grid=(depth, e // tn),
        in_specs=[
            pl.BlockSpec((rows, d), lambda l, j: (0, 0)),
            pl.BlockSpec((None, d, tn), lambda l, j: (l, 0, j)),
            pl.BlockSpec((None, 1, tn), lambda l, j: (l, 0, j)),
        ],
        out_specs=pl.BlockSpec((None, rows, tn), lambda l, j: (l, 0, j)),
        out_shape=jax.ShapeDtypeStruct((depth, rows, e), F32),
        compiler_params=_cparams(("arbitrary", "arbitrary")),
        name="modulation",
    )(cvec, w_mod, b_mod.reshape(depth, 1, e))


def _rms_head(x, g):
    ms = jnp.mean(x * x, axis=-1, keepdims=True)
    return x * lax.rsqrt(ms + EPS) * g


def _rope_head(x, cos, sin, first_half):
    swapped = jnp.where(first_half, pltpu.roll(x, HEAD_DIM - 32, 1), pltpu.roll(x, 32, 1))
    return x * cos + swapped * sin


def _proj_kernel(*refs, rope, cache):
    x_ref, mod_ref, w_ref, qg_ref, kg_ref = refs[:5]
    pos = 5
    if rope:
        cos_ref, sin_ref = refs[pos:pos + 2]
        pos += 2
    if cache is not None:
        pos += cache[0]
    o_ref = refs[pos]
    pos += 1
    if cache is not None:
        ka_ref, va_ref, kn_ref, vn_ref = refs[pos:pos + 4]
        pos += 4
    h_scr = refs[pos]

    shift = mod_ref[0:1, :]
    scale = mod_ref[1:2, :]
    h_scr[...] = (x_ref[...] * (1.0 + scale) + shift).astype(BF16)

    if rope:
        lane = lax.broadcasted_iota(jnp.int32, (1, HEAD_DIM), 1)
        first_half = (lane % 64) < 32

    def normed(y, g):
        y = _rms_head(y, g)
        if rope:
            y = _rope_head(y, cos_ref[...], sin_ref[...], first_half)
        return y

    def to_cache(c_ref, c_head, y):
        if cache is not None:
            c_ref[:, :, c_head * HEAD_DIM:(c_head + 1) * HEAD_DIM] = y.reshape(cache[1], cache[2], HEAD_DIM)

    hpt = PROJ_TN // HEAD_DIM
    for jt in range(IN_HEADS // hpt):
        acc = jnp.dot(h_scr[...], w_ref[:, jt * PROJ_TN:(jt + 1) * PROJ_TN], preferred_element_type=F32)
        for hh in range(hpt):
            head = jt * hpt + hh
            y = acc[:, hh * HEAD_DIM:(hh + 1) * HEAD_DIM]
            if head < COL_KA:
                y = normed(y, qg_ref[...])
            elif head < COL_VA:
                y = normed(y, kg_ref[...])
                to_cache(ka_ref if cache else None, head - COL_KA, y)
            elif head < COL_QN:
                to_cache(va_ref if cache else None, head - COL_VA, y)
            elif head < COL_KN:
                y = y * Q_PRESCALE
            elif head < COL_VN:
                to_cache(kn_ref if cache else None, head - COL_KN, y)
            else:
                to_cache(vn_ref if cache else None, head - COL_VN, y)
            o_ref[:, head * HEAD_DIM:(head + 1) * HEAD_DIM] = y.astype(BF16)


def _in_proj(x, mod_l, mod_row0, w_in_bf, w_layer, q_gain, k_gain, rope_tabs=None, cache=None):
    b, t, d = x.shape
    in_w = w_in_bf.shape[2]
    rope = rope_tabs is not None
    tm = min(PROJ_TM if rope else PROJ_TM_CACHE, t)
    assert in_w == IN_HEADS * HEAD_DIM and in_w % PROJ_TN == 0

    in_specs = [
        pl.BlockSpec((None, tm, d), lambda bi, ti: (bi, ti, 0)),
        pl.BlockSpec((None, N_MOD, d), lambda bi, ti: (bi + mod_row0, 0, 0)),
        pl.BlockSpec((None, d, in_w), lambda bi, ti: (w_layer, 0, 0), pipeline_mode=pl.Buffered(1)),
        pl.BlockSpec((1, HEAD_DIM), lambda bi, ti: (0, 0)),
        pl.BlockSpec((1, HEAD_DIM), lambda bi, ti: (0, 0)),
    ]
    args = [x, mod_l, w_in_bf, q_gain, k_gain]
    if rope:
        in_specs += [pl.BlockSpec((tm, HEAD_DIM), lambda bi, ti: (ti, 0))] * 2
        args += list(rope_tabs)
    out_specs = [pl.BlockSpec((None, tm, in_w), lambda bi, ti: (bi, ti, 0))]
    out_shape = [jax.ShapeDtypeStruct((b, t, in_w), BF16)]
    aliases = {}
    cache_cfg = None
    if cache is not None:
        layer, depth, bsz, seq, prev = cache
        assert b == 1 and tm % seq == 0
        nb = tm // seq
        cache_cfg = (len(prev), nb, seq)
        for k, arr in enumerate(prev):
            aliases[len(args)] = 1 + k
            in_specs.append(pl.BlockSpec(memory_space=pl.ANY))
            args.append(arr)
        widths = (GA_KV, GA_KV, NA_W, NA_W)
        out_specs += [pl.BlockSpec((nb, None, seq, w), lambda bi, ti: (ti, layer, 0, 0)) for w in widths]
        out_shape += [jax.ShapeDtypeStruct((bsz, depth, seq, w), F32) for w in widths]
    outs = pl.pallas_call(
        functools.partial(_proj_kernel, rope=rope, cache=cache_cfg),
        grid=(b, t // tm),
        in_specs=in_specs,
        out_specs=out_specs,
        out_shape=out_shape,
        input_output_aliases=aliases,
        scratch_shapes=[pltpu.VMEM((tm, d), BF16)],
        compiler_params=_cparams(("arbitrary", "arbitrary")),
        name="in_proj_rope" if rope else "in_proj_cache",
    )(*args)
    return outs


def _rope_tables(t):
    half = HEAD_DIM // 4
    tt = jnp.arange(t, dtype=jnp.int32)
    row = (tt // GRID_W).astype(F32)
    col = (tt % GRID_W).astype(F32)
    inv_freq = 1.0 / (ROPE_THETA ** (jnp.arange(half, dtype=F32) / half))
    ar = row[:, None] * inv_freq[None, :]
    ac = col[:, None] * inv_freq[None, :]
    cos = jnp.concatenate([jnp.cos(ar), jnp.cos(ar), jnp.cos(ac), jnp.cos(ac)], axis=-1)
    sin = jnp.concatenate([-jnp.sin(ar), jnp.sin(ar), -jnp.sin(ac), jnp.sin(ac)], axis=-1)
    return cos, sin


_NT = (((1,), (1,)), ((), ()))


def _softmax_pv(scores, values):
    m = scores[0].max(axis=-1, keepdims=True)
    for s in scores[1:]:
        m = jnp.maximum(m, s.max(axis=-1, keepdims=True))
    l = None
    acc = None
    for s, v in zip(scores, values):
        p = jnp.exp2(s - m)
        ps = p.sum(axis=-1, keepdims=True)
        pv = jnp.dot(p.astype(BF16), v, preferred_element_type=F32)
        l = ps if l is None else l + ps
        acc = pv if acc is None else acc + pv
    return acc * (1.0 / l)


def _dense_attn_kernel(*refs, r_heads, gs, has_ctx):
    if has_ctx:
        q_ref, k_ref, v_ref, kc_ref, vc_ref, o_ref = refs
    else:
        q_ref, k_ref, v_ref, o_ref = refs
    for gi in range(gs):
        kv = slice(gi * HEAD_DIM, (gi + 1) * HEAD_DIM)
        k = k_ref[:, kv]
        v = v_ref[:, kv]
        if has_ctx:
            kc = kc_ref[:, kv].astype(BF16)
            vc = vc_ref[:, kv].astype(BF16)
        for r in range(r_heads):
            h = gi * r_heads + r
            sl = slice(h * HEAD_DIM, (h + 1) * HEAD_DIM)
            q = q_ref[:, sl]
            scores = [lax.dot_general(q, k, _NT, preferred_element_type=F32)]
            values = [v]
            if has_ctx:
                scores.append(lax.dot_general(q, kc, _NT, preferred_element_type=F32))
                values.append(vc)
            o_ref[:, sl] = _softmax_pv(scores, values).astype(BF16)


def _dense_attention(proj, q_col, k_col, v_col, groups, r_heads, gs=1, ctx=None):
    b, t, _ = proj.shape
    tq = min(ATTN_TQ, t)
    qw = gs * r_heads * HEAD_DIM
    kw = gs * HEAD_DIM
    assert q_col % (gs * r_heads) == 0 and k_col % gs == 0 and v_col % gs == 0 and groups % gs == 0
    in_specs = [
        pl.BlockSpec((None, tq, qw), lambda bi, g, qi: (bi, qi, q_col // (gs * r_heads) + g)),
        pl.BlockSpec((None, t, kw), lambda bi, g, qi: (bi, 0, k_col // gs + g)),
        pl.BlockSpec((None, t, kw), lambda bi, g, qi: (bi, 0, v_col // gs + g)),
    ]
    args = [proj, proj, proj]
    if ctx is not None:
        ck, cv, layer = ctx
        l_ctx = ck.shape[2]
        spec = pl.BlockSpec((None, None, l_ctx, kw), lambda bi, g, qi: (bi, layer, 0, g))
        in_specs += [spec, spec]
        args += [ck, cv]
    return pl.pallas_call(
        functools.partial(_dense_attn_kernel, r_heads=r_heads, gs=gs, has_ctx=ctx is not None),
        grid=(b, groups // gs, t // tq),
        in_specs=in_specs,
        out_specs=pl.BlockSpec((None, tq, qw), lambda bi, g, qi: (bi, qi, g)),
        out_shape=jax.ShapeDtypeStruct((b, t, groups * r_heads * HEAD_DIM), BF16),
        compiler_params=_cparams(("arbitrary", "arbitrary", "arbitrary")),
        name="dense_attn_ctx" if ctx is not None else "dense_attn",
    )(*args)


def _na_plan(rows):
    kh = min(NA_WIN_H, rows)
    r_blk = min(NA_ROWS, rows)
    assert rows % r_blk == 0
    slab = min(r_blk - 1 + kh, rows)
    row_start = np.clip(np.arange(rows) - kh // 2, 0, rows - kh)
    slab_start, pat_id, pats, sigs = [], [], [], {}
    for r0 in range(0, rows, r_blk):
        ss = min(row_start[r0], rows - slab)
        rel = tuple(int(row_start[r0 + ri] - ss) for ri in range(r_blk))
        sig = (int(ss - r0), rel)
        if sig not in sigs:
            sigs[sig] = len(pats)
            q_row = r0 + np.arange(r_blk)[:, None, None, None]
            k_row = ss + np.arange(slab)[None, None, :, None]
            rs = row_start[r0:r0 + r_blk][:, None, None, None]
            row_valid = ((k_row >= rs) & (k_row < rs + kh))[:, 0, :, 0]
            row_off = (k_row - q_row + (NA_WIN_H - 1))[:, 0, :, 0]
            row_sel = (row_off[:, :, None] == np.arange(2 * NA_WIN_H - 1)).astype(np.float32)
            pats.append((row_valid, row_sel))
        slab_start.append(int(ss))
        pat_id.append(sigs[sig])
    return r_blk, slab, np.array(slab_start, np.int32), np.array(pat_id, np.int32), pats


def _na_bias_tables(rel_bias, pats):
    col = np.arange(GRID_W)
    col_start = np.clip(col - NA_WIN_W // 2, 0, GRID_W - NA_WIN_W)
    col_off = col[None, :] - col[:, None] + (NA_WIN_W - 1)
    col_sel = (col_off[:, :, None] == np.arange(2 * NA_WIN_W - 1)).astype(np.float32)
    col_valid = (col[None, :] >= col_start[:, None]) & (col[None, :] < col_start[:, None] + NA_WIN_W)
    col_aug = np.concatenate([col_sel, np.ones((GRID_W, GRID_W, 1), np.float32),
                              np.where(col_valid, 0.0, MASK_VALUE)[..., None].astype(np.float32)], axis=-1)
    row_sel = np.stack([p[1] for p in pats])
    row_pen = np.stack([np.where(p[0], 0.0, MASK_VALUE) for p in pats]).astype(np.float32)
    hi = lax.Precision.HIGHEST
    by_row = jnp.einsum('lhrc,pisr->lphisc', rel_bias.astype(F32) * LOG2E, row_sel, precision=hi)
    lead = by_row.shape[:5]
    row_aug = jnp.concatenate([by_row, jnp.broadcast_to(row_pen[None, :, None, :, :, None], lead + (1,)),
                               jnp.ones(lead + (1,), F32)], axis=-1)
    bias = jnp.einsum('lphisc,qkc->lphiqsk', row_aug, col_aug, precision=hi)
    l, p, h, r, w, sl, _ = bias.shape
    return bias.reshape(l, p, h, r * w, sl * w)


def _na_kernel(ss_ref, pat_ref, q_ref, k_ref, v_ref, kc_ref, vc_ref, *rest, slab_len, n_heads, n_sub, qn):
    del pat_ref
    bias_refs, o_ref = rest[:n_sub], rest[n_sub]
    rb = pl.program_id(2)
    for h in range(n_heads):
        sl = slice(h * HEAD_DIM, (h + 1) * HEAD_DIM)
        kc = kc_ref[:, sl].astype(BF16)
        vc = vc_ref[:, sl].astype(BF16)
        for u in range(n_sub):
            start = pl.multiple_of(ss_ref[rb * n_sub + u] * GRID_W, GRID_W)
            rows = slice(u * qn, (u + 1) * qn)
            ks = k_ref[pl.ds(start, slab_len), sl]
            vs = v_ref[pl.ds(start, slab_len), sl]
            q = q_ref[rows, sl]
            s_loc = lax.dot_general(q, ks, _NT, preferred_element_type=F32) + bias_refs[u][h]
            s_ctx = lax.dot_general(q, kc, _NT, preferred_element_type=F32)
            o_ref[rows, sl] = _softmax_pv([s_loc, s_ctx], [vs, vc]).astype(BF16)


def _neighborhood_attention(proj, ck, cv, layer, bias_tab, plan):
    b, t, _ = proj.shape
    r_blk, slab, slab_start, pat_id, _ = plan
    qn = r_blk * GRID_W
    sn = slab * GRID_W
    l_ctx = ck.shape[2]
    hb = NA_HEADS_PER_STEP
    hw = hb * HEAD_DIM
    assert COL_QN % hb == 0 and COL_KN % hb == 0 and COL_VN % hb == 0 and N_HEADS_NA % hb == 0
    n_rb = t // qn
    n_sub = NA_BLOCKS_PER_STEP if n_rb % NA_BLOCKS_PER_STEP == 0 else 1

    def bias_spec(u):
        return pl.BlockSpec((None, None, hb, qn, sn),
                            lambda bi, hg, rb, ss, pt: (layer, pt[rb * n_sub + u], hg, 0, 0))

    grid_spec = pltpu.PrefetchScalarGridSpec(
        num_scalar_prefetch=2,
        grid=(b, N_HEADS_NA // hb, n_rb // n_sub),
        in_specs=[
            pl.BlockSpec((None, n_sub * qn, hw), lambda bi, hg, rb, ss, pt: (bi, rb, COL_QN // hb + hg)),
            pl.BlockSpec((None, t, hw), lambda bi, hg, rb, ss, pt: (bi, 0, COL_KN // hb + hg)),
            pl.BlockSpec((None, t, hw), lambda bi, hg, rb, ss, pt: (bi, 0, COL_VN // hb + hg)),
            pl.BlockSpec((None, None, l_ctx, hw), lambda bi, hg, rb, ss, pt: (bi, layer, 0, hg)),
            pl.BlockSpec((None, None, l_ctx, hw), lambda bi, hg, rb, ss, pt: (bi, layer, 0, hg)),
        ] + [bias_spec(u) for u in range(n_sub)],
        out_specs=pl.BlockSpec((None, n_sub * qn, hw), lambda bi, hg, rb, ss, pt: (bi, rb, hg)),
    )
    return pl.pallas_call(
        functools.partial(_na_kernel, slab_len=sn, n_heads=hb, n_sub=n_sub, qn=qn),
        grid_spec=grid_spec,
        out_shape=jax.ShapeDtypeStruct((b, t, NA_W), BF16),
        compiler_params=_cparams(("arbitrary", "arbitrary", "arbitrary")),
        name="neighborhood_attn",
    )(jnp.asarray(slab_start), jnp.asarray(pat_id), proj, proj, proj, ck, cv, *([bias_tab] * n_sub))


def _layer_norm(y, g, b):
    mu = jnp.mean(y, axis=-1, keepdims=True)
    yc = y - mu
    var = jnp.mean(yc * yc, axis=-1, keepdims=True)
    return yc * lax.rsqrt(var + EPS) * g + b


def _top2_of4(vals):
    m1 = jnp.maximum(jnp.maximum(vals[0], vals[1]), jnp.maximum(vals[2], vals[3]))
    i1 = jnp.where(vals[0] == m1, 0, jnp.where(vals[1] == m1, 1, jnp.where(vals[2] == m1, 2, 3)))
    rest = [jnp.where(i1 == i, -1.0, vals[i]) for i in range(4)]
    m2 = jnp.maximum(jnp.maximum(rest[0], rest[1]), jnp.maximum(rest[2], rest[3]))
    i2 = jnp.where(rest[0] == m2, 0, jnp.where(rest[1] == m2, 1, jnp.where(rest[2] == m2, 2, 3)))
    return m1, i1, m2, i2


def _route_rows(logits_t):
    m = logits_t.max(axis=0, keepdims=True)
    e = jnp.exp(logits_t - m)
    probs = e / e.sum(axis=0, keepdims=True)
    rows = [probs[i:i + 1, :] for i in range(N_EXPERTS)]
    groups = [rows[g * EXPERTS_PER_GROUP:(g + 1) * EXPERTS_PER_GROUP] for g in range(N_GROUPS)]
    scores = []
    for g in range(N_GROUPS):
        m1, _, m2, _ = _top2_of4(groups[g])
        scores.append(m1 + m2)
    best = jnp.maximum(jnp.maximum(scores[0], scores[1]), jnp.maximum(scores[2], scores[3]))
    gi = jnp.where(scores[0] == best, 0, jnp.where(scores[1] == best, 1, jnp.where(scores[2] == best, 2, 3)))
    sel = [jnp.where(gi == 0, groups[0][i], jnp.where(gi == 1, groups[1][i],
                     jnp.where(gi == 2, groups[2][i], groups[3][i]))) for i in range(EXPERTS_PER_GROUP)]
    w1, l1, w2, l2 = _top2_of4(sel)
    wsum = w1 + w2
    return (gi * EXPERTS_PER_GROUP + l1, gi * EXPERTS_PER_GROUP + l2, w1 / wsum, w2 / wsum)


def _out_proj_kernel(oa_ref, on_ref, x_ref, mod_ref, wo_ref, g_ref, b_ref, wr_ref, br_ref, h2_in_ref,
                     x1_ref, h2_ref, e_ref, wt_ref):
    del h2_in_ref
    attn = jnp.dot(oa_ref[...], wo_ref[0:GA_Q, :], preferred_element_type=F32)
    attn = attn + jnp.dot(on_ref[...], wo_ref[GA_Q:GA_Q + NA_W, :], preferred_element_type=F32)
    gate1 = mod_ref[2:3, :]
    x1 = _layer_norm(DN_ALPHA * x_ref[...] + gate1 * attn, g_ref[...], b_ref[...])
    x1_ref[...] = x1
    h2 = x1 * (1.0 + mod_ref[4:5, :]) + mod_ref[3:4, :]
    h2_ref[...] = h2
    h_hi = h2.astype(BF16)
    h_lo = (h2 - h_hi.astype(F32)).astype(BF16)
    r_hi = jnp.dot(h_hi, wr_ref[...], preferred_element_type=F32)
    r_lo = jnp.dot(h_lo, wr_ref[...], preferred_element_type=F32)
    logits = r_hi + pltpu.roll(r_hi, LANE - N_EXPERTS, 1) + r_lo
    logits_t = logits.T[0:N_EXPERTS, :] + br_ref[...]
    e1, e2, w1, w2 = _route_rows(logits_t)
    e_ref[0:1, :] = e1
    e_ref[1:2, :] = e2
    n = w1.shape[1]
    row = lax.broadcasted_iota(jnp.int32, (LANE, n), 0)
    w_rows = jnp.where(row == 0, w1, jnp.where(row == 1, w2, 0.0))
    wt_ref[...] = w_rows.T


def _out_proj(oa, on, x, mod_l, mod_row0, w_o_bf, w_layer, ln_g, ln_b, wr_cat, b_router, h2_all, row0):
    b, t, d = x.shape
    tm = math.gcd(min(OUT_TM, t), row0) if row0 else min(OUT_TM, t)
    nt = t // tm
    tok = lambda bi, ti: (bi, ti, 0)
    const2 = lambda bi, ti: (0, 0)
    return pl.pallas_call(
        _out_proj_kernel,
        grid=(b, t // tm),
        in_specs=[
            pl.BlockSpec((None, tm, GA_Q), tok),
            pl.BlockSpec((None, tm, NA_W), tok),
            pl.BlockSpec((None, tm, d), tok),
            pl.BlockSpec((None, N_MOD, d), lambda bi, ti: (bi + mod_row0, 0, 0)),
            pl.BlockSpec((None, GA_Q + NA_W, d), lambda bi, ti: (w_layer, 0, 0), pipeline_mode=pl.Buffered(1)),
            pl.BlockSpec((1, d), const2),
            pl.BlockSpec((1, d), const2),
            pl.BlockSpec((d, LANE), const2),
            pl.BlockSpec((N_EXPERTS, 1), const2),
            pl.BlockSpec(memory_space=pl.ANY),
        ],
        out_specs=[
            pl.BlockSpec((None, tm, d), tok),
            pl.BlockSpec((tm, d), lambda bi, ti: (row0 // tm + bi * nt + ti, 0)),
            pl.BlockSpec((None, TOP_K, tm), lambda bi, ti: (bi, 0, ti)),
            pl.BlockSpec((None, tm, LANE), tok),
        ],
        out_shape=[
            jax.ShapeDtypeStruct((b, t, d), F32),
            jax.ShapeDtypeStruct(h2_all.shape, F32),
            jax.ShapeDtypeStruct((b, TOP_K, t), jnp.int32),
            jax.ShapeDtypeStruct((b, t, LANE), F32),
        ],
        input_output_aliases={9: 1},
        compiler_params=_cparams(("arbitrary", "arbitrary")),
        name="out_proj_ln_router",
    )(oa, on, x, mod_l, w_o_bf, ln_g.reshape(1, d), ln_b.reshape(1, d), wr_cat,
      b_router.reshape(N_EXPERTS, 1).astype(F32), h2_all)


def _row_copy(src_hbm, idx_ref, dst, sem, r, base):
    return pltpu.make_async_copy(src_hbm.at[pl.ds(idx_ref[base + r], 1)], dst.at[pl.ds(r, 1)], sem)


def _start_row_gather(src_hbm, idx_ref, dst, sem, n_rows, base, priority=0):
    def body(r, carry):
        _row_copy(src_hbm, idx_ref, dst, sem, r, base).start(priority=priority)
        return carry
    lax.fori_loop(0, n_rows, body, 0, unroll=8)


def _start_row_gather_inline(src_hbm, idx_ref, dst, sem, r_lo, r_hi, base, priority=0):
    for r in range(r_lo, r_hi):
        _row_copy(src_hbm, idx_ref, dst, sem, r, base).start(priority=priority)


def _wait_row_gather(src_hbm, dst, sem, n_rows):
    pltpu.make_async_copy(src_hbm.at[pl.ds(0, n_rows)], dst, sem).wait()


def _moe_kernel(be_ref, nu_ref, tok_ref, h_hbm, wg_ref, wu_ref, wd_ref, o_ref,
                buf, sem, *, n_blocks):
    del be_ref
    i = pl.program_id(0)
    n_used = nu_ref[0]
    slot = i % GATHER_DEPTH
    ahead = (i + 2) % GATHER_DEPTH
    base2 = jnp.minimum(i + 2, n_blocks - 1) * MOE_BLK

    @pl.when(i == 0)
    def _():
        _start_row_gather(h_hbm, tok_ref, buf.at[0], sem.at[0], MOE_BLK, 0, MOE_GATHER_PRIORITY)
        _start_row_gather(h_hbm, tok_ref, buf.at[1], sem.at[1], MOE_BLK, MOE_BLK, MOE_GATHER_PRIORITY)

    _wait_row_gather(h_hbm, buf.at[slot], sem.at[slot], MOE_BLK)

    @pl.when(i < n_used)
    def _():
        xb = buf[slot].astype(BF16)
        gate = jnp.dot(xb, wg_ref[...], preferred_element_type=F32)
        up = jnp.dot(xb, wu_ref[...], preferred_element_type=F32)
        act = (gate / (1.0 + jnp.exp(-gate)) * up).astype(BF16)
        _start_row_gather_inline(h_hbm, tok_ref, buf.at[ahead], sem.at[ahead], 0, MOE_BLK, base2,
                                 MOE_GATHER_PRIORITY)
        o_ref[...] = jnp.dot(act, wd_ref[...], preferred_element_type=F32)

    @pl.when(i >= n_used)
    def _():
        o_ref[...] = jnp.zeros_like(o_ref)
        _start_row_gather(h_hbm, tok_ref, buf.at[ahead], sem.at[ahead], MOE_BLK, base2, MOE_GATHER_PRIORITY)

    @pl.when(i == n_blocks - 1)
    def _():
        for k in (1, 2):
            s = (i + k) % GATHER_DEPTH
            _wait_row_gather(h_hbm, buf.at[s], sem.at[s], MOE_BLK)


def _moe_ffn(h2_flat, tok_buf, blk_e, n_used, wg, wu, wd, layer):
    n, d = h2_flat.shape
    n_blocks = blk_e.shape[0]
    d_ff = wg.shape[3]
    grid_spec = pltpu.PrefetchScalarGridSpec(
        num_scalar_prefetch=3,
        grid=(n_blocks,),
        in_specs=[
            pl.BlockSpec(memory_space=pl.ANY),
            pl.BlockSpec((None, None, d, d_ff), lambda i, be, nu, tk: (layer, be[i], 0, 0)),
            pl.BlockSpec((None, None, d, d_ff), lambda i, be, nu, tk: (layer, be[i], 0, 0)),
            pl.BlockSpec((None, None, d_ff, d), lambda i, be, nu, tk: (layer, be[i], 0, 0)),
        ],
        out_specs=pl.BlockSpec((MOE_BLK, d), lambda i, be, nu, tk: (i, 0)),
        scratch_shapes=[pltpu.VMEM((GATHER_DEPTH, MOE_BLK, d), F32), pltpu.SemaphoreType.DMA((GATHER_DEPTH,))],
    )
    return pl.pallas_call(
        functools.partial(_moe_kernel, n_blocks=n_blocks),
        grid_spec=grid_spec,
        out_shape=jax.ShapeDtypeStruct((n_blocks * MOE_BLK, d), F32),
        compiler_params=_cparams(("arbitrary",)),
        name="moe_ffn",
    )(blk_e, n_used, tok_buf, h2_flat, wg, wu, wd)


def _combine_kernel(dest_ref, y_hbm, x1_ref, mod_ref, wt_ref, g_ref, b_ref, o_ref, buf, sem, *, n_steps, tm):
    i = pl.program_id(0)
    slot = i % GATHER_DEPTH
    ahead = (i + 2) % GATHER_DEPTH
    step_base = lambda step: jnp.minimum(step, n_steps - 1) * (TOP_K * tm)

    @pl.when(i == 0)
    def _():
        for s in range(2):
            for k in range(TOP_K):
                _start_row_gather(y_hbm, dest_ref, buf.at[s, k], sem.at[s], tm, step_base(s) + k * tm, priority=k)

    def wait(s):
        _wait_row_gather(y_hbm, buf.at[s, 0], sem.at[s], tm)
        _wait_row_gather(y_hbm, buf.at[s, 1], sem.at[s], tm)

    wait(slot)
    y = wt_ref[:, 0:1] * buf[slot, 0] + wt_ref[:, 1:2] * buf[slot, 1]
    gate2 = mod_ref[5:6, :]
    o_ref[...] = _layer_norm(DN_ALPHA * x1_ref[...] + gate2 * y, g_ref[...], b_ref[...])
    for k in range(TOP_K):
        _start_row_gather_inline(y_hbm, dest_ref, buf.at[ahead, k], sem.at[ahead], 0, tm,
                                 step_base(i + 2) + k * tm, priority=k)

    @pl.when(i == n_steps - 1)
    def _():
        for k in (1, 2):
            wait((i + k) % GATHER_DEPTH)


def _combine(y_sorted, dest, x1, mod_l, mod_row0, wt, ln_g, ln_b):
    b, t, d = x1.shape
    tm = min(COMBINE_TM, t)
    nt = t // tm
    n_steps = b * nt
    dest_flat = dest.reshape(b, TOP_K, nt, tm).transpose(0, 2, 1, 3).reshape(n_steps * TOP_K * tm)
    tok = lambda i, ds: (i // nt, i % nt, 0)
    grid_spec = pltpu.PrefetchScalarGridSpec(
        num_scalar_prefetch=1,
        grid=(n_steps,),
        in_specs=[
            pl.BlockSpec(memory_space=pl.ANY),
            pl.BlockSpec((None, tm, d), tok),
            pl.BlockSpec((None, N_MOD, d), lambda i, ds: (i // nt + mod_row0, 0, 0)),
            pl.BlockSpec((None, tm, LANE), tok),
            pl.BlockSpec((1, d), lambda i, ds: (0, 0)),
            pl.BlockSpec((1, d), lambda i, ds: (0, 0)),
        ],
        out_specs=pl.BlockSpec((None, tm, d), tok),
        scratch_shapes=[pltpu.VMEM((GATHER_DEPTH, TOP_K, tm, d), F32), pltpu.SemaphoreType.DMA((GATHER_DEPTH,))],
    )
    return pl.pallas_call(
        functools.partial(_combine_kernel, n_steps=n_steps, tm=tm),
        grid_spec=grid_spec,
        out_shape=jax.ShapeDtypeStruct((b, t, d), F32),
        compiler_params=_cparams(("arbitrary",)),
        name="moe_combine_ln",
    )(dest_flat, y_sorted, x1, mod_l, wt, ln_g.reshape(1, d), ln_b.reshape(1, d))


def _dispatch_plan(e_idxs, row0s):
    flat_e = jnp.concatenate([e_idx.reshape(-1) for e_idx in e_idxs])
    a = flat_e.shape[0]
    onehot = (flat_e[:, None] == jnp.arange(N_EXPERTS, dtype=jnp.int32)[None, :]).astype(jnp.int32)
    counts = onehot.sum(axis=0)
    chunk = 256
    assert a % chunk == 0
    oh = onehot.reshape(a // chunk, chunk, N_EXPERTS).astype(F32)
    tri = jnp.tril(jnp.ones((chunk, chunk), F32))
    within = jnp.einsum('ij,cje->cie', tri, oh, precision=lax.Precision.HIGHEST)
    before = jnp.cumsum(oh.sum(axis=1), axis=0) - oh.sum(axis=1)
    running = (within + before[:, None, :]).reshape(a, N_EXPERTS).astype(jnp.int32)
    rank = (running * onehot).sum(axis=1) - 1
    padded = (counts + MOE_BLK - 1) // MOE_BLK * MOE_BLK
    pad_end = jnp.cumsum(padded)
    pad_start = pad_end - padded
    dest = pad_start[flat_e] + rank
    n_blocks = -(-a // MOE_BLK) + N_EXPERTS
    p = n_blocks * MOE_BLK
    blk_start = jnp.arange(n_blocks, dtype=jnp.int32) * MOE_BLK
    blk_e = jnp.minimum((blk_start[:, None] >= pad_end[None, :]).sum(axis=1), N_EXPERTS - 1).astype(jnp.int32)
    n_used = (pad_end[-1] // MOE_BLK).astype(jnp.int32)
    last_e = blk_e[jnp.maximum(n_used - 1, 0)]
    blk_e = jnp.where(jnp.arange(n_blocks) < n_used, blk_e, last_e)
    assert a < (1 << 16)
    order = lax.sort(flat_e * (1 << 16) + jnp.arange(a, dtype=jnp.int32)) & 0xFFFF
    start = jnp.cumsum(counts) - counts
    order = jnp.concatenate([order, jnp.zeros((MOE_BLK,), jnp.int32)])
    win = jnp.clip(start[blk_e] + blk_start - pad_start[blk_e], 0, a)
    order_blk = order[(win[:, None] + jnp.arange(MOE_BLK, dtype=jnp.int32)[None, :]).reshape(p)]
    tok_buf, o = jnp.zeros((p,), jnp.int32), 0
    for e_idx, row0 in zip(e_idxs, row0s):
        b, k, t = e_idx.shape
        local = order_blk - o
        rows = row0 + (local // (k * t)) * t + local % t
        tok_buf = jnp.where((local >= 0) & (local < e_idx.size), rows, tok_buf)
        o += e_idx.size
    dests, o = [], 0
    for e_idx in e_idxs:
        dests.append(dest[o:o + e_idx.size].reshape(e_idx.shape))
        o += e_idx.size
    return dests, tok_buf, blk_e, n_used.reshape(1)


def _attention_sublayer(x, mod_l, mod_row0, lw, ctx):
    b, t, d = x.shape
    layer = lw['layer']
    if ctx is None:
        xt = x.reshape(1, b * t, d)
        proj, *cache = _in_proj(xt, mod_l, mod_row0, lw['w_in'], layer, lw['q_gain'], lw['k_gain'],
                                cache=(layer, lw['depth'], b, t, lw['prev_cache']))
        proj = proj.reshape(b, t, -1)
        oa = _dense_attention(proj, COL_QA, COL_KA, COL_VA, N_KV_GA, N_HEADS_GA // N_KV_GA, gs=GA_KV_PER_STEP)
        on = _dense_attention(proj, COL_QN, COL_KN, COL_VN, N_HEADS_NA, 1, gs=NA_HEADS_PER_STEP)
        return oa.reshape(1, b * t, -1), on.reshape(1, b * t, -1), xt, cache
    ga_k, ga_v, na_k, na_v, rope_tabs, na_bias, na_plan = ctx
    (proj,) = _in_proj(x, mod_l, mod_row0, lw['w_in'], layer, lw['q_gain'], lw['k_gain'], rope_tabs=rope_tabs)
    oa = _dense_attention(proj, COL_QA, COL_KA, COL_VA, N_KV_GA, N_HEADS_GA // N_KV_GA, gs=GA_KV_PER_STEP,
                          ctx=(ga_k, ga_v, layer))
    on = _neighborhood_attention(proj, na_k, na_v, layer, na_bias, na_plan)
    return oa, on, x, None


def _layer(x_ctx, x_lat, mod_l, lw, shared, lat_ctx, h2_all):
    bc, tc, d = x_ctx.shape
    bl, tl, _ = x_lat.shape
    n_ctx, n_lat = bc * tc, bl * tl
    streams = [_attention_sublayer(x_ctx, mod_l, bl, lw, None),
               _attention_sublayer(x_lat, mod_l, 0, lw, lat_ctx)]
    cache = streams[0][3]
    mod_rows, row0s = (bl, 0), (0, n_ctx)
    x1s, e_idxs, wts = [], [], []
    for (oa, on, xt, _), mod_row0, row0 in zip(streams, mod_rows, row0s):
        x1, h2_all, e_idx, wt = _out_proj(oa, on, xt, mod_l, mod_row0, lw['w_o'], lw['layer'], lw['ln1_g'],
                                          lw['ln1_b'], shared['wr_cat'], shared['b_router'], h2_all, row0)
        x1s.append(x1)
        e_idxs.append(e_idx)
        wts.append(wt)
    dests, tok_buf, blk_e, n_used = _dispatch_plan(e_idxs, row0s)
    y_sorted = _moe_ffn(h2_all, tok_buf, blk_e, n_used, lw['w_gate'], lw['w_up'], lw['w_down'], lw['layer'])
    outs = [_combine(y_sorted, dest, x1, mod_l, mod_row0, wt, lw['ln2_g'], lw['ln2_b'])
            for dest, x1, wt, mod_row0 in zip(dests, x1s, wts, mod_rows)]
    return outs[0].reshape(bc, tc, d), outs[1].reshape(bl, tl, d), cache, h2_all


def kernel(x_prompt, x_sample, c, cache_ga_k, cache_ga_v, cache_na_k, cache_na_v, c_ctx, w_router, b_router, w_mod, b_mod, w_in, q_norm, k_norm, rel_bias, w_o, ln1_g, ln1_b, ln2_g, ln2_b, w_gate, w_up, w_down):
    bsz, seq, d = x_prompt.shape
    dec_b, dec_t, _ = x_sample.shape
    depth = w_mod.shape[0]
    past = cache_ga_k.shape[2]

    mod_rows = 16
    assert dec_b + 1 <= mod_rows
    cvec = jnp.concatenate([c, c_ctx[None, :], jnp.zeros((mod_rows - dec_b - 1, d), F32)], axis=0)
    mod = _modulation(cvec, w_mod, b_mod).reshape(depth, mod_rows, N_MOD, d)

    wr_hi = w_router.astype(BF16)
    wr_lo = (w_router - wr_hi.astype(F32)).astype(BF16)
    wr_cat = jnp.concatenate([wr_hi, wr_lo, jnp.zeros((d, LANE - 2 * N_EXPERTS), BF16)], axis=1)
    shared = {'wr_cat': wr_cat, 'b_router': b_router}

    rope_tabs = _rope_tables(dec_t)
    na_plan = _na_plan(dec_t // GRID_W)
    na_bias = _na_bias_tables(rel_bias, na_plan[4])
    ga_k = cache_ga_k.reshape(dec_b, depth, past, GA_KV)
    ga_v = cache_ga_v.reshape(dec_b, depth, past, GA_KV)
    na_k = cache_na_k.reshape(dec_b, depth, past, NA_W)
    na_v = cache_na_v.reshape(dec_b, depth, past, NA_W)

    w_in_bf, w_o_bf = w_in.astype(BF16), w_o.astype(BF16)
    w_gate_bf, w_up_bf, w_down_bf = w_gate.astype(BF16), w_up.astype(BF16), w_down.astype(BF16)

    y_prompt, y_sample = x_prompt, x_sample
    cache = [jnp.zeros((bsz, depth, seq, w), F32) for w in (GA_KV, GA_KV, NA_W, NA_W)]
    h2_all = jnp.zeros((bsz * seq + dec_b * dec_t, d), F32)
    for i in range(depth):
        lw = {
            'layer': i, 'depth': depth, 'prev_cache': cache,
            'w_in': w_in_bf, 'w_o': w_o_bf,
            'q_gain': (q_norm[i] * Q_PRESCALE).reshape(1, HEAD_DIM).astype(F32),
            'k_gain': k_norm[i].reshape(1, HEAD_DIM).astype(F32),
            'ln1_g': ln1_g[i], 'ln1_b': ln1_b[i], 'ln2_g': ln2_g[i], 'ln2_b': ln2_b[i],
            'w_gate': w_gate_bf, 'w_up': w_up_bf, 'w_down': w_down_bf,
        }
        y_prompt, y_sample, cache, h2_all = _layer(y_prompt, y_sample, mod[i], lw, shared,
                                                   (ga_k, ga_v, na_k, na_v, rope_tabs, na_bias, na_plan), h2_all)

    outs = [arr.reshape(bsz, depth, seq, -1, HEAD_DIM) for arr in cache]
    return (y_prompt, y_sample, outs[0], outs[1], outs[2], outs[3])
```

```python
import functools
import math

import numpy as np
import jax
import jax.numpy as jnp
from jax import lax
from jax.experimental import pallas as pl
from jax.experimental.pallas import tpu as pltpu

F32 = jnp.float32
BF16 = jnp.bfloat16

DEPTH = 2
GRID_W = 64
HEAD_DIM = 128
N_HEADS_GA = 8
N_KV_GA = 2
N_HEADS_NA = 8
NA_WIN_H = 8
NA_WIN_W = 16
ROPE_THETA = 10000.0
N_EXPERTS = 16
N_GROUPS = 4
EXPERTS_PER_GROUP = N_EXPERTS // N_GROUPS
TOP_K = 2
N_MOD = 6
DN_ALPHA = (2.0 * DEPTH) ** 0.25
EPS = 1e-6
ATTN_SCALE = HEAD_DIM ** -0.5
LOG2E = math.log2(math.e)
Q_PRESCALE = ATTN_SCALE * LOG2E
MASK_VALUE = -1e30

GA_Q = N_HEADS_GA * HEAD_DIM
GA_KV = N_KV_GA * HEAD_DIM
NA_W = N_HEADS_NA * HEAD_DIM
COL_QA = 0
COL_KA = COL_QA + N_HEADS_GA
COL_VA = COL_KA + N_KV_GA
COL_QN = COL_VA + N_KV_GA
COL_KN = COL_QN + N_HEADS_NA
COL_VN = COL_KN + N_HEADS_NA
IN_HEADS = COL_VN + N_HEADS_NA

LANE = 128
PROJ_TN = 4 * HEAD_DIM
PROJ_TM = 512
PROJ_TM_CACHE = 256
ATTN_TQ = 512
NA_ROWS = 4
NA_HEADS_PER_STEP = 4
NA_BLOCKS_PER_STEP = 2
GA_KV_PER_STEP = 2
OUT_TM = 512
COMBINE_TM = 512
MOE_BLK = 256
GATHER_DEPTH = 3
MOD_TN = 1024
VMEM_LIMIT = 52 * 1024 * 1024


def _cparams(sem):
    return pltpu.CompilerParams(dimension_semantics=sem, vmem_limit_bytes=VMEM_LIMIT)


def _mod_kernel(c_ref, w_ref, b_ref, o_ref):
    c = c_ref[...]
    s = c / (1.0 + jnp.exp(-c))
    o_ref[...] = jnp.dot(s.astype(BF16), w_ref[...].astype(BF16),
                         preferred_element_type=F32) + b_ref[...]


def _modulation(cvec, w_mod, b_mod):
    depth, d, e = w_mod.shape
    rows = cvec.shape[0]
    tn = MOD_TN if e % MOD_TN == 0 else e
    return pl.pallas_call(
        _mod_kernel,
        grid=(depth, e // tn),
        in_specs=[
            pl.BlockSpec((rows, d), lambda l, j: (0, 0)),
            pl.BlockSpec((None, d, tn), lambda l, j: (l, 0, j)),
            pl.BlockSpec((None, 1, tn), lambda l, j: (l, 0, j)),
        ],
        out_specs=pl.BlockSpec((None, rows, tn), lambda l, j: (l, 0, j)),
        out_shape=jax.ShapeDtypeStruct((depth, rows, e), F32),
        compiler_params=_cparams(("arbitrary", "arbitrary")),
        name="modulation",
    )(cvec, w_mod, b_mod.reshape(depth, 1, e))


def _rms_head(x, g):
    ms = jnp.mean(x * x, axis=-1, keepdims=True)
    return x * lax.rsqrt(ms + EPS) * g


def _rope_head(x, cos, sin, first_half):
    swapped = jnp.where(first_half, pltpu.roll(x, HEAD_DIM - 32, 1), pltpu.roll(x, 32, 1))
    return x * cos + swapped * sin


def _proj_kernel(*refs, rope, cache):
    x_ref, mod_ref, w_ref, qg_ref, kg_ref = refs[:5]
    pos = 5
    if rope:
        cos_ref, sin_ref = refs[pos:pos + 2]
        pos += 2
    if cache is not None:
        pos += cache[0]
    o_ref = refs[pos]
    pos += 1
    if cache is not None:
        ka_ref, va_ref, kn_ref, vn_ref = refs[pos:pos + 4]
        pos += 4
    h_scr = refs[pos]

    shift = mod_ref[0:1, :]
    scale = mod_ref[1:2, :]
    h_scr[...] = (x_ref[...] * (1.0 + scale) + shift).astype(BF16)

    if rope:
        lane = lax.broadcasted_iota(jnp.int32, (1, HEAD_DIM), 1)
        first_half = (lane % 64) < 32

    def normed(y, g):
        y = _rms_head(y, g)
        if rope:
            y = _rope_head(y, cos_ref[...], sin_ref[...], first_half)
        return y

    def to_cache(c_ref, c_head, y):
        if cache is not None:
            c_ref[:, :, c_head * HEAD_DIM:(c_head + 1) * HEAD_DIM] = y.reshape(cache[1], cache[2], HEAD_DIM)

    hpt = PROJ_TN // HEAD_DIM
    for jt in range(IN_HEADS // hpt):
        acc = jnp.dot(h_scr[...], w_ref[:, jt * PROJ_TN:(jt + 1) * PROJ_TN], preferred_element_type=F32)
        for hh in range(hpt):
            head = jt * hpt + hh
            y = acc[:, hh * HEAD_DIM:(hh + 1) * HEAD_DIM]
            if head < COL_KA:
                y = normed(y, qg_ref[...])
            elif head < COL_VA:
                y = normed(y, kg_ref[...])
                to_cache(ka_ref if cache else None, head - COL_KA, y)
            elif head < COL_QN:
                to_cache(va_ref if cache else None, head - COL_VA, y)
            elif head < COL_KN:
                y = y * Q_PRESCALE
            elif head < COL_VN:
                to_cache(kn_ref if cache else None, head - COL_KN, y)
            else:
                to_cache(vn_ref if cache else None, head - COL_VN, y)
            o_ref[:, head * HEAD_DIM:(head + 1) * HEAD_DIM] = y.astype(BF16)


def _in_proj(x, mod_l, mod_row0, w_in_bf, w_layer, q_gain, k_gain, rope_tabs=None, cache=None):
    b, t, d = x.shape
    in_w = w_in_bf.shape[2]
    rope = rope_tabs is not None
    tm = min(PROJ_TM if rope else PROJ_TM_CACHE, t)
    assert in_w == IN_HEADS * HEAD_DIM and in_w % PROJ_TN == 0

    in_specs = [
        pl.BlockSpec((None, tm, d), lambda bi, ti: (bi, ti, 0)),
        pl.BlockSpec((None, N_MOD, d), lambda bi, ti: (bi + mod_row0, 0, 0)),
        pl.BlockSpec((None, d, in_w), lambda bi, ti: (w_layer, 0, 0), pipeline_mode=pl.Buffered(1)),
        pl.BlockSpec((1, HEAD_DIM), lambda bi, ti: (0, 0)),
        pl.BlockSpec((1, HEAD_DIM), lambda bi, ti: (0, 0)),
    ]
    args = [x, mod_l, w_in_bf, q_gain, k_gain]
    if rope:
        in_specs += [pl.BlockSpec((tm, HEAD_DIM), lambda bi, ti: (ti, 0))] * 2
        args += list(rope_tabs)
    out_specs = [pl.BlockSpec((None, tm, in_w), lambda bi, ti: (bi, ti, 0))]
    out_shape = [jax.ShapeDtypeStruct((b, t, in_w), BF16)]
    aliases = {}
    cache_cfg = None
    if cache is not None:
        layer, depth, bsz, seq, prev = cache
        assert b == 1 and tm % seq == 0
        nb = tm // seq
        cache_cfg = (len(prev), nb, seq)
        for k, arr in enumerate(prev):
            aliases[len(args)] = 1 + k
            in_specs.append(pl.BlockSpec(memory_space=pl.ANY))
            args.append(arr)
        widths = (GA_KV, GA_KV, NA_W, NA_W)
        out_specs += [pl.BlockSpec((nb, None, seq, w), lambda bi, ti: (ti, layer, 0, 0)) for w in widths]
        out_shape += [jax.ShapeDtypeStruct((bsz, depth, seq, w), F32) for w in widths]
    outs = pl.pallas_call(
        functools.partial(_proj_kernel, rope=rope, cache=cache_cfg),
        grid=(b, t // tm),
        in_specs=in_specs,
        out_specs=out_specs,
        out_shape=out_shape,
        input_output_aliases=aliases,
        scratch_shapes=[pltpu.VMEM((tm, d), BF16)],
        compiler_params=_cparams(("arbitrary", "arbitrary")),
        name="in_proj_rope" if rope else "in_proj_cache",
    )(*args)
    return outs


def _rope_tables(t):
    half = HEAD_DIM // 4
    tt = jnp.arange(t, dtype=jnp.int32)
    row = (tt // GRID_W).astype(F32)
    col = (tt % GRID_W).astype(F32)
    inv_freq = 1.0 / (ROPE_THETA ** (jnp.arange(half, dtype=F32) / half))
    ar = row[:, None] * inv_freq[None, :]
    ac = col[:, None] * inv_freq[None, :]
    cos = jnp.concatenate([jnp.cos(ar), jnp.cos(ar), jnp.cos(ac), jnp.cos(ac)], axis=-1)
    sin = jnp.concatenate([-jnp.sin(ar), jnp.sin(ar), -jnp.sin(ac), jnp.sin(ac)], axis=-1)
    return cos, sin


_NT = (((1,), (1,)), ((), ()))


def _softmax_pv(scores, values):
    m = scores[0].max(axis=-1, keepdims=True)
    for s in scores[1:]:
        m = jnp.maximum(m, s.max(axis=-1, keepdims=True))
    l = None
    acc = None
    for s, v in zip(scores, values):
        p = jnp.exp2(s - m)
        ps = p.sum(axis=-1, keepdims=True)
        pv = jnp.dot(p.astype(BF16), v, preferred_element_type=F32)
        l = ps if l is None else l + ps
        acc = pv if acc is None else acc + pv
    return acc * (1.0 / l)


def _dense_attn_kernel(*refs, r_heads, gs, has_ctx):
    if has_ctx:
        q_ref, k_ref, v_ref, kc_ref, vc_ref, o_ref = refs
    else:
        q_ref, k_ref, v_ref, o_ref = refs
    for gi in range(gs):
        kv = slice(gi * HEAD_DIM, (gi + 1) * HEAD_DIM)
        k = k_ref[:, kv]
        v = v_ref[:, kv]
        if has_ctx:
            kc = kc_ref[:, kv].astype(BF16)
            vc = vc_ref[:, kv].astype(BF16)
        for r in range(r_heads):
            h = gi * r_heads + r
            sl = slice(h * HEAD_DIM, (h + 1) * HEAD_DIM)
            q = q_ref[:, sl]
            scores = [lax.dot_general(q, k, _NT, preferred_element_type=F32)]
            values = [v]
            if has_ctx:
                scores.append(lax.dot_general(q, kc, _NT, preferred_element_type=F32))
                values.append(vc)
            o_ref[:, sl] = _softmax_pv(scores, values).astype(BF16)


def _dense_attention(proj, q_col, k_col, v_col, groups, r_heads, gs=1, ctx=None):
    b, t, _ = proj.shape
    tq = min(ATTN_TQ, t)
    qw = gs * r_heads * HEAD_DIM
    kw = gs * HEAD_DIM
    assert q_col % (gs * r_heads) == 0 and k_col % gs == 0 and v_col % gs == 0 and groups % gs == 0
    in_specs = [
        pl.BlockSpec((None, tq, qw), lambda bi, g, qi: (bi, qi, q_col // (gs * r_heads) + g)),
        pl.BlockSpec((None, t, kw), lambda bi, g, qi: (bi, 0, k_col // gs + g)),
        pl.BlockSpec((None, t, kw), lambda bi, g, qi: (bi, 0, v_col // gs + g)),
    ]
    args = [proj, proj, proj]
    if ctx is not None:
        ck, cv, layer = ctx
        l_ctx = ck.shape[2]
        spec = pl.BlockSpec((None, None, l_ctx, kw), lambda bi, g, qi: (bi, layer, 0, g))
        in_specs += [spec, spec]
        args += [ck, cv]
    return pl.pallas_call(
        functools.partial(_dense_attn_kernel, r_heads=r_heads, gs=gs, has_ctx=ctx is not None),
        grid=(b, groups // gs, t // tq),
        in_specs=in_specs,
        out_specs=pl.BlockSpec((None, tq, qw), lambda bi, g, qi: (bi, qi, g)),
        out_shape=jax.ShapeDtypeStruct((b, t, groups * r_heads * HEAD_DIM), BF16),
        compiler_params=_cparams(("arbitrary", "arbitrary", "arbitrary")),
        name="dense_attn_ctx" if ctx is not None else "dense_attn",
    )(*args)


def _na_plan(rows):
    kh = min(NA_WIN_H, rows)
    r_blk = min(NA_ROWS, rows)
    assert rows % r_blk == 0
    slab = min(r_blk - 1 + kh, rows)
    row_start = np.clip(np.arange(rows) - kh // 2, 0, rows - kh)
    slab_start, pat_id, pats, sigs = [], [], [], {}
    for r0 in range(0, rows, r_blk):
        ss = min(row_start[r0], rows - slab)
        rel = tuple(int(row_start[r0 + ri] - ss) for ri in range(r_blk))
        sig = (int(ss - r0), rel)
        if sig not in sigs:
            sigs[sig] = len(pats)
            q_row = r0 + np.arange(r_blk)[:, None, None, None]
            k_row = ss + np.arange(slab)[None, None, :, None]
            rs = row_start[r0:r0 + r_blk][:, None, None, None]
            row_valid = ((k_row >= rs) & (k_row < rs + kh))[:, 0, :, 0]
            row_off = (k_row - q_row + (NA_WIN_H - 1))[:, 0, :, 0]
            row_sel = (row_off[:, :, None] == np.arange(2 * NA_WIN_H - 1)).astype(np.float32)
            pats.append((row_valid, row_sel))
        slab_start.append(int(ss))
        pat_id.append(sigs[sig])
    return r_blk, slab, np.array(slab_start, np.int32), np.array(pat_id, np.int32), pats


def _na_bias_tables(rel_bias, pats):
    col = np.arange(GRID_W)
    col_start = np.clip(col - NA_WIN_W // 2, 0, GRID_W - NA_WIN_W)
    col_off = col[None, :] - col[:, None] + (NA_WIN_W - 1)
    col_sel = (col_off[:, :, None] == np.arange(2 * NA_WIN_W - 1)).astype(np.float32)
    col_valid = (col[None, :] >= col_start[:, None]) & (col[None, :] < col_start[:, None] + NA_WIN_W)
    col_aug = np.concatenate([col_sel, np.ones((GRID_W, GRID_W, 1), np.float32),
                              np.where(col_valid, 0.0, MASK_VALUE)[..., None].astype(np.float32)], axis=-1)
    row_sel = np.stack([p[1] for p in pats])
    row_pen = np.stack([np.where(p[0], 0.0, MASK_VALUE) for p in pats]).astype(np.float32)
    hi = lax.Precision.HIGHEST
    by_row = jnp.einsum('lhrc,pisr->lphisc', rel_bias.astype(F32) * LOG2E, row_sel, precision=hi)
    lead = by_row.shape[:5]
    row_aug = jnp.concatenate([by_row, jnp.broadcast_to(row_pen[None, :, None, :, :, None], lead + (1,)),
                               jnp.ones(lead + (1,), F32)], axis=-1)
    bias = jnp.einsum('lphisc,qkc->lphiqsk', row_aug, col_aug, precision=hi)
    l, p, h, r, w, sl, _ = bias.shape
    return bias.reshape(l, p, h, r * w, sl * w)


def _na_kernel(ss_ref, pat_ref, q_ref, k_ref, v_ref, kc_ref, vc_ref, *rest, slab_len, n_heads, n_sub, qn):
    del pat_ref
    bias_refs, o_ref = rest[:n_sub], rest[n_sub]
    rb = pl.program_id(2)
    for h in range(n_heads):
        sl = slice(h * HEAD_DIM, (h + 1) * HEAD_DIM)
        kc = kc_ref[:, sl].astype(BF16)
        vc = vc_ref[:, sl].astype(BF16)
        for u in range(n_sub):
            start = pl.multiple_of(ss_ref[rb * n_sub + u] * GRID_W, GRID_W)
            rows = slice(u * qn, (u + 1) * qn)
            ks = k_ref[pl.ds(start, slab_len), sl]
            vs = v_ref[pl.ds(start, slab_len), sl]
            q = q_ref[rows, sl]
            s_loc = lax.dot_general(q, ks, _NT, preferred_element_type=F32) + bias_refs[u][h]
            s_ctx = lax.dot_general(q, kc, _NT, preferred_element_type=F32)
            o_ref[rows, sl] = _softmax_pv([s_loc, s_ctx], [vs, vc]).astype(BF16)


def _neighborhood_attention(proj, ck, cv, layer, bias_tab, plan):
    b, t, _ = proj.shape
    r_blk, slab, slab_start, pat_id, _ = plan
    qn = r_blk * GRID_W
    sn = slab * GRID_W
    l_ctx = ck.shape[2]
    hb = NA_HEADS_PER_STEP
    hw = hb * HEAD_DIM
    assert COL_QN % hb == 0 and COL_KN % hb == 0 and COL_VN % hb == 0 and N_HEADS_NA % hb == 0
    n_rb = t // qn
    n_sub = NA_BLOCKS_PER_STEP if n_rb % NA_BLOCKS_PER_STEP == 0 else 1

    def bias_spec(u):
        return pl.BlockSpec((None, None, hb, qn, sn),
                            lambda bi, hg, rb, ss, pt: (layer, pt[rb * n_sub + u], hg, 0, 0))

    grid_spec = pltpu.PrefetchScalarGridSpec(
        num_scalar_prefetch=2,
        grid=(b, N_HEADS_NA // hb, n_rb // n_sub),
        in_specs=[
            pl.BlockSpec((None, n_sub * qn, hw), lambda bi, hg, rb, ss, pt: (bi, rb, COL_QN // hb + hg)),
            pl.BlockSpec((None, t, hw), lambda bi, hg, rb, ss, pt: (bi, 0, COL_KN // hb + hg)),
            pl.BlockSpec((None, t, hw), lambda bi, hg, rb, ss, pt: (bi, 0, COL_VN // hb + hg)),
            pl.BlockSpec((None, None, l_ctx, hw), lambda bi, hg, rb, ss, pt: (bi, layer, 0, hg)),
            pl.BlockSpec((None, None, l_ctx, hw), lambda bi, hg, rb, ss, pt: (bi, layer, 0, hg)),
        ] + [bias_spec(u) for u in range(n_sub)],
        out_specs=pl.BlockSpec((None, n_sub * qn, hw), lambda bi, hg, rb, ss, pt: (bi, rb, hg)),
    )
    return pl.pallas_call(
        functools.partial(_na_kernel, slab_len=sn, n_heads=hb, n_sub=n_sub, qn=qn),
        grid_spec=grid_spec,
        out_shape=jax.ShapeDtypeStruct((b, t, NA_W), BF16),
        compiler_params=_cparams(("arbitrary", "arbitrary", "arbitrary")),
        name="neighborhood_attn",
    )(jnp.asarray(slab_start), jnp.asarray(pat_id), proj, proj, proj, ck, cv, *([bias_tab] * n_sub))


def _layer_norm(y, g, b):
    mu = jnp.mean(y, axis=-1, keepdims=True)
    yc = y - mu
    var = jnp.mean(yc * yc, axis=-1, keepdims=True)
    return yc * lax.rsqrt(var + EPS) * g + b


def _top2_of4(vals):
    m1 = jnp.maximum(jnp.maximum(vals[0], vals[1]), jnp.maximum(vals[2], vals[3]))
    i1 = jnp.where(vals[0] == m1, 0, jnp.where(vals[1] == m1, 1, jnp.where(vals[2] == m1, 2, 3)))
    rest = [jnp.where(i1 == i, -1.0, vals[i]) for i in range(4)]
    m2 = jnp.maximum(jnp.maximum(rest[0], rest[1]), jnp.maximum(rest[2], rest[3]))
    i2 = jnp.where(rest[0] == m2, 0, jnp.where(rest[1] == m2, 1, jnp.where(rest[2] == m2, 2, 3)))
    return m1, i1, m2, i2


def _route_rows(logits_t):
    m = logits_t.max(axis=0, keepdims=True)
    e = jnp.exp(logits_t - m)
    probs = e / e.sum(axis=0, keepdims=True)
    rows = [probs[i:i + 1, :] for i in range(N_EXPERTS)]
    groups = [rows[g * EXPERTS_PER_GROUP:(g + 1) * EXPERTS_PER_GROUP] for g in range(N_GROUPS)]
    scores = []
    for g in range(N_GROUPS):
        m1, _, m2, _ = _top2_of4(groups[g])
        scores.append(m1 + m2)
    best = jnp.maximum(jnp.maximum(scores[0], scores[1]), jnp.maximum(scores[2], scores[3]))
    gi = jnp.where(scores[0] == best, 0, jnp.where(scores[1] == best, 1, jnp.where(scores[2] == best, 2, 3)))
    sel = [jnp.where(gi == 0, groups[0][i], jnp.where(gi == 1, groups[1][i],
                     jnp.where(gi == 2, groups[2][i], groups[3][i]))) for i in range(EXPERTS_PER_GROUP)]
    w1, l1, w2, l2 = _top2_of4(sel)
    wsum = w1 + w2
    return (gi * EXPERTS_PER_GROUP + l1, gi * EXPERTS_PER_GROUP + l2, w1 / wsum, w2 / wsum)


def _out_proj_kernel(oa_ref, on_ref, x_ref, mod_ref, wo_ref, g_ref, b_ref, wr_ref, br_ref, h2_in_ref,
                     x1_ref, h2_ref, e_ref, wt_ref):
    del h2_in_ref
    attn = jnp.dot(oa_ref[...], wo_ref[0:GA_Q, :], preferred_element_type=F32)
    attn = attn + jnp.dot(on_ref[...], wo_ref[GA_Q:GA_Q + NA_W, :], preferred_element_type=F32)
    gate1 = mod_ref[2:3, :]
    x1 = _layer_norm(DN_ALPHA * x_ref[...] + gate1 * attn, g_ref[...], b_ref[...])
    x1_ref[...] = x1
    h2 = x1 * (1.0 + mod_ref[4:5, :]) + mod_ref[3:4, :]
    h2_ref[...] = h2
    h_hi = h2.astype(BF16)
    h_lo = (h2 - h_hi.astype(F32)).astype(BF16)
    r_hi = jnp.dot(h_hi, wr_ref[...], preferred_element_type=F32)
    r_lo = jnp.dot(h_lo, wr_ref[...], preferred_element_type=F32)
    logits = r_hi + pltpu.roll(r_hi, LANE - N_EXPERTS, 1) + r_lo
    logits_t = logits.T[0:N_EXPERTS, :] + br_ref[...]
    e1, e2, w1, w2 = _route_rows(logits_t)
    e_ref[0:1, :] = e1
    e_ref[1:2, :] = e2
    n = w1.shape[1]
    row = lax.broadcasted_iota(jnp.int32, (LANE, n), 0)
    w_rows = jnp.where(row == 0, w1, jnp.where(row == 1, w2, 0.0))
    wt_ref[...] = w_rows.T


def _out_proj(oa, on, x, mod_l, mod_row0, w_o_bf, w_layer, ln_g, ln_b, wr_cat, b_router, h2_all, row0):
    b, t, d = x.shape
    tm = math.gcd(min(OUT_TM, t), row0) if row0 else min(OUT_TM, t)
    nt = t // tm
    tok = lambda bi, ti: (bi, ti, 0)
    const2 = lambda bi, ti: (0, 0)
    return pl.pallas_call(
        _out_proj_kernel,
        grid=(b, t // tm),
        in_specs=[
            pl.BlockSpec((None, tm, GA_Q), tok),
            pl.BlockSpec((None, tm, NA_W), tok),
            pl.BlockSpec((None, tm, d), tok),
            pl.BlockSpec((None, N_MOD, d), lambda bi, ti: (bi + mod_row0, 0, 0)),
            pl.BlockSpec((None, GA_Q + NA_W, d), lambda bi, ti: (w_layer, 0, 0), pipeline_mode=pl.Buffered(1)),
            pl.BlockSpec((1, d), const2),
            pl.BlockSpec((1, d), const2),
            pl.BlockSpec((d, LANE), const2),
            pl.BlockSpec((N_EXPERTS, 1), const2),
            pl.BlockSpec(memory_space=pl.ANY),
        ],
        out_specs=[
            pl.BlockSpec((None, tm, d), tok),
            pl.BlockSpec((tm, d), lambda bi, ti: (row0 // tm + bi * nt + ti, 0)),
            pl.BlockSpec((None, TOP_K, tm), lambda bi, ti: (bi, 0, ti)),
            pl.BlockSpec((None, tm, LANE), tok),
        ],
        out_shape=[
            jax.ShapeDtypeStruct((b, t, d), F32),
            jax.ShapeDtypeStruct(h2_all.shape, F32),
            jax.ShapeDtypeStruct((b, TOP_K, t), jnp.int32),
            jax.ShapeDtypeStruct((b, t, LANE), F32),
        ],
        input_output_aliases={9: 1},
        compiler_params=_cparams(("arbitrary", "arbitrary")),
        name="out_proj_ln_router",
    )(oa, on, x, mod_l, w_o_bf, ln_g.reshape(1, d), ln_b.reshape(1, d), wr_cat,
      b_router.reshape(N_EXPERTS, 1).astype(F32), h2_all)


def _row_copy(src_hbm, idx_ref, dst, sem, r, base=None):
    idx = idx_ref[0, r] if base is None else idx_ref[base + r]
    return pltpu.make_async_copy(src_hbm.at[pl.ds(idx, 1)], dst.at[pl.ds(r, 1)], sem)


def _start_row_gather(src_hbm, idx_ref, dst, sem, n_rows, base=None):
    def body(r, carry):
        _row_copy(src_hbm, idx_ref, dst, sem, r, base).start()
        return carry
    lax.fori_loop(0, n_rows, body, 0, unroll=8)


def _start_row_gather_inline(src_hbm, idx_ref, dst, sem, r_lo, r_hi, base=None):
    for r in range(r_lo, r_hi):
        _row_copy(src_hbm, idx_ref, dst, sem, r, base).start()


def _wait_row_gather(src_hbm, dst, sem, n_rows):
    pltpu.make_async_copy(src_hbm.at[pl.ds(0, n_rows)], dst, sem).wait()


def _moe_kernel(be_ref, nu_ref, tok_ref, h_hbm, wg_ref, wu_ref, wd_ref, o_ref,
                buf, sem, *, n_blocks):
    del be_ref
    i = pl.program_id(0)
    n_used = nu_ref[0]
    slot = i % GATHER_DEPTH
    ahead = (i + 2) % GATHER_DEPTH
    base2 = jnp.minimum(i + 2, n_blocks - 1) * MOE_BLK

    @pl.when(i == 0)
    def _():
        _start_row_gather(h_hbm, tok_ref, buf.at[0], sem.at[0], MOE_BLK, base=0)
        _start_row_gather(h_hbm, tok_ref, buf.at[1], sem.at[1], MOE_BLK, base=MOE_BLK)

    _wait_row_gather(h_hbm, buf.at[slot], sem.at[slot], MOE_BLK)

    @pl.when(i < n_used)
    def _():
        xb = buf[slot].astype(BF16)
        gate = jnp.dot(xb, wg_ref[...], preferred_element_type=F32)
        up = jnp.dot(xb, wu_ref[...], preferred_element_type=F32)
        act = (gate / (1.0 + jnp.exp(-gate)) * up).astype(BF16)
        _start_row_gather_inline(h_hbm, tok_ref, buf.at[ahead], sem.at[ahead], 0, MOE_BLK, base=base2)
        o_ref[...] = jnp.dot(act, wd_ref[...], preferred_element_type=F32)

    @pl.when(i >= n_used)
    def _():
        o_ref[...] = jnp.zeros_like(o_ref)
        _start_row_gather(h_hbm, tok_ref, buf.at[ahead], sem.at[ahead], MOE_BLK, base=base2)

    @pl.when(i == n_blocks - 1)
    def _():
        for k in (1, 2):
            s = (i + k) % GATHER_DEPTH
            _wait_row_gather(h_hbm, buf.at[s], sem.at[s], MOE_BLK)


def _moe_ffn(h2_flat, tok_buf, blk_e, n_used, wg, wu, wd, layer):
    n, d = h2_flat.shape
    n_blocks = blk_e.shape[0]
    d_ff = wg.shape[3]
    grid_spec = pltpu.PrefetchScalarGridSpec(
        num_scalar_prefetch=3,
        grid=(n_blocks,),
        in_specs=[
            pl.BlockSpec(memory_space=pl.ANY),
            pl.BlockSpec((None, None, d, d_ff), lambda i, be, nu, tk: (layer, be[i], 0, 0)),
            pl.BlockSpec((None, None, d, d_ff), lambda i, be, nu, tk: (layer, be[i], 0, 0)),
            pl.BlockSpec((None, None, d_ff, d), lambda i, be, nu, tk: (layer, be[i], 0, 0)),
        ],
        out_specs=pl.BlockSpec((MOE_BLK, d), lambda i, be, nu, tk: (i, 0)),
        scratch_shapes=[pltpu.VMEM((GATHER_DEPTH, MOE_BLK, d), F32), pltpu.SemaphoreType.DMA((GATHER_DEPTH,))],
    )
    return pl.pallas_call(
        functools.partial(_moe_kernel, n_blocks=n_blocks),
        grid_spec=grid_spec,
        out_shape=jax.ShapeDtypeStruct((n_blocks * MOE_BLK, d), F32),
        compiler_params=_cparams(("arbitrary",)),
        name="moe_ffn",
    )(blk_e, n_used, tok_buf, h2_flat, wg, wu, wd)


def _combine_kernel(dest_ref, y_hbm, x1_ref, mod_ref, wt_ref, g_ref, b_ref, o_ref, buf, sem, *, n_steps, tm):
    i = pl.program_id(0)
    slot = i % GATHER_DEPTH
    ahead = (i + 2) % GATHER_DEPTH
    step_base = lambda step: jnp.minimum(step, n_steps - 1) * (TOP_K * tm)

    @pl.when(i == 0)
    def _():
        for s in range(2):
            for k in range(TOP_K):
                _start_row_gather(y_hbm, dest_ref, buf.at[s, k], sem.at[s], tm, base=step_base(s) + k * tm)

    def wait(s):
        _wait_row_gather(y_hbm, buf.at[s, 0], sem.at[s], tm)
        _wait_row_gather(y_hbm, buf.at[s, 1], sem.at[s], tm)

    wait(slot)
    y = wt_ref[:, 0:1] * buf[slot, 0] + wt_ref[:, 1:2] * buf[slot, 1]
    gate2 = mod_ref[5:6, :]
    o_ref[...] = _layer_norm(DN_ALPHA * x1_ref[...] + gate2 * y, g_ref[...], b_ref[...])
    for k in range(TOP_K):
        _start_row_gather_inline(y_hbm, dest_ref, buf.at[ahead, k], sem.at[ahead], 0, tm,
                                 base=step_base(i + 2) + k * tm)

    @pl.when(i == n_steps - 1)
    def _():
        for k in (1, 2):
            wait((i + k) % GATHER_DEPTH)


def _combine(y_sorted, dest, x1, mod_l, mod_row0, wt, ln_g, ln_b):
    b, t, d = x1.shape
    tm = min(COMBINE_TM, t)
    nt = t // tm
    n_steps = b * nt
    dest_flat = dest.reshape(b, TOP_K, nt, tm).transpose(0, 2, 1, 3).reshape(n_steps * TOP_K * tm)
    tok = lambda i, ds: (i // nt, i % nt, 0)
    grid_spec = pltpu.PrefetchScalarGridSpec(
        num_scalar_prefetch=1,
        grid=(n_steps,),
        in_specs=[
            pl.BlockSpec(memory_space=pl.ANY),
            pl.BlockSpec((None, tm, d), tok),
            pl.BlockSpec((None, N_MOD, d), lambda i, ds: (i // nt + mod_row0, 0, 0)),
            pl.BlockSpec((None, tm, LANE), tok),
            pl.BlockSpec((1, d), lambda i, ds: (0, 0)),
            pl.BlockSpec((1, d), lambda i, ds: (0, 0)),
        ],
        out_specs=pl.BlockSpec((None, tm, d), tok),
        scratch_shapes=[pltpu.VMEM((GATHER_DEPTH, TOP_K, tm, d), F32), pltpu.SemaphoreType.DMA((GATHER_DEPTH,))],
    )
    return pl.pallas_call(
        functools.partial(_combine_kernel, n_steps=n_steps, tm=tm),
        grid_spec=grid_spec,
        out_shape=jax.ShapeDtypeStruct((b, t, d), F32),
        compiler_params=_cparams(("arbitrary",)),
        name="moe_combine_ln",
    )(dest_flat, y_sorted, x1, mod_l, wt, ln_g.reshape(1, d), ln_b.reshape(1, d))


def _dispatch_plan(e_idxs, row0s):
    flat_e = jnp.concatenate([e_idx.reshape(-1) for e_idx in e_idxs])
    a = flat_e.shape[0]
    onehot = (flat_e[:, None] == jnp.arange(N_EXPERTS, dtype=jnp.int32)[None, :]).astype(jnp.int32)
    counts = onehot.sum(axis=0)
    chunk = 256
    assert a % chunk == 0
    oh = onehot.reshape(a // chunk, chunk, N_EXPERTS).astype(F32)
    tri = jnp.tril(jnp.ones((chunk, chunk), F32))
    within = jnp.einsum('ij,cje->cie', tri, oh, precision=lax.Precision.HIGHEST)
    before = jnp.cumsum(oh.sum(axis=1), axis=0) - oh.sum(axis=1)
    running = (within + before[:, None, :]).reshape(a, N_EXPERTS).astype(jnp.int32)
    rank = (running * onehot).sum(axis=1) - 1
    padded = (counts + MOE_BLK - 1) // MOE_BLK * MOE_BLK
    pad_end = jnp.cumsum(padded)
    pad_start = pad_end - padded
    dest = pad_start[flat_e] + rank
    n_blocks = -(-a // MOE_BLK) + N_EXPERTS
    p = n_blocks * MOE_BLK
    blk_start = jnp.arange(n_blocks, dtype=jnp.int32) * MOE_BLK
    blk_e = jnp.minimum((blk_start[:, None] >= pad_end[None, :]).sum(axis=1), N_EXPERTS - 1).astype(jnp.int32)
    n_used = (pad_end[-1] // MOE_BLK).astype(jnp.int32)
    last_e = blk_e[jnp.maximum(n_used - 1, 0)]
    blk_e = jnp.where(jnp.arange(n_blocks) < n_used, blk_e, last_e)
    assert a < (1 << 16)
    order = lax.sort(flat_e * (1 << 16) + jnp.arange(a, dtype=jnp.int32)) & 0xFFFF
    start = jnp.cumsum(counts) - counts
    order = jnp.concatenate([order, jnp.zeros((MOE_BLK,), jnp.int32)])
    win = jnp.clip(start[blk_e] + blk_start - pad_start[blk_e], 0, a)
    order_blk = order[(win[:, None] + jnp.arange(MOE_BLK, dtype=jnp.int32)[None, :]).reshape(p)]
    tok_buf, o = jnp.zeros((p,), jnp.int32), 0
    for e_idx, row0 in zip(e_idxs, row0s):
        b, k, t = e_idx.shape
        local = order_blk - o
        rows = row0 + (local // (k * t)) * t + local % t
        tok_buf = jnp.where((local >= 0) & (local < e_idx.size), rows, tok_buf)
        o += e_idx.size
    dests, o = [], 0
    for e_idx in e_idxs:
        dests.append(dest[o:o + e_idx.size].reshape(e_idx.shape))
        o += e_idx.size
    return dests, tok_buf, blk_e, n_used.reshape(1)


def _attention_sublayer(x, mod_l, mod_row0, lw, ctx):
    b, t, d = x.shape
    layer = lw['layer']
    if ctx is None:
        xt = x.reshape(1, b * t, d)
        proj, *cache = _in_proj(xt, mod_l, mod_row0, lw['w_in'], layer, lw['q_gain'], lw['k_gain'],
                                cache=(layer, lw['depth'], b, t, lw['prev_cache']))
        proj = proj.reshape(b, t, -1)
        oa = _dense_attention(proj, COL_QA, COL_KA, COL_VA, N_KV_GA, N_HEADS_GA // N_KV_GA, gs=GA_KV_PER_STEP)
        on = _dense_attention(proj, COL_QN, COL_KN, COL_VN, N_HEADS_NA, 1, gs=NA_HEADS_PER_STEP)
        return oa.reshape(1, b * t, -1), on.reshape(1, b * t, -1), xt, cache
    ga_k, ga_v, na_k, na_v, rope_tabs, na_bias, na_plan = ctx
    (proj,) = _in_proj(x, mod_l, mod_row0, lw['w_in'], layer, lw['q_gain'], lw['k_gain'], rope_tabs=rope_tabs)
    oa = _dense_attention(proj, COL_QA, COL_KA, COL_VA, N_KV_GA, N_HEADS_GA // N_KV_GA, gs=GA_KV_PER_STEP,
                          ctx=(ga_k, ga_v, layer))
    on = _neighborhood_attention(proj, na_k, na_v, layer, na_bias, na_plan)
    return oa, on, x, None


def _layer(x_ctx, x_lat, mod_l, lw, shared, lat_ctx, h2_all):
    bc, tc, d = x_ctx.shape
    bl, tl, _ = x_lat.shape
    n_ctx, n_lat = bc * tc, bl * tl
    streams = [_attention_sublayer(x_ctx, mod_l, bl, lw, None),
               _attention_sublayer(x_lat, mod_l, 0, lw, lat_ctx)]
    cache = streams[0][3]
    mod_rows, row0s = (bl, 0), (0, n_ctx)
    x1s, e_idxs, wts = [], [], []
    for (oa, on, xt, _), mod_row0, row0 in zip(streams, mod_rows, row0s):
        x1, h2_all, e_idx, wt = _out_proj(oa, on, xt, mod_l, mod_row0, lw['w_o'], lw['layer'], lw['ln1_g'],
                                          lw['ln1_b'], shared['wr_cat'], shared['b_router'], h2_all, row0)
        x1s.append(x1)
        e_idxs.append(e_idx)
        wts.append(wt)
    dests, tok_buf, blk_e, n_used = _dispatch_plan(e_idxs, row0s)
    y_sorted = _moe_ffn(h2_all, tok_buf, blk_e, n_used, lw['w_gate'], lw['w_up'], lw['w_down'], lw['layer'])
    outs = [_combine(y_sorted, dest, x1, mod_l, mod_row0, wt, lw['ln2_g'], lw['ln2_b'])
            for dest, x1, wt, mod_row0 in zip(dests, x1s, wts, mod_rows)]
    return outs[0].reshape(bc, tc, d), outs[1].reshape(bl, tl, d), cache, h2_all


def kernel(x_prompt, x_sample, c, cache_ga_k, cache_ga_v, cache_na_k, cache_na_v, c_ctx, w_router, b_router, w_mod, b_mod, w_in, q_norm, k_norm, rel_bias, w_o, ln1_g, ln1_b, ln2_g, ln2_b, w_gate, w_up, w_down):
    bsz, seq, d = x_prompt.shape
    dec_b, dec_t, _ = x_sample.shape
    depth = w_mod.shape[0]
    past = cache_ga_k.shape[2]

    mod_rows = 16
    assert dec_b + 1 <= mod_rows
    cvec = jnp.concatenate([c, c_ctx[None, :], jnp.zeros((mod_rows - dec_b - 1, d), F32)], axis=0)
    mod = _modulation(cvec, w_mod, b_mod).reshape(depth, mod_rows, N_MOD, d)

    wr_hi = w_router.astype(BF16)
    wr_lo = (w_router - wr_hi.astype(F32)).astype(BF16)
    wr_cat = jnp.concatenate([wr_hi, wr_lo, jnp.zeros((d, LANE - 2 * N_EXPERTS), BF16)], axis=1)
    shared = {'wr_cat': wr_cat, 'b_router': b_router}

    rope_tabs = _rope_tables(dec_t)
    na_plan = _na_plan(dec_t // GRID_W)
    na_bias = _na_bias_tables(rel_bias, na_plan[4])
    ga_k = cache_ga_k.reshape(dec_b, depth, past, GA_KV)
    ga_v = cache_ga_v.reshape(dec_b, depth, past, GA_KV)
    na_k = cache_na_k.reshape(dec_b, depth, past, NA_W)
    na_v = cache_na_v.reshape(dec_b, depth, past, NA_W)

    w_in_bf, w_o_bf = w_in.astype(BF16), w_o.astype(BF16)
    w_gate_bf, w_up_bf, w_down_bf = w_gate.astype(BF16), w_up.astype(BF16), w_down.astype(BF16)

    y_prompt, y_sample = x_prompt, x_sample
    cache = [jnp.zeros((bsz, depth, seq, w), F32) for w in (GA_KV, GA_KV, NA_W, NA_W)]
    h2_all = jnp.zeros((bsz * seq + dec_b * dec_t, d), F32)
    for i in range(depth):
        lw = {
            'layer': i, 'depth': depth, 'prev_cache': cache,
            'w_in': w_in_bf, 'w_o': w_o_bf,
            'q_gain': (q_norm[i] * Q_PRESCALE).reshape(1, HEAD_DIM).astype(F32),
            'k_gain': k_norm[i].reshape(1, HEAD_DIM).astype(F32),
            'ln1_g': ln1_g[i], 'ln1_b': ln1_b[i], 'ln2_g': ln2_g[i], 'ln2_b': ln2_b[i],
            'w_gate': w_gate_bf, 'w_up': w_up_bf, 'w_down': w_down_bf,
        }
        y_prompt, y_sample, cache, h2_all = _layer(y_prompt, y_sample, mod[i], lw, shared,
                                                   (ga_k, ga_v, na_k, na_v, rope_tabs, na_bias, na_plan), h2_all)

    outs = [arr.reshape(bsz, depth, seq, -1, HEAD_DIM) for arr in cache]
    return (y_prompt, y_sample, outs[0], outs[1], outs[2], outs[3])
```
